```python
import math
import jax
import jax.numpy as jnp
from jax import lax
import numpy as np

D_MODEL = 1024
BATCH = 8
SEQ = 2048
DEPTH = 4

GRID_W = 64
CTX_LEN = 256
N_MIXERS = 3
CHUNK = 64
ADA_CHUNKS = 6

DN_HEADS = 8
DN_DK = D_MODEL // DN_HEADS
DN_DV = D_MODEL // DN_HEADS
SHORT_CONV = 5

HG_HEADS = 8
HG_DK = D_MODEL // HG_HEADS
HG_DV = D_MODEL // HG_HEADS

GLA_HEADS = 4
GLA_DK = D_MODEL // 2 // GLA_HEADS
GLA_DV = D_MODEL // GLA_HEADS
GLA_RANK = 16
GLA_GATE_NORM = 16.0

N_EXPERTS = 32
TOP_K = 4
D_FF = D_MODEL
SWIGLU_LIMIT = 7.0
SWIGLU_ALPHA = 1.702

LN_EPS = 1e-5
RMS_EPS = 1e-6

kernel_name = 'hybrid_deltanet_hgrn2_gla_moe_prefix_trunk'


def _layer_norm(t, g, b):
    tf = t.astype(jnp.float32)
    mu = jnp.mean(tf, axis=-1, keepdims=True)
    var = jnp.mean(jnp.square(tf - mu), axis=-1, keepdims=True)
    return ((tf - mu) * lax.rsqrt(var + LN_EPS) * g + b).astype(t.dtype)


def _l2norm(t):
    t = t.astype(jnp.float32)
    return t * lax.rsqrt(jnp.sum(t * t, axis=-1, keepdims=True) + 1e-6)


def _heads(t, n_heads):
    B, T, _ = t.shape
    return t.reshape(B, T, n_heads, -1).transpose(0, 2, 1, 3)


def _gated_rmsnorm(o, w, gate):
    B, H, T, d = o.shape
    o = o * lax.rsqrt(jnp.mean(o * o, axis=-1, keepdims=True) + RMS_EPS) * w
    o = o.transpose(0, 2, 1, 3).reshape(B, T, H * d)
    return o.astype(gate.dtype) * jax.nn.silu(gate)


def _short_conv(t, w):
    return lax.conv_general_dilated(t, w[:, None, :], window_strides=(1,), padding='SAME',
                                    dimension_numbers=('NWC', 'WIO', 'NWC'),
                                    feature_group_count=t.shape[-1])


def _flip(t):
    return jnp.flip(t, axis=2)


def _to_chunks(t):
    B, H, T = t.shape[:3]
    return jnp.moveaxis(t.reshape(B, H, T // CHUNK, CHUNK, t.shape[-1]), 2, 0)


def _from_chunks(o):
    n, B, H, C, d = o.shape
    return jnp.moveaxis(o, 0, 2).reshape(B, H, n * C, d)


def _to_col_major(t, rows):
    B, T, D = t.shape
    return t.reshape(B, rows, GRID_W, D).transpose(0, 2, 1, 3).reshape(B, T, D)


def _to_row_major(t, rows):
    B, T, D = t.shape
    return t.reshape(B, GRID_W, rows, D).transpose(0, 2, 1, 3).reshape(B, T, D)


def gla_scan(q, k, v, g, s0):
    f32 = jnp.float32
    q, k, v, g = (t.astype(f32) for t in (q, k, v, g))
    idx = jnp.arange(CHUNK)
    incl = (idx[:, None] >= idx[None, :])[:, :, None]

    def step(s, inp):
        qc, kc, vc, gc = inp
        G = jnp.cumsum(gc, axis=2)
        decay = jnp.exp(jnp.where(incl, G[:, :, :, None, :] - G[:, :, None, :, :], -jnp.inf))
        att = jnp.einsum('bhid,bhjd,bhijd->bhij', qc, kc, decay)
        o = (jnp.einsum('bhij,bhjv->bhiv', att, vc)
             + jnp.einsum('bhid,bhdv->bhiv', qc * jnp.exp(G), s))
        g_last = G[:, :, -1:, :]
        s = (jnp.exp(g_last[:, :, 0, :])[..., None] * s
             + jnp.einsum('bhjd,bhjv->bhdv', kc * jnp.exp(g_last - G), vc))
        return s, o

    xs = tuple(_to_chunks(t) for t in (q, k, v, g))
    s, o = lax.scan(step, s0.astype(f32), xs)
    return _from_chunks(o), s


def delta_scan(q, k, v, beta, g, s0):
    f32 = jnp.float32
    q, k, v, beta, g = (t.astype(f32) for t in (q, k, v, beta, g))
    B, H, T, _ = q.shape
    n = T // CHUNK
    q, k, v = (t.reshape(B, H, n, CHUNK, t.shape[-1]) for t in (q, k, v))
    beta, g = (t.reshape(B, H, n, CHUNK) for t in (beta, g))
    G = jnp.cumsum(g, axis=-1)
    idx = jnp.arange(CHUNK)
    incl = idx[:, None] >= idx[None, :]
    strict = idx[:, None] > idx[None, :]
    gamma = jnp.exp(jnp.where(incl, G[..., :, None] - G[..., None, :], -jnp.inf))
    kb = k * beta[..., None]
    m = jnp.einsum('bhnid,bhnjd->bhnij', kb, k) * gamma * strict
    tri = m + jnp.eye(CHUNK, dtype=f32)
    w = lax.linalg.triangular_solve(tri, kb * jnp.exp(G)[..., None], left_side=True,
                                    lower=True, unit_diagonal=True)
    u = lax.linalg.triangular_solve(tri, v * beta[..., None], left_side=True,
                                    lower=True, unit_diagonal=True)
    att = jnp.einsum('bhnid,bhnjd->bhnij', q, k) * gamma
    qg = q * jnp.exp(G)[..., None]
    kg = k * jnp.exp(G[..., -1:] - G)[..., None]
    decay_last = jnp.exp(G[..., -1])

    def step(s, inp):
        w_c, u_c, att_c, qg_c, kg_c, dl_c = inp
        v_new = u_c - jnp.einsum('bhcd,bhdv->bhcv', w_c, s)
        o = (jnp.einsum('bhcd,bhdv->bhcv', qg_c, s)
             + jnp.einsum('bhij,bhjv->bhiv', att_c, v_new))
        s = dl_c[..., None, None] * s + jnp.einsum('bhcd,bhcv->bhdv', kg_c, v_new)
        return s, o

    xs = tuple(jnp.moveaxis(t, 2, 0) for t in (w, u, att, qg, kg, decay_last))
    s, o = lax.scan(step, s0.astype(f32), xs)
    return _from_chunks(o), s


def _bidirectional_mix(scan, branch, h_lat, h_ctx, n_heads, dk, dv, norm_w, w_out, ctx_out):
    fwd_c, bwd_c, gate_c = branch(h_ctx)
    fwd_l, bwd_l, gate_l = branch(h_lat)
    s0 = jnp.zeros((h_ctx.shape[0], n_heads, dk, dv), jnp.float32)
    o_cf, s_f = scan(*fwd_c, s0)
    o_cb, s_b = scan(*tuple(_flip(t) for t in bwd_c), s0)
    o_lf, _ = scan(*fwd_l, s_f)
    o_lb, _ = scan(*tuple(_flip(t) for t in bwd_l), s_b)
    y_lat = _gated_rmsnorm(o_lf + _flip(o_lb), norm_w, gate_l) @ w_out
    y_ctx = (_gated_rmsnorm(o_cf + _flip(o_cb), norm_w, gate_c) @ w_out) if ctx_out else None
    return y_lat, y_ctx


def mixer_gated_deltanet(h_lat, h_ctx, w_in, conv_w, a_log, dt_bias, norm_w, w_out, ctx_out):
    qk = DN_HEADS * DN_DK
    vd = DN_HEADS * DN_DV
    f32 = jnp.float32

    def branch(h):
        B, T, _ = h.shape
        p = h @ w_in
        qkv = jax.nn.silu(_short_conv(p[..., :2 * qk + vd], conv_w))
        q = _l2norm(_heads(qkv[..., :qk], DN_HEADS)) * DN_DK ** -0.5
        k = _l2norm(_heads(qkv[..., qk:2 * qk], DN_HEADS))
        v = _heads(qkv[..., 2 * qk:], DN_HEADS)
        gate = p[..., 2 * qk + vd:2 * qk + 2 * vd]
        ba = p[..., 2 * qk + 2 * vd:].astype(f32).reshape(B, T, 4, DN_HEADS).transpose(2, 0, 3, 1)
        beta = jax.nn.sigmoid(ba[:2])
        g = (-jnp.exp(a_log.astype(f32))[:, None, :, None]
             * jax.nn.softplus(ba[2:] + dt_bias.astype(f32)[:, None, :, None]))
        return (q, k, v, beta[0], g[0]), (q, k, v, beta[1], g[1]), gate

    return _bidirectional_mix(delta_scan, branch, h_lat, h_ctx, DN_HEADS, DN_DK, DN_DV,
                              norm_w, w_out, ctx_out)


def mixer_hgrn2(h_lat, h_ctx, lower_bound, w_in, norm_w, w_out, ctx_out):
    fd = HG_HEADS * HG_DK
    idim = HG_HEADS * HG_DV
    lb = lower_bound.astype(jnp.float32).reshape(1, HG_HEADS, 1, HG_DK)
    log_lb, log_1m_lb = jnp.log(lb), jnp.log1p(-lb)

    def forget(f):
        return ((1.0 - lb) * jax.nn.sigmoid(-f),
                jnp.logaddexp(log_lb, log_1m_lb + jax.nn.log_sigmoid(f)))

    def branch(h):
        p = h @ w_in
        q = _heads(jax.nn.silu(p[..., :fd]), HG_HEADS)
        i = _heads(p[..., fd:fd + idim], HG_HEADS)
        gate = p[..., fd + idim:fd + 2 * idim]
        f = p[..., fd + 2 * idim:].astype(jnp.float32)
        k_f, g_f = forget(_heads(f[..., :fd], HG_HEADS))
        k_b, g_b = forget(_heads(f[..., fd:], HG_HEADS))
        return (q, k_f, i, g_f), (q, k_b, i, g_b), gate

    return _bidirectional_mix(gla_scan, branch, h_lat, h_ctx, HG_HEADS, HG_DK, HG_DV,
                              norm_w, w_out, ctx_out)


def mixer_gla(h_lat, h_ctx, w_in, gk_w, gk_b, norm_w, w_out, ctx_out):
    kd = GLA_HEADS * GLA_DK
    vd = GLA_HEADS * GLA_DV

    def branch(h):
        B, T, _ = h.shape
        p = h @ w_in
        q = _heads(p[..., :kd], GLA_HEADS) * GLA_DK ** -0.5
        k = _heads(p[..., kd:2 * kd], GLA_HEADS)
        v = _heads(p[..., 2 * kd:2 * kd + vd], GLA_HEADS)
        gate = p[..., 2 * kd + vd:2 * kd + 2 * vd]
        lr = p[..., 2 * kd + 2 * vd:].reshape(B, T, 2, GLA_RANK)
        gk = jnp.einsum('btzr,zrk->zbtk', lr, gk_w) + gk_b[:, None, None, :]
        gk = jax.nn.log_sigmoid(gk.astype(jnp.float32)) / GLA_GATE_NORM
        return (q, k, v, _heads(gk[0], GLA_HEADS)), (q, k, v, _heads(gk[1], GLA_HEADS)), gate

    return _bidirectional_mix(gla_scan, branch, h_lat, h_ctx, GLA_HEADS, GLA_DK, GLA_DV,
                              norm_w, w_out, ctx_out)


def _clamped_swiglu(u):
    x_glu = jnp.minimum(u[..., 0::2], SWIGLU_LIMIT)
    x_lin = jnp.clip(u[..., 1::2], -SWIGLU_LIMIT, SWIGLU_LIMIT)
    return x_glu * jax.nn.sigmoid(SWIGLU_ALPHA * x_glu) * (x_lin + 1.0)


def moe_ffn(h, router_w, router_b, w1, b1, w2, b2):
    logits = (h @ router_w + router_b).astype(jnp.float32)
    top_val, top_idx = lax.top_k(logits, TOP_K)
    weights = jax.nn.softmax(top_val, axis=-1)
    combine = jnp.einsum('nk,nke->ne', weights,
                         jax.nn.one_hot(top_idx, N_EXPERTS, dtype=jnp.float32))
    out = jnp.zeros(h.shape, jnp.float32)
    for e in range(N_EXPERTS):
        y = _clamped_swiglu(h @ w1[e] + b1[e]) @ w2[e] + b2[e]
        out = out + combine[:, e:e + 1] * y
    return out.astype(h.dtype)


def setup_inputs(seed: int = 0) -> dict:
    key = jax.random.key(seed)
    k = jax.random.split(key, 29)
    f32 = jnp.float32
    D = D_MODEL
    n_a = len(range(0, DEPTH, N_MIXERS))
    n_b = len(range(1, DEPTH, N_MIXERS))
    n_c = len(range(2, DEPTH, N_MIXERS))
    out_scale = (8.0 * DEPTH) ** -0.25
    dn_in = 2 * DN_HEADS * DN_DK + 2 * DN_HEADS * DN_DV + 4 * DN_HEADS
    hg_in = 3 * HG_HEADS * HG_DK + 2 * HG_HEADS * HG_DV
    gla_in = 2 * GLA_HEADS * GLA_DK + 2 * GLA_HEADS * GLA_DV + 2 * GLA_RANK

    def normal(i, shape, scale):
        return jax.random.normal(k[i], shape, f32) * scale

    dt = jnp.exp(jax.random.uniform(k[10], (n_a, 2, DN_HEADS), f32,
                                    math.log(1e-3), math.log(1e-1)))
    return {
        'x': normal(0, (BATCH, SEQ, D), 1.0),
        'c': normal(1, (BATCH, D), 1.0),
        'ctx': normal(2, (BATCH, CTX_LEN, D), 1.0),
        'c_ctx': normal(3, (D,), 1.0),
        'ada_w': normal(4, (DEPTH, D, ADA_CHUNKS * D), 0.5 * D ** -0.5),
        'ada_b': normal(5, (DEPTH, ADA_CHUNKS * D), 0.02),
        'ln_g': 1.0 + normal(6, (DEPTH, 2, D), 0.05),
        'ln_b': normal(7, (DEPTH, 2, D), 0.02),
        'dn_w_in': normal(8, (n_a, D, dn_in), D ** -0.5),
        'dn_conv_w': normal(9, (n_a, SHORT_CONV, 2 * DN_HEADS * DN_DK + DN_HEADS * DN_DV),
                            SHORT_CONV ** -0.5),
        'dn_a_log': jnp.log(jax.random.uniform(k[11], (n_a, 2, DN_HEADS), f32, 1.0, 16.0)),
        'dn_dt_bias': jnp.log(jnp.expm1(dt)),
        'dn_norm_w': 1.0 + normal(12, (n_a, DN_DV), 0.05),
        'dn_w_out': normal(13, (n_a, DN_HEADS * DN_DV, D), out_scale * (DN_HEADS * DN_DV) ** -0.5),
        'hg_w_in': normal(14, (n_b, D, hg_in), D ** -0.5),
        'hg_lower_bound': normal(15, (DEPTH, HG_HEADS * HG_DK), 0.1),
        'hg_norm_w': 1.0 + normal(16, (n_b, HG_DV), 0.05),
        'hg_w_out': normal(17, (n_b, HG_HEADS * HG_DV, D), out_scale * (HG_HEADS * HG_DV) ** -0.5),
        'gla_w_in': normal(18, (n_c, D, gla_in), D ** -0.5),
        'gla_gk_w': normal(19, (n_c, 2, GLA_RANK, GLA_HEADS * GLA_DK), GLA_RANK ** -0.5),
        'gla_gk_b': normal(20, (n_c, 2, GLA_HEADS * GLA_DK), 0.02),
        'gla_norm_w': 1.0 + normal(21, (n_c, GLA_DV), 0.05),
        'gla_w_out': normal(22, (n_c, GLA_HEADS * GLA_DV, D), out_scale * (GLA_HEADS * GLA_DV) ** -0.5),
        'router_w': normal(23, (DEPTH, D, N_EXPERTS), D ** -0.5),
        'router_b': normal(24, (DEPTH, N_EXPERTS), 0.01),
        'exp_w1': normal(25, (DEPTH, N_EXPERTS, D, 2 * D_FF), D ** -0.5),
        'exp_b1': normal(26, (DEPTH, N_EXPERTS, 2 * D_FF), 0.01),
        'exp_w2': normal(27, (DEPTH, N_EXPERTS, D_FF, D), out_scale * D_FF ** -0.5),
        'exp_b2': normal(28, (DEPTH, N_EXPERTS, D), 0.01),
    }


def reference(x, c, ctx, c_ctx, ada_w, ada_b, ln_g, ln_b, dn_w_in, dn_conv_w, dn_a_log,
              dn_dt_bias, dn_norm_w, dn_w_out, hg_w_in, hg_lower_bound, hg_norm_w, hg_w_out,
              gla_w_in, gla_gk_w, gla_gk_b, gla_norm_w, gla_w_out, router_w, router_b,
              exp_w1, exp_b1, exp_w2, exp_b2):
    B, T, D = x.shape
    rows = T // GRID_W
    alpha = (2.0 * DEPTH) ** 0.25
    lb_soft = jax.nn.softmax(hg_lower_bound.astype(jnp.float32), axis=0)
    lower_bounds = jnp.cumsum(lb_soft, axis=0) - lb_soft[0]

    for i in range(DEPTH):
        last = i == DEPTH - 1
        kind, j = i % N_MIXERS, i // N_MIXERS
        m_lat = (jax.nn.silu(c) @ ada_w[i] + ada_b[i]).reshape(B, ADA_CHUNKS, 1, D)
        m_ctx = (jax.nn.silu(c_ctx) @ ada_w[i] + ada_b[i]).reshape(ADA_CHUNKS, D)

        h_lat = x * (1 + m_lat[:, 1]) + m_lat[:, 0]
        h_ctx = ctx * (1 + m_ctx[1]) + m_ctx[0]
        col_major = i % 2 == 1
        if col_major:
            h_lat = _to_col_major(h_lat, rows)
        if kind == 0:
            y_lat, y_ctx = mixer_gated_deltanet(h_lat, h_ctx, dn_w_in[j], dn_conv_w[j], dn_a_log[j],
                                                dn_dt_bias[j], dn_norm_w[j], dn_w_out[j], not last)
        elif kind == 1:
            y_lat, y_ctx = mixer_hgrn2(h_lat, h_ctx, lower_bounds[i], hg_w_in[j], hg_norm_w[j],
                                       hg_w_out[j], not last)
        else:
            y_lat, y_ctx = mixer_gla(h_lat, h_ctx, gla_w_in[j], gla_gk_w[j], gla_gk_b[j],
                                     gla_norm_w[j], gla_w_out[j], not last)
        if col_major:
            y_lat = _to_row_major(y_lat, rows)
        x = _layer_norm(alpha * x + m_lat[:, 2] * y_lat, ln_g[i, 0], ln_b[i, 0])

        h_lat = (x * (1 + m_lat[:, 4]) + m_lat[:, 3]).reshape(B * T, D)
        if last:
            f_lat = moe_ffn(h_lat, router_w[i], router_b[i], exp_w1[i], exp_b1[i],
                            exp_w2[i], exp_b2[i])
        else:
            ctx = _layer_norm(alpha * ctx + m_ctx[2] * y_ctx, ln_g[i, 0], ln_b[i, 0])
            h_ctx = (ctx * (1 + m_ctx[4]) + m_ctx[3]).reshape(-1, D)
            f_all = moe_ffn(jnp.concatenate([h_lat, h_ctx], axis=0), router_w[i], router_b[i],
                            exp_w1[i], exp_b1[i], exp_w2[i], exp_b2[i])
            f_lat, f_ctx = f_all[:B * T], f_all[B * T:]
            ctx = _layer_norm(alpha * ctx + m_ctx[5] * f_ctx.reshape(ctx.shape),
                              ln_g[i, 1], ln_b[i, 1])
        x = _layer_norm(alpha * x + m_lat[:, 5] * f_lat.reshape(B, T, D), ln_g[i, 1], ln_b[i, 1])
    return x
```

```python
import functools

import jax
import jax.numpy as jnp
from jax import lax
from jax.experimental import pallas as pl
from jax.experimental.pallas import tpu as pltpu

F32 = jnp.float32
BF16 = jnp.bfloat16
HIGHEST = lax.Precision.HIGHEST

D_MODEL = 1024
BATCH = 8
SEQ = 2048
CTX_LEN = 256
DEPTH = 4
GRID_W = 64
CHUNK = 64
ADA_CHUNKS = 6
N_LAT = BATCH * SEQ
N_CTX = BATCH * CTX_LEN
N_TOK = N_LAT + N_CTX
TILE = 256
LAT_TILES = N_LAT // TILE
ALL_TILES = N_TOK // TILE
TILES_PER_SEQ = SEQ // TILE
MOD_ROWS = 16
LANES = 128
SUBLANES = 8

N_EXPERTS = 32
TOP_K = 4
D_FF = D_MODEL
SWIGLU_LIMIT = 7.0
SWIGLU_ALPHA = 1.702
EXPERT_TILE = 256
N_SLOT_TILES = N_TOK * TOP_K // EXPERT_TILE + N_EXPERTS
N_SLOTS = N_SLOT_TILES * EXPERT_TILE

LN_EPS = 1e-5
RMS_EPS = 1e-6
ALPHA = (2.0 * DEPTH) ** 0.25

VMEM_LIMIT = 48 * 1024 * 1024


def _cparams(*sem):
    return pltpu.CompilerParams(dimension_semantics=sem, vmem_limit_bytes=VMEM_LIMIT)


def _sigmoid(x):
    return 1.0 / (1.0 + jnp.exp(-x))


def _silu(x):
    return x * _sigmoid(x)


def _softplus(x):
    return jnp.maximum(x, 0.0) + jnp.log(1.0 + jnp.exp(-jnp.abs(x)))


def _log_sigmoid(x):
    return -_softplus(-x)


def _nt(a, b):
    return lax.dot_general(a, b, (((1,), (1,)), ((), ())), preferred_element_type=F32)


def _tn(a, b):
    return lax.dot_general(a, b, (((0,), (0,)), ((), ())), preferred_element_type=F32)


def _mm(a, b):
    return jnp.dot(a, b, preferred_element_type=F32)


def _bf(x):
    return x.astype(BF16)


def _mod_row(t):
    return jnp.minimum(t // TILES_PER_SEQ, BATCH)


ADA_NBLK = 1536


def _ada_kernel(c_ref, w_ref, b_ref, o_ref):
    s = _silu(c_ref[...])
    o_ref[0] = jnp.dot(s, w_ref[0], precision=HIGHEST, preferred_element_type=F32) + b_ref[0]


def _ada_all(cc, ada_w, ada_b):
    n = ADA_CHUNKS * D_MODEL
    return pl.pallas_call(
        _ada_kernel,
        grid=(DEPTH, n // ADA_NBLK),
        in_specs=[
            pl.BlockSpec((MOD_ROWS, D_MODEL), lambda i, j: (0, 0)),
            pl.BlockSpec((1, D_MODEL, ADA_NBLK), lambda i, j: (i, 0, j)),
            pl.BlockSpec((1, 1, ADA_NBLK), lambda i, j: (i, 0, j)),
        ],
        out_specs=pl.BlockSpec((1, MOD_ROWS, ADA_NBLK), lambda i, j: (i, 0, j)),
        out_shape=jax.ShapeDtypeStruct((DEPTH, MOD_ROWS, n), F32),
        compiler_params=_cparams("parallel", "parallel"),
        name="ada",
    )(cc, ada_w, ada_b.reshape(DEPTH, 1, n))


def _modulated(x_ref, mod_ref, shift, scale):
    return x_ref[...] * (1.0 + mod_ref[0, scale:scale + 1, :]) + mod_ref[0, shift:shift + 1, :]


def _store_heads(ref, val, width):
    for h in range(val.shape[1] // width):
        ref[h] = val[:, h * width:(h + 1) * width]


def _hg_proj_kernel(x_ref, mod_ref, w_ref, lb_ref, q_ref, v_ref, gate_ref, kf_ref, gf_ref, kb_ref, gb_ref):
    h = _bf(_modulated(x_ref, mod_ref, 0, 1))
    d = D_MODEL
    _store_heads(q_ref, _silu(_mm(h, w_ref[:, 0:d])), LANES)
    _store_heads(v_ref, _mm(h, w_ref[:, d:2 * d]), LANES)
    gate_ref[...] = _mm(h, w_ref[:, 2 * d:3 * d])
    lb = lb_ref[...]
    for k_ref, g_ref, lo in ((kf_ref, gf_ref, 3 * d), (kb_ref, gb_ref, 4 * d)):
        f = _mm(h, w_ref[:, lo:lo + d])
        _store_heads(k_ref, (1.0 - lb) * _sigmoid(-f), LANES)
        _store_heads(g_ref, jnp.log(lb + (1.0 - lb) * _sigmoid(f)), LANES)


def _head_spec(n_heads, width):
    return pl.BlockSpec((n_heads, TILE, width), lambda t: (0, t, 0))


def _head_shape(n_heads, width):
    return jax.ShapeDtypeStruct((n_heads, N_TOK, width), F32)


def _tok_spec(width):
    return pl.BlockSpec((TILE, width), lambda t: (t, 0))


def _mod_spec():
    return pl.BlockSpec((1, ADA_CHUNKS, D_MODEL), lambda t: (_mod_row(t), 0, 0))


def _full_spec(shape):
    return pl.BlockSpec(shape, lambda t: (0,) * len(shape))


def _hg_proj(xs, mod, w_in, lb):
    n = w_in.shape[1]
    hs = _head_shape(8, LANES)
    return pl.pallas_call(
        _hg_proj_kernel,
        grid=(ALL_TILES,),
        in_specs=[_tok_spec(D_MODEL), _mod_spec(), _full_spec((D_MODEL, n)), _full_spec((1, D_MODEL))],
        out_specs=[_head_spec(8, LANES), _head_spec(8, LANES), _tok_spec(D_MODEL),
                   _head_spec(8, LANES), _head_spec(8, LANES), _head_spec(8, LANES), _head_spec(8, LANES)],
        out_shape=[hs, hs, jax.ShapeDtypeStruct((N_TOK, D_MODEL), F32), hs, hs, hs, hs],
        compiler_params=_cparams("parallel"),
        name="hg_proj",
    )(xs, mod, w_in, lb)


GLA_HEADS = 4
GLA_DK = 128
GLA_DV = 256
GLA_RANK = 16
GLA_GATE_NORM = 16.0


def _gla_proj_kernel(x_ref, mod_ref, w_ref, wlr_ref, gkw_ref, gkb_ref, q_ref, k_ref, v_ref, gate_ref, gf_ref, gb_ref):
    h = _bf(_modulated(x_ref, mod_ref, 0, 1))
    kd = GLA_HEADS * GLA_DK
    vd = GLA_HEADS * GLA_DV
    _store_heads(q_ref, _mm(h, w_ref[:, 0:kd]) * GLA_DK ** -0.5, GLA_DK)
    _store_heads(k_ref, _mm(h, w_ref[:, kd:2 * kd]), GLA_DK)
    _store_heads(v_ref, _mm(h, w_ref[:, 2 * kd:2 * kd + vd]), GLA_DV)
    gate_ref[...] = _mm(h, w_ref[:, 2 * kd + vd:2 * kd + 2 * vd])
    for z, g_ref in ((0, gf_ref), (1, gb_ref)):
        lr = _mm(h, wlr_ref[:, z * LANES:(z + 1) * LANES])
        gk = _mm(_bf(lr), gkw_ref[z]) + gkb_ref[z]
        _store_heads(g_ref, _log_sigmoid(gk) / GLA_GATE_NORM, GLA_DK)


def _gla_proj(xs, mod, w_main, w_lr, gk_w, gk_b):
    kd = GLA_HEADS * GLA_DK
    hk = _head_shape(GLA_HEADS, GLA_DK)
    return pl.pallas_call(
        _gla_proj_kernel,
        grid=(ALL_TILES,),
        in_specs=[_tok_spec(D_MODEL), _mod_spec(), _full_spec(w_main.shape), _full_spec(w_lr.shape),
                  _full_spec(gk_w.shape), _full_spec(gk_b.shape)],
        out_specs=[_head_spec(GLA_HEADS, GLA_DK), _head_spec(GLA_HEADS, GLA_DK), _head_spec(GLA_HEADS, GLA_DV),
                   _tok_spec(D_MODEL), _head_spec(GLA_HEADS, GLA_DK), _head_spec(GLA_HEADS, GLA_DK)],
        out_shape=[hk, hk, _head_shape(GLA_HEADS, GLA_DV), jax.ShapeDtypeStruct((N_TOK, D_MODEL), F32), hk, hk],
        compiler_params=_cparams("parallel"),
        name="gla_proj",
    )(xs, mod, w_main, w_lr, gk_w, gk_b)


DN_HEADS = 8
DN_QKV = 3 * D_MODEL
SHORT_CONV = 5
CHUNKS_PER_TILE = TILE // CHUNK


def _seg_scan(x, pos, axis, reverse):
    n = x.shape[axis]
    sh = 1
    while sh < CHUNK:
        if reverse:
            x = x + jnp.where(pos < CHUNK - sh, pltpu.roll(x, n - sh, axis), 0.0)
        else:
            x = x + jnp.where(pos >= sh, pltpu.roll(x, sh, axis), 0.0)
        sh *= 2
    return x


def _dn_gates(p, a_neg, dtb, idx, pos, axis):
    beta = _sigmoid(p)
    g = a_neg * _softplus(p + dtb)
    gp = _seg_scan(g, pos, axis, False)
    gs = _seg_scan(g, pos, axis, True)
    return jnp.where(idx < 2 * DN_HEADS, beta, jnp.where(idx < 3 * DN_HEADS, gp, gs))


def _dn_proj_kernel(x_ref, mod_ref, w_ref, wba_ref, wbat_ref, ac_ref, dc_ref, ar_ref, dr_ref,
                    qkv_ref, gate_ref, bgc_ref, bgr_ref):
    h = _bf(_modulated(x_ref, mod_ref, 0, 1))
    qkv_ref[...] = _mm(h, w_ref[:, 0:DN_QKV])
    gate_ref[...] = _mm(h, w_ref[:, DN_QKV:DN_QKV + D_MODEL])
    pc = _mm(h, wba_ref[...])
    lane = lax.broadcasted_iota(jnp.int32, pc.shape, 1)
    rowpos = lax.broadcasted_iota(jnp.int32, pc.shape, 0) % CHUNK
    bgc_ref[...] = _dn_gates(pc, ac_ref[...], dc_ref[...], lane, rowpos, 0)
    pr = _nt(wbat_ref[...], h)
    sub = lax.broadcasted_iota(jnp.int32, pr.shape, 0)
    lanepos = lax.broadcasted_iota(jnp.int32, pr.shape, 1) % CHUNK
    gr = _dn_gates(pr, ar_ref[...], dr_ref[...], sub, lanepos, 1)
    for c in range(CHUNKS_PER_TILE):
        bgr_ref[c] = gr[:, c * CHUNK:(c + 1) * CHUNK]


def _dn_proj(xs, mod, w_main, w_ba, w_bat, a_col, d_col, a_row, d_row):
    nb = 4 * DN_HEADS
    return pl.pallas_call(
        _dn_proj_kernel,
        grid=(ALL_TILES,),
        in_specs=[_tok_spec(D_MODEL), _mod_spec(), _full_spec(w_main.shape), _full_spec(w_ba.shape),
                  _full_spec(w_bat.shape), _full_spec(a_col.shape), _full_spec(d_col.shape),
                  _full_spec(a_row.shape), _full_spec(d_row.shape)],
        out_specs=[_tok_spec(DN_QKV), _tok_spec(D_MODEL), _tok_spec(LANES),
                   pl.BlockSpec((CHUNKS_PER_TILE, nb, CHUNK), lambda t: (t, 0, 0))],
        out_shape=[jax.ShapeDtypeStruct((N_TOK, DN_QKV), F32), jax.ShapeDtypeStruct((N_TOK, D_MODEL), F32),
                   jax.ShapeDtypeStruct((N_TOK, LANES), F32),
                   jax.ShapeDtypeStruct((N_TOK // CHUNK, nb, CHUNK), F32)],
        compiler_params=_cparams("parallel"),
        name="dn_proj",
    )(xs, mod, w_main, w_ba, w_bat, a_col, d_col, a_row, d_row)


HALO = SUBLANES


def _dn_conv_kernel(pm_ref, pp_ref, pn_ref, cw_ref, q_ref, k_ref, v_ref):
    t = pl.program_id(0)
    is_lat = t < LAT_TILES
    first = jnp.logical_or(jnp.logical_not(is_lat), t % TILES_PER_SEQ == 0)
    last = jnp.logical_or(jnp.logical_not(is_lat), t % TILES_PER_SEQ == TILES_PER_SEQ - 1)
    keep_prev = jnp.where(first, 0.0, 1.0)
    keep_next = jnp.where(last, 0.0, 1.0)
    n_ext = TILE + 2 * HALO
    pad = SHORT_CONV // 2
    for s in range(DN_QKV // LANES):
        cols = slice(s * LANES, (s + 1) * LANES)
        ext = jnp.concatenate([pp_ref[:, cols] * keep_prev, pm_ref[:, cols], pn_ref[:, cols] * keep_next], axis=0)
        acc = None
        for kk in range(SHORT_CONV):
            off = HALO - pad + kk
            tap = pltpu.roll(ext, n_ext - off, 0)[0:TILE] * cw_ref[kk:kk + 1, cols]
            acc = tap if acc is None else acc + tap
        a = _silu(acc)
        head = s % DN_HEADS
        if s < 2 * DN_HEADS:
            a = a * lax.rsqrt(jnp.sum(a * a, axis=1, keepdims=True) + 1e-6)
            if s < DN_HEADS:
                q_ref[head] = a * LANES ** -0.5
            else:
                k_ref[head] = a
        else:
            v_ref[head] = a


def _dn_conv(p_qkv, conv_w):
    rows8 = TILE // HALO
    last8 = N_TOK // HALO - 1
    hs = _head_shape(DN_HEADS, LANES)
    return pl.pallas_call(
        _dn_conv_kernel,
        grid=(ALL_TILES,),
        in_specs=[_tok_spec(DN_QKV),
                  pl.BlockSpec((HALO, DN_QKV), lambda t: (jnp.maximum(t * rows8 - 1, 0), 0)),
                  pl.BlockSpec((HALO, DN_QKV), lambda t: (jnp.minimum((t + 1) * rows8, last8), 0)),
                  _full_spec(conv_w.shape)],
        out_specs=[_head_spec(DN_HEADS, LANES)] * 3,
        out_shape=[hs, hs, hs],
        compiler_params=_cparams("parallel"),
        name="dn_conv",
    )(p_qkv, p_qkv, p_qkv, conv_w)


def _chunk_scan(x, row, reverse):
    sh = 1
    while sh < CHUNK:
        if reverse:
            x = x + jnp.where(row < CHUNK - sh, pltpu.roll(x, CHUNK - sh, 0), 0.0)
        else:
            x = x + jnp.where(row >= sh, pltpu.roll(x, sh, 0), 0.0)
        sh *= 2
    return x


def _gla_chunk(q, k, v, g, st, reverse, row, r64, c64):
    cum = _chunk_scan(g, row, reverse)
    g_end = cum[0:1, :] if reverse else cum[CHUNK - 1:CHUNK, :]
    att = jnp.where(r64 == c64, _nt(_bf(q), _bf(k)), 0.0)
    edge = cum
    s = 1
    while s < CHUNK:
        late = (row & s) != 0
        if reverse:
            cut = jnp.where(late, edge, pltpu.roll(edge, CHUNK - s, 0))
        else:
            cut = jnp.where(late, pltpu.roll(edge, s, 0), edge)
        e = jnp.exp(-jnp.abs(cum - cut))
        qe = q * e
        ke = k * e
        q_side = late != reverse
        a = _nt(_bf(jnp.where(q_side, qe, 0.0)), _bf(jnp.where(q_side, 0.0, ke)))
        if 2 * s < CHUNK:
            shift = (2 * s).bit_length() - 1
            a = jnp.where((r64 >> shift) == (c64 >> shift), a, 0.0)
        att = att + a
        if reverse:
            edge = jnp.where(late, pltpu.roll(edge, s, 0), edge)
        else:
            edge = jnp.where(late, edge, pltpu.roll(edge, CHUNK - s, 0))
        s *= 2
    st_b = _bf(st)
    o = _mm(_bf(att), _bf(v)) + _nt(_bf(q * jnp.exp(cum)), st_b)
    st_new = st * jnp.exp(g_end) + _tn(_bf(v), _bf(k * jnp.exp(g_end - cum)))
    return o, st_new


def _scan_iotas():
    row = lax.broadcasted_iota(jnp.int32, (CHUNK, LANES), 0)
    r64 = lax.broadcasted_iota(jnp.int32, (CHUNK, CHUNK), 0)
    c64 = lax.broadcasted_iota(jnp.int32, (CHUNK, CHUNK), 1)
    return row, r64, c64


def _gla_scan_kernel(qf_ref, kf_ref, vf_ref, gf_ref, qb_ref, kb_ref, vb_ref, gb_ref,
                     of_ref, ob_ref, st_ref, *, n_heads):
    @pl.when(pl.program_id(1) == 0)
    def _():
        st_ref[...] = jnp.zeros_like(st_ref)

    row, r64, c64 = _scan_iotas()

    def body(h, carry):
        o, st = _gla_chunk(qf_ref[h], kf_ref[h], vf_ref[h], gf_ref[h], st_ref[0, h], False, row, r64, c64)
        of_ref[h] = o
        st_ref[0, h] = st
        o, st = _gla_chunk(qb_ref[h], kb_ref[h], vb_ref[h], gb_ref[h], st_ref[1, h], True, row, r64, c64)
        ob_ref[h] = o
        st_ref[1, h] = st
        return carry

    lax.fori_loop(0, n_heads, body, 0)


CTX_CHUNKS = CTX_LEN // CHUNK
LAT_CHUNKS = SEQ // CHUNK
SCAN_STEPS = CTX_CHUNKS + LAT_CHUNKS


def _fwd_chunk(b, n):
    return jnp.where(n < CTX_CHUNKS, N_LAT // CHUNK + b * CTX_CHUNKS + n, b * LAT_CHUNKS + n - CTX_CHUNKS)


def _bwd_chunk(b, n):
    return jnp.where(n < CTX_CHUNKS, N_LAT // CHUNK + b * CTX_CHUNKS + CTX_CHUNKS - 1 - n,
                     b * LAT_CHUNKS + SCAN_STEPS - 1 - n)


def _gla_scan(q, k_f, k_b, v, g_f, g_b):
    n_heads, _, dk = q.shape
    dv = v.shape[2]
    fwd = lambda b, n: (0, _fwd_chunk(b, n), 0)
    bwd = lambda b, n: (0, _bwd_chunk(b, n), 0)
    kspec = lambda m: pl.BlockSpec((n_heads, CHUNK, dk), m)
    vspec = lambda m: pl.BlockSpec((n_heads, CHUNK, dv), m)
    oshape = jax.ShapeDtypeStruct((n_heads, N_TOK, dv), F32)
    return pl.pallas_call(
        functools.partial(_gla_scan_kernel, n_heads=n_heads),
        grid=(BATCH, SCAN_STEPS),
        in_specs=[kspec(fwd), kspec(fwd), vspec(fwd), kspec(fwd), kspec(bwd), kspec(bwd), vspec(bwd), kspec(bwd)],
        out_specs=[vspec(fwd), vspec(bwd)],
        out_shape=[oshape, oshape],
        scratch_shapes=[pltpu.VMEM((2, n_heads, dv, dk), F32)],
        compiler_params=_cparams("parallel", "arbitrary"),
        name="gla_scan",
    )(q, k_f, v, g_f, q, k_b, v, g_b)


def _lane_pick(x, lane, idx):
    return jnp.sum(jnp.where(lane == idx, x, 0.0), axis=1, keepdims=True)


def _dn_chunk(q, k, v, beta, g_col, g_row, st, reverse, r64, c64):
    g_end = g_row[:, 0:1] if reverse else g_row[:, CHUNK - 1:CHUNK]
    causal = (r64 <= c64) if reverse else (r64 >= c64)
    gam = jnp.where(causal, jnp.exp(jnp.minimum(g_col - g_row, 0.0)), 0.0)
    kb = k * beta
    k_b16 = _bf(k)
    m = jnp.where(r64 == c64, 0.0, _nt(_bf(kb), k_b16) * gam)
    att = _nt(_bf(q), k_b16) * gam
    y = None
    s = 1
    while s < CHUNK:
        shift = (2 * s).bit_length() - 1
        rlate = (r64 & s) != 0
        clate = (c64 & s) != 0
        pair = jnp.logical_and(clate, jnp.logical_not(rlate)) if reverse else jnp.logical_and(rlate, jnp.logical_not(clate))
        if 2 * s < CHUNK:
            pair = jnp.logical_and(pair, (r64 >> shift) == (c64 >> shift))
        c = jnp.where(pair, m, 0.0)
        if y is None:
            y = -c
        else:
            c_b = _bf(c)
            y_b = _bf(y)
            p = c + _mm(y_b, c_b)
            y = y - (p + _mm(_bf(p), y_b))
        s *= 2
    e_col = jnp.exp(g_col)
    rhs = jnp.concatenate([kb * e_col, v * beta], axis=1)
    wu = rhs + _mm(_bf(y), _bf(rhs))
    dk = k.shape[1]
    st_b = _bf(st)
    v_new = wu[:, dk:] - _nt(_bf(wu[:, :dk]), st_b)
    o = _nt(_bf(q * e_col), st_b) + _mm(_bf(att), _bf(v_new))
    st_new = st * jnp.exp(g_end) + _tn(_bf(v_new), _bf(k * jnp.exp(g_end - g_col)))
    return o, st_new


def _dn_scan_kernel(qf_ref, kf_ref, vf_ref, cf_ref, rf_ref, qb_ref, kb_ref, vb_ref, cb_ref, rb_ref,
                    of_ref, ob_ref, st_ref):
    @pl.when(pl.program_id(1) == 0)
    def _():
        st_ref[...] = jnp.zeros_like(st_ref)

    _, r64, c64 = _scan_iotas()
    lane = lax.broadcasted_iota(jnp.int32, (CHUNK, LANES), 1)

    def body(h, carry):
        cf = cf_ref[...]
        o, st = _dn_chunk(qf_ref[h], kf_ref[h], vf_ref[h], _lane_pick(cf, lane, h), _lane_pick(cf, lane, 2 * DN_HEADS + h),
                          rf_ref[0, pl.ds(2 * DN_HEADS + h, 1), :], st_ref[0, h], False, r64, c64)
        of_ref[h] = o
        st_ref[0, h] = st
        cb = cb_ref[...]
        o, st = _dn_chunk(qb_ref[h], kb_ref[h], vb_ref[h], _lane_pick(cb, lane, DN_HEADS + h),
                          _lane_pick(cb, lane, 3 * DN_HEADS + h),
                          rb_ref[0, pl.ds(3 * DN_HEADS + h, 1), :], st_ref[1, h], True, r64, c64)
        ob_ref[h] = o
        st_ref[1, h] = st
        return carry

    lax.fori_loop(0, DN_HEADS, body, 0)


def _dn_scan(q, k, v, bgc, bgr):
    fwd = lambda b, n: (0, _fwd_chunk(b, n), 0)
    bwd = lambda b, n: (0, _bwd_chunk(b, n), 0)
    fwd2 = lambda b, n: (_fwd_chunk(b, n), 0)
    bwd2 = lambda b, n: (_bwd_chunk(b, n), 0)
    fwd3 = lambda b, n: (_fwd_chunk(b, n), 0, 0)
    bwd3 = lambda b, n: (_bwd_chunk(b, n), 0, 0)
    hspec = lambda m: pl.BlockSpec((DN_HEADS, CHUNK, LANES), m)
    cspec = lambda m: pl.BlockSpec((CHUNK, LANES), m)
    rspec = lambda m: pl.BlockSpec((1, 4 * DN_HEADS, CHUNK), m)
    oshape = _head_shape(DN_HEADS, LANES)
    return pl.pallas_call(
        _dn_scan_kernel,
        grid=(BATCH, SCAN_STEPS),
        in_specs=[hspec(fwd), hspec(fwd), hspec(fwd), cspec(fwd2), rspec(fwd3),
                  hspec(bwd), hspec(bwd), hspec(bwd), cspec(bwd2), rspec(bwd3)],
        out_specs=[hspec(fwd), hspec(bwd)],
        out_shape=[oshape, oshape],
        scratch_shapes=[pltpu.VMEM((2, DN_HEADS, LANES, LANES), F32)],
        compiler_params=_cparams("parallel", "arbitrary"),
        name="dn_scan",
    )(q, k, v, bgc, bgr, q, k, v, bgc, bgr)


def _layer_norm(z, g, b):
    mu = jnp.mean(z, axis=1, keepdims=True)
    zc = z - mu
    var = jnp.mean(zc * zc, axis=1, keepdims=True)
    return zc * lax.rsqrt(var + LN_EPS) * g + b


def _post_kernel(of_ref, ob_ref, gate_ref, x_ref, mod_ref, nw_ref, wo_ref, lg_ref, lb_ref, rw_ref, rb_ref,
                 x1_ref, h2_ref, logit_ref, *, n_heads):
    parts = []
    for h in range(n_heads):
        o = of_ref[h] + ob_ref[h]
        o = o * lax.rsqrt(jnp.mean(o * o, axis=1, keepdims=True) + RMS_EPS) * nw_ref[...]
        parts.append(o)
    on = jnp.concatenate(parts, axis=1) * _silu(gate_ref[...])
    y = _mm(_bf(on), wo_ref[...])
    z = ALPHA * x_ref[...] + mod_ref[0, 2:3, :] * y
    x1 = _layer_norm(z, lg_ref[...], lb_ref[...])
    x1_ref[...] = x1
    h2 = x1 * (1.0 + mod_ref[0, 4:5, :]) + mod_ref[0, 3:4, :]
    h2_ref[...] = h2
    logit_ref[...] = jnp.dot(h2, rw_ref[...], precision=HIGHEST, preferred_element_type=F32) + rb_ref[...]


def _post(o_f, o_b, gate, xs, mod, norm_w, w_out, ln_g, ln_b, rw, rb, n_tiles):
    n_heads, _, dv = o_f.shape
    n_rows = n_tiles * TILE
    tok = lambda w: jax.ShapeDtypeStruct((n_rows, w), F32)
    return pl.pallas_call(
        functools.partial(_post_kernel, n_heads=n_heads),
        grid=(n_tiles,),
        in_specs=[_head_spec(n_heads, dv), _head_spec(n_heads, dv), _tok_spec(D_MODEL), _tok_spec(D_MODEL), _mod_spec(),
                  _full_spec(norm_w.shape), _full_spec(w_out.shape), _full_spec(ln_g.shape), _full_spec(ln_b.shape),
                  _full_spec(rw.shape), _full_spec(rb.shape)],
        out_specs=[_tok_spec(D_MODEL), _tok_spec(D_MODEL), _tok_spec(LANES)],
        out_shape=[tok(D_MODEL), tok(D_MODEL), tok(LANES)],
        compiler_params=_cparams("parallel"),
        name="post_mixer",
    )(o_f, o_b, gate, xs, mod, norm_w, w_out, ln_g, ln_b, rw, rb)


def _route_kernel(logit_ref, route_ref, count_ref, run_ref):
    t = pl.program_id(0)

    @pl.when(t == 0)
    def _():
        run_ref[...] = jnp.zeros_like(run_ref)

    lane = lax.broadcasted_iota(jnp.int32, (TILE, LANES), 1)
    lanef = lane.astype(F32)
    neg = jnp.float32(-jnp.inf)
    l = jnp.where(lane < N_EXPERTS, logit_ref[...], neg)
    vals, picks, hots = [], [], []
    for _ in range(TOP_K):
        m = jnp.max(l, axis=1, keepdims=True)
        pick = jnp.min(jnp.where(l == m, lanef, float(LANES)), axis=1, keepdims=True)
        hot = lanef == pick
        l = jnp.where(hot, neg, l)
        vals.append(m)
        picks.append(pick)
        hots.append(hot)
    es = [jnp.exp(v - vals[0]) for v in vals]
    denom = es[0] + es[1] + es[2] + es[3]
    onehot = jnp.zeros((TILE, LANES), F32)
    for hot in hots:
        onehot = onehot + jnp.where(hot, 1.0, 0.0)
    r = lax.broadcasted_iota(jnp.int32, (TILE, TILE), 0)
    c = lax.broadcasted_iota(jnp.int32, (TILE, TILE), 1)
    before = jnp.where(c < r, 1.0, 0.0).astype(BF16)
    run = run_ref[0:1, :]
    excl = _mm(before, _bf(onehot)) + run
    out = jnp.zeros((TILE, LANES), F32)
    for kk in range(TOP_K):
        rank = jnp.sum(jnp.where(hots[kk], excl, 0.0), axis=1, keepdims=True)
        out = jnp.where(lane == kk, picks[kk], out)
        out = jnp.where(lane == TOP_K + kk, rank, out)
        out = jnp.where(lane == 2 * TOP_K + kk, es[kk] / denom, out)
    route_ref[...] = out
    new_run = run + jnp.sum(onehot, axis=0, keepdims=True)
    run_ref[...] = jnp.broadcast_to(new_run, run_ref.shape)
    count_ref[...] = jnp.broadcast_to(new_run, count_ref.shape)


def _route(logits):
    n_tiles = logits.shape[0] // TILE
    return pl.pallas_call(
        _route_kernel,
        grid=(n_tiles,),
        in_specs=[_tok_spec(LANES)],
        out_specs=[_tok_spec(LANES), pl.BlockSpec((SUBLANES, LANES), lambda t: (0, 0))],
        out_shape=[jax.ShapeDtypeStruct(logits.shape, F32), jax.ShapeDtypeStruct((SUBLANES, LANES), F32)],
        scratch_shapes=[pltpu.VMEM((SUBLANES, LANES), F32)],
        compiler_params=_cparams("arbitrary"),
        name="route",
    )(logits)


def _dispatch_kernel(slot_ref, h_ref, xs_ref, sem):
    def copy(r, kk):
        return pltpu.make_async_copy(h_ref.at[pl.ds(r, 1), :], xs_ref.at[pl.ds(slot_ref[0, kk, r], 1), :], sem)

    def start(r, carry):
        for kk in range(TOP_K):
            copy(r, kk).start()
        return carry

    def wait(r, carry):
        for kk in range(TOP_K):
            copy(r, kk).wait()
        return carry

    lax.fori_loop(0, TILE, start, 0)
    lax.fori_loop(0, TILE, wait, 0)


def _dispatch(slots, h2):
    n_tiles = h2.shape[0] // TILE
    return pl.pallas_call(
        _dispatch_kernel,
        grid=(n_tiles,),
        in_specs=[pl.BlockSpec((1, TOP_K, TILE), lambda t: (t, 0, 0), memory_space=pltpu.SMEM),
                  _tok_spec(D_MODEL)],
        out_specs=pl.BlockSpec(memory_space=pl.ANY),
        out_shape=jax.ShapeDtypeStruct((N_SLOTS, D_MODEL), F32),
        scratch_shapes=[pltpu.SemaphoreType.DMA(())],
        compiler_params=_cparams("arbitrary"),
        name="dispatch",
    )(slots, h2)


def _expert_kernel(te_ref, tv_ref, nt_ref, xs_ref, w1g_ref, w1l_ref, b1g_ref, b1l_ref, w2_ref, b2_ref, ys_ref):
    i = pl.program_id(0)

    @pl.when(i < nt_ref[0])
    def _():
        rowi = lax.broadcasted_iota(jnp.int32, (EXPERT_TILE, D_MODEL), 0)
        x = _bf(jnp.where(rowi < tv_ref[i], xs_ref[...], 0.0))
        glu = jnp.minimum(_mm(x, w1g_ref[0]) + b1g_ref[0], SWIGLU_LIMIT)
        lin = jnp.clip(_mm(x, w1l_ref[0]) + b1l_ref[0], -SWIGLU_LIMIT, SWIGLU_LIMIT)
        act = glu * _sigmoid(SWIGLU_ALPHA * glu) * (lin + 1.0)
        ys_ref[...] = _mm(_bf(act), w2_ref[0]) + b2_ref[0]


def _experts(tile_expert, tile_valid, n_tiles, xs, w1g, w1l, b1g, b1l, w2, b2):
    tile_map = lambda i, te, tv, nt: (jnp.minimum(i, nt[0] - 1), 0)
    wmap = lambda i, te, tv, nt: (te[i], 0, 0)
    grid_spec = pltpu.PrefetchScalarGridSpec(
        num_scalar_prefetch=3,
        grid=(N_SLOT_TILES,),
        in_specs=[pl.BlockSpec((EXPERT_TILE, D_MODEL), tile_map),
                  pl.BlockSpec((1, D_MODEL, D_FF), wmap), pl.BlockSpec((1, D_MODEL, D_FF), wmap),
                  pl.BlockSpec((1, 1, D_FF), wmap), pl.BlockSpec((1, 1, D_FF), wmap),
                  pl.BlockSpec((1, D_FF, D_MODEL), wmap), pl.BlockSpec((1, 1, D_MODEL), wmap)],
        out_specs=pl.BlockSpec((EXPERT_TILE, D_MODEL), tile_map),
    )
    return pl.pallas_call(
        _expert_kernel,
        grid_spec=grid_spec,
        out_shape=jax.ShapeDtypeStruct((N_SLOTS, D_MODEL), F32),
        compiler_params=_cparams("arbitrary"),
        name="experts",
    )(tile_expert, tile_valid, n_tiles, xs, w1g, w1l, b1g, b1l, w2, b2)


def _combine_kernel(slot_ref, ys_ref, route_ref, x1_ref, mod_ref, lg_ref, lb_ref, x2_ref, buf_ref, sem):
    def copy(r, kk):
        return pltpu.make_async_copy(ys_ref.at[pl.ds(slot_ref[0, kk, r], 1), :], buf_ref.at[kk, pl.ds(r, 1), :], sem)

    def start(r, carry):
        for kk in range(TOP_K):
            copy(r, kk).start()
        return carry

    def wait(r, carry):
        for kk in range(TOP_K):
            copy(r, kk).wait()
        return carry

    lax.fori_loop(0, TILE, start, 0)
    lax.fori_loop(0, TILE, wait, 0)
    route = route_ref[...]
    lane = lax.broadcasted_iota(jnp.int32, route.shape, 1)
    f = None
    for kk in range(TOP_K):
        term = _lane_pick(route, lane, 2 * TOP_K + kk) * buf_ref[kk]
        f = term if f is None else f + term
    z = ALPHA * x1_ref[...] + mod_ref[0, 5:6, :] * f
    x2_ref[...] = _layer_norm(z, lg_ref[...], lb_ref[...])


def _combine(slots, ys, route, x1, mod, ln_g, ln_b):
    n_tiles = x1.shape[0] // TILE
    return pl.pallas_call(
        _combine_kernel,
        grid=(n_tiles,),
        in_specs=[pl.BlockSpec((1, TOP_K, TILE), lambda t: (t, 0, 0), memory_space=pltpu.SMEM),
                  pl.BlockSpec(memory_space=pl.ANY), _tok_spec(LANES), _tok_spec(D_MODEL), _mod_spec(),
                  _full_spec(ln_g.shape), _full_spec(ln_b.shape)],
        out_specs=_tok_spec(D_MODEL),
        out_shape=jax.ShapeDtypeStruct(x1.shape, F32),
        scratch_shapes=[pltpu.VMEM((TOP_K, TILE, D_MODEL), F32), pltpu.SemaphoreType.DMA(())],
        compiler_params=_cparams("arbitrary"),
        name="combine",
    )(slots, ys, route, x1, mod, ln_g, ln_b)


def _moe(h2, logits, x1, mod, ln_g, ln_b, w1g, w1l, b1g, b1l, w2, b2):
    n_tok = h2.shape[0]
    route, counts = _route(logits)
    cnt = counts[0, :N_EXPERTS].astype(jnp.int32)
    padded = (cnt + EXPERT_TILE - 1) // EXPERT_TILE * EXPERT_TILE
    ends = jnp.cumsum(padded)
    starts = ends - padded
    pick = route[:, 0:TOP_K].astype(jnp.int32)
    rank = route[:, TOP_K:2 * TOP_K].astype(jnp.int32)
    slot = starts[pick] + rank
    slots = slot.reshape(n_tok // TILE, TILE, TOP_K).transpose(0, 2, 1)
    tile_start = jnp.arange(N_SLOT_TILES, dtype=jnp.int32) * EXPERT_TILE
    tile_expert = jnp.minimum(jnp.sum(tile_start[:, None] >= ends[None, :], axis=1), N_EXPERTS - 1).astype(jnp.int32)
    tile_valid = jnp.clip(starts[tile_expert] + cnt[tile_expert] - tile_start, 0, EXPERT_TILE).astype(jnp.int32)
    n_tiles = (ends[N_EXPERTS - 1:] // EXPERT_TILE).astype(jnp.int32)
    xs = _dispatch(slots, h2)
    ys = _experts(tile_expert, tile_valid, n_tiles, xs, w1g, w1l, b1g, b1l, w2, b2)
    return _combine(slots, ys, route, x1, mod, ln_g, ln_b)


def _to_col_major(xs):
    lat = xs[:N_LAT].reshape(BATCH, SEQ // GRID_W, GRID_W, D_MODEL).transpose(0, 2, 1, 3).reshape(N_LAT, D_MODEL)
    return jnp.concatenate([lat, xs[N_LAT:]], axis=0)


def _lat_to_row_major(lat):
    return lat.reshape(BATCH, GRID_W, SEQ // GRID_W, D_MODEL).transpose(0, 2, 1, 3).reshape(N_LAT, D_MODEL)


def kernel(x, c, ctx, c_ctx, ada_w, ada_b, ln_g, ln_b, dn_w_in, dn_conv_w, dn_a_log, dn_dt_bias, dn_norm_w, dn_w_out,
           hg_w_in, hg_lower_bound, hg_norm_w, hg_w_out, gla_w_in, gla_gk_w, gla_gk_b, gla_norm_w, gla_w_out,
           router_w, router_b, exp_w1, exp_b1, exp_w2, exp_b2):
    xs = jnp.concatenate([x.reshape(N_LAT, D_MODEL), ctx.reshape(N_CTX, D_MODEL)], axis=0)
    cc = jnp.zeros((MOD_ROWS, D_MODEL), F32).at[:BATCH].set(c).at[BATCH].set(c_ctx)
    mods = _ada_all(cc, ada_w, ada_b).reshape(DEPTH, MOD_ROWS, ADA_CHUNKS, D_MODEL)
    lb_soft = jax.nn.softmax(hg_lower_bound.astype(F32), axis=0)
    lower_bounds = jnp.cumsum(lb_soft, axis=0) - lb_soft[0]

    for i in range(DEPTH):
        last = i == DEPTH - 1
        kind, j = i % 3, i // 3
        mod = mods[i]
        col_major = i % 2 == 1
        if col_major:
            xs = _to_col_major(xs)

        if kind == 0:
            nq = 2 * D_MODEL + 2 * D_MODEL
            w = dn_w_in[j]
            w_ba = jnp.zeros((D_MODEL, LANES), F32).at[:, :4 * DN_HEADS].set(w[:, nq:])
            a_neg = jnp.zeros((LANES,), F32).at[2 * DN_HEADS:4 * DN_HEADS].set(-jnp.exp(dn_a_log[j].astype(F32)).reshape(-1))
            dtb = jnp.zeros((LANES,), F32).at[2 * DN_HEADS:4 * DN_HEADS].set(dn_dt_bias[j].astype(F32).reshape(-1))
            nb = 4 * DN_HEADS
            p_qkv, gate, bgc, bgr = _dn_proj(xs, mod, _bf(w[:, :nq]), _bf(w_ba), _bf(w[:, nq:].T),
                                            a_neg.reshape(1, LANES), dtb.reshape(1, LANES),
                                            a_neg[:nb].reshape(nb, 1), dtb[:nb].reshape(nb, 1))
            cw = jnp.zeros((SUBLANES, DN_QKV), F32).at[:SHORT_CONV].set(dn_conv_w[j])
            q, k, v = _dn_conv(p_qkv, cw)
            o_f, o_b = _dn_scan(q, k, v, bgc, bgr)
            norm_w, w_out = dn_norm_w[j], dn_w_out[j]
        elif kind == 1:
            q, v, gate, k_f, g_f, k_b, g_b = _hg_proj(xs, mod, _bf(hg_w_in[j]), lower_bounds[i].reshape(1, D_MODEL))
            o_f, o_b = _gla_scan(q, k_f, k_b, v, g_f, g_b)
            norm_w, w_out = hg_norm_w[j], hg_w_out[j]
        else:
            kd = GLA_HEADS * GLA_DK
            n_main = 2 * kd + 2 * D_MODEL
            w = gla_w_in[j]
            w_lr = jnp.zeros((D_MODEL, 2 * LANES), F32)
            w_lr = w_lr.at[:, :GLA_RANK].set(w[:, n_main:n_main + GLA_RANK])
            w_lr = w_lr.at[:, LANES:LANES + GLA_RANK].set(w[:, n_main + GLA_RANK:])
            gk_w = jnp.zeros((2, LANES, kd), F32).at[:, :GLA_RANK].set(gla_gk_w[j])
            q, k, v, gate, g_f, g_b = _gla_proj(xs, mod, _bf(w[:, :n_main]), _bf(w_lr), _bf(gk_w),
                                                gla_gk_b[j].reshape(2, 1, kd))
            o_f, o_b = _gla_scan(q, k, k, v, g_f, g_b)
            norm_w, w_out = gla_norm_w[j], gla_w_out[j]

        n_tiles = LAT_TILES if last else ALL_TILES
        rw = jnp.zeros((D_MODEL, LANES), F32).at[:, :N_EXPERTS].set(router_w[i])
        rb = jnp.zeros((1, LANES), F32).at[0, :N_EXPERTS].set(router_b[i])
        x1, h2, logits = _post(o_f, o_b, gate, xs, mod, norm_w.reshape(1, -1), _bf(w_out),
                               ln_g[i, 0].reshape(1, D_MODEL), ln_b[i, 0].reshape(1, D_MODEL), rw, rb, n_tiles)
        w1 = exp_w1[i]
        b1 = exp_b1[i]
        xs = _moe(h2, logits, x1, mod, ln_g[i, 1].reshape(1, D_MODEL), ln_b[i, 1].reshape(1, D_MODEL),
                  _bf(w1[:, :, 0::2]), _bf(w1[:, :, 1::2]), b1[:, None, 0::2], b1[:, None, 1::2],
                  _bf(exp_w2[i]), exp_b2[i][:, None, :])
        if col_major:
            lat = _lat_to_row_major(xs[:N_LAT])
            xs = lat if last else jnp.concatenate([lat, xs[N_LAT:]], axis=0)
    return xs[:N_LAT].reshape(BATCH, SEQ, D_MODEL)
```

```python
import functools

import jax
import jax.numpy as jnp
from jax import lax
from jax.experimental import pallas as pl
from jax.experimental.pallas import tpu as pltpu

F32 = jnp.float32
BF16 = jnp.bfloat16
HIGHEST = lax.Precision.HIGHEST

D_MODEL = 1024
BATCH = 8
SEQ = 2048
CTX_LEN = 256
DEPTH = 4
GRID_W = 64
CHUNK = 64
ADA_CHUNKS = 6
N_LAT = BATCH * SEQ
N_CTX = BATCH * CTX_LEN
N_TOK = N_LAT + N_CTX
TILE = 256
LAT_TILES = N_LAT // TILE
ALL_TILES = N_TOK // TILE
TILES_PER_SEQ = SEQ // TILE
MOD_ROWS = 16
LANES = 128
SUBLANES = 8

N_EXPERTS = 32
TOP_K = 4
D_FF = D_MODEL
SWIGLU_LIMIT = 7.0
SWIGLU_ALPHA = 1.702
EXPERT_TILE = 256
N_SLOT_TILES = N_TOK * TOP_K // EXPERT_TILE + N_EXPERTS
N_SLOTS = N_SLOT_TILES * EXPERT_TILE

LN_EPS = 1e-5
RMS_EPS = 1e-6
ALPHA = (2.0 * DEPTH) ** 0.25

VMEM_LIMIT = 48 * 1024 * 1024


def _cparams(*sem):
    return pltpu.CompilerParams(dimension_semantics=sem, vmem_limit_bytes=VMEM_LIMIT)


def _sigmoid(x):
    return 1.0 / (1.0 + jnp.exp(-x))


def _silu(x):
    return x * _sigmoid(x)


def _softplus(x):
    return jnp.maximum(x, 0.0) + jnp.log(1.0 + jnp.exp(-jnp.abs(x)))


def _log_sigmoid(x):
    return -_softplus(-x)


def _nt(a, b):
    return lax.dot_general(a, b, (((1,), (1,)), ((), ())), preferred_element_type=F32)


def _tn(a, b):
    return lax.dot_general(a, b, (((0,), (0,)), ((), ())), preferred_element_type=F32)


def _mm(a, b):
    return jnp.dot(a, b, preferred_element_type=F32)


def _bf(x):
    return x.astype(BF16)


def _mod_row(t):
    return jnp.minimum(t // TILES_PER_SEQ, BATCH)


ADA_NBLK = 1536


def _ada_kernel(c_ref, w_ref, b_ref, o_ref):
    s = _silu(c_ref[...])
    o_ref[0] = jnp.dot(s, w_ref[0], precision=HIGHEST, preferred_element_type=F32) + b_ref[0]


def _ada_all(cc, ada_w, ada_b):
    n = ADA_CHUNKS * D_MODEL
    return pl.pallas_call(
        _ada_kernel,
        grid=(DEPTH, n // ADA_NBLK),
        in_specs=[
            pl.BlockSpec((MOD_ROWS, D_MODEL), lambda i, j: (0, 0)),
            pl.BlockSpec((1, D_MODEL, ADA_NBLK), lambda i, j: (i, 0, j)),
            pl.BlockSpec((1, 1, ADA_NBLK), lambda i, j: (i, 0, j)),
        ],
        out_specs=pl.BlockSpec((1, MOD_ROWS, ADA_NBLK), lambda i, j: (i, 0, j)),
        out_shape=jax.ShapeDtypeStruct((DEPTH, MOD_ROWS, n), F32),
        compiler_params=_cparams("parallel", "parallel"),
        name="ada",
    )(cc, ada_w, ada_b.reshape(DEPTH, 1, n))


def _modulated(x_ref, mod_ref, shift, scale):
    return x_ref[...] * (1.0 + mod_ref[0, scale:scale + 1, :]) + mod_ref[0, shift:shift + 1, :]


def _store_heads(ref, val, width):
    for h in range(val.shape[1] // width):
        ref[h] = val[:, h * width:(h + 1) * width]


def _hg_proj_kernel(x_ref, mod_ref, w_ref, lb_ref, q_ref, v_ref, gate_ref, kf_ref, gf_ref, kb_ref, gb_ref):
    h = _bf(_modulated(x_ref, mod_ref, 0, 1))
    d = D_MODEL
    _store_heads(q_ref, _silu(_mm(h, w_ref[:, 0:d])), LANES)
    _store_heads(v_ref, _mm(h, w_ref[:, d:2 * d]), LANES)
    gate_ref[...] = _mm(h, w_ref[:, 2 * d:3 * d])
    lb = lb_ref[...]
    for k_ref, g_ref, lo in ((kf_ref, gf_ref, 3 * d), (kb_ref, gb_ref, 4 * d)):
        f = _mm(h, w_ref[:, lo:lo + d])
        _store_heads(k_ref, (1.0 - lb) * _sigmoid(-f), LANES)
        _store_heads(g_ref, jnp.log(lb + (1.0 - lb) * _sigmoid(f)), LANES)


def _head_spec(n_heads, width):
    return pl.BlockSpec((n_heads, TILE, width), lambda t: (0, t, 0))


def _head_shape(n_heads, width):
    return jax.ShapeDtypeStruct((n_heads, N_TOK, width), F32)


def _tok_spec(width):
    return pl.BlockSpec((TILE, width), lambda t: (t, 0))


def _mod_spec():
    return pl.BlockSpec((1, ADA_CHUNKS, D_MODEL), lambda t: (_mod_row(t), 0, 0))


def _full_spec(shape):
    return pl.BlockSpec(shape, lambda t: (0,) * len(shape))


def _hg_proj(xs, mod, w_in, lb):
    n = w_in.shape[1]
    hs = _head_shape(8, LANES)
    return pl.pallas_call(
        _hg_proj_kernel,
        grid=(ALL_TILES,),
        in_specs=[_tok_spec(D_MODEL), _mod_spec(), _full_spec((D_MODEL, n)), _full_spec((1, D_MODEL))],
        out_specs=[_head_spec(8, LANES), _head_spec(8, LANES), _tok_spec(D_MODEL),
                   _head_spec(8, LANES), _head_spec(8, LANES), _head_spec(8, LANES), _head_spec(8, LANES)],
        out_shape=[hs, hs, jax.ShapeDtypeStruct((N_TOK, D_MODEL), F32), hs, hs, hs, hs],
        compiler_params=_cparams("parallel"),
        name="hg_proj",
    )(xs, mod, w_in, lb)


GLA_HEADS = 4
GLA_DK = 128
GLA_DV = 256
GLA_RANK = 16
GLA_GATE_NORM = 16.0


def _gla_proj_kernel(x_ref, mod_ref, w_ref, wlr_ref, gkw_ref, gkb_ref, q_ref, k_ref, v_ref, gate_ref, gf_ref, gb_ref):
    h = _bf(_modulated(x_ref, mod_ref, 0, 1))
    kd = GLA_HEADS * GLA_DK
    vd = GLA_HEADS * GLA_DV
    _store_heads(q_ref, _mm(h, w_ref[:, 0:kd]) * GLA_DK ** -0.5, GLA_DK)
    _store_heads(k_ref, _mm(h, w_ref[:, kd:2 * kd]), GLA_DK)
    _store_heads(v_ref, _mm(h, w_ref[:, 2 * kd:2 * kd + vd]), GLA_DV)
    gate_ref[...] = _mm(h, w_ref[:, 2 * kd + vd:2 * kd + 2 * vd])
    for z, g_ref in ((0, gf_ref), (1, gb_ref)):
        lr = _mm(h, wlr_ref[:, z * LANES:(z + 1) * LANES])
        gk = _mm(_bf(lr), gkw_ref[z]) + gkb_ref[z]
        _store_heads(g_ref, _log_sigmoid(gk) / GLA_GATE_NORM, GLA_DK)


def _gla_proj(xs, mod, w_main, w_lr, gk_w, gk_b):
    kd = GLA_HEADS * GLA_DK
    hk = _head_shape(GLA_HEADS, GLA_DK)
    return pl.pallas_call(
        _gla_proj_kernel,
        grid=(ALL_TILES,),
        in_specs=[_tok_spec(D_MODEL), _mod_spec(), _full_spec(w_main.shape), _full_spec(w_lr.shape),
                  _full_spec(gk_w.shape), _full_spec(gk_b.shape)],
        out_specs=[_head_spec(GLA_HEADS, GLA_DK), _head_spec(GLA_HEADS, GLA_DK), _head_spec(GLA_HEADS, GLA_DV),
                   _tok_spec(D_MODEL), _head_spec(GLA_HEADS, GLA_DK), _head_spec(GLA_HEADS, GLA_DK)],
        out_shape=[hk, hk, _head_shape(GLA_HEADS, GLA_DV), jax.ShapeDtypeStruct((N_TOK, D_MODEL), F32), hk, hk],
        compiler_params=_cparams("parallel"),
        name="gla_proj",
    )(xs, mod, w_main, w_lr, gk_w, gk_b)


DN_HEADS = 8
DN_QKV = 3 * D_MODEL
SHORT_CONV = 5
CHUNKS_PER_TILE = TILE // CHUNK


def _seg_scan(x, pos, axis, reverse):
    n = x.shape[axis]
    sh = 1
    while sh < CHUNK:
        if reverse:
            x = x + jnp.where(pos < CHUNK - sh, pltpu.roll(x, n - sh, axis), 0.0)
        else:
            x = x + jnp.where(pos >= sh, pltpu.roll(x, sh, axis), 0.0)
        sh *= 2
    return x


def _dn_gates(p, a_neg, dtb, idx, pos, axis):
    beta = _sigmoid(p)
    g = a_neg * _softplus(p + dtb)
    gp = _seg_scan(g, pos, axis, False)
    gs = _seg_scan(g, pos, axis, True)
    return jnp.where(idx < 2 * DN_HEADS, beta, jnp.where(idx < 3 * DN_HEADS, gp, gs))


def _dn_proj_kernel(x_ref, mod_ref, w_ref, wba_ref, wbat_ref, ac_ref, dc_ref, ar_ref, dr_ref,
                    qkv_ref, gate_ref, bgc_ref, bgr_ref):
    h = _bf(_modulated(x_ref, mod_ref, 0, 1))
    qkv_ref[...] = _mm(h, w_ref[:, 0:DN_QKV])
    gate_ref[...] = _mm(h, w_ref[:, DN_QKV:DN_QKV + D_MODEL])
    pc = _mm(h, wba_ref[...])
    lane = lax.broadcasted_iota(jnp.int32, pc.shape, 1)
    rowpos = lax.broadcasted_iota(jnp.int32, pc.shape, 0) % CHUNK
    bgc_ref[...] = _dn_gates(pc, ac_ref[...], dc_ref[...], lane, rowpos, 0)
    pr = _nt(wbat_ref[...], h)
    sub = lax.broadcasted_iota(jnp.int32, pr.shape, 0)
    lanepos = lax.broadcasted_iota(jnp.int32, pr.shape, 1) % CHUNK
    gr = _dn_gates(pr, ar_ref[...], dr_ref[...], sub, lanepos, 1)
    for c in range(CHUNKS_PER_TILE):
        bgr_ref[c] = gr[:, c * CHUNK:(c + 1) * CHUNK]


def _dn_proj(xs, mod, w_main, w_ba, w_bat, a_col, d_col, a_row, d_row):
    nb = 4 * DN_HEADS
    return pl.pallas_call(
        _dn_proj_kernel,
        grid=(ALL_TILES,),
        in_specs=[_tok_spec(D_MODEL), _mod_spec(), _full_spec(w_main.shape), _full_spec(w_ba.shape),
                  _full_spec(w_bat.shape), _full_spec(a_col.shape), _full_spec(d_col.shape),
                  _full_spec(a_row.shape), _full_spec(d_row.shape)],
        out_specs=[_tok_spec(DN_QKV), _tok_spec(D_MODEL), _tok_spec(LANES),
                   pl.BlockSpec((CHUNKS_PER_TILE, nb, CHUNK), lambda t: (t, 0, 0))],
        out_shape=[jax.ShapeDtypeStruct((N_TOK, DN_QKV), F32), jax.ShapeDtypeStruct((N_TOK, D_MODEL), F32),
                   jax.ShapeDtypeStruct((N_TOK, LANES), F32),
                   jax.ShapeDtypeStruct((N_TOK // CHUNK, nb, CHUNK), F32)],
        compiler_params=_cparams("parallel"),
        name="dn_proj",
    )(xs, mod, w_main, w_ba, w_bat, a_col, d_col, a_row, d_row)


HALO = SUBLANES


def _dn_conv_kernel(pm_ref, pp_ref, pn_ref, cw_ref, q_ref, k_ref, v_ref):
    t = pl.program_id(0)
    is_lat = t < LAT_TILES
    first = jnp.logical_or(jnp.logical_not(is_lat), t % TILES_PER_SEQ == 0)
    last = jnp.logical_or(jnp.logical_not(is_lat), t % TILES_PER_SEQ == TILES_PER_SEQ - 1)
    keep_prev = jnp.where(first, 0.0, 1.0)
    keep_next = jnp.where(last, 0.0, 1.0)
    n_ext = TILE + 2 * HALO
    pad = SHORT_CONV // 2
    for s in range(DN_QKV // LANES):
        cols = slice(s * LANES, (s + 1) * LANES)
        ext = jnp.concatenate([pp_ref[:, cols] * keep_prev, pm_ref[:, cols], pn_ref[:, cols] * keep_next], axis=0)
        acc = None
        for kk in range(SHORT_CONV):
            off = HALO - pad + kk
            tap = pltpu.roll(ext, n_ext - off, 0)[0:TILE] * cw_ref[kk:kk + 1, cols]
            acc = tap if acc is None else acc + tap
        a = _silu(acc)
        head = s % DN_HEADS
        if s < 2 * DN_HEADS:
            a = a * lax.rsqrt(jnp.sum(a * a, axis=1, keepdims=True) + 1e-6)
            if s < DN_HEADS:
                q_ref[head] = a * LANES ** -0.5
            else:
                k_ref[head] = a
        else:
            v_ref[head] = a


def _dn_conv(p_qkv, conv_w):
    rows8 = TILE // HALO
    last8 = N_TOK // HALO - 1
    hs = _head_shape(DN_HEADS, LANES)
    return pl.pallas_call(
        _dn_conv_kernel,
        grid=(ALL_TILES,),
        in_specs=[_tok_spec(DN_QKV),
                  pl.BlockSpec((HALO, DN_QKV), lambda t: (jnp.maximum(t * rows8 - 1, 0), 0)),
                  pl.BlockSpec((HALO, DN_QKV), lambda t: (jnp.minimum((t + 1) * rows8, last8), 0)),
                  _full_spec(conv_w.shape)],
        out_specs=[_head_spec(DN_HEADS, LANES)] * 3,
        out_shape=[hs, hs, hs],
        compiler_params=_cparams("parallel"),
        name="dn_conv",
    )(p_qkv, p_qkv, p_qkv, conv_w)


def _chunk_scan(x, row, reverse):
    sh = 1
    while sh < CHUNK:
        if reverse:
            x = x + jnp.where(row < CHUNK - sh, pltpu.roll(x, CHUNK - sh, 0), 0.0)
        else:
            x = x + jnp.where(row >= sh, pltpu.roll(x, sh, 0), 0.0)
        sh *= 2
    return x


def _gla_chunks(q, k, v, g, st, rev, row, r64, c64):
    n = range(len(q))
    cum = [_chunk_scan(g[i], row, rev[i]) for i in n]
    g_end = [cum[i][0:1, :] if rev[i] else cum[i][CHUNK - 1:CHUNK, :] for i in n]
    att = [jnp.where(r64 == c64, _nt(_bf(q[i]), _bf(k[i])), 0.0) for i in n]
    edge = list(cum)
    s = 1
    while s < CHUNK:
        late = (row & s) != 0
        early = jnp.logical_not(late)
        same = None
        if 2 * s < CHUNK:
            shift = (2 * s).bit_length() - 1
            same = (r64 >> shift) == (c64 >> shift)
        for i in n:
            if rev[i]:
                cut = jnp.where(late, edge[i], pltpu.roll(edge[i], CHUNK - s, 0))
            else:
                cut = jnp.where(late, pltpu.roll(edge[i], s, 0), edge[i])
            e = jnp.exp(-jnp.abs(cum[i] - cut))
            q_side = early if rev[i] else late
            a = _nt(_bf(jnp.where(q_side, q[i] * e, 0.0)), _bf(jnp.where(q_side, 0.0, k[i] * e)))
            if same is not None:
                a = jnp.where(same, a, 0.0)
            att[i] = att[i] + a
            if rev[i]:
                edge[i] = jnp.where(late, pltpu.roll(edge[i], s, 0), edge[i])
            else:
                edge[i] = jnp.where(late, edge[i], pltpu.roll(edge[i], CHUNK - s, 0))
        s *= 2
    st16 = [_bf(st[i]) for i in n]
    o = [_mm(_bf(att[i]), _bf(v[i])) + _nt(_bf(q[i] * jnp.exp(cum[i])), st16[i]) for i in n]
    st_new = [st[i] * jnp.exp(g_end[i]) + _tn(_bf(v[i]), _bf(k[i] * jnp.exp(g_end[i] - cum[i]))) for i in n]
    return o, st_new


def _scan_iotas():
    row = lax.broadcasted_iota(jnp.int32, (CHUNK, LANES), 0)
    r64 = lax.broadcasted_iota(jnp.int32, (CHUNK, CHUNK), 0)
    c64 = lax.broadcasted_iota(jnp.int32, (CHUNK, CHUNK), 1)
    return row, r64, c64


def _gla_scan_kernel(qf_ref, kf_ref, vf_ref, gf_ref, qb_ref, kb_ref, vb_ref, gb_ref,
                     of_ref, ob_ref, st_ref, *, n_heads):
    @pl.when(pl.program_id(1) == 0)
    def _():
        st_ref[...] = jnp.zeros_like(st_ref)

    row, r64, c64 = _scan_iotas()
    heads = range(n_heads)
    rev = [False] * n_heads + [True] * n_heads
    o, st = _gla_chunks(
        [qf_ref[h] for h in heads] + [qb_ref[h] for h in heads],
        [kf_ref[h] for h in heads] + [kb_ref[h] for h in heads],
        [vf_ref[h] for h in heads] + [vb_ref[h] for h in heads],
        [gf_ref[h] for h in heads] + [gb_ref[h] for h in heads],
        [st_ref[0, h] for h in heads] + [st_ref[1, h] for h in heads],
        rev, row, r64, c64)
    for h in heads:
        of_ref[h] = o[h]
        ob_ref[h] = o[n_heads + h]
        st_ref[0, h] = st[h]
        st_ref[1, h] = st[n_heads + h]


CTX_CHUNKS = CTX_LEN // CHUNK
LAT_CHUNKS = SEQ // CHUNK
SCAN_STEPS = CTX_CHUNKS + LAT_CHUNKS


def _fwd_chunk(b, n):
    return jnp.where(n < CTX_CHUNKS, N_LAT // CHUNK + b * CTX_CHUNKS + n, b * LAT_CHUNKS + n - CTX_CHUNKS)


def _bwd_chunk(b, n):
    return jnp.where(n < CTX_CHUNKS, N_LAT // CHUNK + b * CTX_CHUNKS + CTX_CHUNKS - 1 - n,
                     b * LAT_CHUNKS + SCAN_STEPS - 1 - n)


def _gla_scan(q, k_f, k_b, v, g_f, g_b):
    n_heads, _, dk = q.shape
    dv = v.shape[2]
    fwd = lambda b, n: (0, _fwd_chunk(b, n), 0)
    bwd = lambda b, n: (0, _bwd_chunk(b, n), 0)
    kspec = lambda m: pl.BlockSpec((n_heads, CHUNK, dk), m)
    vspec = lambda m: pl.BlockSpec((n_heads, CHUNK, dv), m)
    oshape = jax.ShapeDtypeStruct((n_heads, N_TOK, dv), F32)
    return pl.pallas_call(
        functools.partial(_gla_scan_kernel, n_heads=n_heads),
        grid=(BATCH, SCAN_STEPS),
        in_specs=[kspec(fwd), kspec(fwd), vspec(fwd), kspec(fwd), kspec(bwd), kspec(bwd), vspec(bwd), kspec(bwd)],
        out_specs=[vspec(fwd), vspec(bwd)],
        out_shape=[oshape, oshape],
        scratch_shapes=[pltpu.VMEM((2, n_heads, dv, dk), F32)],
        compiler_params=_cparams("parallel", "arbitrary"),
        name="gla_scan",
    )(q, k_f, v, g_f, q, k_b, v, g_b)


def _lane_pick(x, lane, idx):
    return jnp.sum(jnp.where(lane == idx, x, 0.0), axis=1, keepdims=True)


def _dn_chunks(q, k, v, beta, g_col, g_row, st, rev, r64, c64):
    n = range(len(q))
    dk = k[0].shape[1]
    g_end = [g_row[i][:, 0:1] if rev[i] else g_row[i][:, CHUNK - 1:CHUNK] for i in n]
    causal = {False: r64 >= c64, True: r64 <= c64}
    gam = [jnp.where(causal[rev[i]], jnp.exp(jnp.minimum(g_col[i] - g_row[i], 0.0)), 0.0) for i in n]
    kb = [k[i] * beta[i] for i in n]
    k16 = [_bf(k[i]) for i in n]
    m = [jnp.where(r64 == c64, 0.0, _nt(_bf(kb[i]), k16[i]) * gam[i]) for i in n]
    att = [_nt(_bf(q[i]), k16[i]) * gam[i] for i in n]
    y = None
    s = 1
    while s < CHUNK:
        shift = (2 * s).bit_length() - 1
        rlate = (r64 & s) != 0
        clate = (c64 & s) != 0
        pair = {False: jnp.logical_and(rlate, jnp.logical_not(clate)), True: jnp.logical_and(clate, jnp.logical_not(rlate))}
        if 2 * s < CHUNK:
            same = (r64 >> shift) == (c64 >> shift)
            pair = {d: jnp.logical_and(p, same) for d, p in pair.items()}
        c = [jnp.where(pair[rev[i]], m[i], 0.0) for i in n]
        if y is None:
            y = [-c[i] for i in n]
        else:
            c16 = [_bf(c[i]) for i in n]
            y16 = [_bf(y[i]) for i in n]
            p = [c[i] + _mm(y16[i], c16[i]) for i in n]
            y = [y[i] - (p[i] + _mm(_bf(p[i]), y16[i])) for i in n]
        s *= 2
    e_col = [jnp.exp(g_col[i]) for i in n]
    rhs = [jnp.concatenate([kb[i] * e_col[i], v[i] * beta[i]], axis=1) for i in n]
    wu = [rhs[i] + _mm(_bf(y[i]), _bf(rhs[i])) for i in n]
    st16 = [_bf(st[i]) for i in n]
    v_new = [wu[i][:, dk:] - _nt(_bf(wu[i][:, :dk]), st16[i]) for i in n]
    o = [_nt(_bf(q[i] * e_col[i]), st16[i]) + _mm(_bf(att[i]), _bf(v_new[i])) for i in n]
    st_new = [st[i] * jnp.exp(g_end[i]) + _tn(_bf(v_new[i]), _bf(k[i] * jnp.exp(g_end[i] - g_col[i]))) for i in n]
    return o, st_new


def _dn_scan_kernel(qf_ref, kf_ref, vf_ref, cf_ref, rf_ref, qb_ref, kb_ref, vb_ref, cb_ref, rb_ref,
                    of_ref, ob_ref, st_ref):
    @pl.when(pl.program_id(1) == 0)
    def _():
        st_ref[...] = jnp.zeros_like(st_ref)

    _, r64, c64 = _scan_iotas()
    cf = cf_ref[...]
    cb = cb_ref[...]
    col = lambda x, i: x[:, i:i + 1]
    heads = range(DN_HEADS)
    rev = [False] * DN_HEADS + [True] * DN_HEADS
    o, st = _dn_chunks(
        [qf_ref[h] for h in heads] + [qb_ref[h] for h in heads],
        [kf_ref[h] for h in heads] + [kb_ref[h] for h in heads],
        [vf_ref[h] for h in heads] + [vb_ref[h] for h in heads],
        [col(cf, h) for h in heads] + [col(cb, DN_HEADS + h) for h in heads],
        [col(cf, 2 * DN_HEADS + h) for h in heads] + [col(cb, 3 * DN_HEADS + h) for h in heads],
        [rf_ref[0, 2 * DN_HEADS + h:2 * DN_HEADS + h + 1, :] for h in heads]
        + [rb_ref[0, 3 * DN_HEADS + h:3 * DN_HEADS + h + 1, :] for h in heads],
        [st_ref[0, h] for h in heads] + [st_ref[1, h] for h in heads],
        rev, r64, c64)
    for h in heads:
        of_ref[h] = o[h]
        ob_ref[h] = o[DN_HEADS + h]
        st_ref[0, h] = st[h]
        st_ref[1, h] = st[DN_HEADS + h]


def _dn_scan(q, k, v, bgc, bgr):
    fwd = lambda b, n: (0, _fwd_chunk(b, n), 0)
    bwd = lambda b, n: (0, _bwd_chunk(b, n), 0)
    fwd2 = lambda b, n: (_fwd_chunk(b, n), 0)
    bwd2 = lambda b, n: (_bwd_chunk(b, n), 0)
    fwd3 = lambda b, n: (_fwd_chunk(b, n), 0, 0)
    bwd3 = lambda b, n: (_bwd_chunk(b, n), 0, 0)
    hspec = lambda m: pl.BlockSpec((DN_HEADS, CHUNK, LANES), m)
    cspec = lambda m: pl.BlockSpec((CHUNK, LANES), m)
    rspec = lambda m: pl.BlockSpec((1, 4 * DN_HEADS, CHUNK), m)
    oshape = _head_shape(DN_HEADS, LANES)
    return pl.pallas_call(
        _dn_scan_kernel,
        grid=(BATCH, SCAN_STEPS),
        in_specs=[hspec(fwd), hspec(fwd), hspec(fwd), cspec(fwd2), rspec(fwd3),
                  hspec(bwd), hspec(bwd), hspec(bwd), cspec(bwd2), rspec(bwd3)],
        out_specs=[hspec(fwd), hspec(bwd)],
        out_shape=[oshape, oshape],
        scratch_shapes=[pltpu.VMEM((2, DN_HEADS, LANES, LANES), F32)],
        compiler_params=_cparams("parallel", "arbitrary"),
        name="dn_scan",
    )(q, k, v, bgc, bgr, q, k, v, bgc, bgr)


def _layer_norm(z, g, b):
    mu = jnp.mean(z, axis=1, keepdims=True)
    zc = z - mu
    var = jnp.mean(zc * zc, axis=1, keepdims=True)
    return zc * lax.rsqrt(var + LN_EPS) * g + b


def _post_kernel(of_ref, ob_ref, gate_ref, x_ref, mod_ref, nw_ref, wo_ref, lg_ref, lb_ref, rw_ref, rb_ref,
                 x1_ref, h2_ref, logit_ref, *, n_heads):
    parts = []
    for h in range(n_heads):
        o = of_ref[h] + ob_ref[h]
        o = o * lax.rsqrt(jnp.mean(o * o, axis=1, keepdims=True) + RMS_EPS) * nw_ref[...]
        parts.append(o)
    on = jnp.concatenate(parts, axis=1) * _silu(gate_ref[...])
    y = _mm(_bf(on), wo_ref[...])
    z = ALPHA * x_ref[...] + mod_ref[0, 2:3, :] * y
    x1 = _layer_norm(z, lg_ref[...], lb_ref[...])
    x1_ref[...] = x1
    h2 = x1 * (1.0 + mod_ref[0, 4:5, :]) + mod_ref[0, 3:4, :]
    h2_ref[...] = h2
    logit_ref[...] = jnp.dot(h2, rw_ref[...], precision=HIGHEST, preferred_element_type=F32) + rb_ref[...]


def _post(o_f, o_b, gate, xs, mod, norm_w, w_out, ln_g, ln_b, rw, rb, n_tiles):
    n_heads, _, dv = o_f.shape
    n_rows = n_tiles * TILE
    tok = lambda w: jax.ShapeDtypeStruct((n_rows, w), F32)
    return pl.pallas_call(
        functools.partial(_post_kernel, n_heads=n_heads),
        grid=(n_tiles,),
        in_specs=[_head_spec(n_heads, dv), _head_spec(n_heads, dv), _tok_spec(D_MODEL), _tok_spec(D_MODEL), _mod_spec(),
                  _full_spec(norm_w.shape), _full_spec(w_out.shape), _full_spec(ln_g.shape), _full_spec(ln_b.shape),
                  _full_spec(rw.shape), _full_spec(rb.shape)],
        out_specs=[_tok_spec(D_MODEL), _tok_spec(D_MODEL), _tok_spec(LANES)],
        out_shape=[tok(D_MODEL), tok(D_MODEL), tok(LANES)],
        compiler_params=_cparams("parallel"),
        name="post_mixer",
    )(o_f, o_b, gate, xs, mod, norm_w, w_out, ln_g, ln_b, rw, rb)


def _route_kernel(logit_ref, route_ref, count_ref, run_ref):
    t = pl.program_id(0)

    @pl.when(t == 0)
    def _():
        run_ref[...] = jnp.zeros_like(run_ref)

    lane = lax.broadcasted_iota(jnp.int32, (TILE, LANES), 1)
    lanef = lane.astype(F32)
    neg = jnp.float32(-jnp.inf)
    l = jnp.where(lane < N_EXPERTS, logit_ref[...], neg)
    vals, picks, hots = [], [], []
    for _ in range(TOP_K):
        m = jnp.max(l, axis=1, keepdims=True)
        pick = jnp.min(jnp.where(l == m, lanef, float(LANES)), axis=1, keepdims=True)
        hot = lanef == pick
        l = jnp.where(hot, neg, l)
        vals.append(m)
        picks.append(pick)
        hots.append(hot)
    es = [jnp.exp(v - vals[0]) for v in vals]
    denom = es[0] + es[1] + es[2] + es[3]
    onehot = jnp.zeros((TILE, LANES), F32)
    for hot in hots:
        onehot = onehot + jnp.where(hot, 1.0, 0.0)
    r = lax.broadcasted_iota(jnp.int32, (TILE, TILE), 0)
    c = lax.broadcasted_iota(jnp.int32, (TILE, TILE), 1)
    before = jnp.where(c < r, 1.0, 0.0).astype(BF16)
    run = run_ref[0:1, :]
    excl = _mm(before, _bf(onehot)) + run
    out = jnp.zeros((TILE, LANES), F32)
    for kk in range(TOP_K):
        rank = jnp.sum(jnp.where(hots[kk], excl, 0.0), axis=1, keepdims=True)
        out = jnp.where(lane == kk, picks[kk], out)
        out = jnp.where(lane == TOP_K + kk, rank, out)
        out = jnp.where(lane == 2 * TOP_K + kk, es[kk] / denom, out)
    route_ref[...] = out
    new_run = run + jnp.sum(onehot, axis=0, keepdims=True)
    run_ref[...] = jnp.broadcast_to(new_run, run_ref.shape)
    count_ref[...] = jnp.broadcast_to(new_run, count_ref.shape)


def _route(logits):
    n_tiles = logits.shape[0] // TILE
    return pl.pallas_call(
        _route_kernel,
        grid=(n_tiles,),
        in_specs=[_tok_spec(LANES)],
        out_specs=[_tok_spec(LANES), pl.BlockSpec((SUBLANES, LANES), lambda t: (0, 0))],
        out_shape=[jax.ShapeDtypeStruct(logits.shape, F32), jax.ShapeDtypeStruct((SUBLANES, LANES), F32)],
        scratch_shapes=[pltpu.VMEM((SUBLANES, LANES), F32)],
        compiler_params=_cparams("arbitrary"),
        name="route",
    )(logits)


def _dispatch_kernel(slot_ref, h_ref, xs_ref, sem):
    def copy(r, kk):
        return pltpu.make_async_copy(h_ref.at[pl.ds(r, 1), :], xs_ref.at[pl.ds(slot_ref[0, kk, r], 1), :], sem)

    def start(r, carry):
        for kk in range(TOP_K):
            copy(r, kk).start()
        return carry

    def wait(r, carry):
        for kk in range(TOP_K):
            copy(r, kk).wait()
        return carry

    lax.fori_loop(0, TILE, start, 0)
    lax.fori_loop(0, TILE, wait, 0)


def _dispatch(slots, h2):
    n_tiles = h2.shape[0] // TILE
    return pl.pallas_call(
        _dispatch_kernel,
        grid=(n_tiles,),
        in_specs=[pl.BlockSpec((1, TOP_K, TILE), lambda t: (t, 0, 0), memory_space=pltpu.SMEM),
                  _tok_spec(D_MODEL)],
        out_specs=pl.BlockSpec(memory_space=pl.ANY),
        out_shape=jax.ShapeDtypeStruct((N_SLOTS, D_MODEL), F32),
        scratch_shapes=[pltpu.SemaphoreType.DMA(())],
        compiler_params=_cparams("arbitrary"),
        name="dispatch",
    )(slots, h2)


PAIR_BLOCK = 2 * LANES


def _expert_kernel(te_ref, tv_ref, nt_ref, xs_ref, w1_ref, b1_ref, w2_ref, b2_ref, ys_ref, w1p_ref, w2b_ref):
    i = pl.program_id(0)
    valid = i < nt_ref[0]
    new_expert = jnp.logical_or(i == 0, te_ref[i] != te_ref[jnp.maximum(i - 1, 0)])

    @pl.when(jnp.logical_and(valid, new_expert))
    def _():
        r = lax.broadcasted_iota(jnp.int32, (PAIR_BLOCK, PAIR_BLOCK), 0)
        c = lax.broadcasted_iota(jnp.int32, (PAIR_BLOCK, PAIR_BLOCK), 1)
        src = jnp.where(c < LANES, 2 * c, 2 * (c - LANES) + 1)
        perm = jnp.where(r == src, 1.0, 0.0).astype(BF16)
        for b in range(2 * D_FF // PAIR_BLOCK):
            cols = slice(b * PAIR_BLOCK, (b + 1) * PAIR_BLOCK)
            w1p_ref[:, cols] = _bf(_mm(_bf(w1_ref[0, 0, :, cols]), perm))
        w2b_ref[...] = _bf(w2_ref[0, 0])

    @pl.when(valid)
    def _():
        rowi = lax.broadcasted_iota(jnp.int32, (EXPERT_TILE, D_MODEL), 0)
        x = _bf(jnp.where(rowi < tv_ref[i], xs_ref[...], 0.0))
        u = _mm(x, w1p_ref[...]) + b1_ref[0]
        parts = []
        for b in range(2 * D_FF // PAIR_BLOCK):
            glu = jnp.minimum(u[:, b * PAIR_BLOCK:b * PAIR_BLOCK + LANES], SWIGLU_LIMIT)
            lin = jnp.clip(u[:, b * PAIR_BLOCK + LANES:(b + 1) * PAIR_BLOCK], -SWIGLU_LIMIT, SWIGLU_LIMIT)
            parts.append(glu * _sigmoid(SWIGLU_ALPHA * glu) * (lin + 1.0))
        act = jnp.concatenate(parts, axis=1)
        ys_ref[...] = _mm(_bf(act), w2b_ref[...]) + b2_ref[0]


def _experts(layer, tile_expert, tile_valid, n_tiles, xs, w1, b1p, w2, b2):
    tile_map = lambda i, te, tv, nt: (jnp.minimum(i, nt[0] - 1), 0)
    wmap = lambda i, te, tv, nt: (te[i], 0, 0)
    lmap = lambda i, te, tv, nt: (layer, te[i], 0, 0)
    grid_spec = pltpu.PrefetchScalarGridSpec(
        num_scalar_prefetch=3,
        grid=(N_SLOT_TILES,),
        in_specs=[pl.BlockSpec((EXPERT_TILE, D_MODEL), tile_map),
                  pl.BlockSpec((1, 1, D_MODEL, 2 * D_FF), lmap), pl.BlockSpec((1, 1, 2 * D_FF), wmap),
                  pl.BlockSpec((1, 1, D_FF, D_MODEL), lmap), pl.BlockSpec((1, 1, D_MODEL), wmap)],
        out_specs=pl.BlockSpec((EXPERT_TILE, D_MODEL), tile_map),
        scratch_shapes=[pltpu.VMEM((D_MODEL, 2 * D_FF), BF16), pltpu.VMEM((D_FF, D_MODEL), BF16)],
    )
    return pl.pallas_call(
        _expert_kernel,
        grid_spec=grid_spec,
        out_shape=jax.ShapeDtypeStruct((N_SLOTS, D_MODEL), F32),
        compiler_params=_cparams("arbitrary"),
        name="experts",
    )(tile_expert, tile_valid, n_tiles, xs, w1, b1p, w2, b2)


def _combine_kernel(slot_ref, ys_ref, route_ref, x1_ref, mod_ref, lg_ref, lb_ref, x2_ref, buf_ref, sem):
    def copy(r, kk):
        return pltpu.make_async_copy(ys_ref.at[pl.ds(slot_ref[0, kk, r], 1), :], buf_ref.at[kk, pl.ds(r, 1), :], sem)

    def start(r, carry):
        for kk in range(TOP_K):
            copy(r, kk).start()
        return carry

    def wait(r, carry):
        for kk in range(TOP_K):
            copy(r, kk).wait()
        return carry

    lax.fori_loop(0, TILE, start, 0)
    lax.fori_loop(0, TILE, wait, 0)
    route = route_ref[...]
    lane = lax.broadcasted_iota(jnp.int32, route.shape, 1)
    f = None
    for kk in range(TOP_K):
        term = _lane_pick(route, lane, 2 * TOP_K + kk) * buf_ref[kk]
        f = term if f is None else f + term
    z = ALPHA * x1_ref[...] + mod_ref[0, 5:6, :] * f
    x2_ref[...] = _layer_norm(z, lg_ref[...], lb_ref[...])


def _combine(slots, ys, route, x1, mod, ln_g, ln_b):
    n_tiles = x1.shape[0] // TILE
    return pl.pallas_call(
        _combine_kernel,
        grid=(n_tiles,),
        in_specs=[pl.BlockSpec((1, TOP_K, TILE), lambda t: (t, 0, 0), memory_space=pltpu.SMEM),
                  pl.BlockSpec(memory_space=pl.ANY), _tok_spec(LANES), _tok_spec(D_MODEL), _mod_spec(),
                  _full_spec(ln_g.shape), _full_spec(ln_b.shape)],
        out_specs=_tok_spec(D_MODEL),
        out_shape=jax.ShapeDtypeStruct(x1.shape, F32),
        scratch_shapes=[pltpu.VMEM((TOP_K, TILE, D_MODEL), F32), pltpu.SemaphoreType.DMA(())],
        compiler_params=_cparams("arbitrary"),
        name="combine",
    )(slots, ys, route, x1, mod, ln_g, ln_b)


def _moe(layer, h2, logits, x1, mod, ln_g, ln_b, w1, b1p, w2, b2):
    n_tok = h2.shape[0]
    route, counts = _route(logits)
    cnt = counts[0, :N_EXPERTS].astype(jnp.int32)
    padded = (cnt + EXPERT_TILE - 1) // EXPERT_TILE * EXPERT_TILE
    ends = jnp.cumsum(padded)
    starts = ends - padded
    pick = route[:, 0:TOP_K].astype(jnp.int32)
    rank = route[:, TOP_K:2 * TOP_K].astype(jnp.int32)
    slot = starts[pick] + rank
    slots = slot.reshape(n_tok // TILE, TILE, TOP_K).transpose(0, 2, 1)
    tile_start = jnp.arange(N_SLOT_TILES, dtype=jnp.int32) * EXPERT_TILE
    tile_expert = jnp.minimum(jnp.sum(tile_start[:, None] >= ends[None, :], axis=1), N_EXPERTS - 1).astype(jnp.int32)
    tile_valid = jnp.clip(starts[tile_expert] + cnt[tile_expert] - tile_start, 0, EXPERT_TILE).astype(jnp.int32)
    n_tiles = (ends[N_EXPERTS - 1:] // EXPERT_TILE).astype(jnp.int32)
    xs = _dispatch(slots, h2)
    ys = _experts(layer, tile_expert, tile_valid, n_tiles, xs, w1, b1p, w2, b2)
    return _combine(slots, ys, route, x1, mod, ln_g, ln_b)


def _to_col_major(xs):
    lat = xs[:N_LAT].reshape(BATCH, SEQ // GRID_W, GRID_W, D_MODEL).transpose(0, 2, 1, 3).reshape(N_LAT, D_MODEL)
    return jnp.concatenate([lat, xs[N_LAT:]], axis=0)


def _lat_to_row_major(lat):
    return lat.reshape(BATCH, GRID_W, SEQ // GRID_W, D_MODEL).transpose(0, 2, 1, 3).reshape(N_LAT, D_MODEL)


def kernel(x, c, ctx, c_ctx, ada_w, ada_b, ln_g, ln_b, dn_w_in, dn_conv_w, dn_a_log, dn_dt_bias, dn_norm_w, dn_w_out,
           hg_w_in, hg_lower_bound, hg_norm_w, hg_w_out, gla_w_in, gla_gk_w, gla_gk_b, gla_norm_w, gla_w_out,
           router_w, router_b, exp_w1, exp_b1, exp_w2, exp_b2):
    xs = jnp.concatenate([x.reshape(N_LAT, D_MODEL), ctx.reshape(N_CTX, D_MODEL)], axis=0)
    cc = jnp.zeros((MOD_ROWS, D_MODEL), F32).at[:BATCH].set(c).at[BATCH].set(c_ctx)
    mods = _ada_all(cc, ada_w, ada_b).reshape(DEPTH, MOD_ROWS, ADA_CHUNKS, D_MODEL)
    lb_soft = jax.nn.softmax(hg_lower_bound.astype(F32), axis=0)
    lower_bounds = jnp.cumsum(lb_soft, axis=0) - lb_soft[0]

    for i in range(DEPTH):
        last = i == DEPTH - 1
        kind, j = i % 3, i // 3
        mod = mods[i]
        col_major = i % 2 == 1
        if col_major:
            xs = _to_col_major(xs)

        if kind == 0:
            nq = 2 * D_MODEL + 2 * D_MODEL
            w = dn_w_in[j]
            w_ba = jnp.zeros((D_MODEL, LANES), F32).at[:, :4 * DN_HEADS].set(w[:, nq:])
            a_neg = jnp.zeros((LANES,), F32).at[2 * DN_HEADS:4 * DN_HEADS].set(-jnp.exp(dn_a_log[j].astype(F32)).reshape(-1))
            dtb = jnp.zeros((LANES,), F32).at[2 * DN_HEADS:4 * DN_HEADS].set(dn_dt_bias[j].astype(F32).reshape(-1))
            nb = 4 * DN_HEADS
            p_qkv, gate, bgc, bgr = _dn_proj(xs, mod, _bf(w[:, :nq]), _bf(w_ba), _bf(w[:, nq:].T),
                                            a_neg.reshape(1, LANES), dtb.reshape(1, LANES),
                                            a_neg[:nb].reshape(nb, 1), dtb[:nb].reshape(nb, 1))
            cw = jnp.zeros((SUBLANES, DN_QKV), F32).at[:SHORT_CONV].set(dn_conv_w[j])
            q, k, v = _dn_conv(p_qkv, cw)
            o_f, o_b = _dn_scan(q, k, v, bgc, bgr)
            norm_w, w_out = dn_norm_w[j], dn_w_out[j]
        elif kind == 1:
            q, v, gate, k_f, g_f, k_b, g_b = _hg_proj(xs, mod, _bf(hg_w_in[j]), lower_bounds[i].reshape(1, D_MODEL))
            o_f, o_b = _gla_scan(q, k_f, k_b, v, g_f, g_b)
            norm_w, w_out = hg_norm_w[j], hg_w_out[j]
        else:
            kd = GLA_HEADS * GLA_DK
            n_main = 2 * kd + 2 * D_MODEL
            w = gla_w_in[j]
            w_lr = jnp.zeros((D_MODEL, 2 * LANES), F32)
            w_lr = w_lr.at[:, :GLA_RANK].set(w[:, n_main:n_main + GLA_RANK])
            w_lr = w_lr.at[:, LANES:LANES + GLA_RANK].set(w[:, n_main + GLA_RANK:])
            gk_w = jnp.zeros((2, LANES, kd), F32).at[:, :GLA_RANK].set(gla_gk_w[j])
            q, k, v, gate, g_f, g_b = _gla_proj(xs, mod, _bf(w[:, :n_main]), _bf(w_lr), _bf(gk_w),
                                                gla_gk_b[j].reshape(2, 1, kd))
            o_f, o_b = _gla_scan(q, k, k, v, g_f, g_b)
            norm_w, w_out = gla_norm_w[j], gla_w_out[j]

        n_tiles = LAT_TILES if last else ALL_TILES
        rw = jnp.zeros((D_MODEL, LANES), F32).at[:, :N_EXPERTS].set(router_w[i])
        rb = jnp.zeros((1, LANES), F32).at[0, :N_EXPERTS].set(router_b[i])
        x1, h2, logits = _post(o_f, o_b, gate, xs, mod, norm_w.reshape(1, -1), _bf(w_out),
                               ln_g[i, 0].reshape(1, D_MODEL), ln_b[i, 0].reshape(1, D_MODEL), rw, rb, n_tiles)
        b1p = exp_b1[i].reshape(N_EXPERTS, 2 * D_FF // PAIR_BLOCK, LANES, 2).transpose(0, 1, 3, 2)
        xs = _moe(i, h2, logits, x1, mod, ln_g[i, 1].reshape(1, D_MODEL), ln_b[i, 1].reshape(1, D_MODEL),
                  exp_w1, b1p.reshape(N_EXPERTS, 1, 2 * D_FF), exp_w2, exp_b2[i][:, None, :])
        if col_major:
            lat = _lat_to_row_major(xs[:N_LAT])
            xs = lat if last else jnp.concatenate([lat, xs[N_LAT:]], axis=0)
    return xs[:N_LAT].reshape(BATCH, SEQ, D_MODEL)
```

```python
import functools

import jax
import jax.numpy as jnp
from jax import lax
from jax.experimental import pallas as pl
from jax.experimental.pallas import tpu as pltpu
from jax.experimental.pallas import tpu_sc as plsc

F32 = jnp.float32
BF16 = jnp.bfloat16
HIGHEST = lax.Precision.HIGHEST

D_MODEL = 1024
BATCH = 8
SEQ = 2048
CTX_LEN = 256
DEPTH = 4
GRID_W = 64
CHUNK = 64
ADA_CHUNKS = 6
N_LAT = BATCH * SEQ
N_CTX = BATCH * CTX_LEN
N_TOK = N_LAT + N_CTX
TILE = 256
LAT_TILES = N_LAT // TILE
ALL_TILES = N_TOK // TILE
TILES_PER_SEQ = SEQ // TILE
MOD_ROWS = 16
LANES = 128
SUBLANES = 8

N_EXPERTS = 32
TOP_K = 4
D_FF = D_MODEL
SWIGLU_LIMIT = 7.0
SWIGLU_ALPHA = 1.702
EXPERT_TILE = 256
N_SLOT_TILES = N_TOK * TOP_K // EXPERT_TILE + N_EXPERTS
N_SLOTS = N_SLOT_TILES * EXPERT_TILE

LN_EPS = 1e-5
RMS_EPS = 1e-6
ALPHA = (2.0 * DEPTH) ** 0.25

VMEM_LIMIT = 48 * 1024 * 1024


def _cparams(*sem):
    return pltpu.CompilerParams(dimension_semantics=sem, vmem_limit_bytes=VMEM_LIMIT)


def _sigmoid(x):
    return 1.0 / (1.0 + jnp.exp(-x))


def _silu(x):
    return x * _sigmoid(x)


def _softplus(x):
    return jnp.maximum(x, 0.0) + jnp.log(1.0 + jnp.exp(-jnp.abs(x)))


def _log_sigmoid(x):
    return -_softplus(-x)


def _nt(a, b):
    return lax.dot_general(a, b, (((1,), (1,)), ((), ())), preferred_element_type=F32)


def _tn(a, b):
    return lax.dot_general(a, b, (((0,), (0,)), ((), ())), preferred_element_type=F32)


def _mm(a, b):
    return jnp.dot(a, b, preferred_element_type=F32)


def _bf(x):
    return x.astype(BF16)


def _mod_row(t):
    return jnp.minimum(t // TILES_PER_SEQ, BATCH)


ADA_NBLK = 1536


def _ada_kernel(c_ref, w_ref, b_ref, o_ref):
    s = _silu(c_ref[...])
    o_ref[0] = jnp.dot(s, w_ref[0], precision=HIGHEST, preferred_element_type=F32) + b_ref[0]


def _ada_all(cc, ada_w, ada_b):
    n = ADA_CHUNKS * D_MODEL
    return pl.pallas_call(
        _ada_kernel,
        grid=(DEPTH, n // ADA_NBLK),
        in_specs=[
            pl.BlockSpec((MOD_ROWS, D_MODEL), lambda i, j: (0, 0)),
            pl.BlockSpec((1, D_MODEL, ADA_NBLK), lambda i, j: (i, 0, j)),
            pl.BlockSpec((1, 1, ADA_NBLK), lambda i, j: (i, 0, j)),
        ],
        out_specs=pl.BlockSpec((1, MOD_ROWS, ADA_NBLK), lambda i, j: (i, 0, j)),
        out_shape=jax.ShapeDtypeStruct((DEPTH, MOD_ROWS, n), F32),
        compiler_params=_cparams("parallel", "parallel"),
        name="ada",
    )(cc, ada_w, ada_b.reshape(DEPTH, 1, n))


def _modulated(x_ref, mod_ref, shift, scale):
    return x_ref[...] * (1.0 + mod_ref[0, scale:scale + 1, :]) + mod_ref[0, shift:shift + 1, :]


def _store_heads(ref, val, width):
    for h in range(val.shape[1] // width):
        ref[h] = val[:, h * width:(h + 1) * width]


def _hg_proj_kernel(x_ref, mod_ref, w_ref, lb_ref, q_ref, v_ref, gate_ref, kf_ref, gf_ref, kb_ref, gb_ref):
    h = _bf(_modulated(x_ref, mod_ref, 0, 1))
    d = D_MODEL
    _store_heads(q_ref, _silu(_mm(h, w_ref[:, 0:d])), LANES)
    _store_heads(v_ref, _mm(h, w_ref[:, d:2 * d]), LANES)
    gate_ref[...] = _mm(h, w_ref[:, 2 * d:3 * d])
    lb = lb_ref[...]
    for k_ref, g_ref, lo in ((kf_ref, gf_ref, 3 * d), (kb_ref, gb_ref, 4 * d)):
        f = _mm(h, w_ref[:, lo:lo + d])
        _store_heads(k_ref, (1.0 - lb) * _sigmoid(-f), LANES)
        _store_heads(g_ref, jnp.log(lb + (1.0 - lb) * _sigmoid(f)), LANES)


def _head_spec(n_heads, width):
    return pl.BlockSpec((n_heads, TILE, width), lambda t: (0, t, 0))


def _head_shape(n_heads, width):
    return jax.ShapeDtypeStruct((n_heads, N_TOK, width), F32)


def _tok_spec(width):
    return pl.BlockSpec((TILE, width), lambda t: (t, 0))


def _mod_spec():
    return pl.BlockSpec((1, ADA_CHUNKS, D_MODEL), lambda t: (_mod_row(t), 0, 0))


def _full_spec(shape):
    return pl.BlockSpec(shape, lambda t: (0,) * len(shape))


def _hg_proj(xs, mod, w_in, lb):
    n = w_in.shape[1]
    hs = _head_shape(8, LANES)
    return pl.pallas_call(
        _hg_proj_kernel,
        grid=(ALL_TILES,),
        in_specs=[_tok_spec(D_MODEL), _mod_spec(), _full_spec((D_MODEL, n)), _full_spec((1, D_MODEL))],
        out_specs=[_head_spec(8, LANES), _head_spec(8, LANES), _tok_spec(D_MODEL),
                   _head_spec(8, LANES), _head_spec(8, LANES), _head_spec(8, LANES), _head_spec(8, LANES)],
        out_shape=[hs, hs, jax.ShapeDtypeStruct((N_TOK, D_MODEL), F32), hs, hs, hs, hs],
        compiler_params=_cparams("parallel"),
        name="hg_proj",
    )(xs, mod, w_in, lb)


GLA_HEADS = 4
GLA_DK = 128
GLA_DV = 256
GLA_RANK = 16
GLA_GATE_NORM = 16.0


def _gla_proj_kernel(x_ref, mod_ref, w_ref, wlr_ref, gkw_ref, gkb_ref, q_ref, k_ref, v_ref, gate_ref, gf_ref, gb_ref):
    h = _bf(_modulated(x_ref, mod_ref, 0, 1))
    kd = GLA_HEADS * GLA_DK
    vd = GLA_HEADS * GLA_DV
    _store_heads(q_ref, _mm(h, w_ref[:, 0:kd]) * GLA_DK ** -0.5, GLA_DK)
    _store_heads(k_ref, _mm(h, w_ref[:, kd:2 * kd]), GLA_DK)
    _store_heads(v_ref, _mm(h, w_ref[:, 2 * kd:2 * kd + vd]), GLA_DV)
    gate_ref[...] = _mm(h, w_ref[:, 2 * kd + vd:2 * kd + 2 * vd])
    for z, g_ref in ((0, gf_ref), (1, gb_ref)):
        lr = _mm(h, wlr_ref[:, z * LANES:(z + 1) * LANES])
        gk = _mm(_bf(lr), gkw_ref[z]) + gkb_ref[z]
        _store_heads(g_ref, _log_sigmoid(gk) / GLA_GATE_NORM, GLA_DK)


def _gla_proj(xs, mod, w_main, w_lr, gk_w, gk_b):
    kd = GLA_HEADS * GLA_DK
    hk = _head_shape(GLA_HEADS, GLA_DK)
    return pl.pallas_call(
        _gla_proj_kernel,
        grid=(ALL_TILES,),
        in_specs=[_tok_spec(D_MODEL), _mod_spec(), _full_spec(w_main.shape), _full_spec(w_lr.shape),
                  _full_spec(gk_w.shape), _full_spec(gk_b.shape)],
        out_specs=[_head_spec(GLA_HEADS, GLA_DK), _head_spec(GLA_HEADS, GLA_DK), _head_spec(GLA_HEADS, GLA_DV),
                   _tok_spec(D_MODEL), _head_spec(GLA_HEADS, GLA_DK), _head_spec(GLA_HEADS, GLA_DK)],
        out_shape=[hk, hk, _head_shape(GLA_HEADS, GLA_DV), jax.ShapeDtypeStruct((N_TOK, D_MODEL), F32), hk, hk],
        compiler_params=_cparams("parallel"),
        name="gla_proj",
    )(xs, mod, w_main, w_lr, gk_w, gk_b)


DN_HEADS = 8
DN_QKV = 3 * D_MODEL
SHORT_CONV = 5
CHUNKS_PER_TILE = TILE // CHUNK


def _seg_scan(x, pos, axis, reverse):
    n = x.shape[axis]
    sh = 1
    while sh < CHUNK:
        if reverse:
            x = x + jnp.where(pos < CHUNK - sh, pltpu.roll(x, n - sh, axis), 0.0)
        else:
            x = x + jnp.where(pos >= sh, pltpu.roll(x, sh, axis), 0.0)
        sh *= 2
    return x


def _dn_gates(p, a_neg, dtb, idx, pos, axis):
    beta = _sigmoid(p)
    g = a_neg * _softplus(p + dtb)
    gp = _seg_scan(g, pos, axis, False)
    gs = _seg_scan(g, pos, axis, True)
    return jnp.where(idx < 2 * DN_HEADS, beta, jnp.where(idx < 3 * DN_HEADS, gp, gs))


def _dn_proj_kernel(x_ref, mod_ref, w_ref, wba_ref, wbat_ref, ac_ref, dc_ref, ar_ref, dr_ref,
                    qkv_ref, gate_ref, bgc_ref, bgr_ref):
    h = _bf(_modulated(x_ref, mod_ref, 0, 1))
    qkv_ref[...] = _mm(h, w_ref[:, 0:DN_QKV])
    gate_ref[...] = _mm(h, w_ref[:, DN_QKV:DN_QKV + D_MODEL])
    pc = _mm(h, wba_ref[...])
    lane = lax.broadcasted_iota(jnp.int32, pc.shape, 1)
    rowpos = lax.broadcasted_iota(jnp.int32, pc.shape, 0) % CHUNK
    bgc_ref[...] = _dn_gates(pc, ac_ref[...], dc_ref[...], lane, rowpos, 0)
    pr = _nt(wbat_ref[...], h)
    sub = lax.broadcasted_iota(jnp.int32, pr.shape, 0)
    lanepos = lax.broadcasted_iota(jnp.int32, pr.shape, 1) % CHUNK
    gr = _dn_gates(pr, ar_ref[...], dr_ref[...], sub, lanepos, 1)
    for c in range(CHUNKS_PER_TILE):
        bgr_ref[c] = gr[:, c * CHUNK:(c + 1) * CHUNK]


def _dn_proj(xs, mod, w_main, w_ba, w_bat, a_col, d_col, a_row, d_row):
    nb = 4 * DN_HEADS
    return pl.pallas_call(
        _dn_proj_kernel,
        grid=(ALL_TILES,),
        in_specs=[_tok_spec(D_MODEL), _mod_spec(), _full_spec(w_main.shape), _full_spec(w_ba.shape),
                  _full_spec(w_bat.shape), _full_spec(a_col.shape), _full_spec(d_col.shape),
                  _full_spec(a_row.shape), _full_spec(d_row.shape)],
        out_specs=[_tok_spec(DN_QKV), _tok_spec(D_MODEL), _tok_spec(LANES),
                   pl.BlockSpec((CHUNKS_PER_TILE, nb, CHUNK), lambda t: (t, 0, 0))],
        out_shape=[jax.ShapeDtypeStruct((N_TOK, DN_QKV), F32), jax.ShapeDtypeStruct((N_TOK, D_MODEL), F32),
                   jax.ShapeDtypeStruct((N_TOK, LANES), F32),
                   jax.ShapeDtypeStruct((N_TOK // CHUNK, nb, CHUNK), F32)],
        compiler_params=_cparams("parallel"),
        name="dn_proj",
    )(xs, mod, w_main, w_ba, w_bat, a_col, d_col, a_row, d_row)


HALO = SUBLANES


def _dn_conv_kernel(pm_ref, pp_ref, pn_ref, cw_ref, q_ref, k_ref, v_ref):
    t = pl.program_id(0)
    is_lat = t < LAT_TILES
    first = jnp.logical_or(jnp.logical_not(is_lat), t % TILES_PER_SEQ == 0)
    last = jnp.logical_or(jnp.logical_not(is_lat), t % TILES_PER_SEQ == TILES_PER_SEQ - 1)
    keep_prev = jnp.where(first, 0.0, 1.0)
    keep_next = jnp.where(last, 0.0, 1.0)
    n_ext = TILE + 2 * HALO
    pad = SHORT_CONV // 2
    for s in range(DN_QKV // LANES):
        cols = slice(s * LANES, (s + 1) * LANES)
        ext = jnp.concatenate([pp_ref[:, cols] * keep_prev, pm_ref[:, cols], pn_ref[:, cols] * keep_next], axis=0)
        acc = None
        for kk in range(SHORT_CONV):
            off = HALO - pad + kk
            tap = pltpu.roll(ext, n_ext - off, 0)[0:TILE] * cw_ref[kk:kk + 1, cols]
            acc = tap if acc is None else acc + tap
        a = _silu(acc)
        head = s % DN_HEADS
        if s < 2 * DN_HEADS:
            a = a * lax.rsqrt(jnp.sum(a * a, axis=1, keepdims=True) + 1e-6)
            if s < DN_HEADS:
                q_ref[head] = a * LANES ** -0.5
            else:
                k_ref[head] = a
        else:
            v_ref[head] = a


def _dn_conv(p_qkv, conv_w):
    rows8 = TILE // HALO
    last8 = N_TOK // HALO - 1
    hs = _head_shape(DN_HEADS, LANES)
    return pl.pallas_call(
        _dn_conv_kernel,
        grid=(ALL_TILES,),
        in_specs=[_tok_spec(DN_QKV),
                  pl.BlockSpec((HALO, DN_QKV), lambda t: (jnp.maximum(t * rows8 - 1, 0), 0)),
                  pl.BlockSpec((HALO, DN_QKV), lambda t: (jnp.minimum((t + 1) * rows8, last8), 0)),
                  _full_spec(conv_w.shape)],
        out_specs=[_head_spec(DN_HEADS, LANES)] * 3,
        out_shape=[hs, hs, hs],
        compiler_params=_cparams("parallel"),
        name="dn_conv",
    )(p_qkv, p_qkv, p_qkv, conv_w)


def _chunk_scan(x, row, reverse):
    sh = 1
    while sh < CHUNK:
        if reverse:
            x = x + jnp.where(row < CHUNK - sh, pltpu.roll(x, CHUNK - sh, 0), 0.0)
        else:
            x = x + jnp.where(row >= sh, pltpu.roll(x, sh, 0), 0.0)
        sh *= 2
    return x


def _gla_chunks(q, k, v, g, st, rev, row, r64, c64):
    n = range(len(q))
    cum = [_chunk_scan(g[i], row, rev[i]) for i in n]
    g_end = [cum[i][0:1, :] if rev[i] else cum[i][CHUNK - 1:CHUNK, :] for i in n]
    att = [jnp.where(r64 == c64, _nt(_bf(q[i]), _bf(k[i])), 0.0) for i in n]
    edge = list(cum)
    s = 1
    while s < CHUNK:
        late = (row & s) != 0
        early = jnp.logical_not(late)
        same = None
        if 2 * s < CHUNK:
            shift = (2 * s).bit_length() - 1
            same = (r64 >> shift) == (c64 >> shift)
        for i in n:
            if rev[i]:
                cut = jnp.where(late, edge[i], pltpu.roll(edge[i], CHUNK - s, 0))
            else:
                cut = jnp.where(late, pltpu.roll(edge[i], s, 0), edge[i])
            e = jnp.exp(-jnp.abs(cum[i] - cut))
            q_side = early if rev[i] else late
            a = _nt(_bf(jnp.where(q_side, q[i] * e, 0.0)), _bf(jnp.where(q_side, 0.0, k[i] * e)))
            if same is not None:
                a = jnp.where(same, a, 0.0)
            att[i] = att[i] + a
            if rev[i]:
                edge[i] = jnp.where(late, pltpu.roll(edge[i], s, 0), edge[i])
            else:
                edge[i] = jnp.where(late, edge[i], pltpu.roll(edge[i], CHUNK - s, 0))
        s *= 2
    st16 = [_bf(st[i]) for i in n]
    o = [_mm(_bf(att[i]), _bf(v[i])) + _nt(_bf(q[i] * jnp.exp(cum[i])), st16[i]) for i in n]
    st_new = [st[i] * jnp.exp(g_end[i]) + _tn(_bf(v[i]), _bf(k[i] * jnp.exp(g_end[i] - cum[i]))) for i in n]
    return o, st_new


def _scan_iotas():
    row = lax.broadcasted_iota(jnp.int32, (CHUNK, LANES), 0)
    r64 = lax.broadcasted_iota(jnp.int32, (CHUNK, CHUNK), 0)
    c64 = lax.broadcasted_iota(jnp.int32, (CHUNK, CHUNK), 1)
    return row, r64, c64


def _gla_scan_kernel(qf_ref, kf_ref, vf_ref, gf_ref, qb_ref, kb_ref, vb_ref, gb_ref,
                     of_ref, ob_ref, st_ref, *, n_heads):
    @pl.when(pl.program_id(1) == 0)
    def _():
        st_ref[...] = jnp.zeros_like(st_ref)

    row, r64, c64 = _scan_iotas()
    heads = range(n_heads)
    rev = [False] * n_heads + [True] * n_heads
    o, st = _gla_chunks(
        [qf_ref[h] for h in heads] + [qb_ref[h] for h in heads],
        [kf_ref[h] for h in heads] + [kb_ref[h] for h in heads],
        [vf_ref[h] for h in heads] + [vb_ref[h] for h in heads],
        [gf_ref[h] for h in heads] + [gb_ref[h] for h in heads],
        [st_ref[0, h] for h in heads] + [st_ref[1, h] for h in heads],
        rev, row, r64, c64)
    for h in heads:
        of_ref[h] = o[h]
        ob_ref[h] = o[n_heads + h]
        st_ref[0, h] = st[h]
        st_ref[1, h] = st[n_heads + h]


CTX_CHUNKS = CTX_LEN // CHUNK
LAT_CHUNKS = SEQ // CHUNK
SCAN_STEPS = CTX_CHUNKS + LAT_CHUNKS


def _fwd_chunk(b, n):
    return jnp.where(n < CTX_CHUNKS, N_LAT // CHUNK + b * CTX_CHUNKS + n, b * LAT_CHUNKS + n - CTX_CHUNKS)


def _bwd_chunk(b, n):
    return jnp.where(n < CTX_CHUNKS, N_LAT // CHUNK + b * CTX_CHUNKS + CTX_CHUNKS - 1 - n,
                     b * LAT_CHUNKS + SCAN_STEPS - 1 - n)


def _gla_scan(q, k_f, k_b, v, g_f, g_b):
    n_heads, _, dk = q.shape
    dv = v.shape[2]
    fwd = lambda b, n: (0, _fwd_chunk(b, n), 0)
    bwd = lambda b, n: (0, _bwd_chunk(b, n), 0)
    kspec = lambda m: pl.BlockSpec((n_heads, CHUNK, dk), m)
    vspec = lambda m: pl.BlockSpec((n_heads, CHUNK, dv), m)
    oshape = jax.ShapeDtypeStruct((n_heads, N_TOK, dv), F32)
    return pl.pallas_call(
        functools.partial(_gla_scan_kernel, n_heads=n_heads),
        grid=(BATCH, SCAN_STEPS),
        in_specs=[kspec(fwd), kspec(fwd), vspec(fwd), kspec(fwd), kspec(bwd), kspec(bwd), vspec(bwd), kspec(bwd)],
        out_specs=[vspec(fwd), vspec(bwd)],
        out_shape=[oshape, oshape],
        scratch_shapes=[pltpu.VMEM((2, n_heads, dv, dk), F32)],
        compiler_params=_cparams("parallel", "arbitrary"),
        name="gla_scan",
    )(q, k_f, v, g_f, q, k_b, v, g_b)


def _lane_pick(x, lane, idx):
    return jnp.sum(jnp.where(lane == idx, x, 0.0), axis=1, keepdims=True)


def _dn_chunks(q, k, v, beta, g_col, g_row, st, rev, r64, c64):
    n = range(len(q))
    dk = k[0].shape[1]
    g_end = [g_row[i][:, 0:1] if rev[i] else g_row[i][:, CHUNK - 1:CHUNK] for i in n]
    causal = {False: r64 >= c64, True: r64 <= c64}
    gam = [jnp.where(causal[rev[i]], jnp.exp(jnp.minimum(g_col[i] - g_row[i], 0.0)), 0.0) for i in n]
    kb = [k[i] * beta[i] for i in n]
    k16 = [_bf(k[i]) for i in n]
    m = [jnp.where(r64 == c64, 0.0, _nt(_bf(kb[i]), k16[i]) * gam[i]) for i in n]
    att = [_nt(_bf(q[i]), k16[i]) * gam[i] for i in n]
    y = None
    s = 1
    while s < CHUNK:
        shift = (2 * s).bit_length() - 1
        rlate = (r64 & s) != 0
        clate = (c64 & s) != 0
        pair = {False: jnp.logical_and(rlate, jnp.logical_not(clate)), True: jnp.logical_and(clate, jnp.logical_not(rlate))}
        if 2 * s < CHUNK:
            same = (r64 >> shift) == (c64 >> shift)
            pair = {d: jnp.logical_and(p, same) for d, p in pair.items()}
        c = [jnp.where(pair[rev[i]], m[i], 0.0) for i in n]
        if y is None:
            y = [-c[i] for i in n]
        else:
            c16 = [_bf(c[i]) for i in n]
            y16 = [_bf(y[i]) for i in n]
            p = [c[i] + _mm(y16[i], c16[i]) for i in n]
            y = [y[i] - (p[i] + _mm(_bf(p[i]), y16[i])) for i in n]
        s *= 2
    e_col = [jnp.exp(g_col[i]) for i in n]
    rhs = [jnp.concatenate([kb[i] * e_col[i], v[i] * beta[i]], axis=1) for i in n]
    wu = [rhs[i] + _mm(_bf(y[i]), _bf(rhs[i])) for i in n]
    st16 = [_bf(st[i]) for i in n]
    v_new = [wu[i][:, dk:] - _nt(_bf(wu[i][:, :dk]), st16[i]) for i in n]
    o = [_nt(_bf(q[i] * e_col[i]), st16[i]) + _mm(_bf(att[i]), _bf(v_new[i])) for i in n]
    st_new = [st[i] * jnp.exp(g_end[i]) + _tn(_bf(v_new[i]), _bf(k[i] * jnp.exp(g_end[i] - g_col[i]))) for i in n]
    return o, st_new


def _dn_scan_kernel(qf_ref, kf_ref, vf_ref, cf_ref, rf_ref, qb_ref, kb_ref, vb_ref, cb_ref, rb_ref,
                    of_ref, ob_ref, st_ref):
    @pl.when(pl.program_id(1) == 0)
    def _():
        st_ref[...] = jnp.zeros_like(st_ref)

    _, r64, c64 = _scan_iotas()
    cf = cf_ref[...]
    cb = cb_ref[...]
    col = lambda x, i: x[:, i:i + 1]
    heads = range(DN_HEADS)
    rev = [False] * DN_HEADS + [True] * DN_HEADS
    o, st = _dn_chunks(
        [qf_ref[h] for h in heads] + [qb_ref[h] for h in heads],
        [kf_ref[h] for h in heads] + [kb_ref[h] for h in heads],
        [vf_ref[h] for h in heads] + [vb_ref[h] for h in heads],
        [col(cf, h) for h in heads] + [col(cb, DN_HEADS + h) for h in heads],
        [col(cf, 2 * DN_HEADS + h) for h in heads] + [col(cb, 3 * DN_HEADS + h) for h in heads],
        [rf_ref[0, 2 * DN_HEADS + h:2 * DN_HEADS + h + 1, :] for h in heads]
        + [rb_ref[0, 3 * DN_HEADS + h:3 * DN_HEADS + h + 1, :] for h in heads],
        [st_ref[0, h] for h in heads] + [st_ref[1, h] for h in heads],
        rev, r64, c64)
    for h in heads:
        of_ref[h] = o[h]
        ob_ref[h] = o[DN_HEADS + h]
        st_ref[0, h] = st[h]
        st_ref[1, h] = st[DN_HEADS + h]


def _dn_scan(q, k, v, bgc, bgr):
    fwd = lambda b, n: (0, _fwd_chunk(b, n), 0)
    bwd = lambda b, n: (0, _bwd_chunk(b, n), 0)
    fwd2 = lambda b, n: (_fwd_chunk(b, n), 0)
    bwd2 = lambda b, n: (_bwd_chunk(b, n), 0)
    fwd3 = lambda b, n: (_fwd_chunk(b, n), 0, 0)
    bwd3 = lambda b, n: (_bwd_chunk(b, n), 0, 0)
    hspec = lambda m: pl.BlockSpec((DN_HEADS, CHUNK, LANES), m)
    cspec = lambda m: pl.BlockSpec((CHUNK, LANES), m)
    rspec = lambda m: pl.BlockSpec((1, 4 * DN_HEADS, CHUNK), m)
    oshape = _head_shape(DN_HEADS, LANES)
    return pl.pallas_call(
        _dn_scan_kernel,
        grid=(BATCH, SCAN_STEPS),
        in_specs=[hspec(fwd), hspec(fwd), hspec(fwd), cspec(fwd2), rspec(fwd3),
                  hspec(bwd), hspec(bwd), hspec(bwd), cspec(bwd2), rspec(bwd3)],
        out_specs=[hspec(fwd), hspec(bwd)],
        out_shape=[oshape, oshape],
        scratch_shapes=[pltpu.VMEM((2, DN_HEADS, LANES, LANES), F32)],
        compiler_params=_cparams("parallel", "arbitrary"),
        name="dn_scan",
    )(q, k, v, bgc, bgr, q, k, v, bgc, bgr)


def _layer_norm(z, g, b):
    mu = jnp.mean(z, axis=1, keepdims=True)
    zc = z - mu
    var = jnp.mean(zc * zc, axis=1, keepdims=True)
    return zc * lax.rsqrt(var + LN_EPS) * g + b


def _post_kernel(of_ref, ob_ref, gate_ref, x_ref, mod_ref, nw_ref, wo_ref, lg_ref, lb_ref, rw_ref, rb_ref,
                 x1_ref, h2_ref, logit_ref, *, n_heads):
    parts = []
    for h in range(n_heads):
        o = of_ref[h] + ob_ref[h]
        o = o * lax.rsqrt(jnp.mean(o * o, axis=1, keepdims=True) + RMS_EPS) * nw_ref[...]
        parts.append(o)
    on = jnp.concatenate(parts, axis=1) * _silu(gate_ref[...])
    y = _mm(_bf(on), wo_ref[...])
    z = ALPHA * x_ref[...] + mod_ref[0, 2:3, :] * y
    x1 = _layer_norm(z, lg_ref[...], lb_ref[...])
    x1_ref[...] = x1
    h2 = x1 * (1.0 + mod_ref[0, 4:5, :]) + mod_ref[0, 3:4, :]
    h2_ref[...] = h2
    logit_ref[...] = jnp.dot(h2, rw_ref[...], precision=HIGHEST, preferred_element_type=F32) + rb_ref[...]


def _post(o_f, o_b, gate, xs, mod, norm_w, w_out, ln_g, ln_b, rw, rb, n_tiles):
    n_heads, _, dv = o_f.shape
    n_rows = n_tiles * TILE
    tok = lambda w: jax.ShapeDtypeStruct((n_rows, w), F32)
    return pl.pallas_call(
        functools.partial(_post_kernel, n_heads=n_heads),
        grid=(n_tiles,),
        in_specs=[_head_spec(n_heads, dv), _head_spec(n_heads, dv), _tok_spec(D_MODEL), _tok_spec(D_MODEL), _mod_spec(),
                  _full_spec(norm_w.shape), _full_spec(w_out.shape), _full_spec(ln_g.shape), _full_spec(ln_b.shape),
                  _full_spec(rw.shape), _full_spec(rb.shape)],
        out_specs=[_tok_spec(D_MODEL), _tok_spec(D_MODEL), _tok_spec(LANES)],
        out_shape=[tok(D_MODEL), tok(D_MODEL), tok(LANES)],
        compiler_params=_cparams("parallel"),
        name="post_mixer",
    )(o_f, o_b, gate, xs, mod, norm_w, w_out, ln_g, ln_b, rw, rb)


def _route_kernel(logit_ref, route_ref, count_ref, run_ref):
    t = pl.program_id(0)

    @pl.when(t == 0)
    def _():
        run_ref[...] = jnp.zeros_like(run_ref)

    lane = lax.broadcasted_iota(jnp.int32, (TILE, LANES), 1)
    lanef = lane.astype(F32)
    neg = jnp.float32(-jnp.inf)
    l = jnp.where(lane < N_EXPERTS, logit_ref[...], neg)
    vals, picks, hots = [], [], []
    for _ in range(TOP_K):
        m = jnp.max(l, axis=1, keepdims=True)
        pick = jnp.min(jnp.where(l == m, lanef, float(LANES)), axis=1, keepdims=True)
        hot = lanef == pick
        l = jnp.where(hot, neg, l)
        vals.append(m)
        picks.append(pick)
        hots.append(hot)
    es = [jnp.exp(v - vals[0]) for v in vals]
    denom = es[0] + es[1] + es[2] + es[3]
    onehot = jnp.zeros((TILE, LANES), F32)
    for hot in hots:
        onehot = onehot + jnp.where(hot, 1.0, 0.0)
    r = lax.broadcasted_iota(jnp.int32, (TILE, TILE), 0)
    c = lax.broadcasted_iota(jnp.int32, (TILE, TILE), 1)
    before = jnp.where(c < r, 1.0, 0.0).astype(BF16)
    run = run_ref[0:1, :]
    excl = _mm(before, _bf(onehot)) + run
    out = jnp.zeros((TILE, LANES), F32)
    for kk in range(TOP_K):
        rank = jnp.sum(jnp.where(hots[kk], excl, 0.0), axis=1, keepdims=True)
        out = jnp.where(lane == kk, picks[kk], out)
        out = jnp.where(lane == TOP_K + kk, rank, out)
        out = jnp.where(lane == 2 * TOP_K + kk, es[kk] / denom, out)
    route_ref[...] = out
    new_run = run + jnp.sum(onehot, axis=0, keepdims=True)
    run_ref[...] = jnp.broadcast_to(new_run, run_ref.shape)
    count_ref[...] = jnp.broadcast_to(new_run, count_ref.shape)


def _route(logits):
    n_tiles = logits.shape[0] // TILE
    return pl.pallas_call(
        _route_kernel,
        grid=(n_tiles,),
        in_specs=[_tok_spec(LANES)],
        out_specs=[_tok_spec(LANES), pl.BlockSpec((SUBLANES, LANES), lambda t: (0, 0))],
        out_shape=[jax.ShapeDtypeStruct(logits.shape, F32), jax.ShapeDtypeStruct((SUBLANES, LANES), F32)],
        scratch_shapes=[pltpu.VMEM((SUBLANES, LANES), F32)],
        compiler_params=_cparams("arbitrary"),
        name="route",
    )(logits)


GATHER_ROWS = 16
GATHER_BUFS = 4


def _sc_gather(table, idx):
    n_rows = idx.shape[0]
    d = table.shape[1]
    info = plsc.get_sparse_core_info()
    n_workers = info.num_cores * info.num_subcores
    rows_per_worker = n_rows // n_workers
    n_chunks = rows_per_worker // GATHER_ROWS
    assert rows_per_worker * n_workers == n_rows and n_chunks * GATHER_ROWS == rows_per_worker
    assert n_chunks % GATHER_BUFS == 0
    mesh = plsc.VectorSubcoreMesh(core_axis_name="c", subcore_axis_name="s")
    scratch = ([pltpu.VMEM((GATHER_ROWS,), jnp.int32)] * GATHER_BUFS
               + [pltpu.VMEM((GATHER_ROWS, d), table.dtype)] * GATHER_BUFS
               + [pltpu.SemaphoreType.DMA] * (2 * GATHER_BUFS))

    def body(table_hbm, idx_hbm, out_hbm, *scr):
        idx_v, rows_v = scr[:GATHER_BUFS], scr[GATHER_BUFS:2 * GATHER_BUFS]
        gather_sem, write_sem = scr[2 * GATHER_BUFS:3 * GATHER_BUFS], scr[3 * GATHER_BUFS:]
        worker = lax.axis_index("s") * info.num_cores + lax.axis_index("c")
        base = worker * rows_per_worker

        @pl.loop(0, n_chunks, step=GATHER_BUFS)
        def _(c0):
            offs = [pl.multiple_of(base + (c0 + b) * GATHER_ROWS, SUBLANES) for b in range(GATHER_BUFS)]
            gathers = []
            for b in range(GATHER_BUFS):
                pltpu.sync_copy(idx_hbm.at[pl.ds(offs[b], GATHER_ROWS)], idx_v[b])
                gathers.append(pltpu.async_copy(table_hbm.at[idx_v[b]], rows_v[b], gather_sem[b]))
            writes = []
            for b in range(GATHER_BUFS):
                gathers[b].wait()
                writes.append(pltpu.async_copy(rows_v[b], out_hbm.at[pl.ds(offs[b], GATHER_ROWS)], write_sem[b]))
            for w in writes:
                w.wait()

    return pl.kernel(body, out_type=jax.ShapeDtypeStruct((n_rows, d), table.dtype), mesh=mesh,
                     scratch_types=scratch, name="sc_gather")(table, idx)


PAIR_BLOCK = 2 * LANES


def _expert_kernel(te_ref, tv_ref, nt_ref, xs_ref, w1_ref, b1_ref, w2_ref, b2_ref, ys_ref, w1p_ref, w2b_ref):
    i = pl.program_id(0)
    valid = i < nt_ref[0]
    new_expert = jnp.logical_or(i == 0, te_ref[i] != te_ref[jnp.maximum(i - 1, 0)])

    @pl.when(jnp.logical_and(valid, new_expert))
    def _():
        r = lax.broadcasted_iota(jnp.int32, (PAIR_BLOCK, PAIR_BLOCK), 0)
        c = lax.broadcasted_iota(jnp.int32, (PAIR_BLOCK, PAIR_BLOCK), 1)
        src = jnp.where(c < LANES, 2 * c, 2 * (c - LANES) + 1)
        perm = jnp.where(r == src, 1.0, 0.0).astype(BF16)
        for b in range(2 * D_FF // PAIR_BLOCK):
            cols = slice(b * PAIR_BLOCK, (b + 1) * PAIR_BLOCK)
            w1p_ref[:, cols] = _bf(_mm(_bf(w1_ref[0, 0, :, cols]), perm))
        w2b_ref[...] = _bf(w2_ref[0, 0])

    @pl.when(valid)
    def _():
        rowi = lax.broadcasted_iota(jnp.int32, (EXPERT_TILE, D_MODEL), 0)
        x = _bf(jnp.where(rowi < tv_ref[i], xs_ref[...], 0.0))
        u = _mm(x, w1p_ref[...]) + b1_ref[0]
        parts = []
        for b in range(2 * D_FF // PAIR_BLOCK):
            glu = jnp.minimum(u[:, b * PAIR_BLOCK:b * PAIR_BLOCK + LANES], SWIGLU_LIMIT)
            lin = jnp.clip(u[:, b * PAIR_BLOCK + LANES:(b + 1) * PAIR_BLOCK], -SWIGLU_LIMIT, SWIGLU_LIMIT)
            parts.append(glu * _sigmoid(SWIGLU_ALPHA * glu) * (lin + 1.0))
        act = jnp.concatenate(parts, axis=1)
        ys_ref[...] = _mm(_bf(act), w2b_ref[...]) + b2_ref[0]


def _experts(layer, tile_expert, tile_valid, n_tiles, xs, w1, b1p, w2, b2):
    tile_map = lambda i, te, tv, nt: (jnp.minimum(i, nt[0] - 1), 0)
    wmap = lambda i, te, tv, nt: (te[i], 0, 0)
    lmap = lambda i, te, tv, nt: (layer, te[i], 0, 0)
    grid_spec = pltpu.PrefetchScalarGridSpec(
        num_scalar_prefetch=3,
        grid=(N_SLOT_TILES,),
        in_specs=[pl.BlockSpec((EXPERT_TILE, D_MODEL), tile_map),
                  pl.BlockSpec((1, 1, D_MODEL, 2 * D_FF), lmap), pl.BlockSpec((1, 1, 2 * D_FF), wmap),
                  pl.BlockSpec((1, 1, D_FF, D_MODEL), lmap), pl.BlockSpec((1, 1, D_MODEL), wmap)],
        out_specs=pl.BlockSpec((EXPERT_TILE, D_MODEL), tile_map),
        scratch_shapes=[pltpu.VMEM((D_MODEL, 2 * D_FF), BF16), pltpu.VMEM((D_FF, D_MODEL), BF16)],
    )
    return pl.pallas_call(
        _expert_kernel,
        grid_spec=grid_spec,
        out_shape=jax.ShapeDtypeStruct((N_SLOTS, D_MODEL), F32),
        compiler_params=_cparams("arbitrary"),
        name="experts",
    )(tile_expert, tile_valid, n_tiles, xs, w1, b1p, w2, b2)


def _combine_kernel(y_ref, route_ref, x1_ref, mod_ref, lg_ref, lb_ref, x2_ref):
    route = route_ref[...]
    lane = lax.broadcasted_iota(jnp.int32, route.shape, 1)
    f = None
    for kk in range(TOP_K):
        term = _lane_pick(route, lane, 2 * TOP_K + kk) * y_ref[kk]
        f = term if f is None else f + term
    z = ALPHA * x1_ref[...] + mod_ref[0, 5:6, :] * f
    x2_ref[...] = _layer_norm(z, lg_ref[...], lb_ref[...])


def _combine(y_tok, route, x1, mod, ln_g, ln_b):
    n_tiles = x1.shape[0] // TILE
    return pl.pallas_call(
        _combine_kernel,
        grid=(n_tiles,),
        in_specs=[pl.BlockSpec((TOP_K, TILE, D_MODEL), lambda t: (0, t, 0)), _tok_spec(LANES), _tok_spec(D_MODEL),
                  _mod_spec(), _full_spec(ln_g.shape), _full_spec(ln_b.shape)],
        out_specs=_tok_spec(D_MODEL),
        out_shape=jax.ShapeDtypeStruct(x1.shape, F32),
        compiler_params=_cparams("parallel"),
        name="combine",
    )(y_tok, route, x1, mod, ln_g, ln_b)


def _moe(layer, h2, logits, x1, mod, ln_g, ln_b, w1, b1p, w2, b2):
    n_tok = h2.shape[0]
    route, counts = _route(logits)
    cnt = counts[0, :N_EXPERTS].astype(jnp.int32)
    padded = (cnt + EXPERT_TILE - 1) // EXPERT_TILE * EXPERT_TILE
    ends = jnp.cumsum(padded)
    starts = ends - padded
    pick = route[:, 0:TOP_K].astype(jnp.int32)
    rank = route[:, TOP_K:2 * TOP_K].astype(jnp.int32)
    slot = starts[pick] + rank
    token = jnp.broadcast_to(jnp.arange(n_tok, dtype=jnp.int32)[:, None], slot.shape)
    slot_token = (jnp.arange(N_SLOTS, dtype=jnp.int32) % n_tok).at[slot.reshape(-1)].set(token.reshape(-1))
    tile_start = jnp.arange(N_SLOT_TILES, dtype=jnp.int32) * EXPERT_TILE
    tile_expert = jnp.minimum(jnp.sum(tile_start[:, None] >= ends[None, :], axis=1), N_EXPERTS - 1).astype(jnp.int32)
    tile_valid = jnp.clip(starts[tile_expert] + cnt[tile_expert] - tile_start, 0, EXPERT_TILE).astype(jnp.int32)
    n_tiles = (ends[N_EXPERTS - 1:] // EXPERT_TILE).astype(jnp.int32)
    xs = _sc_gather(h2, slot_token)
    ys = _experts(layer, tile_expert, tile_valid, n_tiles, xs, w1, b1p, w2, b2)
    y_tok = _sc_gather(ys, slot.T.reshape(-1)).reshape(TOP_K, n_tok, D_MODEL)
    return _combine(y_tok, route, x1, mod, ln_g, ln_b)


def _to_col_major(xs):
    lat = xs[:N_LAT].reshape(BATCH, SEQ // GRID_W, GRID_W, D_MODEL).transpose(0, 2, 1, 3).reshape(N_LAT, D_MODEL)
    return jnp.concatenate([lat, xs[N_LAT:]], axis=0)


def _lat_to_row_major(lat):
    return lat.reshape(BATCH, GRID_W, SEQ // GRID_W, D_MODEL).transpose(0, 2, 1, 3).reshape(N_LAT, D_MODEL)


def kernel(x, c, ctx, c_ctx, ada_w, ada_b, ln_g, ln_b, dn_w_in, dn_conv_w, dn_a_log, dn_dt_bias, dn_norm_w, dn_w_out,
           hg_w_in, hg_lower_bound, hg_norm_w, hg_w_out, gla_w_in, gla_gk_w, gla_gk_b, gla_norm_w, gla_w_out,
           router_w, router_b, exp_w1, exp_b1, exp_w2, exp_b2):
    xs = jnp.concatenate([x.reshape(N_LAT, D_MODEL), ctx.reshape(N_CTX, D_MODEL)], axis=0)
    cc = jnp.zeros((MOD_ROWS, D_MODEL), F32).at[:BATCH].set(c).at[BATCH].set(c_ctx)
    mods = _ada_all(cc, ada_w, ada_b).reshape(DEPTH, MOD_ROWS, ADA_CHUNKS, D_MODEL)
    lb_soft = jax.nn.softmax(hg_lower_bound.astype(F32), axis=0)
    lower_bounds = jnp.cumsum(lb_soft, axis=0) - lb_soft[0]

    for i in range(DEPTH):
        last = i == DEPTH - 1
        kind, j = i % 3, i // 3
        mod = mods[i]
        col_major = i % 2 == 1
        if col_major:
            xs = _to_col_major(xs)

        if kind == 0:
            nq = 2 * D_MODEL + 2 * D_MODEL
            w = dn_w_in[j]
            w_ba = jnp.zeros((D_MODEL, LANES), F32).at[:, :4 * DN_HEADS].set(w[:, nq:])
            a_neg = jnp.zeros((LANES,), F32).at[2 * DN_HEADS:4 * DN_HEADS].set(-jnp.exp(dn_a_log[j].astype(F32)).reshape(-1))
            dtb = jnp.zeros((LANES,), F32).at[2 * DN_HEADS:4 * DN_HEADS].set(dn_dt_bias[j].astype(F32).reshape(-1))
            nb = 4 * DN_HEADS
            p_qkv, gate, bgc, bgr = _dn_proj(xs, mod, _bf(w[:, :nq]), _bf(w_ba), _bf(w[:, nq:].T),
                                            a_neg.reshape(1, LANES), dtb.reshape(1, LANES),
                                            a_neg[:nb].reshape(nb, 1), dtb[:nb].reshape(nb, 1))
            cw = jnp.zeros((SUBLANES, DN_QKV), F32).at[:SHORT_CONV].set(dn_conv_w[j])
            q, k, v = _dn_conv(p_qkv, cw)
            o_f, o_b = _dn_scan(q, k, v, bgc, bgr)
            norm_w, w_out = dn_norm_w[j], dn_w_out[j]
        elif kind == 1:
            q, v, gate, k_f, g_f, k_b, g_b = _hg_proj(xs, mod, _bf(hg_w_in[j]), lower_bounds[i].reshape(1, D_MODEL))
            o_f, o_b = _gla_scan(q, k_f, k_b, v, g_f, g_b)
            norm_w, w_out = hg_norm_w[j], hg_w_out[j]
        else:
            kd = GLA_HEADS * GLA_DK
            n_main = 2 * kd + 2 * D_MODEL
            w = gla_w_in[j]
            w_lr = jnp.zeros((D_MODEL, 2 * LANES), F32)
            w_lr = w_lr.at[:, :GLA_RANK].set(w[:, n_main:n_main + GLA_RANK])
            w_lr = w_lr.at[:, LANES:LANES + GLA_RANK].set(w[:, n_main + GLA_RANK:])
            gk_w = jnp.zeros((2, LANES, kd), F32).at[:, :GLA_RANK].set(gla_gk_w[j])
            q, k, v, gate, g_f, g_b = _gla_proj(xs, mod, _bf(w[:, :n_main]), _bf(w_lr), _bf(gk_w),
                                                gla_gk_b[j].reshape(2, 1, kd))
            o_f, o_b = _gla_scan(q, k, k, v, g_f, g_b)
            norm_w, w_out = gla_norm_w[j], gla_w_out[j]

        n_tiles = LAT_TILES if last else ALL_TILES
        rw = jnp.zeros((D_MODEL, LANES), F32).at[:, :N_EXPERTS].set(router_w[i])
        rb = jnp.zeros((1, LANES), F32).at[0, :N_EXPERTS].set(router_b[i])
        x1, h2, logits = _post(o_f, o_b, gate, xs, mod, norm_w.reshape(1, -1), _bf(w_out),
                               ln_g[i, 0].reshape(1, D_MODEL), ln_b[i, 0].reshape(1, D_MODEL), rw, rb, n_tiles)
        b1p = exp_b1[i].reshape(N_EXPERTS, 2 * D_FF // PAIR_BLOCK, LANES, 2).transpose(0, 1, 3, 2)
        xs = _moe(i, h2, logits, x1, mod, ln_g[i, 1].reshape(1, D_MODEL), ln_b[i, 1].reshape(1, D_MODEL),
                  exp_w1, b1p.reshape(N_EXPERTS, 1, 2 * D_FF), exp_w2, exp_b2[i][:, None, :])
        if col_major:
            lat = _lat_to_row_major(xs[:N_LAT])
            xs = lat if last else jnp.concatenate([lat, xs[N_LAT:]], axis=0)
    return xs[:N_LAT].reshape(BATCH, SEQ, D_MODEL)
```

```python
import functools

import jax
import jax.numpy as jnp
from jax import lax
from jax.experimental import pallas as pl
from jax.experimental.pallas import tpu as pltpu
from jax.experimental.pallas import tpu_sc as plsc

F32 = jnp.float32
BF16 = jnp.bfloat16
HIGHEST = lax.Precision.HIGHEST

D_MODEL = 1024
BATCH = 8
SEQ = 2048
CTX_LEN = 256
DEPTH = 4
GRID_W = 64
CHUNK = 64
ADA_CHUNKS = 6
N_LAT = BATCH * SEQ
N_CTX = BATCH * CTX_LEN
N_TOK = N_LAT + N_CTX
TILE = 256
LAT_TILES = N_LAT // TILE
ALL_TILES = N_TOK // TILE
TILES_PER_SEQ = SEQ // TILE
MOD_ROWS = 16
LANES = 128
SUBLANES = 8

N_EXPERTS = 32
TOP_K = 4
D_FF = D_MODEL
SWIGLU_LIMIT = 7.0
SWIGLU_ALPHA = 1.702
EXPERT_TILE = 512
N_SLOT_TILES = N_TOK * TOP_K // EXPERT_TILE + N_EXPERTS
N_SLOTS = N_SLOT_TILES * EXPERT_TILE

LN_EPS = 1e-5
RMS_EPS = 1e-6
ALPHA = (2.0 * DEPTH) ** 0.25

VMEM_LIMIT = 48 * 1024 * 1024
EXPERT_VMEM_LIMIT = (2 * 4 * (D_MODEL * 2 * D_FF + D_FF * D_MODEL) + 2 * (D_MODEL * 2 * D_FF + D_FF * D_MODEL)
                     + 4 * 4 * EXPERT_TILE * D_MODEL + 4 * EXPERT_TILE * (2 * D_FF + 2 * D_MODEL) + (4 << 20))


def _cparams(*sem):
    return pltpu.CompilerParams(dimension_semantics=sem, vmem_limit_bytes=VMEM_LIMIT)


def _sigmoid(x):
    return 1.0 / (1.0 + jnp.exp(-x))


def _silu(x):
    return x * _sigmoid(x)


def _softplus(x):
    return jnp.maximum(x, 0.0) + jnp.log(1.0 + jnp.exp(-jnp.abs(x)))


def _log_sigmoid(x):
    return -_softplus(-x)


def _nt(a, b):
    return lax.dot_general(a, b, (((1,), (1,)), ((), ())), preferred_element_type=F32)


def _tn(a, b):
    return lax.dot_general(a, b, (((0,), (0,)), ((), ())), preferred_element_type=F32)


def _mm(a, b):
    return jnp.dot(a, b, preferred_element_type=F32)


def _bf(x):
    return x.astype(BF16)


def _mod_row(t):
    return jnp.minimum(t // TILES_PER_SEQ, BATCH)


ADA_NBLK = 1536


def _ada_kernel(c_ref, w_ref, b_ref, o_ref):
    s = _silu(c_ref[...])
    o_ref[0] = jnp.dot(s, w_ref[0], precision=HIGHEST, preferred_element_type=F32) + b_ref[0]


def _ada_all(cc, ada_w, ada_b):
    n = ADA_CHUNKS * D_MODEL
    return pl.pallas_call(
        _ada_kernel,
        grid=(DEPTH, n // ADA_NBLK),
        in_specs=[
            pl.BlockSpec((MOD_ROWS, D_MODEL), lambda i, j: (0, 0)),
            pl.BlockSpec((1, D_MODEL, ADA_NBLK), lambda i, j: (i, 0, j)),
            pl.BlockSpec((1, 1, ADA_NBLK), lambda i, j: (i, 0, j)),
        ],
        out_specs=pl.BlockSpec((1, MOD_ROWS, ADA_NBLK), lambda i, j: (i, 0, j)),
        out_shape=jax.ShapeDtypeStruct((DEPTH, MOD_ROWS, n), F32),
        compiler_params=_cparams("parallel", "parallel"),
        name="ada",
    )(cc, ada_w, ada_b.reshape(DEPTH, 1, n))


def _modulated(x_ref, mod_ref, shift, scale):
    return x_ref[...] * (1.0 + mod_ref[0, scale:scale + 1, :]) + mod_ref[0, shift:shift + 1, :]


def _store_heads(ref, val, width):
    for h in range(val.shape[1] // width):
        ref[h] = val[:, h * width:(h + 1) * width]


def _hg_proj_kernel(x_ref, mod_ref, w_ref, lb_ref, q_ref, v_ref, gate_ref, kf_ref, gf_ref, kb_ref, gb_ref):
    h = _bf(_modulated(x_ref, mod_ref, 0, 1))
    d = D_MODEL
    _store_heads(q_ref, _silu(_mm(h, w_ref[:, 0:d])), LANES)
    _store_heads(v_ref, _mm(h, w_ref[:, d:2 * d]), LANES)
    gate_ref[...] = _mm(h, w_ref[:, 2 * d:3 * d])
    lb = lb_ref[...]
    for k_ref, g_ref, lo in ((kf_ref, gf_ref, 3 * d), (kb_ref, gb_ref, 4 * d)):
        f = _mm(h, w_ref[:, lo:lo + d])
        _store_heads(k_ref, (1.0 - lb) * _sigmoid(-f), LANES)
        _store_heads(g_ref, jnp.log(lb + (1.0 - lb) * _sigmoid(f)), LANES)


def _head_spec(n_heads, width):
    return pl.BlockSpec((n_heads, TILE, width), lambda t: (0, t, 0))


def _head_shape(n_heads, width):
    return jax.ShapeDtypeStruct((n_heads, N_TOK, width), F32)


def _tok_spec(width):
    return pl.BlockSpec((TILE, width), lambda t: (t, 0))


def _mod_spec():
    return pl.BlockSpec((1, ADA_CHUNKS, D_MODEL), lambda t: (_mod_row(t), 0, 0))


def _full_spec(shape):
    return pl.BlockSpec(shape, lambda t: (0,) * len(shape))


def _hg_proj(xs, mod, w_in, lb):
    n = w_in.shape[1]
    hs = _head_shape(8, LANES)
    return pl.pallas_call(
        _hg_proj_kernel,
        grid=(ALL_TILES,),
        in_specs=[_tok_spec(D_MODEL), _mod_spec(), _full_spec((D_MODEL, n)), _full_spec((1, D_MODEL))],
        out_specs=[_head_spec(8, LANES), _head_spec(8, LANES), _tok_spec(D_MODEL),
                   _head_spec(8, LANES), _head_spec(8, LANES), _head_spec(8, LANES), _head_spec(8, LANES)],
        out_shape=[hs, hs, jax.ShapeDtypeStruct((N_TOK, D_MODEL), F32), hs, hs, hs, hs],
        compiler_params=_cparams("parallel"),
        name="hg_proj",
    )(xs, mod, w_in, lb)


GLA_HEADS = 4
GLA_DK = 128
GLA_DV = 256
GLA_RANK = 16
GLA_GATE_NORM = 16.0


def _gla_proj_kernel(x_ref, mod_ref, w_ref, wlr_ref, gkw_ref, gkb_ref, q_ref, k_ref, v_ref, gate_ref, gf_ref, gb_ref):
    h = _bf(_modulated(x_ref, mod_ref, 0, 1))
    kd = GLA_HEADS * GLA_DK
    vd = GLA_HEADS * GLA_DV
    _store_heads(q_ref, _mm(h, w_ref[:, 0:kd]) * GLA_DK ** -0.5, GLA_DK)
    _store_heads(k_ref, _mm(h, w_ref[:, kd:2 * kd]), GLA_DK)
    _store_heads(v_ref, _mm(h, w_ref[:, 2 * kd:2 * kd + vd]), GLA_DV)
    gate_ref[...] = _mm(h, w_ref[:, 2 * kd + vd:2 * kd + 2 * vd])
    for z, g_ref in ((0, gf_ref), (1, gb_ref)):
        lr = _mm(h, wlr_ref[:, z * LANES:(z + 1) * LANES])
        gk = _mm(_bf(lr), gkw_ref[z]) + gkb_ref[z]
        _store_heads(g_ref, _log_sigmoid(gk) / GLA_GATE_NORM, GLA_DK)


def _gla_proj(xs, mod, w_main, w_lr, gk_w, gk_b):
    kd = GLA_HEADS * GLA_DK
    hk = _head_shape(GLA_HEADS, GLA_DK)
    return pl.pallas_call(
        _gla_proj_kernel,
        grid=(ALL_TILES,),
        in_specs=[_tok_spec(D_MODEL), _mod_spec(), _full_spec(w_main.shape), _full_spec(w_lr.shape),
                  _full_spec(gk_w.shape), _full_spec(gk_b.shape)],
        out_specs=[_head_spec(GLA_HEADS, GLA_DK), _head_spec(GLA_HEADS, GLA_DK), _head_spec(GLA_HEADS, GLA_DV),
                   _tok_spec(D_MODEL), _head_spec(GLA_HEADS, GLA_DK), _head_spec(GLA_HEADS, GLA_DK)],
        out_shape=[hk, hk, _head_shape(GLA_HEADS, GLA_DV), jax.ShapeDtypeStruct((N_TOK, D_MODEL), F32), hk, hk],
        compiler_params=_cparams("parallel"),
        name="gla_proj",
    )(xs, mod, w_main, w_lr, gk_w, gk_b)


DN_HEADS = 8
DN_QKV = 3 * D_MODEL
SHORT_CONV = 5
CHUNKS_PER_TILE = TILE // CHUNK


def _seg_scan(x, pos, axis, reverse):
    n = x.shape[axis]
    sh = 1
    while sh < CHUNK:
        if reverse:
            x = x + jnp.where(pos < CHUNK - sh, pltpu.roll(x, n - sh, axis), 0.0)
        else:
            x = x + jnp.where(pos >= sh, pltpu.roll(x, sh, axis), 0.0)
        sh *= 2
    return x


def _dn_gates(p, a_neg, dtb, idx, pos, axis):
    beta = _sigmoid(p)
    g = a_neg * _softplus(p + dtb)
    gp = _seg_scan(g, pos, axis, False)
    gs = _seg_scan(g, pos, axis, True)
    return jnp.where(idx < 2 * DN_HEADS, beta, jnp.where(idx < 3 * DN_HEADS, gp, gs))


def _dn_proj_kernel(x_ref, mod_ref, w_ref, wba_ref, wbat_ref, ac_ref, dc_ref, ar_ref, dr_ref,
                    qkv_ref, gate_ref, bgc_ref, bgr_ref):
    h = _bf(_modulated(x_ref, mod_ref, 0, 1))
    qkv_ref[...] = _mm(h, w_ref[:, 0:DN_QKV])
    gate_ref[...] = _mm(h, w_ref[:, DN_QKV:DN_QKV + D_MODEL])
    pc = _mm(h, wba_ref[...])
    lane = lax.broadcasted_iota(jnp.int32, pc.shape, 1)
    rowpos = lax.broadcasted_iota(jnp.int32, pc.shape, 0) % CHUNK
    bgc_ref[...] = _dn_gates(pc, ac_ref[...], dc_ref[...], lane, rowpos, 0)
    pr = _nt(wbat_ref[...], h)
    sub = lax.broadcasted_iota(jnp.int32, pr.shape, 0)
    lanepos = lax.broadcasted_iota(jnp.int32, pr.shape, 1) % CHUNK
    gr = _dn_gates(pr, ar_ref[...], dr_ref[...], sub, lanepos, 1)
    for c in range(CHUNKS_PER_TILE):
        bgr_ref[c] = gr[:, c * CHUNK:(c + 1) * CHUNK]


def _dn_proj(xs, mod, w_main, w_ba, w_bat, a_col, d_col, a_row, d_row):
    nb = 4 * DN_HEADS
    return pl.pallas_call(
        _dn_proj_kernel,
        grid=(ALL_TILES,),
        in_specs=[_tok_spec(D_MODEL), _mod_spec(), _full_spec(w_main.shape), _full_spec(w_ba.shape),
                  _full_spec(w_bat.shape), _full_spec(a_col.shape), _full_spec(d_col.shape),
                  _full_spec(a_row.shape), _full_spec(d_row.shape)],
        out_specs=[_tok_spec(DN_QKV), _tok_spec(D_MODEL), _tok_spec(LANES),
                   pl.BlockSpec((CHUNKS_PER_TILE, nb, CHUNK), lambda t: (t, 0, 0))],
        out_shape=[jax.ShapeDtypeStruct((N_TOK, DN_QKV), F32), jax.ShapeDtypeStruct((N_TOK, D_MODEL), F32),
                   jax.ShapeDtypeStruct((N_TOK, LANES), F32),
                   jax.ShapeDtypeStruct((N_TOK // CHUNK, nb, CHUNK), F32)],
        compiler_params=_cparams("parallel"),
        name="dn_proj",
    )(xs, mod, w_main, w_ba, w_bat, a_col, d_col, a_row, d_row)


HALO = SUBLANES


def _dn_conv_kernel(pm_ref, pp_ref, pn_ref, cw_ref, q_ref, k_ref, v_ref):
    t = pl.program_id(0)
    is_lat = t < LAT_TILES
    first = jnp.logical_or(jnp.logical_not(is_lat), t % TILES_PER_SEQ == 0)
    last = jnp.logical_or(jnp.logical_not(is_lat), t % TILES_PER_SEQ == TILES_PER_SEQ - 1)
    keep_prev = jnp.where(first, 0.0, 1.0)
    keep_next = jnp.where(last, 0.0, 1.0)
    n_ext = TILE + 2 * HALO
    pad = SHORT_CONV // 2
    for s in range(DN_QKV // LANES):
        cols = slice(s * LANES, (s + 1) * LANES)
        ext = jnp.concatenate([pp_ref[:, cols] * keep_prev, pm_ref[:, cols], pn_ref[:, cols] * keep_next], axis=0)
        acc = None
        for kk in range(SHORT_CONV):
            off = HALO - pad + kk
            tap = pltpu.roll(ext, n_ext - off, 0)[0:TILE] * cw_ref[kk:kk + 1, cols]
            acc = tap if acc is None else acc + tap
        a = _silu(acc)
        head = s % DN_HEADS
        if s < 2 * DN_HEADS:
            a = a * lax.rsqrt(jnp.sum(a * a, axis=1, keepdims=True) + 1e-6)
            if s < DN_HEADS:
                q_ref[head] = a * LANES ** -0.5
            else:
                k_ref[head] = a
        else:
            v_ref[head] = a


def _dn_conv(p_qkv, conv_w):
    rows8 = TILE // HALO
    last8 = N_TOK // HALO - 1
    hs = _head_shape(DN_HEADS, LANES)
    return pl.pallas_call(
        _dn_conv_kernel,
        grid=(ALL_TILES,),
        in_specs=[_tok_spec(DN_QKV),
                  pl.BlockSpec((HALO, DN_QKV), lambda t: (jnp.maximum(t * rows8 - 1, 0), 0)),
                  pl.BlockSpec((HALO, DN_QKV), lambda t: (jnp.minimum((t + 1) * rows8, last8), 0)),
                  _full_spec(conv_w.shape)],
        out_specs=[_head_spec(DN_HEADS, LANES)] * 3,
        out_shape=[hs, hs, hs],
        compiler_params=_cparams("parallel"),
        name="dn_conv",
    )(p_qkv, p_qkv, p_qkv, conv_w)


def _chunk_scan(x, row, reverse):
    sh = 1
    while sh < CHUNK:
        if reverse:
            x = x + jnp.where(row < CHUNK - sh, pltpu.roll(x, CHUNK - sh, 0), 0.0)
        else:
            x = x + jnp.where(row >= sh, pltpu.roll(x, sh, 0), 0.0)
        sh *= 2
    return x


def _gla_chunks(q, k, v, g, st, rev, row, r64, c64):
    n = range(len(q))
    cum = [_chunk_scan(g[i], row, rev[i]) for i in n]
    g_end = [cum[i][0:1, :] if rev[i] else cum[i][CHUNK - 1:CHUNK, :] for i in n]
    att = [jnp.where(r64 == c64, _nt(_bf(q[i]), _bf(k[i])), 0.0) for i in n]
    edge = list(cum)
    s = 1
    while s < CHUNK:
        late = (row & s) != 0
        early = jnp.logical_not(late)
        same = None
        if 2 * s < CHUNK:
            shift = (2 * s).bit_length() - 1
            same = (r64 >> shift) == (c64 >> shift)
        for i in n:
            if rev[i]:
                cut = jnp.where(late, edge[i], pltpu.roll(edge[i], CHUNK - s, 0))
            else:
                cut = jnp.where(late, pltpu.roll(edge[i], s, 0), edge[i])
            e = jnp.exp(-jnp.abs(cum[i] - cut))
            q_side = early if rev[i] else late
            a = _nt(_bf(jnp.where(q_side, q[i] * e, 0.0)), _bf(jnp.where(q_side, 0.0, k[i] * e)))
            if same is not None:
                a = jnp.where(same, a, 0.0)
            att[i] = att[i] + a
            if rev[i]:
                edge[i] = jnp.where(late, pltpu.roll(edge[i], s, 0), edge[i])
            else:
                edge[i] = jnp.where(late, edge[i], pltpu.roll(edge[i], CHUNK - s, 0))
        s *= 2
    st16 = [_bf(st[i]) for i in n]
    o = [_mm(_bf(att[i]), _bf(v[i])) + _nt(_bf(q[i] * jnp.exp(cum[i])), st16[i]) for i in n]
    st_new = [st[i] * jnp.exp(g_end[i]) + _tn(_bf(v[i]), _bf(k[i] * jnp.exp(g_end[i] - cum[i]))) for i in n]
    return o, st_new


def _scan_iotas():
    row = lax.broadcasted_iota(jnp.int32, (CHUNK, LANES), 0)
    r64 = lax.broadcasted_iota(jnp.int32, (CHUNK, CHUNK), 0)
    c64 = lax.broadcasted_iota(jnp.int32, (CHUNK, CHUNK), 1)
    return row, r64, c64


def _gla_scan_kernel(qf_ref, kf_ref, vf_ref, gf_ref, qb_ref, kb_ref, vb_ref, gb_ref,
                     of_ref, ob_ref, st_ref, *, n_heads):
    @pl.when(pl.program_id(1) == 0)
    def _():
        st_ref[...] = jnp.zeros_like(st_ref)

    row, r64, c64 = _scan_iotas()
    heads = range(n_heads)
    rev = [False] * n_heads + [True] * n_heads
    o, st = _gla_chunks(
        [qf_ref[h] for h in heads] + [qb_ref[h] for h in heads],
        [kf_ref[h] for h in heads] + [kb_ref[h] for h in heads],
        [vf_ref[h] for h in heads] + [vb_ref[h] for h in heads],
        [gf_ref[h] for h in heads] + [gb_ref[h] for h in heads],
        [st_ref[0, h] for h in heads] + [st_ref[1, h] for h in heads],
        rev, row, r64, c64)
    for h in heads:
        of_ref[h] = o[h]
        ob_ref[h] = o[n_heads + h]
        st_ref[0, h] = st[h]
        st_ref[1, h] = st[n_heads + h]


CTX_CHUNKS = CTX_LEN // CHUNK
LAT_CHUNKS = SEQ // CHUNK
SCAN_STEPS = CTX_CHUNKS + LAT_CHUNKS


def _fwd_chunk(b, n):
    return jnp.where(n < CTX_CHUNKS, N_LAT // CHUNK + b * CTX_CHUNKS + n, b * LAT_CHUNKS + n - CTX_CHUNKS)


def _bwd_chunk(b, n):
    return jnp.where(n < CTX_CHUNKS, N_LAT // CHUNK + b * CTX_CHUNKS + CTX_CHUNKS - 1 - n,
                     b * LAT_CHUNKS + SCAN_STEPS - 1 - n)


def _gla_scan(q, k_f, k_b, v, g_f, g_b):
    n_heads, _, dk = q.shape
    dv = v.shape[2]
    fwd = lambda b, n: (0, _fwd_chunk(b, n), 0)
    bwd = lambda b, n: (0, _bwd_chunk(b, n), 0)
    kspec = lambda m: pl.BlockSpec((n_heads, CHUNK, dk), m)
    vspec = lambda m: pl.BlockSpec((n_heads, CHUNK, dv), m)
    oshape = jax.ShapeDtypeStruct((n_heads, N_TOK, dv), F32)
    return pl.pallas_call(
        functools.partial(_gla_scan_kernel, n_heads=n_heads),
        grid=(BATCH, SCAN_STEPS),
        in_specs=[kspec(fwd), kspec(fwd), vspec(fwd), kspec(fwd), kspec(bwd), kspec(bwd), vspec(bwd), kspec(bwd)],
        out_specs=[vspec(fwd), vspec(bwd)],
        out_shape=[oshape, oshape],
        scratch_shapes=[pltpu.VMEM((2, n_heads, dv, dk), F32)],
        compiler_params=_cparams("parallel", "arbitrary"),
        name="gla_scan",
    )(q, k_f, v, g_f, q, k_b, v, g_b)


def _lane_pick(x, lane, idx):
    return jnp.sum(jnp.where(lane == idx, x, 0.0), axis=1, keepdims=True)


def _dn_chunks(q, k, v, beta, g_col, g_row, st, rev, r64, c64):
    n = range(len(q))
    dk = k[0].shape[1]
    g_end = [g_row[i][:, 0:1] if rev[i] else g_row[i][:, CHUNK - 1:CHUNK] for i in n]
    causal = {False: r64 >= c64, True: r64 <= c64}
    gam = [jnp.where(causal[rev[i]], jnp.exp(jnp.minimum(g_col[i] - g_row[i], 0.0)), 0.0) for i in n]
    kb = [k[i] * beta[i] for i in n]
    k16 = [_bf(k[i]) for i in n]
    m = [jnp.where(r64 == c64, 0.0, _nt(_bf(kb[i]), k16[i]) * gam[i]) for i in n]
    att = [_nt(_bf(q[i]), k16[i]) * gam[i] for i in n]
    y = None
    s = 1
    while s < CHUNK:
        shift = (2 * s).bit_length() - 1
        rlate = (r64 & s) != 0
        clate = (c64 & s) != 0
        pair = {False: jnp.logical_and(rlate, jnp.logical_not(clate)), True: jnp.logical_and(clate, jnp.logical_not(rlate))}
        if 2 * s < CHUNK:
            same = (r64 >> shift) == (c64 >> shift)
            pair = {d: jnp.logical_and(p, same) for d, p in pair.items()}
        c = [jnp.where(pair[rev[i]], m[i], 0.0) for i in n]
        if y is None:
            y = [-c[i] for i in n]
        else:
            c16 = [_bf(c[i]) for i in n]
            y16 = [_bf(y[i]) for i in n]
            p = [c[i] + _mm(y16[i], c16[i]) for i in n]
            y = [y[i] - (p[i] + _mm(_bf(p[i]), y16[i])) for i in n]
        s *= 2
    e_col = [jnp.exp(g_col[i]) for i in n]
    rhs = [jnp.concatenate([kb[i] * e_col[i], v[i] * beta[i]], axis=1) for i in n]
    wu = [rhs[i] + _mm(_bf(y[i]), _bf(rhs[i])) for i in n]
    st16 = [_bf(st[i]) for i in n]
    v_new = [wu[i][:, dk:] - _nt(_bf(wu[i][:, :dk]), st16[i]) for i in n]
    o = [_nt(_bf(q[i] * e_col[i]), st16[i]) + _mm(_bf(att[i]), _bf(v_new[i])) for i in n]
    st_new = [st[i] * jnp.exp(g_end[i]) + _tn(_bf(v_new[i]), _bf(k[i] * jnp.exp(g_end[i] - g_col[i]))) for i in n]
    return o, st_new


def _dn_scan_kernel(qf_ref, kf_ref, vf_ref, cf_ref, rf_ref, qb_ref, kb_ref, vb_ref, cb_ref, rb_ref,
                    of_ref, ob_ref, st_ref):
    @pl.when(pl.program_id(1) == 0)
    def _():
        st_ref[...] = jnp.zeros_like(st_ref)

    _, r64, c64 = _scan_iotas()
    cf = cf_ref[...]
    cb = cb_ref[...]
    col = lambda x, i: x[:, i:i + 1]
    heads = range(DN_HEADS)
    rev = [False] * DN_HEADS + [True] * DN_HEADS
    o, st = _dn_chunks(
        [qf_ref[h] for h in heads] + [qb_ref[h] for h in heads],
        [kf_ref[h] for h in heads] + [kb_ref[h] for h in heads],
        [vf_ref[h] for h in heads] + [vb_ref[h] for h in heads],
        [col(cf, h) for h in heads] + [col(cb, DN_HEADS + h) for h in heads],
        [col(cf, 2 * DN_HEADS + h) for h in heads] + [col(cb, 3 * DN_HEADS + h) for h in heads],
        [rf_ref[0, 2 * DN_HEADS + h:2 * DN_HEADS + h + 1, :] for h in heads]
        + [rb_ref[0, 3 * DN_HEADS + h:3 * DN_HEADS + h + 1, :] for h in heads],
        [st_ref[0, h] for h in heads] + [st_ref[1, h] for h in heads],
        rev, r64, c64)
    for h in heads:
        of_ref[h] = o[h]
        ob_ref[h] = o[DN_HEADS + h]
        st_ref[0, h] = st[h]
        st_ref[1, h] = st[DN_HEADS + h]


def _dn_scan(q, k, v, bgc, bgr):
    fwd = lambda b, n: (0, _fwd_chunk(b, n), 0)
    bwd = lambda b, n: (0, _bwd_chunk(b, n), 0)
    fwd2 = lambda b, n: (_fwd_chunk(b, n), 0)
    bwd2 = lambda b, n: (_bwd_chunk(b, n), 0)
    fwd3 = lambda b, n: (_fwd_chunk(b, n), 0, 0)
    bwd3 = lambda b, n: (_bwd_chunk(b, n), 0, 0)
    hspec = lambda m: pl.BlockSpec((DN_HEADS, CHUNK, LANES), m)
    cspec = lambda m: pl.BlockSpec((CHUNK, LANES), m)
    rspec = lambda m: pl.BlockSpec((1, 4 * DN_HEADS, CHUNK), m)
    oshape = _head_shape(DN_HEADS, LANES)
    return pl.pallas_call(
        _dn_scan_kernel,
        grid=(BATCH, SCAN_STEPS),
        in_specs=[hspec(fwd), hspec(fwd), hspec(fwd), cspec(fwd2), rspec(fwd3),
                  hspec(bwd), hspec(bwd), hspec(bwd), cspec(bwd2), rspec(bwd3)],
        out_specs=[hspec(fwd), hspec(bwd)],
        out_shape=[oshape, oshape],
        scratch_shapes=[pltpu.VMEM((2, DN_HEADS, LANES, LANES), F32)],
        compiler_params=_cparams("parallel", "arbitrary"),
        name="dn_scan",
    )(q, k, v, bgc, bgr, q, k, v, bgc, bgr)


def _layer_norm(z, g, b):
    mu = jnp.mean(z, axis=1, keepdims=True)
    zc = z - mu
    var = jnp.mean(zc * zc, axis=1, keepdims=True)
    return zc * lax.rsqrt(var + LN_EPS) * g + b


def _post_kernel(of_ref, ob_ref, gate_ref, x_ref, mod_ref, nw_ref, wo_ref, lg_ref, lb_ref, rw_ref, rb_ref,
                 x1_ref, h2_ref, logit_ref, *, n_heads):
    parts = []
    for h in range(n_heads):
        o = of_ref[h] + ob_ref[h]
        o = o * lax.rsqrt(jnp.mean(o * o, axis=1, keepdims=True) + RMS_EPS) * nw_ref[...]
        parts.append(o)
    on = jnp.concatenate(parts, axis=1) * _silu(gate_ref[...])
    y = _mm(_bf(on), wo_ref[...])
    z = ALPHA * x_ref[...] + mod_ref[0, 2:3, :] * y
    x1 = _layer_norm(z, lg_ref[...], lb_ref[...])
    x1_ref[...] = x1
    h2 = x1 * (1.0 + mod_ref[0, 4:5, :]) + mod_ref[0, 3:4, :]
    h2_ref[...] = h2
    logit_ref[...] = jnp.dot(h2, rw_ref[...], precision=HIGHEST, preferred_element_type=F32) + rb_ref[...]


def _post(o_f, o_b, gate, xs, mod, norm_w, w_out, ln_g, ln_b, rw, rb, n_tiles):
    n_heads, _, dv = o_f.shape
    n_rows = n_tiles * TILE
    tok = lambda w: jax.ShapeDtypeStruct((n_rows, w), F32)
    return pl.pallas_call(
        functools.partial(_post_kernel, n_heads=n_heads),
        grid=(n_tiles,),
        in_specs=[_head_spec(n_heads, dv), _head_spec(n_heads, dv), _tok_spec(D_MODEL), _tok_spec(D_MODEL), _mod_spec(),
                  _full_spec(norm_w.shape), _full_spec(w_out.shape), _full_spec(ln_g.shape), _full_spec(ln_b.shape),
                  _full_spec(rw.shape), _full_spec(rb.shape)],
        out_specs=[_tok_spec(D_MODEL), _tok_spec(D_MODEL), _tok_spec(LANES)],
        out_shape=[tok(D_MODEL), tok(D_MODEL), tok(LANES)],
        compiler_params=_cparams("parallel"),
        name="post_mixer",
    )(o_f, o_b, gate, xs, mod, norm_w, w_out, ln_g, ln_b, rw, rb)


def _route_kernel(logit_ref, route_ref, count_ref, run_ref):
    t = pl.program_id(0)

    @pl.when(t == 0)
    def _():
        run_ref[...] = jnp.zeros_like(run_ref)

    lane = lax.broadcasted_iota(jnp.int32, (TILE, LANES), 1)
    lanef = lane.astype(F32)
    neg = jnp.float32(-jnp.inf)
    l = jnp.where(lane < N_EXPERTS, logit_ref[...], neg)
    vals, picks, hots = [], [], []
    for _ in range(TOP_K):
        m = jnp.max(l, axis=1, keepdims=True)
        pick = jnp.min(jnp.where(l == m, lanef, float(LANES)), axis=1, keepdims=True)
        hot = lanef == pick
        l = jnp.where(hot, neg, l)
        vals.append(m)
        picks.append(pick)
        hots.append(hot)
    es = [jnp.exp(v - vals[0]) for v in vals]
    denom = es[0] + es[1] + es[2] + es[3]
    onehot = jnp.zeros((TILE, LANES), F32)
    for hot in hots:
        onehot = onehot + jnp.where(hot, 1.0, 0.0)
    r = lax.broadcasted_iota(jnp.int32, (TILE, TILE), 0)
    c = lax.broadcasted_iota(jnp.int32, (TILE, TILE), 1)
    before = jnp.where(c < r, 1.0, 0.0).astype(BF16)
    run = run_ref[0:1, :]
    excl = _mm(before, _bf(onehot)) + run
    out = jnp.zeros((TILE, LANES), F32)
    for kk in range(TOP_K):
        rank = jnp.sum(jnp.where(hots[kk], excl, 0.0), axis=1, keepdims=True)
        out = jnp.where(lane == kk, picks[kk], out)
        out = jnp.where(lane == TOP_K + kk, rank, out)
        out = jnp.where(lane == 2 * TOP_K + kk, es[kk] / denom, out)
    route_ref[...] = out
    new_run = run + jnp.sum(onehot, axis=0, keepdims=True)
    run_ref[...] = jnp.broadcast_to(new_run, run_ref.shape)
    count_ref[...] = jnp.broadcast_to(new_run, count_ref.shape)


def _route(logits):
    n_tiles = logits.shape[0] // TILE
    return pl.pallas_call(
        _route_kernel,
        grid=(n_tiles,),
        in_specs=[_tok_spec(LANES)],
        out_specs=[_tok_spec(LANES), pl.BlockSpec((SUBLANES, LANES), lambda t: (0, 0))],
        out_shape=[jax.ShapeDtypeStruct(logits.shape, F32), jax.ShapeDtypeStruct((SUBLANES, LANES), F32)],
        scratch_shapes=[pltpu.VMEM((SUBLANES, LANES), F32)],
        compiler_params=_cparams("arbitrary"),
        name="route",
    )(logits)


GATHER_ROWS = 16
GATHER_BUFS = 4


def _sc_gather(table, idx):
    n_rows = idx.shape[0]
    d = table.shape[1]
    info = plsc.get_sparse_core_info()
    n_workers = info.num_cores * info.num_subcores
    rows_per_worker = n_rows // n_workers
    n_chunks = rows_per_worker // GATHER_ROWS
    assert rows_per_worker * n_workers == n_rows and n_chunks * GATHER_ROWS == rows_per_worker
    assert n_chunks % GATHER_BUFS == 0
    mesh = plsc.VectorSubcoreMesh(core_axis_name="c", subcore_axis_name="s")
    scratch = ([pltpu.VMEM((GATHER_ROWS,), jnp.int32)] * GATHER_BUFS
               + [pltpu.VMEM((GATHER_ROWS, d), table.dtype)] * GATHER_BUFS
               + [pltpu.SemaphoreType.DMA] * (2 * GATHER_BUFS))

    def body(table_hbm, idx_hbm, out_hbm, *scr):
        idx_v, rows_v = scr[:GATHER_BUFS], scr[GATHER_BUFS:2 * GATHER_BUFS]
        gather_sem, write_sem = scr[2 * GATHER_BUFS:3 * GATHER_BUFS], scr[3 * GATHER_BUFS:]
        worker = lax.axis_index("s") * info.num_cores + lax.axis_index("c")
        base = worker * rows_per_worker

        @pl.loop(0, n_chunks, step=GATHER_BUFS)
        def _(c0):
            offs = [pl.multiple_of(base + (c0 + b) * GATHER_ROWS, SUBLANES) for b in range(GATHER_BUFS)]
            gathers = []
            for b in range(GATHER_BUFS):
                pltpu.sync_copy(idx_hbm.at[pl.ds(offs[b], GATHER_ROWS)], idx_v[b])
                gathers.append(pltpu.async_copy(table_hbm.at[idx_v[b]], rows_v[b], gather_sem[b]))
            writes = []
            for b in range(GATHER_BUFS):
                gathers[b].wait()
                writes.append(pltpu.async_copy(rows_v[b], out_hbm.at[pl.ds(offs[b], GATHER_ROWS)], write_sem[b]))
            for w in writes:
                w.wait()

    return pl.kernel(body, out_type=jax.ShapeDtypeStruct((n_rows, d), table.dtype), mesh=mesh,
                     scratch_types=scratch, name="sc_gather")(table, idx)


def _sc_scatter(src, idx, n_out):
    n, d = src.shape
    info = plsc.get_sparse_core_info()
    n_workers = info.num_cores * info.num_subcores
    rows_per_worker = n // n_workers
    n_chunks = rows_per_worker // GATHER_ROWS
    assert rows_per_worker * n_workers == n and n_chunks * GATHER_ROWS == rows_per_worker
    assert n_chunks % GATHER_BUFS == 0
    mesh = plsc.VectorSubcoreMesh(core_axis_name="c", subcore_axis_name="s")
    scratch = ([pltpu.VMEM((GATHER_ROWS,), jnp.int32)] * (GATHER_BUFS * TOP_K)
               + [pltpu.VMEM((GATHER_ROWS, d), src.dtype)] * GATHER_BUFS + [pltpu.SemaphoreType.DMA] * GATHER_BUFS)

    def body(src_hbm, idx_hbm, out_hbm, *scr):
        idx_v = scr[:GATHER_BUFS * TOP_K]
        rows_v = scr[GATHER_BUFS * TOP_K:GATHER_BUFS * TOP_K + GATHER_BUFS]
        sem = scr[GATHER_BUFS * TOP_K + GATHER_BUFS:]
        worker = lax.axis_index("s") * info.num_cores + lax.axis_index("c")
        base = worker * rows_per_worker

        @pl.loop(0, n_chunks, step=GATHER_BUFS)
        def _(c0):
            copies = []
            for b in range(GATHER_BUFS):
                off = pl.multiple_of(base + (c0 + b) * GATHER_ROWS, SUBLANES)
                pltpu.sync_copy(src_hbm.at[pl.ds(off, GATHER_ROWS)], rows_v[b])
                for kk in range(TOP_K):
                    pltpu.sync_copy(idx_hbm.at[pl.ds(pl.multiple_of(kk * n + off, SUBLANES), GATHER_ROWS)],
                                    idx_v[b * TOP_K + kk])
                for kk in range(TOP_K):
                    copies.append(pltpu.async_copy(rows_v[b], out_hbm.at[idx_v[b * TOP_K + kk]], sem[b]))
            for c in copies:
                c.wait()

    return pl.kernel(body, out_type=jax.ShapeDtypeStruct((n_out, d), src.dtype), mesh=mesh,
                     scratch_types=scratch, name="sc_scatter")(src, idx)


PAIR_BLOCK = 2 * LANES


def _expert_kernel(te_ref, tv_ref, nt_ref, xs_ref, w1_ref, b1_ref, w2_ref, b2_ref, ys_ref, w1p_ref, w2b_ref):
    i = pl.program_id(0)
    valid = i < nt_ref[0]
    new_expert = jnp.logical_or(i == 0, te_ref[i] != te_ref[jnp.maximum(i - 1, 0)])

    @pl.when(jnp.logical_and(valid, new_expert))
    def _():
        r = lax.broadcasted_iota(jnp.int32, (PAIR_BLOCK, PAIR_BLOCK), 0)
        c = lax.broadcasted_iota(jnp.int32, (PAIR_BLOCK, PAIR_BLOCK), 1)
        src = jnp.where(c < LANES, 2 * c, 2 * (c - LANES) + 1)
        perm = jnp.where(r == src, 1.0, 0.0).astype(BF16)
        for b in range(2 * D_FF // PAIR_BLOCK):
            cols = slice(b * PAIR_BLOCK, (b + 1) * PAIR_BLOCK)
            w1p_ref[:, cols] = _bf(_mm(_bf(w1_ref[0, 0, :, cols]), perm))
        w2b_ref[...] = _bf(w2_ref[0, 0])

    @pl.when(valid)
    def _():
        rowi = lax.broadcasted_iota(jnp.int32, (EXPERT_TILE, D_MODEL), 0)
        x = _bf(jnp.where(rowi < tv_ref[i], xs_ref[...], 0.0))
        u = _mm(x, w1p_ref[...]) + b1_ref[0]
        parts = []
        for b in range(2 * D_FF // PAIR_BLOCK):
            glu = jnp.minimum(u[:, b * PAIR_BLOCK:b * PAIR_BLOCK + LANES], SWIGLU_LIMIT)
            lin = jnp.clip(u[:, b * PAIR_BLOCK + LANES:(b + 1) * PAIR_BLOCK], -SWIGLU_LIMIT, SWIGLU_LIMIT)
            parts.append(glu * _sigmoid(SWIGLU_ALPHA * glu) * (lin + 1.0))
        act = jnp.concatenate(parts, axis=1)
        ys_ref[...] = _mm(_bf(act), w2b_ref[...]) + b2_ref[0]


def _experts(layer, tile_expert, tile_valid, n_tiles, xs, w1, b1p, w2, b2):
    tile_map = lambda i, te, tv, nt: (jnp.minimum(i, nt[0] - 1), 0)
    wmap = lambda i, te, tv, nt: (te[i], 0, 0)
    lmap = lambda i, te, tv, nt: (layer, te[i], 0, 0)
    grid_spec = pltpu.PrefetchScalarGridSpec(
        num_scalar_prefetch=3,
        grid=(N_SLOT_TILES,),
        in_specs=[pl.BlockSpec((EXPERT_TILE, D_MODEL), tile_map),
                  pl.BlockSpec((1, 1, D_MODEL, 2 * D_FF), lmap), pl.BlockSpec((1, 1, 2 * D_FF), wmap),
                  pl.BlockSpec((1, 1, D_FF, D_MODEL), lmap), pl.BlockSpec((1, 1, D_MODEL), wmap)],
        out_specs=pl.BlockSpec((EXPERT_TILE, D_MODEL), tile_map),
        scratch_shapes=[pltpu.VMEM((D_MODEL, 2 * D_FF), BF16), pltpu.VMEM((D_FF, D_MODEL), BF16)],
    )
    return pl.pallas_call(
        _expert_kernel,
        grid_spec=grid_spec,
        out_shape=jax.ShapeDtypeStruct((N_SLOTS, D_MODEL), F32),
        compiler_params=pltpu.CompilerParams(dimension_semantics=("arbitrary",), vmem_limit_bytes=EXPERT_VMEM_LIMIT),
        name="experts",
    )(tile_expert, tile_valid, n_tiles, xs, w1, b1p, w2, b2)


def _combine_kernel(y_ref, route_ref, x1_ref, mod_ref, lg_ref, lb_ref, x2_ref):
    route = route_ref[...]
    lane = lax.broadcasted_iota(jnp.int32, route.shape, 1)
    f = None
    for kk in range(TOP_K):
        term = _lane_pick(route, lane, 2 * TOP_K + kk) * y_ref[kk]
        f = term if f is None else f + term
    z = ALPHA * x1_ref[...] + mod_ref[0, 5:6, :] * f
    x2_ref[...] = _layer_norm(z, lg_ref[...], lb_ref[...])


def _combine(y_tok, route, x1, mod, ln_g, ln_b):
    n_tiles = x1.shape[0] // TILE
    return pl.pallas_call(
        _combine_kernel,
        grid=(n_tiles,),
        in_specs=[pl.BlockSpec((TOP_K, TILE, D_MODEL), lambda t: (0, t, 0)), _tok_spec(LANES), _tok_spec(D_MODEL),
                  _mod_spec(), _full_spec(ln_g.shape), _full_spec(ln_b.shape)],
        out_specs=_tok_spec(D_MODEL),
        out_shape=jax.ShapeDtypeStruct(x1.shape, F32),
        compiler_params=_cparams("parallel"),
        name="combine",
    )(y_tok, route, x1, mod, ln_g, ln_b)


def _moe(layer, h2, logits, x1, mod, ln_g, ln_b, w1, b1p, w2, b2):
    n_tok = h2.shape[0]
    route, counts = _route(logits)
    cnt = counts[0, :N_EXPERTS].astype(jnp.int32)
    padded = (cnt + EXPERT_TILE - 1) // EXPERT_TILE * EXPERT_TILE
    ends = jnp.cumsum(padded)
    starts = ends - padded
    pick = route[:, 0:TOP_K].astype(jnp.int32)
    rank = route[:, TOP_K:2 * TOP_K].astype(jnp.int32)
    slot = starts[pick] + rank
    slot_by_choice = slot.T.reshape(-1)
    tile_start = jnp.arange(N_SLOT_TILES, dtype=jnp.int32) * EXPERT_TILE
    tile_expert = jnp.minimum(jnp.sum(tile_start[:, None] >= ends[None, :], axis=1), N_EXPERTS - 1).astype(jnp.int32)
    tile_valid = jnp.clip(starts[tile_expert] + cnt[tile_expert] - tile_start, 0, EXPERT_TILE).astype(jnp.int32)
    n_tiles = (ends[N_EXPERTS - 1:] // EXPERT_TILE).astype(jnp.int32)
    xs = _sc_scatter(h2, slot_by_choice, N_SLOTS)
    ys = _experts(layer, tile_expert, tile_valid, n_tiles, xs, w1, b1p, w2, b2)
    y_tok = _sc_gather(ys, slot_by_choice).reshape(TOP_K, n_tok, D_MODEL)
    return _combine(y_tok, route, x1, mod, ln_g, ln_b)


def _to_col_major(xs):
    lat = xs[:N_LAT].reshape(BATCH, SEQ // GRID_W, GRID_W, D_MODEL).transpose(0, 2, 1, 3).reshape(N_LAT, D_MODEL)
    return jnp.concatenate([lat, xs[N_LAT:]], axis=0)


def _lat_to_row_major(lat):
    return lat.reshape(BATCH, GRID_W, SEQ // GRID_W, D_MODEL).transpose(0, 2, 1, 3).reshape(N_LAT, D_MODEL)


def kernel(x, c, ctx, c_ctx, ada_w, ada_b, ln_g, ln_b, dn_w_in, dn_conv_w, dn_a_log, dn_dt_bias, dn_norm_w, dn_w_out,
           hg_w_in, hg_lower_bound, hg_norm_w, hg_w_out, gla_w_in, gla_gk_w, gla_gk_b, gla_norm_w, gla_w_out,
           router_w, router_b, exp_w1, exp_b1, exp_w2, exp_b2):
    xs = jnp.concatenate([x.reshape(N_LAT, D_MODEL), ctx.reshape(N_CTX, D_MODEL)], axis=0)
    cc = jnp.zeros((MOD_ROWS, D_MODEL), F32).at[:BATCH].set(c).at[BATCH].set(c_ctx)
    mods = _ada_all(cc, ada_w, ada_b).reshape(DEPTH, MOD_ROWS, ADA_CHUNKS, D_MODEL)
    lb_soft = jax.nn.softmax(hg_lower_bound.astype(F32), axis=0)
    lower_bounds = jnp.cumsum(lb_soft, axis=0) - lb_soft[0]

    for i in range(DEPTH):
        last = i == DEPTH - 1
        kind, j = i % 3, i // 3
        mod = mods[i]
        col_major = i % 2 == 1
        if col_major:
            xs = _to_col_major(xs)

        if kind == 0:
            nq = 2 * D_MODEL + 2 * D_MODEL
            w = dn_w_in[j]
            w_ba = jnp.zeros((D_MODEL, LANES), F32).at[:, :4 * DN_HEADS].set(w[:, nq:])
            a_neg = jnp.zeros((LANES,), F32).at[2 * DN_HEADS:4 * DN_HEADS].set(-jnp.exp(dn_a_log[j].astype(F32)).reshape(-1))
            dtb = jnp.zeros((LANES,), F32).at[2 * DN_HEADS:4 * DN_HEADS].set(dn_dt_bias[j].astype(F32).reshape(-1))
            nb = 4 * DN_HEADS
            p_qkv, gate, bgc, bgr = _dn_proj(xs, mod, _bf(w[:, :nq]), _bf(w_ba), _bf(w[:, nq:].T),
                                            a_neg.reshape(1, LANES), dtb.reshape(1, LANES),
                                            a_neg[:nb].reshape(nb, 1), dtb[:nb].reshape(nb, 1))
            cw = jnp.zeros((SUBLANES, DN_QKV), F32).at[:SHORT_CONV].set(dn_conv_w[j])
            q, k, v = _dn_conv(p_qkv, cw)
            o_f, o_b = _dn_scan(q, k, v, bgc, bgr)
            norm_w, w_out = dn_norm_w[j], dn_w_out[j]
        elif kind == 1:
            q, v, gate, k_f, g_f, k_b, g_b = _hg_proj(xs, mod, _bf(hg_w_in[j]), lower_bounds[i].reshape(1, D_MODEL))
            o_f, o_b = _gla_scan(q, k_f, k_b, v, g_f, g_b)
            norm_w, w_out = hg_norm_w[j], hg_w_out[j]
        else:
            kd = GLA_HEADS * GLA_DK
            n_main = 2 * kd + 2 * D_MODEL
            w = gla_w_in[j]
            w_lr = jnp.zeros((D_MODEL, 2 * LANES), F32)
            w_lr = w_lr.at[:, :GLA_RANK].set(w[:, n_main:n_main + GLA_RANK])
            w_lr = w_lr.at[:, LANES:LANES + GLA_RANK].set(w[:, n_main + GLA_RANK:])
            gk_w = jnp.zeros((2, LANES, kd), F32).at[:, :GLA_RANK].set(gla_gk_w[j])
            q, k, v, gate, g_f, g_b = _gla_proj(xs, mod, _bf(w[:, :n_main]), _bf(w_lr), _bf(gk_w),
                                                gla_gk_b[j].reshape(2, 1, kd))
            o_f, o_b = _gla_scan(q, k, k, v, g_f, g_b)
            norm_w, w_out = gla_norm_w[j], gla_w_out[j]

        n_tiles = LAT_TILES if last else ALL_TILES
        rw = jnp.zeros((D_MODEL, LANES), F32).at[:, :N_EXPERTS].set(router_w[i])
        rb = jnp.zeros((1, LANES), F32).at[0, :N_EXPERTS].set(router_b[i])
        x1, h2, logits = _post(o_f, o_b, gate, xs, mod, norm_w.reshape(1, -1), _bf(w_out),
                               ln_g[i, 0].reshape(1, D_MODEL), ln_b[i, 0].reshape(1, D_MODEL), rw, rb, n_tiles)
        b1p = exp_b1[i].reshape(N_EXPERTS, 2 * D_FF // PAIR_BLOCK, LANES, 2).transpose(0, 1, 3, 2)
        xs = _moe(i, h2, logits, x1, mod, ln_g[i, 1].reshape(1, D_MODEL), ln_b[i, 1].reshape(1, D_MODEL),
                  exp_w1, b1p.reshape(N_EXPERTS, 1, 2 * D_FF), exp_w2, exp_b2[i][:, None, :])
        if col_major:
            lat = _lat_to_row_major(xs[:N_LAT])
            xs = lat if last else jnp.concatenate([lat, xs[N_LAT:]], axis=0)
    return xs[:N_LAT].reshape(BATCH, SEQ, D_MODEL)
```

```python
import functools

import jax
import jax.numpy as jnp
from jax import lax
from jax.experimental import pallas as pl
from jax.experimental.pallas import tpu as pltpu
from jax.experimental.pallas import tpu_sc as plsc

F32 = jnp.float32
BF16 = jnp.bfloat16
HIGHEST = lax.Precision.HIGHEST

D_MODEL = 1024
BATCH = 8
SEQ = 2048
CTX_LEN = 256
DEPTH = 4
GRID_W = 64
CHUNK = 64
ADA_CHUNKS = 6
N_LAT = BATCH * SEQ
N_CTX = BATCH * CTX_LEN
N_TOK = N_LAT + N_CTX
TILE = 256
LAT_TILES = N_LAT // TILE
ALL_TILES = N_TOK // TILE
TILES_PER_SEQ = SEQ // TILE
MOD_ROWS = 16
LANES = 128
SUBLANES = 8

N_EXPERTS = 32
TOP_K = 4
D_FF = D_MODEL
SWIGLU_LIMIT = 7.0
SWIGLU_ALPHA = 1.702
EXPERT_TILE = 512
N_SLOT_TILES = N_TOK * TOP_K // EXPERT_TILE + N_EXPERTS
N_SLOTS = N_SLOT_TILES * EXPERT_TILE

LN_EPS = 1e-5
RMS_EPS = 1e-6
ALPHA = (2.0 * DEPTH) ** 0.25

VMEM_LIMIT = 48 * 1024 * 1024
EXPERT_VMEM_LIMIT = (2 * 4 * (D_MODEL * 2 * D_FF + D_FF * D_MODEL) + 2 * (D_MODEL * 2 * D_FF + D_FF * D_MODEL)
                     + 4 * 4 * EXPERT_TILE * D_MODEL + 4 * EXPERT_TILE * (2 * D_FF + 2 * D_MODEL) + (4 << 20))


def _cparams(*sem):
    return pltpu.CompilerParams(dimension_semantics=sem, vmem_limit_bytes=VMEM_LIMIT)


def _sigmoid(x):
    return 1.0 / (1.0 + jnp.exp(-x))


def _silu(x):
    return x * _sigmoid(x)


def _softplus(x):
    return jnp.maximum(x, 0.0) + jnp.log(1.0 + jnp.exp(-jnp.abs(x)))


def _log_sigmoid(x):
    return -_softplus(-x)


def _nt(a, b):
    return lax.dot_general(a, b, (((1,), (1,)), ((), ())), preferred_element_type=F32)


def _tn(a, b):
    return lax.dot_general(a, b, (((0,), (0,)), ((), ())), preferred_element_type=F32)


def _mm(a, b):
    return jnp.dot(a, b, preferred_element_type=F32)


def _bf(x):
    return x.astype(BF16)


def _mod_row(t):
    return jnp.minimum(t // TILES_PER_SEQ, BATCH)


ADA_NBLK = 1536


def _ada_kernel(c_ref, w_ref, b_ref, o_ref):
    s = _silu(c_ref[...])
    o_ref[0] = jnp.dot(s, w_ref[0], precision=HIGHEST, preferred_element_type=F32) + b_ref[0]


def _ada_all(cc, ada_w, ada_b):
    n = ADA_CHUNKS * D_MODEL
    return pl.pallas_call(
        _ada_kernel,
        grid=(DEPTH, n // ADA_NBLK),
        in_specs=[
            pl.BlockSpec((MOD_ROWS, D_MODEL), lambda i, j: (0, 0)),
            pl.BlockSpec((1, D_MODEL, ADA_NBLK), lambda i, j: (i, 0, j)),
            pl.BlockSpec((1, 1, ADA_NBLK), lambda i, j: (i, 0, j)),
        ],
        out_specs=pl.BlockSpec((1, MOD_ROWS, ADA_NBLK), lambda i, j: (i, 0, j)),
        out_shape=jax.ShapeDtypeStruct((DEPTH, MOD_ROWS, n), F32),
        compiler_params=_cparams("parallel", "parallel"),
        name="ada",
    )(cc, ada_w, ada_b.reshape(DEPTH, 1, n))


def _stream_rows(xl_ref, xc_ref):
    return jnp.where(pl.program_id(0) < LAT_TILES, xl_ref[...], xc_ref[...])


def _stream_specs():
    return [pl.BlockSpec((TILE, D_MODEL), lambda t: (jnp.minimum(t, LAT_TILES - 1), 0)),
            pl.BlockSpec((TILE, D_MODEL), lambda t: (jnp.maximum(t - LAT_TILES, 0), 0))]


def _stream_shapes(with_ctx):
    shapes = [jax.ShapeDtypeStruct((N_LAT, D_MODEL), F32)]
    return shapes + [jax.ShapeDtypeStruct((N_CTX, D_MODEL), F32)] if with_ctx else shapes


def _stream_store(refs, value):
    is_lat = pl.program_id(0) < LAT_TILES

    @pl.when(is_lat)
    def _():
        refs[0][...] = value

    if len(refs) > 1:
        @pl.when(jnp.logical_not(is_lat))
        def _():
            refs[1][...] = value


def _modulated(xl_ref, xc_ref, mod_ref, shift, scale):
    return _stream_rows(xl_ref, xc_ref) * (1.0 + mod_ref[0, scale:scale + 1, :]) + mod_ref[0, shift:shift + 1, :]


def _store_heads(ref, val, width):
    for h in range(val.shape[1] // width):
        ref[h] = val[:, h * width:(h + 1) * width].astype(ref.dtype)


def _hg_proj_kernel(xl_ref, xc_ref, mod_ref, w_ref, lb_ref, q_ref, v_ref, gate_ref, kf_ref, gf_ref, kb_ref, gb_ref):
    h = _bf(_modulated(xl_ref, xc_ref, mod_ref, 0, 1))
    d = D_MODEL
    _store_heads(q_ref, _silu(_mm(h, w_ref[:, 0:d])), LANES)
    _store_heads(v_ref, _mm(h, w_ref[:, d:2 * d]), LANES)
    gate_ref[...] = _bf(_mm(h, w_ref[:, 2 * d:3 * d]))
    lb = lb_ref[...]
    for k_ref, g_ref, lo in ((kf_ref, gf_ref, 3 * d), (kb_ref, gb_ref, 4 * d)):
        f = _mm(h, w_ref[:, lo:lo + d])
        _store_heads(k_ref, (1.0 - lb) * _sigmoid(-f), LANES)
        _store_heads(g_ref, jnp.log(lb + (1.0 - lb) * _sigmoid(f)), LANES)


def _head_spec(n_heads, width):
    return pl.BlockSpec((n_heads, TILE, width), lambda t: (0, t, 0))


def _head_shape(n_heads, width, dtype=None):
    return jax.ShapeDtypeStruct((n_heads, N_TOK, width), BF16 if dtype is None else dtype)


def _tok_shape(width):
    return jax.ShapeDtypeStruct((N_TOK, width), BF16)


def _tok_spec(width):
    return pl.BlockSpec((TILE, width), lambda t: (t, 0))


def _mod_spec():
    return pl.BlockSpec((1, ADA_CHUNKS, D_MODEL), lambda t: (_mod_row(t), 0, 0))


def _full_spec(shape):
    return pl.BlockSpec(shape, lambda t: (0,) * len(shape))


def _hg_proj(xl, xc, mod, w_in, lb):
    n = w_in.shape[1]
    hs = _head_shape(8, LANES)
    hg = _head_shape(8, LANES, F32)
    return pl.pallas_call(
        _hg_proj_kernel,
        grid=(ALL_TILES,),
        in_specs=_stream_specs() + [_mod_spec(), _full_spec((D_MODEL, n)), _full_spec((1, D_MODEL))],
        out_specs=[_head_spec(8, LANES), _head_spec(8, LANES), _tok_spec(D_MODEL),
                   _head_spec(8, LANES), _head_spec(8, LANES), _head_spec(8, LANES), _head_spec(8, LANES)],
        out_shape=[hs, hs, _tok_shape(D_MODEL), hs, hg, hs, hg],
        compiler_params=_cparams("parallel"),
        name="hg_proj",
    )(xl, xc, mod, w_in, lb)


GLA_HEADS = 4
GLA_DK = 128
GLA_DV = 256
GLA_RANK = 16
GLA_GATE_NORM = 16.0


def _gla_proj_kernel(xl_ref, xc_ref, mod_ref, w_ref, wlr_ref, gkw_ref, gkb_ref, q_ref, k_ref, v_ref, gate_ref, gf_ref, gb_ref):
    h = _bf(_modulated(xl_ref, xc_ref, mod_ref, 0, 1))
    kd = GLA_HEADS * GLA_DK
    vd = GLA_HEADS * GLA_DV
    _store_heads(q_ref, _mm(h, w_ref[:, 0:kd]) * GLA_DK ** -0.5, GLA_DK)
    _store_heads(k_ref, _mm(h, w_ref[:, kd:2 * kd]), GLA_DK)
    _store_heads(v_ref, _mm(h, w_ref[:, 2 * kd:2 * kd + vd]), GLA_DV)
    gate_ref[...] = _bf(_mm(h, w_ref[:, 2 * kd + vd:2 * kd + 2 * vd]))
    for z, g_ref in ((0, gf_ref), (1, gb_ref)):
        lr = _mm(h, wlr_ref[:, z * LANES:(z + 1) * LANES])
        gk = _mm(_bf(lr), gkw_ref[z]) + gkb_ref[z]
        _store_heads(g_ref, _log_sigmoid(gk) / GLA_GATE_NORM, GLA_DK)


def _gla_proj(xl, xc, mod, w_main, w_lr, gk_w, gk_b):
    kd = GLA_HEADS * GLA_DK
    hk = _head_shape(GLA_HEADS, GLA_DK)
    return pl.pallas_call(
        _gla_proj_kernel,
        grid=(ALL_TILES,),
        in_specs=_stream_specs() + [_mod_spec(), _full_spec(w_main.shape), _full_spec(w_lr.shape),
                  _full_spec(gk_w.shape), _full_spec(gk_b.shape)],
        out_specs=[_head_spec(GLA_HEADS, GLA_DK), _head_spec(GLA_HEADS, GLA_DK), _head_spec(GLA_HEADS, GLA_DV),
                   _tok_spec(D_MODEL), _head_spec(GLA_HEADS, GLA_DK), _head_spec(GLA_HEADS, GLA_DK)],
        out_shape=[hk, hk, _head_shape(GLA_HEADS, GLA_DV), _tok_shape(D_MODEL),
                   _head_shape(GLA_HEADS, GLA_DK, F32), _head_shape(GLA_HEADS, GLA_DK, F32)],
        compiler_params=_cparams("parallel"),
        name="gla_proj",
    )(xl, xc, mod, w_main, w_lr, gk_w, gk_b)


DN_HEADS = 8
DN_QKV = 3 * D_MODEL
SHORT_CONV = 5
CHUNKS_PER_TILE = TILE // CHUNK


def _seg_scan(x, pos, axis, reverse):
    n = x.shape[axis]
    sh = 1
    while sh < CHUNK:
        if reverse:
            x = x + jnp.where(pos < CHUNK - sh, pltpu.roll(x, n - sh, axis), 0.0)
        else:
            x = x + jnp.where(pos >= sh, pltpu.roll(x, sh, axis), 0.0)
        sh *= 2
    return x


def _dn_gates(p, a_neg, dtb, idx, pos, axis):
    beta = _sigmoid(p)
    g = a_neg * _softplus(p + dtb)
    gp = _seg_scan(g, pos, axis, False)
    gs = _seg_scan(g, pos, axis, True)
    return jnp.where(idx < 2 * DN_HEADS, beta, jnp.where(idx < 3 * DN_HEADS, gp, gs))


def _dn_proj_kernel(xl_ref, xc_ref, mod_ref, w_ref, wba_ref, wbat_ref, ac_ref, dc_ref, ar_ref, dr_ref,
                    qkv_ref, gate_ref, bgc_ref, bgr_ref):
    h = _bf(_modulated(xl_ref, xc_ref, mod_ref, 0, 1))
    qkv_ref[...] = _bf(_mm(h, w_ref[:, 0:DN_QKV]))
    gate_ref[...] = _bf(_mm(h, w_ref[:, DN_QKV:DN_QKV + D_MODEL]))
    pc = _mm(h, wba_ref[...])
    lane = lax.broadcasted_iota(jnp.int32, pc.shape, 1)
    rowpos = lax.broadcasted_iota(jnp.int32, pc.shape, 0) % CHUNK
    bgc_ref[...] = _dn_gates(pc, ac_ref[...], dc_ref[...], lane, rowpos, 0)
    pr = _nt(wbat_ref[...], h)
    sub = lax.broadcasted_iota(jnp.int32, pr.shape, 0)
    lanepos = lax.broadcasted_iota(jnp.int32, pr.shape, 1) % CHUNK
    gr = _dn_gates(pr, ar_ref[...], dr_ref[...], sub, lanepos, 1)
    for c in range(CHUNKS_PER_TILE):
        bgr_ref[c] = gr[:, c * CHUNK:(c + 1) * CHUNK]


def _dn_proj(xl, xc, mod, w_main, w_ba, w_bat, a_col, d_col, a_row, d_row):
    nb = 4 * DN_HEADS
    return pl.pallas_call(
        _dn_proj_kernel,
        grid=(ALL_TILES,),
        in_specs=_stream_specs() + [_mod_spec(), _full_spec(w_main.shape), _full_spec(w_ba.shape),
                  _full_spec(w_bat.shape), _full_spec(a_col.shape), _full_spec(d_col.shape),
                  _full_spec(a_row.shape), _full_spec(d_row.shape)],
        out_specs=[_tok_spec(DN_QKV), _tok_spec(D_MODEL), _tok_spec(LANES),
                   pl.BlockSpec((CHUNKS_PER_TILE, nb, CHUNK), lambda t: (t, 0, 0))],
        out_shape=[_tok_shape(DN_QKV), _tok_shape(D_MODEL), jax.ShapeDtypeStruct((N_TOK, LANES), F32),
                   jax.ShapeDtypeStruct((N_TOK // CHUNK, nb, CHUNK), F32)],
        compiler_params=_cparams("parallel"),
        name="dn_proj",
    )(xl, xc, mod, w_main, w_ba, w_bat, a_col, d_col, a_row, d_row)


HALO = 2 * SUBLANES


def _dn_conv_kernel(pm_ref, pp_ref, pn_ref, cw_ref, q_ref, k_ref, v_ref):
    t = pl.program_id(0)
    is_lat = t < LAT_TILES
    first = jnp.logical_or(jnp.logical_not(is_lat), t % TILES_PER_SEQ == 0)
    last = jnp.logical_or(jnp.logical_not(is_lat), t % TILES_PER_SEQ == TILES_PER_SEQ - 1)
    keep_prev = jnp.where(first, 0.0, 1.0)
    keep_next = jnp.where(last, 0.0, 1.0)
    n_ext = TILE + 2 * HALO
    pad = SHORT_CONV // 2
    for s in range(DN_QKV // LANES):
        cols = slice(s * LANES, (s + 1) * LANES)
        ext = jnp.concatenate([pp_ref[:, cols].astype(F32) * keep_prev, pm_ref[:, cols].astype(F32),
                               pn_ref[:, cols].astype(F32) * keep_next], axis=0)
        acc = None
        for kk in range(SHORT_CONV):
            off = HALO - pad + kk
            tap = pltpu.roll(ext, n_ext - off, 0)[0:TILE] * cw_ref[kk:kk + 1, cols]
            acc = tap if acc is None else acc + tap
        a = _silu(acc)
        head = s % DN_HEADS
        if s < 2 * DN_HEADS:
            a = a * lax.rsqrt(jnp.sum(a * a, axis=1, keepdims=True) + 1e-6)
            if s < DN_HEADS:
                q_ref[head] = _bf(a * LANES ** -0.5)
            else:
                k_ref[head] = _bf(a)
        else:
            v_ref[head] = _bf(a)


def _dn_conv(p_qkv, conv_w):
    rows8 = TILE // HALO
    last8 = N_TOK // HALO - 1
    hs = _head_shape(DN_HEADS, LANES)
    return pl.pallas_call(
        _dn_conv_kernel,
        grid=(ALL_TILES,),
        in_specs=[_tok_spec(DN_QKV),
                  pl.BlockSpec((HALO, DN_QKV), lambda t: (jnp.maximum(t * rows8 - 1, 0), 0)),
                  pl.BlockSpec((HALO, DN_QKV), lambda t: (jnp.minimum((t + 1) * rows8, last8), 0)),
                  _full_spec(conv_w.shape)],
        out_specs=[_head_spec(DN_HEADS, LANES)] * 3,
        out_shape=[hs, hs, hs],
        compiler_params=_cparams("parallel"),
        name="dn_conv",
    )(p_qkv, p_qkv, p_qkv, conv_w)


def _chunk_scan(x, row, reverse):
    sh = 1
    while sh < CHUNK:
        if reverse:
            x = x + jnp.where(row < CHUNK - sh, pltpu.roll(x, CHUNK - sh, 0), 0.0)
        else:
            x = x + jnp.where(row >= sh, pltpu.roll(x, sh, 0), 0.0)
        sh *= 2
    return x


def _gla_chunks(q, k, v, g, st, rev, row, r64, c64):
    n = range(len(q))
    q32 = [q[i].astype(F32) for i in n]
    k32 = [k[i].astype(F32) for i in n]
    cum = [_chunk_scan(g[i], row, rev[i]) for i in n]
    g_end = [cum[i][0:1, :] if rev[i] else cum[i][CHUNK - 1:CHUNK, :] for i in n]
    att = [jnp.where(r64 == c64, _nt(_bf(q[i]), _bf(k[i])), 0.0) for i in n]
    edge = list(cum)
    s = 1
    while s < CHUNK:
        late = (row & s) != 0
        early = jnp.logical_not(late)
        same = None
        if 2 * s < CHUNK:
            shift = (2 * s).bit_length() - 1
            same = (r64 >> shift) == (c64 >> shift)
        for i in n:
            if rev[i]:
                cut = jnp.where(late, edge[i], pltpu.roll(edge[i], CHUNK - s, 0))
            else:
                cut = jnp.where(late, pltpu.roll(edge[i], s, 0), edge[i])
            e = jnp.exp(-jnp.abs(cum[i] - cut))
            q_side = early if rev[i] else late
            a = _nt(_bf(jnp.where(q_side, q32[i] * e, 0.0)), _bf(jnp.where(q_side, 0.0, k32[i] * e)))
            if same is not None:
                a = jnp.where(same, a, 0.0)
            att[i] = att[i] + a
            if rev[i]:
                edge[i] = jnp.where(late, pltpu.roll(edge[i], s, 0), edge[i])
            else:
                edge[i] = jnp.where(late, edge[i], pltpu.roll(edge[i], CHUNK - s, 0))
        s *= 2
    st16 = [_bf(st[i]) for i in n]
    o = [_mm(_bf(att[i]), _bf(v[i])) + _nt(_bf(q32[i] * jnp.exp(cum[i])), st16[i]) for i in n]
    st_new = [st[i] * jnp.exp(g_end[i]) + _tn(_bf(v[i]), _bf(k32[i] * jnp.exp(g_end[i] - cum[i]))) for i in n]
    return o, st_new


def _scan_iotas():
    row = lax.broadcasted_iota(jnp.int32, (CHUNK, LANES), 0)
    r64 = lax.broadcasted_iota(jnp.int32, (CHUNK, CHUNK), 0)
    c64 = lax.broadcasted_iota(jnp.int32, (CHUNK, CHUNK), 1)
    return row, r64, c64


def _gla_scan_kernel(qf_ref, kf_ref, vf_ref, gf_ref, qb_ref, kb_ref, vb_ref, gb_ref,
                     of_ref, ob_ref, st_ref, *, n_heads):
    @pl.when(pl.program_id(1) == 0)
    def _():
        st_ref[...] = jnp.zeros_like(st_ref)

    row, r64, c64 = _scan_iotas()
    heads = range(n_heads)
    rev = [False] * n_heads + [True] * n_heads
    o, st = _gla_chunks(
        [qf_ref[h] for h in heads] + [qb_ref[h] for h in heads],
        [kf_ref[h] for h in heads] + [kb_ref[h] for h in heads],
        [vf_ref[h] for h in heads] + [vb_ref[h] for h in heads],
        [gf_ref[h] for h in heads] + [gb_ref[h] for h in heads],
        [st_ref[0, h] for h in heads] + [st_ref[1, h] for h in heads],
        rev, row, r64, c64)
    for h in heads:
        of_ref[h] = _bf(o[h])
        ob_ref[h] = _bf(o[n_heads + h])
        st_ref[0, h] = st[h]
        st_ref[1, h] = st[n_heads + h]


CTX_CHUNKS = CTX_LEN // CHUNK
LAT_CHUNKS = SEQ // CHUNK
SCAN_STEPS = CTX_CHUNKS + LAT_CHUNKS


def _fwd_chunk(b, n):
    return jnp.where(n < CTX_CHUNKS, N_LAT // CHUNK + b * CTX_CHUNKS + n, b * LAT_CHUNKS + n - CTX_CHUNKS)


def _bwd_chunk(b, n):
    return jnp.where(n < CTX_CHUNKS, N_LAT // CHUNK + b * CTX_CHUNKS + CTX_CHUNKS - 1 - n,
                     b * LAT_CHUNKS + SCAN_STEPS - 1 - n)


def _gla_scan(q, k_f, k_b, v, g_f, g_b):
    n_heads, _, dk = q.shape
    dv = v.shape[2]
    fwd = lambda b, n: (0, _fwd_chunk(b, n), 0)
    bwd = lambda b, n: (0, _bwd_chunk(b, n), 0)
    kspec = lambda m: pl.BlockSpec((n_heads, CHUNK, dk), m)
    vspec = lambda m: pl.BlockSpec((n_heads, CHUNK, dv), m)
    oshape = _head_shape(n_heads, dv)
    return pl.pallas_call(
        functools.partial(_gla_scan_kernel, n_heads=n_heads),
        grid=(BATCH, SCAN_STEPS),
        in_specs=[kspec(fwd), kspec(fwd), vspec(fwd), kspec(fwd), kspec(bwd), kspec(bwd), vspec(bwd), kspec(bwd)],
        out_specs=[vspec(fwd), vspec(bwd)],
        out_shape=[oshape, oshape],
        scratch_shapes=[pltpu.VMEM((2, n_heads, dv, dk), F32)],
        compiler_params=_cparams("parallel", "arbitrary"),
        name="gla_scan",
    )(q, k_f, v, g_f, q, k_b, v, g_b)


def _lane_pick(x, lane, idx):
    return jnp.sum(jnp.where(lane == idx, x, 0.0), axis=1, keepdims=True)


def _dn_chunks(q, k, v, beta, g_col, g_row, st, rev, r64, c64):
    n = range(len(q))
    dk = k[0].shape[1]
    g_end = [g_row[i][:, 0:1] if rev[i] else g_row[i][:, CHUNK - 1:CHUNK] for i in n]
    causal = {False: r64 >= c64, True: r64 <= c64}
    gam = [jnp.where(causal[rev[i]], jnp.exp(jnp.minimum(g_col[i] - g_row[i], 0.0)), 0.0) for i in n]
    q32 = [q[i].astype(F32) for i in n]
    k32 = [k[i].astype(F32) for i in n]
    kb = [k32[i] * beta[i] for i in n]
    k16 = [_bf(k[i]) for i in n]
    m = [jnp.where(r64 == c64, 0.0, _nt(_bf(kb[i]), k16[i]) * gam[i]) for i in n]
    att = [_nt(_bf(q[i]), k16[i]) * gam[i] for i in n]
    y = None
    s = 1
    while s < CHUNK:
        shift = (2 * s).bit_length() - 1
        rlate = (r64 & s) != 0
        clate = (c64 & s) != 0
        pair = {False: jnp.logical_and(rlate, jnp.logical_not(clate)), True: jnp.logical_and(clate, jnp.logical_not(rlate))}
        if 2 * s < CHUNK:
            same = (r64 >> shift) == (c64 >> shift)
            pair = {d: jnp.logical_and(p, same) for d, p in pair.items()}
        c = [jnp.where(pair[rev[i]], m[i], 0.0) for i in n]
        if y is None:
            y = [-c[i] for i in n]
        else:
            c16 = [_bf(c[i]) for i in n]
            y16 = [_bf(y[i]) for i in n]
            p = [c[i] + _mm(y16[i], c16[i]) for i in n]
            y = [y[i] - (p[i] + _mm(_bf(p[i]), y16[i])) for i in n]
        s *= 2
    e_col = [jnp.exp(g_col[i]) for i in n]
    rhs = [jnp.concatenate([kb[i] * e_col[i], v[i].astype(F32) * beta[i]], axis=1) for i in n]
    wu = [rhs[i] + _mm(_bf(y[i]), _bf(rhs[i])) for i in n]
    st16 = [_bf(st[i]) for i in n]
    v_new = [wu[i][:, dk:] - _nt(_bf(wu[i][:, :dk]), st16[i]) for i in n]
    o = [_nt(_bf(q32[i] * e_col[i]), st16[i]) + _mm(_bf(att[i]), _bf(v_new[i])) for i in n]
    st_new = [st[i] * jnp.exp(g_end[i]) + _tn(_bf(v_new[i]), _bf(k32[i] * jnp.exp(g_end[i] - g_col[i]))) for i in n]
    return o, st_new


def _dn_scan_kernel(qf_ref, kf_ref, vf_ref, cf_ref, rf_ref, qb_ref, kb_ref, vb_ref, cb_ref, rb_ref,
                    of_ref, ob_ref, st_ref):
    @pl.when(pl.program_id(1) == 0)
    def _():
        st_ref[...] = jnp.zeros_like(st_ref)

    _, r64, c64 = _scan_iotas()
    cf = cf_ref[...]
    cb = cb_ref[...]
    col = lambda x, i: x[:, i:i + 1]
    heads = range(DN_HEADS)
    rev = [False] * DN_HEADS + [True] * DN_HEADS
    o, st = _dn_chunks(
        [qf_ref[h] for h in heads] + [qb_ref[h] for h in heads],
        [kf_ref[h] for h in heads] + [kb_ref[h] for h in heads],
        [vf_ref[h] for h in heads] + [vb_ref[h] for h in heads],
        [col(cf, h) for h in heads] + [col(cb, DN_HEADS + h) for h in heads],
        [col(cf, 2 * DN_HEADS + h) for h in heads] + [col(cb, 3 * DN_HEADS + h) for h in heads],
        [rf_ref[0, 2 * DN_HEADS + h:2 * DN_HEADS + h + 1, :] for h in heads]
        + [rb_ref[0, 3 * DN_HEADS + h:3 * DN_HEADS + h + 1, :] for h in heads],
        [st_ref[0, h] for h in heads] + [st_ref[1, h] for h in heads],
        rev, r64, c64)
    for h in heads:
        of_ref[h] = _bf(o[h])
        ob_ref[h] = _bf(o[DN_HEADS + h])
        st_ref[0, h] = st[h]
        st_ref[1, h] = st[DN_HEADS + h]


def _dn_scan(q, k, v, bgc, bgr):
    fwd = lambda b, n: (0, _fwd_chunk(b, n), 0)
    bwd = lambda b, n: (0, _bwd_chunk(b, n), 0)
    fwd2 = lambda b, n: (_fwd_chunk(b, n), 0)
    bwd2 = lambda b, n: (_bwd_chunk(b, n), 0)
    fwd3 = lambda b, n: (_fwd_chunk(b, n), 0, 0)
    bwd3 = lambda b, n: (_bwd_chunk(b, n), 0, 0)
    hspec = lambda m: pl.BlockSpec((DN_HEADS, CHUNK, LANES), m)
    cspec = lambda m: pl.BlockSpec((CHUNK, LANES), m)
    rspec = lambda m: pl.BlockSpec((1, 4 * DN_HEADS, CHUNK), m)
    oshape = _head_shape(DN_HEADS, LANES)
    return pl.pallas_call(
        _dn_scan_kernel,
        grid=(BATCH, SCAN_STEPS),
        in_specs=[hspec(fwd), hspec(fwd), hspec(fwd), cspec(fwd2), rspec(fwd3),
                  hspec(bwd), hspec(bwd), hspec(bwd), cspec(bwd2), rspec(bwd3)],
        out_specs=[hspec(fwd), hspec(bwd)],
        out_shape=[oshape, oshape],
        scratch_shapes=[pltpu.VMEM((2, DN_HEADS, LANES, LANES), F32)],
        compiler_params=_cparams("parallel", "arbitrary"),
        name="dn_scan",
    )(q, k, v, bgc, bgr, q, k, v, bgc, bgr)


def _layer_norm(z, g, b):
    mu = jnp.mean(z, axis=1, keepdims=True)
    zc = z - mu
    var = jnp.mean(zc * zc, axis=1, keepdims=True)
    return zc * lax.rsqrt(var + LN_EPS) * g + b


def _post_kernel(of_ref, ob_ref, gate_ref, xl_ref, xc_ref, mod_ref, nw_ref, wo_ref, lg_ref, lb_ref, rw_ref, rb_ref,
                 *out_refs, n_heads):
    x1_refs, (h2_ref, logit_ref) = out_refs[:-2], out_refs[-2:]
    parts = []
    for h in range(n_heads):
        o = of_ref[h].astype(F32) + ob_ref[h].astype(F32)
        o = o * lax.rsqrt(jnp.mean(o * o, axis=1, keepdims=True) + RMS_EPS) * nw_ref[...]
        parts.append(o)
    on = jnp.concatenate(parts, axis=1) * _silu(gate_ref[...].astype(F32))
    y = _mm(_bf(on), wo_ref[...])
    z = ALPHA * _stream_rows(xl_ref, xc_ref) + mod_ref[0, 2:3, :] * y
    x1 = _layer_norm(z, lg_ref[...], lb_ref[...])
    _stream_store(x1_refs, x1)
    h2 = x1 * (1.0 + mod_ref[0, 4:5, :]) + mod_ref[0, 3:4, :]
    h2_ref[...] = h2
    logit_ref[...] = jnp.dot(h2, rw_ref[...], precision=HIGHEST, preferred_element_type=F32) + rb_ref[...]


def _post(o_f, o_b, gate, xl, xc, mod, norm_w, w_out, ln_g, ln_b, rw, rb, with_ctx):
    n_heads, _, dv = o_f.shape
    n_tiles = ALL_TILES if with_ctx else LAT_TILES
    n_rows = n_tiles * TILE
    tok = lambda w: jax.ShapeDtypeStruct((n_rows, w), F32)
    return pl.pallas_call(
        functools.partial(_post_kernel, n_heads=n_heads),
        grid=(n_tiles,),
        in_specs=[_head_spec(n_heads, dv), _head_spec(n_heads, dv), _tok_spec(D_MODEL)] + _stream_specs() + [
            _mod_spec(), _full_spec(norm_w.shape), _full_spec(w_out.shape), _full_spec(ln_g.shape),
            _full_spec(ln_b.shape), _full_spec(rw.shape), _full_spec(rb.shape)],
        out_specs=_stream_specs()[:2 if with_ctx else 1] + [_tok_spec(D_MODEL), _tok_spec(LANES)],
        out_shape=_stream_shapes(with_ctx) + [tok(D_MODEL), tok(LANES)],
        compiler_params=_cparams("arbitrary"),
        name="post_mixer",
    )(o_f, o_b, gate, xl, xc, mod, norm_w, w_out, ln_g, ln_b, rw, rb)


def _route_kernel(logit_ref, route_ref, slot_ref, count_ref, run_ref, start_ref, table_ref):
    sweep = pl.program_id(0)
    t = pl.program_id(1)
    lane = lax.broadcasted_iota(jnp.int32, (TILE, LANES), 1)
    lanef = lane.astype(F32)

    def add_counts(onehot):
        new_run = run_ref[0:1, :] + jnp.sum(onehot, axis=0, keepdims=True)
        run_ref[...] = jnp.broadcast_to(new_run, run_ref.shape)
        return new_run

    @pl.when(t == 0)
    def _():
        run_ref[...] = jnp.zeros_like(run_ref)

    @pl.when(sweep == 0)
    def _():
        neg = jnp.float32(-jnp.inf)
        l = jnp.where(lane < N_EXPERTS, logit_ref[...], neg)
        vals, picks = [], []
        onehot = jnp.zeros((TILE, LANES), F32)
        for _ in range(TOP_K):
            m = jnp.max(l, axis=1, keepdims=True)
            pick = jnp.min(jnp.where(l == m, lanef, float(LANES)), axis=1, keepdims=True)
            hot = lanef == pick
            l = jnp.where(hot, neg, l)
            onehot = onehot + jnp.where(hot, 1.0, 0.0)
            vals.append(m)
            picks.append(pick)
        es = [jnp.exp(v - vals[0]) for v in vals]
        denom = es[0] + es[1] + es[2] + es[3]
        out = jnp.zeros((TILE, LANES), F32)
        for kk in range(TOP_K):
            out = jnp.where(lane == kk, picks[kk], out)
            out = jnp.where(lane == ROUTE_WEIGHT + kk, es[kk] / denom, out)
        table_ref[t] = out
        new_run = add_counts(onehot)

        @pl.when(t == pl.num_programs(1) - 1)
        def _():
            lane8 = lax.broadcasted_iota(jnp.int32, (SUBLANES, LANES), 1)
            size = jnp.ceil(jnp.broadcast_to(new_run, (SUBLANES, LANES)) * (1.0 / EXPERT_TILE)) * EXPERT_TILE
            size = jnp.where(lane8 < N_EXPERTS, size, 0.0)
            ends = size
            sh = 1
            while sh < N_EXPERTS:
                ends = ends + jnp.where(lane8 >= sh, pltpu.roll(ends, sh, 1), 0.0)
                sh *= 2
            start_ref[...] = ends - size

    @pl.when(sweep == 1)
    def _():
        out = table_ref[t]
        hots = [lanef == out[:, kk:kk + 1] for kk in range(TOP_K)]
        onehot = jnp.zeros((TILE, LANES), F32)
        for hot in hots:
            onehot = onehot + jnp.where(hot, 1.0, 0.0)
        r = lax.broadcasted_iota(jnp.int32, (TILE, TILE), 0)
        c = lax.broadcasted_iota(jnp.int32, (TILE, TILE), 1)
        before = jnp.where(c < r, 1.0, 0.0).astype(BF16)
        slot_of = _mm(before, _bf(onehot)) + run_ref[0:1, :] + start_ref[0:1, :]
        for kk in range(TOP_K):
            slot = jnp.sum(jnp.where(hots[kk], slot_of, 0.0), axis=1, keepdims=True)
            out = jnp.where(lane == ROUTE_SLOT + kk, slot, out)
        route_ref[...] = out
        slot_ref[...] = out.T[ROUTE_WEIGHT:ROUTE_WEIGHT + SUBLANES, :].astype(jnp.int32)
        count_ref[...] = jnp.broadcast_to(add_counts(onehot), count_ref.shape)


ROUTE_WEIGHT = SUBLANES
ROUTE_SLOT = SUBLANES + TOP_K


def _route(logits):
    n_tok = logits.shape[0]
    n_tiles = n_tok // TILE
    return pl.pallas_call(
        _route_kernel,
        grid=(2, n_tiles),
        in_specs=[pl.BlockSpec((TILE, LANES), lambda s, t: (t, 0))],
        out_specs=[pl.BlockSpec((TILE, LANES), lambda s, t: (s * t, 0)),
                   pl.BlockSpec((SUBLANES, TILE), lambda s, t: (0, s * t)),
                   pl.BlockSpec((SUBLANES, LANES), lambda s, t: (0, 0))],
        out_shape=[jax.ShapeDtypeStruct(logits.shape, F32), jax.ShapeDtypeStruct((SUBLANES, n_tok), jnp.int32),
                   jax.ShapeDtypeStruct((SUBLANES, LANES), F32)],
        scratch_shapes=[pltpu.VMEM((SUBLANES, LANES), F32), pltpu.VMEM((SUBLANES, LANES), F32),
                        pltpu.VMEM((n_tiles, TILE, LANES), F32)],
        compiler_params=_cparams("arbitrary", "arbitrary"),
        name="route",
    )(logits)


GATHER_ROWS = 16
GATHER_BUFS = 4


def _sc_gather(table, idx):
    n_rows = idx.shape[0]
    d = table.shape[1]
    info = plsc.get_sparse_core_info()
    n_workers = info.num_cores * info.num_subcores
    rows_per_worker = n_rows // n_workers
    n_chunks = rows_per_worker // GATHER_ROWS
    assert rows_per_worker * n_workers == n_rows and n_chunks * GATHER_ROWS == rows_per_worker
    assert n_chunks % GATHER_BUFS == 0
    mesh = plsc.VectorSubcoreMesh(core_axis_name="c", subcore_axis_name="s")
    scratch = ([pltpu.VMEM((GATHER_ROWS,), jnp.int32)] * GATHER_BUFS
               + [pltpu.VMEM((GATHER_ROWS, d), table.dtype)] * GATHER_BUFS
               + [pltpu.SemaphoreType.DMA] * (2 * GATHER_BUFS))

    def body(table_hbm, idx_hbm, out_hbm, *scr):
        idx_v, rows_v = scr[:GATHER_BUFS], scr[GATHER_BUFS:2 * GATHER_BUFS]
        gather_sem, write_sem = scr[2 * GATHER_BUFS:3 * GATHER_BUFS], scr[3 * GATHER_BUFS:]
        worker = lax.axis_index("s") * info.num_cores + lax.axis_index("c")
        base = worker * rows_per_worker

        @pl.loop(0, n_chunks, step=GATHER_BUFS)
        def _(c0):
            offs = [pl.multiple_of(base + (c0 + b) * GATHER_ROWS, SUBLANES) for b in range(GATHER_BUFS)]
            gathers = []
            for b in range(GATHER_BUFS):
                pltpu.sync_copy(idx_hbm.at[pl.ds(offs[b], GATHER_ROWS)], idx_v[b])
                gathers.append(pltpu.async_copy(table_hbm.at[idx_v[b]], rows_v[b], gather_sem[b]))
            writes = []
            for b in range(GATHER_BUFS):
                gathers[b].wait()
                writes.append(pltpu.async_copy(rows_v[b], out_hbm.at[pl.ds(offs[b], GATHER_ROWS)], write_sem[b]))
            for w in writes:
                w.wait()

    return pl.kernel(body, out_type=jax.ShapeDtypeStruct((n_rows, d), table.dtype), mesh=mesh,
                     scratch_types=scratch, name="sc_gather")(table, idx)


def _sc_scatter(src, idx, n_out):
    n, d = src.shape
    info = plsc.get_sparse_core_info()
    n_workers = info.num_cores * info.num_subcores
    rows_per_worker = n // n_workers
    n_chunks = rows_per_worker // GATHER_ROWS
    assert rows_per_worker * n_workers == n and n_chunks * GATHER_ROWS == rows_per_worker
    assert n_chunks % GATHER_BUFS == 0
    mesh = plsc.VectorSubcoreMesh(core_axis_name="c", subcore_axis_name="s")
    scratch = ([pltpu.VMEM((GATHER_ROWS,), jnp.int32)] * (GATHER_BUFS * TOP_K)
               + [pltpu.VMEM((GATHER_ROWS, d), src.dtype)] * GATHER_BUFS + [pltpu.SemaphoreType.DMA] * GATHER_BUFS)

    def body(src_hbm, idx_hbm, out_hbm, *scr):
        idx_v = scr[:GATHER_BUFS * TOP_K]
        rows_v = scr[GATHER_BUFS * TOP_K:GATHER_BUFS * TOP_K + GATHER_BUFS]
        sem = scr[GATHER_BUFS * TOP_K + GATHER_BUFS:]
        worker = lax.axis_index("s") * info.num_cores + lax.axis_index("c")
        base = worker * rows_per_worker

        @pl.loop(0, n_chunks, step=GATHER_BUFS)
        def _(c0):
            copies = []
            for b in range(GATHER_BUFS):
                off = pl.multiple_of(base + (c0 + b) * GATHER_ROWS, SUBLANES)
                pltpu.sync_copy(src_hbm.at[pl.ds(off, GATHER_ROWS)], rows_v[b])
                for kk in range(TOP_K):
                    pltpu.sync_copy(idx_hbm.at[pl.ds(pl.multiple_of(kk * n + off, SUBLANES), GATHER_ROWS)],
                                    idx_v[b * TOP_K + kk])
                for kk in range(TOP_K):
                    copies.append(pltpu.async_copy(rows_v[b], out_hbm.at[idx_v[b * TOP_K + kk]], sem[b]))
            for c in copies:
                c.wait()

    return pl.kernel(body, out_type=jax.ShapeDtypeStruct((n_out, d), src.dtype), mesh=mesh,
                     scratch_types=scratch, name="sc_scatter")(src, idx)


PAIR_BLOCK = 2 * LANES


def _expert_kernel(te_ref, tv_ref, nt_ref, xs_ref, w1_ref, b1_ref, w2_ref, b2_ref, ys_ref, w1p_ref, w2b_ref):
    i = pl.program_id(0)
    valid = i < nt_ref[0]
    new_expert = jnp.logical_or(i == 0, te_ref[i] != te_ref[jnp.maximum(i - 1, 0)])

    @pl.when(jnp.logical_and(valid, new_expert))
    def _():
        r = lax.broadcasted_iota(jnp.int32, (PAIR_BLOCK, PAIR_BLOCK), 0)
        c = lax.broadcasted_iota(jnp.int32, (PAIR_BLOCK, PAIR_BLOCK), 1)
        src = jnp.where(c < LANES, 2 * c, 2 * (c - LANES) + 1)
        perm = jnp.where(r == src, 1.0, 0.0).astype(BF16)
        for b in range(2 * D_FF // PAIR_BLOCK):
            cols = slice(b * PAIR_BLOCK, (b + 1) * PAIR_BLOCK)
            w1p_ref[:, cols] = _bf(_mm(_bf(w1_ref[0, 0, :, cols]), perm))
        w2b_ref[...] = _bf(w2_ref[0, 0])

    @pl.when(valid)
    def _():
        rowi = lax.broadcasted_iota(jnp.int32, (EXPERT_TILE, D_MODEL), 0)
        x = _bf(jnp.where(rowi < tv_ref[i], xs_ref[...], 0.0))
        u = _mm(x, w1p_ref[...]) + b1_ref[0]
        parts = []
        for b in range(2 * D_FF // PAIR_BLOCK):
            glu = jnp.minimum(u[:, b * PAIR_BLOCK:b * PAIR_BLOCK + LANES], SWIGLU_LIMIT)
            lin = jnp.clip(u[:, b * PAIR_BLOCK + LANES:(b + 1) * PAIR_BLOCK], -SWIGLU_LIMIT, SWIGLU_LIMIT)
            parts.append(glu * _sigmoid(SWIGLU_ALPHA * glu) * (lin + 1.0))
        act = jnp.concatenate(parts, axis=1)
        ys_ref[...] = _mm(_bf(act), w2b_ref[...]) + b2_ref[0]


def _experts(layer, tile_expert, tile_valid, n_tiles, xs, w1, b1p, w2, b2):
    tile_map = lambda i, te, tv, nt: (jnp.minimum(i, nt[0] - 1), 0)
    wmap = lambda i, te, tv, nt: (te[i], 0, 0)
    lmap = lambda i, te, tv, nt: (layer, te[i], 0, 0)
    grid_spec = pltpu.PrefetchScalarGridSpec(
        num_scalar_prefetch=3,
        grid=(N_SLOT_TILES,),
        in_specs=[pl.BlockSpec((EXPERT_TILE, D_MODEL), tile_map),
                  pl.BlockSpec((1, 1, D_MODEL, 2 * D_FF), lmap), pl.BlockSpec((1, 1, 2 * D_FF), wmap),
                  pl.BlockSpec((1, 1, D_FF, D_MODEL), lmap), pl.BlockSpec((1, 1, D_MODEL), wmap)],
        out_specs=pl.BlockSpec((EXPERT_TILE, D_MODEL), tile_map),
        scratch_shapes=[pltpu.VMEM((D_MODEL, 2 * D_FF), BF16), pltpu.VMEM((D_FF, D_MODEL), BF16)],
    )
    return pl.pallas_call(
        _expert_kernel,
        grid_spec=grid_spec,
        out_shape=jax.ShapeDtypeStruct((N_SLOTS, D_MODEL), F32),
        compiler_params=pltpu.CompilerParams(dimension_semantics=("arbitrary",), vmem_limit_bytes=EXPERT_VMEM_LIMIT),
        name="experts",
    )(tile_expert, tile_valid, n_tiles, xs, w1, b1p, w2, b2)


def _combine_kernel(y_ref, route_ref, x1l_ref, x1c_ref, mod_ref, lg_ref, lb_ref, *x2_refs):
    route = route_ref[...]
    lane = lax.broadcasted_iota(jnp.int32, route.shape, 1)
    f = None
    for kk in range(TOP_K):
        term = _lane_pick(route, lane, ROUTE_WEIGHT + kk) * y_ref[kk]
        f = term if f is None else f + term
    z = ALPHA * _stream_rows(x1l_ref, x1c_ref) + mod_ref[0, 5:6, :] * f
    _stream_store(x2_refs, _layer_norm(z, lg_ref[...], lb_ref[...]))


def _combine(y_tok, route, x1l, x1c, mod, ln_g, ln_b, with_ctx):
    n_tiles = ALL_TILES if with_ctx else LAT_TILES
    return pl.pallas_call(
        _combine_kernel,
        grid=(n_tiles,),
        in_specs=[pl.BlockSpec((TOP_K, TILE, D_MODEL), lambda t: (0, t, 0)), _tok_spec(LANES)] + _stream_specs() + [
            _mod_spec(), _full_spec(ln_g.shape), _full_spec(ln_b.shape)],
        out_specs=_stream_specs()[:2 if with_ctx else 1],
        out_shape=_stream_shapes(with_ctx),
        compiler_params=_cparams("arbitrary"),
        name="combine",
    )(y_tok, route, x1l, x1c, mod, ln_g, ln_b)


def _moe(layer, h2, logits, x1l, x1c, mod, ln_g, ln_b, w1, b1p, w2, b2, with_ctx):
    n_tok = h2.shape[0]
    route, slot_rows, counts = _route(logits)
    slot_by_choice = slot_rows[TOP_K:2 * TOP_K].reshape(-1)
    cnt = counts[0, :N_EXPERTS].astype(jnp.int32)
    padded = (cnt + EXPERT_TILE - 1) // EXPERT_TILE * EXPERT_TILE
    ends = jnp.cumsum(padded)
    starts = ends - padded
    tile_start = jnp.arange(N_SLOT_TILES, dtype=jnp.int32) * EXPERT_TILE
    tile_expert = jnp.minimum(jnp.sum(tile_start[:, None] >= ends[None, :], axis=1), N_EXPERTS - 1).astype(jnp.int32)
    tile_valid = jnp.clip(starts[tile_expert] + cnt[tile_expert] - tile_start, 0, EXPERT_TILE).astype(jnp.int32)
    n_tiles = (ends[N_EXPERTS - 1:] // EXPERT_TILE).astype(jnp.int32)
    xs = _sc_scatter(h2, slot_by_choice, N_SLOTS)
    ys = _experts(layer, tile_expert, tile_valid, n_tiles, xs, w1, b1p, w2, b2)
    y_tok = _sc_gather(ys, slot_by_choice).reshape(TOP_K, n_tok, D_MODEL)
    return _combine(y_tok, route, x1l, x1c, mod, ln_g, ln_b, with_ctx)


def _lat_to_col_major(lat):
    return lat.reshape(BATCH, SEQ // GRID_W, GRID_W, D_MODEL).transpose(0, 2, 1, 3).reshape(N_LAT, D_MODEL)


def _lat_to_row_major(lat):
    return lat.reshape(BATCH, GRID_W, SEQ // GRID_W, D_MODEL).transpose(0, 2, 1, 3).reshape(N_LAT, D_MODEL)


def kernel(x, c, ctx, c_ctx, ada_w, ada_b, ln_g, ln_b, dn_w_in, dn_conv_w, dn_a_log, dn_dt_bias, dn_norm_w, dn_w_out,
           hg_w_in, hg_lower_bound, hg_norm_w, hg_w_out, gla_w_in, gla_gk_w, gla_gk_b, gla_norm_w, gla_w_out,
           router_w, router_b, exp_w1, exp_b1, exp_w2, exp_b2):
    xl = x.reshape(N_LAT, D_MODEL)
    xc = ctx.reshape(N_CTX, D_MODEL)
    cc = jnp.zeros((MOD_ROWS, D_MODEL), F32).at[:BATCH].set(c).at[BATCH].set(c_ctx)
    mods = _ada_all(cc, ada_w, ada_b).reshape(DEPTH, MOD_ROWS, ADA_CHUNKS, D_MODEL)
    lb_soft = jax.nn.softmax(hg_lower_bound.astype(F32), axis=0)
    lower_bounds = jnp.cumsum(lb_soft, axis=0) - lb_soft[0]

    for i in range(DEPTH):
        last = i == DEPTH - 1
        kind, j = i % 3, i // 3
        mod = mods[i]
        col_major = i % 2 == 1
        if col_major:
            xl = _lat_to_col_major(xl)

        if kind == 0:
            nq = 2 * D_MODEL + 2 * D_MODEL
            w = dn_w_in[j]
            w_ba = jnp.zeros((D_MODEL, LANES), F32).at[:, :4 * DN_HEADS].set(w[:, nq:])
            a_neg = jnp.zeros((LANES,), F32).at[2 * DN_HEADS:4 * DN_HEADS].set(-jnp.exp(dn_a_log[j].astype(F32)).reshape(-1))
            dtb = jnp.zeros((LANES,), F32).at[2 * DN_HEADS:4 * DN_HEADS].set(dn_dt_bias[j].astype(F32).reshape(-1))
            nb = 4 * DN_HEADS
            p_qkv, gate, bgc, bgr = _dn_proj(xl, xc, mod, _bf(w[:, :nq]), _bf(w_ba), _bf(w[:, nq:].T),
                                            a_neg.reshape(1, LANES), dtb.reshape(1, LANES),
                                            a_neg[:nb].reshape(nb, 1), dtb[:nb].reshape(nb, 1))
            cw = jnp.zeros((SUBLANES, DN_QKV), F32).at[:SHORT_CONV].set(dn_conv_w[j])
            q, k, v = _dn_conv(p_qkv, cw)
            o_f, o_b = _dn_scan(q, k, v, bgc, bgr)
            norm_w, w_out = dn_norm_w[j], dn_w_out[j]
        elif kind == 1:
            q, v, gate, k_f, g_f, k_b, g_b = _hg_proj(xl, xc, mod, _bf(hg_w_in[j]), lower_bounds[i].reshape(1, D_MODEL))
            o_f, o_b = _gla_scan(q, k_f, k_b, v, g_f, g_b)
            norm_w, w_out = hg_norm_w[j], hg_w_out[j]
        else:
            kd = GLA_HEADS * GLA_DK
            n_main = 2 * kd + 2 * D_MODEL
            w = gla_w_in[j]
            w_lr = jnp.zeros((D_MODEL, 2 * LANES), F32)
            w_lr = w_lr.at[:, :GLA_RANK].set(w[:, n_main:n_main + GLA_RANK])
            w_lr = w_lr.at[:, LANES:LANES + GLA_RANK].set(w[:, n_main + GLA_RANK:])
            gk_w = jnp.zeros((2, LANES, kd), F32).at[:, :GLA_RANK].set(gla_gk_w[j])
            q, k, v, gate, g_f, g_b = _gla_proj(xl, xc, mod, _bf(w[:, :n_main]), _bf(w_lr), _bf(gk_w),
                                                gla_gk_b[j].reshape(2, 1, kd))
            o_f, o_b = _gla_scan(q, k, k, v, g_f, g_b)
            norm_w, w_out = gla_norm_w[j], gla_w_out[j]

        with_ctx = not last
        rw = jnp.zeros((D_MODEL, LANES), F32).at[:, :N_EXPERTS].set(router_w[i])
        rb = jnp.zeros((1, LANES), F32).at[0, :N_EXPERTS].set(router_b[i])
        *x1, h2, logits = _post(o_f, o_b, gate, xl, xc, mod, norm_w.reshape(1, -1), _bf(w_out),
                                ln_g[i, 0].reshape(1, D_MODEL), ln_b[i, 0].reshape(1, D_MODEL), rw, rb, with_ctx)
        b1p = exp_b1[i].reshape(N_EXPERTS, 2 * D_FF // PAIR_BLOCK, LANES, 2).transpose(0, 1, 3, 2)
        x2 = _moe(i, h2, logits, x1[0], x1[1] if with_ctx else xc, mod,
                  ln_g[i, 1].reshape(1, D_MODEL), ln_b[i, 1].reshape(1, D_MODEL),
                  exp_w1, b1p.reshape(N_EXPERTS, 1, 2 * D_FF), exp_w2, exp_b2[i][:, None, :], with_ctx)
        xl = _lat_to_row_major(x2[0]) if col_major else x2[0]
        if with_ctx:
            xc = x2[1]
    return xl.reshape(BATCH, SEQ, D_MODEL)
```

```python
import functools

import jax
import jax.numpy as jnp
from jax import lax
from jax.experimental import pallas as pl
from jax.experimental.pallas import tpu as pltpu
from jax.experimental.pallas import tpu_sc as plsc

F32 = jnp.float32
BF16 = jnp.bfloat16
HIGHEST = lax.Precision.HIGHEST

D_MODEL = 1024
BATCH = 8
SEQ = 2048
CTX_LEN = 256
DEPTH = 4
GRID_W = 64
CHUNK = 64
ADA_CHUNKS = 6
N_LAT = BATCH * SEQ
N_CTX = BATCH * CTX_LEN
N_TOK = N_LAT + N_CTX
TILE = 256
LAT_TILES = N_LAT // TILE
ALL_TILES = N_TOK // TILE
TILES_PER_SEQ = SEQ // TILE
MOD_ROWS = 16
LANES = 128
SUBLANES = 8

N_EXPERTS = 32
TOP_K = 4
D_FF = D_MODEL
SWIGLU_LIMIT = 7.0
SWIGLU_ALPHA = 1.702
EXPERT_TILE = 512
N_SLOT_TILES = N_TOK * TOP_K // EXPERT_TILE + N_EXPERTS
N_SLOTS = N_SLOT_TILES * EXPERT_TILE

LN_EPS = 1e-5
RMS_EPS = 1e-6
ALPHA = (2.0 * DEPTH) ** 0.25

VMEM_LIMIT = 48 * 1024 * 1024
EXPERT_VMEM_LIMIT = (2 * 4 * (D_MODEL * 2 * D_FF + D_FF * D_MODEL) + 2 * (D_MODEL * 2 * D_FF + D_FF * D_MODEL)
                     + 4 * 4 * EXPERT_TILE * D_MODEL + 4 * EXPERT_TILE * (2 * D_FF + 2 * D_MODEL) + (4 << 20))


def _cparams(*sem):
    return pltpu.CompilerParams(dimension_semantics=sem, vmem_limit_bytes=VMEM_LIMIT)


def _sigmoid(x):
    return 1.0 / (1.0 + jnp.exp(-x))


def _silu(x):
    return x * _sigmoid(x)


def _softplus(x):
    return jnp.maximum(x, 0.0) + jnp.log(1.0 + jnp.exp(-jnp.abs(x)))


def _log_sigmoid(x):
    return -_softplus(-x)


def _nt(a, b):
    return lax.dot_general(a, b, (((1,), (1,)), ((), ())), preferred_element_type=F32)


def _tn(a, b):
    return lax.dot_general(a, b, (((0,), (0,)), ((), ())), preferred_element_type=F32)


def _mm(a, b):
    return jnp.dot(a, b, preferred_element_type=F32)


def _bf(x):
    return x.astype(BF16)


PACKED_D = D_MODEL // 2
HIGH_HALF = 0xFFFF0000


def _pack_halves(x):
    m = x.shape[1] // 2
    lo = pltpu.bitcast(x[:, :m].astype(BF16).astype(F32), jnp.uint32) >> 16
    hi = pltpu.bitcast(x[:, m:].astype(BF16).astype(F32), jnp.uint32) & jnp.uint32(HIGH_HALF)
    return pltpu.bitcast(hi | lo, F32)


def _unpack_halves(p):
    u = pltpu.bitcast(p, jnp.uint32)
    return jnp.concatenate([pltpu.bitcast(u << 16, F32), pltpu.bitcast(u & jnp.uint32(HIGH_HALF), F32)], axis=1)


def _mod_row(t):
    return jnp.minimum(t // TILES_PER_SEQ, BATCH)


ADA_NBLK = 1536


def _ada_kernel(c_ref, w_ref, b_ref, o_ref):
    s = _silu(c_ref[...])
    o_ref[0] = jnp.dot(s, w_ref[0], precision=HIGHEST, preferred_element_type=F32) + b_ref[0]


def _ada_all(cc, ada_w, ada_b):
    n = ADA_CHUNKS * D_MODEL
    return pl.pallas_call(
        _ada_kernel,
        grid=(DEPTH, n // ADA_NBLK),
        in_specs=[
            pl.BlockSpec((MOD_ROWS, D_MODEL), lambda i, j: (0, 0)),
            pl.BlockSpec((1, D_MODEL, ADA_NBLK), lambda i, j: (i, 0, j)),
            pl.BlockSpec((1, 1, ADA_NBLK), lambda i, j: (i, 0, j)),
        ],
        out_specs=pl.BlockSpec((1, MOD_ROWS, ADA_NBLK), lambda i, j: (i, 0, j)),
        out_shape=jax.ShapeDtypeStruct((DEPTH, MOD_ROWS, n), F32),
        compiler_params=_cparams("parallel", "parallel"),
        name="ada",
    )(cc, ada_w, ada_b.reshape(DEPTH, 1, n))


def _stream_rows(xl_ref, xc_ref):
    return jnp.where(pl.program_id(0) < LAT_TILES, xl_ref[...], xc_ref[...])


def _stream_specs():
    return [pl.BlockSpec((TILE, D_MODEL), lambda t: (jnp.minimum(t, LAT_TILES - 1), 0)),
            pl.BlockSpec((TILE, D_MODEL), lambda t: (jnp.maximum(t - LAT_TILES, 0), 0))]


def _stream_shapes(with_ctx):
    shapes = [jax.ShapeDtypeStruct((N_LAT, D_MODEL), F32)]
    return shapes + [jax.ShapeDtypeStruct((N_CTX, D_MODEL), F32)] if with_ctx else shapes


def _stream_store(refs, value):
    is_lat = pl.program_id(0) < LAT_TILES

    @pl.when(is_lat)
    def _():
        refs[0][...] = value

    if len(refs) > 1:
        @pl.when(jnp.logical_not(is_lat))
        def _():
            refs[1][...] = value


def _modulated(xl_ref, xc_ref, mod_ref, shift, scale):
    return _stream_rows(xl_ref, xc_ref) * (1.0 + mod_ref[0, scale:scale + 1, :]) + mod_ref[0, shift:shift + 1, :]


def _store_heads(ref, val, width):
    for h in range(val.shape[1] // width):
        ref[h] = val[:, h * width:(h + 1) * width].astype(ref.dtype)


def _hg_proj_kernel(xl_ref, xc_ref, mod_ref, w_ref, lb_ref, q_ref, v_ref, gate_ref, kf_ref, gf_ref, kb_ref, gb_ref):
    h = _bf(_modulated(xl_ref, xc_ref, mod_ref, 0, 1))
    d = D_MODEL
    _store_heads(q_ref, _silu(_mm(h, w_ref[:, 0:d])), LANES)
    _store_heads(v_ref, _mm(h, w_ref[:, d:2 * d]), LANES)
    gate_ref[...] = _bf(_mm(h, w_ref[:, 2 * d:3 * d]))
    lb = lb_ref[...]
    for k_ref, g_ref, lo in ((kf_ref, gf_ref, 3 * d), (kb_ref, gb_ref, 4 * d)):
        f = _mm(h, w_ref[:, lo:lo + d])
        _store_heads(k_ref, (1.0 - lb) * _sigmoid(-f), LANES)
        _store_heads(g_ref, jnp.log(lb + (1.0 - lb) * _sigmoid(f)), LANES)


def _head_spec(n_heads, width):
    return pl.BlockSpec((n_heads, TILE, width), lambda t: (0, t, 0))


def _head_shape(n_heads, width, dtype=None):
    return jax.ShapeDtypeStruct((n_heads, N_TOK, width), BF16 if dtype is None else dtype)


def _tok_shape(width):
    return jax.ShapeDtypeStruct((N_TOK, width), BF16)


def _tok_spec(width):
    return pl.BlockSpec((TILE, width), lambda t: (t, 0))


def _mod_spec():
    return pl.BlockSpec((1, ADA_CHUNKS, D_MODEL), lambda t: (_mod_row(t), 0, 0))


def _full_spec(shape):
    return pl.BlockSpec(shape, lambda t: (0,) * len(shape))


def _hg_proj(xl, xc, mod, w_in, lb):
    n = w_in.shape[1]
    hs = _head_shape(8, LANES)
    hg = _head_shape(8, LANES, F32)
    return pl.pallas_call(
        _hg_proj_kernel,
        grid=(ALL_TILES,),
        in_specs=_stream_specs() + [_mod_spec(), _full_spec((D_MODEL, n)), _full_spec((1, D_MODEL))],
        out_specs=[_head_spec(8, LANES), _head_spec(8, LANES), _tok_spec(D_MODEL),
                   _head_spec(8, LANES), _head_spec(8, LANES), _head_spec(8, LANES), _head_spec(8, LANES)],
        out_shape=[hs, hs, _tok_shape(D_MODEL), hs, hg, hs, hg],
        compiler_params=_cparams("parallel"),
        name="hg_proj",
    )(xl, xc, mod, w_in, lb)


GLA_HEADS = 4
GLA_DK = 128
GLA_DV = 256
GLA_RANK = 16
GLA_GATE_NORM = 16.0


def _gla_proj_kernel(xl_ref, xc_ref, mod_ref, w_ref, wlr_ref, gkw_ref, gkb_ref, q_ref, k_ref, v_ref, gate_ref, gf_ref, gb_ref):
    h = _bf(_modulated(xl_ref, xc_ref, mod_ref, 0, 1))
    kd = GLA_HEADS * GLA_DK
    vd = GLA_HEADS * GLA_DV
    _store_heads(q_ref, _mm(h, w_ref[:, 0:kd]) * GLA_DK ** -0.5, GLA_DK)
    _store_heads(k_ref, _mm(h, w_ref[:, kd:2 * kd]), GLA_DK)
    _store_heads(v_ref, _mm(h, w_ref[:, 2 * kd:2 * kd + vd]), GLA_DV)
    gate_ref[...] = _bf(_mm(h, w_ref[:, 2 * kd + vd:2 * kd + 2 * vd]))
    for z, g_ref in ((0, gf_ref), (1, gb_ref)):
        lr = _mm(h, wlr_ref[:, z * LANES:(z + 1) * LANES])
        gk = _mm(_bf(lr), gkw_ref[z]) + gkb_ref[z]
        _store_heads(g_ref, _log_sigmoid(gk) / GLA_GATE_NORM, GLA_DK)


def _gla_proj(xl, xc, mod, w_main, w_lr, gk_w, gk_b):
    kd = GLA_HEADS * GLA_DK
    hk = _head_shape(GLA_HEADS, GLA_DK)
    return pl.pallas_call(
        _gla_proj_kernel,
        grid=(ALL_TILES,),
        in_specs=_stream_specs() + [_mod_spec(), _full_spec(w_main.shape), _full_spec(w_lr.shape),
                  _full_spec(gk_w.shape), _full_spec(gk_b.shape)],
        out_specs=[_head_spec(GLA_HEADS, GLA_DK), _head_spec(GLA_HEADS, GLA_DK), _head_spec(GLA_HEADS, GLA_DV),
                   _tok_spec(D_MODEL), _head_spec(GLA_HEADS, GLA_DK), _head_spec(GLA_HEADS, GLA_DK)],
        out_shape=[hk, hk, _head_shape(GLA_HEADS, GLA_DV), _tok_shape(D_MODEL),
                   _head_shape(GLA_HEADS, GLA_DK, F32), _head_shape(GLA_HEADS, GLA_DK, F32)],
        compiler_params=_cparams("parallel"),
        name="gla_proj",
    )(xl, xc, mod, w_main, w_lr, gk_w, gk_b)


DN_HEADS = 8
DN_QKV = 3 * D_MODEL
SHORT_CONV = 5
CHUNKS_PER_TILE = TILE // CHUNK


def _seg_scan(x, pos, axis, reverse):
    n = x.shape[axis]
    sh = 1
    while sh < CHUNK:
        if reverse:
            x = x + jnp.where(pos < CHUNK - sh, pltpu.roll(x, n - sh, axis), 0.0)
        else:
            x = x + jnp.where(pos >= sh, pltpu.roll(x, sh, axis), 0.0)
        sh *= 2
    return x


def _dn_gates(p, a_neg, dtb, idx, pos, axis):
    beta = _sigmoid(p)
    g = a_neg * _softplus(p + dtb)
    gp = _seg_scan(g, pos, axis, False)
    gs = _seg_scan(g, pos, axis, True)
    return jnp.where(idx < 2 * DN_HEADS, beta, jnp.where(idx < 3 * DN_HEADS, gp, gs))


def _dn_proj_kernel(xl_ref, xc_ref, mod_ref, w_ref, wba_ref, wbat_ref, ac_ref, dc_ref, ar_ref, dr_ref,
                    qkv_ref, gate_ref, bgc_ref, bgr_ref):
    h = _bf(_modulated(xl_ref, xc_ref, mod_ref, 0, 1))
    qkv_ref[...] = _bf(_mm(h, w_ref[:, 0:DN_QKV]))
    gate_ref[...] = _bf(_mm(h, w_ref[:, DN_QKV:DN_QKV + D_MODEL]))
    pc = _mm(h, wba_ref[...])
    lane = lax.broadcasted_iota(jnp.int32, pc.shape, 1)
    rowpos = lax.broadcasted_iota(jnp.int32, pc.shape, 0) % CHUNK
    bgc_ref[...] = _dn_gates(pc, ac_ref[...], dc_ref[...], lane, rowpos, 0)
    pr = _nt(wbat_ref[...], h)
    sub = lax.broadcasted_iota(jnp.int32, pr.shape, 0)
    lanepos = lax.broadcasted_iota(jnp.int32, pr.shape, 1) % CHUNK
    gr = _dn_gates(pr, ar_ref[...], dr_ref[...], sub, lanepos, 1)
    for c in range(CHUNKS_PER_TILE):
        bgr_ref[c] = gr[:, c * CHUNK:(c + 1) * CHUNK]


def _dn_proj(xl, xc, mod, w_main, w_ba, w_bat, a_col, d_col, a_row, d_row):
    nb = 4 * DN_HEADS
    return pl.pallas_call(
        _dn_proj_kernel,
        grid=(ALL_TILES,),
        in_specs=_stream_specs() + [_mod_spec(), _full_spec(w_main.shape), _full_spec(w_ba.shape),
                  _full_spec(w_bat.shape), _full_spec(a_col.shape), _full_spec(d_col.shape),
                  _full_spec(a_row.shape), _full_spec(d_row.shape)],
        out_specs=[_tok_spec(DN_QKV), _tok_spec(D_MODEL), _tok_spec(LANES),
                   pl.BlockSpec((CHUNKS_PER_TILE, nb, CHUNK), lambda t: (t, 0, 0))],
        out_shape=[_tok_shape(DN_QKV), _tok_shape(D_MODEL), jax.ShapeDtypeStruct((N_TOK, LANES), F32),
                   jax.ShapeDtypeStruct((N_TOK // CHUNK, nb, CHUNK), F32)],
        compiler_params=_cparams("parallel"),
        name="dn_proj",
    )(xl, xc, mod, w_main, w_ba, w_bat, a_col, d_col, a_row, d_row)


HALO = 2 * SUBLANES


def _dn_conv_kernel(pm_ref, pp_ref, pn_ref, cw_ref, q_ref, k_ref, v_ref):
    t = pl.program_id(0)
    is_lat = t < LAT_TILES
    first = jnp.logical_or(jnp.logical_not(is_lat), t % TILES_PER_SEQ == 0)
    last = jnp.logical_or(jnp.logical_not(is_lat), t % TILES_PER_SEQ == TILES_PER_SEQ - 1)
    keep_prev = jnp.where(first, 0.0, 1.0)
    keep_next = jnp.where(last, 0.0, 1.0)
    n_ext = TILE + 2 * HALO
    pad = SHORT_CONV // 2
    for s in range(DN_QKV // LANES):
        cols = slice(s * LANES, (s + 1) * LANES)
        ext = jnp.concatenate([pp_ref[:, cols].astype(F32) * keep_prev, pm_ref[:, cols].astype(F32),
                               pn_ref[:, cols].astype(F32) * keep_next], axis=0)
        acc = None
        for kk in range(SHORT_CONV):
            off = HALO - pad + kk
            tap = pltpu.roll(ext, n_ext - off, 0)[0:TILE] * cw_ref[kk:kk + 1, cols]
            acc = tap if acc is None else acc + tap
        a = _silu(acc)
        head = s % DN_HEADS
        if s < 2 * DN_HEADS:
            a = a * lax.rsqrt(jnp.sum(a * a, axis=1, keepdims=True) + 1e-6)
            if s < DN_HEADS:
                q_ref[head] = _bf(a * LANES ** -0.5)
            else:
                k_ref[head] = _bf(a)
        else:
            v_ref[head] = _bf(a)


def _dn_conv(p_qkv, conv_w):
    rows8 = TILE // HALO
    last8 = N_TOK // HALO - 1
    hs = _head_shape(DN_HEADS, LANES)
    return pl.pallas_call(
        _dn_conv_kernel,
        grid=(ALL_TILES,),
        in_specs=[_tok_spec(DN_QKV),
                  pl.BlockSpec((HALO, DN_QKV), lambda t: (jnp.maximum(t * rows8 - 1, 0), 0)),
                  pl.BlockSpec((HALO, DN_QKV), lambda t: (jnp.minimum((t + 1) * rows8, last8), 0)),
                  _full_spec(conv_w.shape)],
        out_specs=[_head_spec(DN_HEADS, LANES)] * 3,
        out_shape=[hs, hs, hs],
        compiler_params=_cparams("parallel"),
        name="dn_conv",
    )(p_qkv, p_qkv, p_qkv, conv_w)


def _chunk_scan(x, row, reverse):
    sh = 1
    while sh < CHUNK:
        if reverse:
            x = x + jnp.where(row < CHUNK - sh, pltpu.roll(x, CHUNK - sh, 0), 0.0)
        else:
            x = x + jnp.where(row >= sh, pltpu.roll(x, sh, 0), 0.0)
        sh *= 2
    return x


def _gla_chunks(q, k, v, g, st, rev, row, r64, c64):
    n = range(len(q))
    q32 = [q[i].astype(F32) for i in n]
    k32 = [k[i].astype(F32) for i in n]
    cum = [_chunk_scan(g[i], row, rev[i]) for i in n]
    g_end = [cum[i][0:1, :] if rev[i] else cum[i][CHUNK - 1:CHUNK, :] for i in n]
    att = [jnp.where(r64 == c64, _nt(_bf(q[i]), _bf(k[i])), 0.0) for i in n]
    edge = list(cum)
    s = 1
    while s < CHUNK:
        late = (row & s) != 0
        early = jnp.logical_not(late)
        rlate = (r64 & s) != 0
        clate = (c64 & s) != 0
        pair = {False: jnp.logical_and(rlate, jnp.logical_not(clate)), True: jnp.logical_and(clate, jnp.logical_not(rlate))}
        if 2 * s < CHUNK:
            shift = (2 * s).bit_length() - 1
            same = (r64 >> shift) == (c64 >> shift)
            pair = {d: jnp.logical_and(p, same) for d, p in pair.items()}
        for i in n:
            if rev[i]:
                cut = jnp.where(late, edge[i], pltpu.roll(edge[i], CHUNK - s, 0))
            else:
                cut = jnp.where(late, pltpu.roll(edge[i], s, 0), edge[i])
            e = jnp.exp(-jnp.abs(cum[i] - cut))
            q_side = early if rev[i] else late
            x16 = _bf(jnp.where(q_side, q32[i], k32[i]) * e)
            att[i] = att[i] + jnp.where(pair[rev[i]], _nt(x16, x16), 0.0)
            if rev[i]:
                edge[i] = jnp.where(late, pltpu.roll(edge[i], s, 0), edge[i])
            else:
                edge[i] = jnp.where(late, edge[i], pltpu.roll(edge[i], CHUNK - s, 0))
        s *= 2
    st16 = [_bf(st[i]) for i in n]
    o = [_mm(_bf(att[i]), _bf(v[i])) + _nt(_bf(q32[i] * jnp.exp(cum[i])), st16[i]) for i in n]
    st_new = [st[i] * jnp.exp(g_end[i]) + _tn(_bf(v[i]), _bf(k32[i] * jnp.exp(g_end[i] - cum[i]))) for i in n]
    return o, st_new


def _scan_iotas():
    row = lax.broadcasted_iota(jnp.int32, (CHUNK, LANES), 0)
    r64 = lax.broadcasted_iota(jnp.int32, (CHUNK, CHUNK), 0)
    c64 = lax.broadcasted_iota(jnp.int32, (CHUNK, CHUNK), 1)
    return row, r64, c64


def _gla_scan_kernel(qf_ref, kf_ref, vf_ref, gf_ref, qb_ref, kb_ref, vb_ref, gb_ref,
                     of_ref, ob_ref, st_ref, *, n_heads):
    @pl.when(pl.program_id(1) == 0)
    def _():
        st_ref[...] = jnp.zeros_like(st_ref)

    row, r64, c64 = _scan_iotas()
    heads = range(n_heads)
    rev = [False] * n_heads + [True] * n_heads
    o, st = _gla_chunks(
        [qf_ref[h] for h in heads] + [qb_ref[h] for h in heads],
        [kf_ref[h] for h in heads] + [kb_ref[h] for h in heads],
        [vf_ref[h] for h in heads] + [vb_ref[h] for h in heads],
        [gf_ref[h] for h in heads] + [gb_ref[h] for h in heads],
        [st_ref[0, h] for h in heads] + [st_ref[1, h] for h in heads],
        rev, row, r64, c64)
    for h in heads:
        of_ref[h] = _bf(o[h])
        ob_ref[h] = _bf(o[n_heads + h])
        st_ref[0, h] = st[h]
        st_ref[1, h] = st[n_heads + h]


CTX_CHUNKS = CTX_LEN // CHUNK
LAT_CHUNKS = SEQ // CHUNK
SCAN_STEPS = CTX_CHUNKS + LAT_CHUNKS


def _fwd_chunk(b, n):
    return jnp.where(n < CTX_CHUNKS, N_LAT // CHUNK + b * CTX_CHUNKS + n, b * LAT_CHUNKS + n - CTX_CHUNKS)


def _bwd_chunk(b, n):
    return jnp.where(n < CTX_CHUNKS, N_LAT // CHUNK + b * CTX_CHUNKS + CTX_CHUNKS - 1 - n,
                     b * LAT_CHUNKS + SCAN_STEPS - 1 - n)


def _gla_scan(q, k_f, k_b, v, g_f, g_b):
    n_heads, _, dk = q.shape
    dv = v.shape[2]
    fwd = lambda b, n: (0, _fwd_chunk(b, n), 0)
    bwd = lambda b, n: (0, _bwd_chunk(b, n), 0)
    kspec = lambda m: pl.BlockSpec((n_heads, CHUNK, dk), m)
    vspec = lambda m: pl.BlockSpec((n_heads, CHUNK, dv), m)
    oshape = _head_shape(n_heads, dv)
    return pl.pallas_call(
        functools.partial(_gla_scan_kernel, n_heads=n_heads),
        grid=(BATCH, SCAN_STEPS),
        in_specs=[kspec(fwd), kspec(fwd), vspec(fwd), kspec(fwd), kspec(bwd), kspec(bwd), vspec(bwd), kspec(bwd)],
        out_specs=[vspec(fwd), vspec(bwd)],
        out_shape=[oshape, oshape],
        scratch_shapes=[pltpu.VMEM((2, n_heads, dv, dk), F32)],
        compiler_params=_cparams("parallel", "arbitrary"),
        name="gla_scan",
    )(q, k_f, v, g_f, q, k_b, v, g_b)


def _lane_pick(x, lane, idx):
    return jnp.sum(jnp.where(lane == idx, x, 0.0), axis=1, keepdims=True)


def _dn_chunks(q, k, v, beta, g_col, g_row, st, rev, r64, c64):
    n = range(len(q))
    dk = k[0].shape[1]
    g_end = [g_row[i][:, 0:1] if rev[i] else g_row[i][:, CHUNK - 1:CHUNK] for i in n]
    causal = {False: r64 >= c64, True: r64 <= c64}
    gam = [jnp.where(causal[rev[i]], jnp.exp(jnp.minimum(g_col[i] - g_row[i], 0.0)), 0.0) for i in n]
    q32 = [q[i].astype(F32) for i in n]
    k32 = [k[i].astype(F32) for i in n]
    kb = [k32[i] * beta[i] for i in n]
    k16 = [_bf(k[i]) for i in n]
    m = [jnp.where(r64 == c64, 0.0, _nt(_bf(kb[i]), k16[i]) * gam[i]) for i in n]
    att = [_nt(_bf(q[i]), k16[i]) * gam[i] for i in n]
    y = None
    s = 1
    while s < CHUNK:
        shift = (2 * s).bit_length() - 1
        rlate = (r64 & s) != 0
        clate = (c64 & s) != 0
        pair = {False: jnp.logical_and(rlate, jnp.logical_not(clate)), True: jnp.logical_and(clate, jnp.logical_not(rlate))}
        if 2 * s < CHUNK:
            same = (r64 >> shift) == (c64 >> shift)
            pair = {d: jnp.logical_and(p, same) for d, p in pair.items()}
        c = [jnp.where(pair[rev[i]], m[i], 0.0) for i in n]
        if y is None:
            y = [-c[i] for i in n]
        else:
            c16 = [_bf(c[i]) for i in n]
            y16 = [_bf(y[i]) for i in n]
            p = [c[i] + _mm(y16[i], c16[i]) for i in n]
            y = [y[i] - (p[i] + _mm(_bf(p[i]), y16[i])) for i in n]
        s *= 2
    e_col = [jnp.exp(g_col[i]) for i in n]
    rhs = [jnp.concatenate([kb[i] * e_col[i], v[i].astype(F32) * beta[i]], axis=1) for i in n]
    wu = [rhs[i] + _mm(_bf(y[i]), _bf(rhs[i])) for i in n]
    st16 = [_bf(st[i]) for i in n]
    v_new = [wu[i][:, dk:] - _nt(_bf(wu[i][:, :dk]), st16[i]) for i in n]
    o = [_nt(_bf(q32[i] * e_col[i]), st16[i]) + _mm(_bf(att[i]), _bf(v_new[i])) for i in n]
    st_new = [st[i] * jnp.exp(g_end[i]) + _tn(_bf(v_new[i]), _bf(k32[i] * jnp.exp(g_end[i] - g_col[i]))) for i in n]
    return o, st_new


def _dn_scan_kernel(qf_ref, kf_ref, vf_ref, cf_ref, rf_ref, qb_ref, kb_ref, vb_ref, cb_ref, rb_ref,
                    of_ref, ob_ref, st_ref):
    @pl.when(pl.program_id(1) == 0)
    def _():
        st_ref[...] = jnp.zeros_like(st_ref)

    _, r64, c64 = _scan_iotas()
    cf = cf_ref[...]
    cb = cb_ref[...]
    col = lambda x, i: x[:, i:i + 1]
    heads = range(DN_HEADS)
    rev = [False] * DN_HEADS + [True] * DN_HEADS
    o, st = _dn_chunks(
        [qf_ref[h] for h in heads] + [qb_ref[h] for h in heads],
        [kf_ref[h] for h in heads] + [kb_ref[h] for h in heads],
        [vf_ref[h] for h in heads] + [vb_ref[h] for h in heads],
        [col(cf, h) for h in heads] + [col(cb, DN_HEADS + h) for h in heads],
        [col(cf, 2 * DN_HEADS + h) for h in heads] + [col(cb, 3 * DN_HEADS + h) for h in heads],
        [rf_ref[0, 2 * DN_HEADS + h:2 * DN_HEADS + h + 1, :] for h in heads]
        + [rb_ref[0, 3 * DN_HEADS + h:3 * DN_HEADS + h + 1, :] for h in heads],
        [st_ref[0, h] for h in heads] + [st_ref[1, h] for h in heads],
        rev, r64, c64)
    for h in heads:
        of_ref[h] = _bf(o[h])
        ob_ref[h] = _bf(o[DN_HEADS + h])
        st_ref[0, h] = st[h]
        st_ref[1, h] = st[DN_HEADS + h]


def _dn_scan(q, k, v, bgc, bgr):
    fwd = lambda b, n: (0, _fwd_chunk(b, n), 0)
    bwd = lambda b, n: (0, _bwd_chunk(b, n), 0)
    fwd2 = lambda b, n: (_fwd_chunk(b, n), 0)
    bwd2 = lambda b, n: (_bwd_chunk(b, n), 0)
    fwd3 = lambda b, n: (_fwd_chunk(b, n), 0, 0)
    bwd3 = lambda b, n: (_bwd_chunk(b, n), 0, 0)
    hspec = lambda m: pl.BlockSpec((DN_HEADS, CHUNK, LANES), m)
    cspec = lambda m: pl.BlockSpec((CHUNK, LANES), m)
    rspec = lambda m: pl.BlockSpec((1, 4 * DN_HEADS, CHUNK), m)
    oshape = _head_shape(DN_HEADS, LANES)
    return pl.pallas_call(
        _dn_scan_kernel,
        grid=(BATCH, SCAN_STEPS),
        in_specs=[hspec(fwd), hspec(fwd), hspec(fwd), cspec(fwd2), rspec(fwd3),
                  hspec(bwd), hspec(bwd), hspec(bwd), cspec(bwd2), rspec(bwd3)],
        out_specs=[hspec(fwd), hspec(bwd)],
        out_shape=[oshape, oshape],
        scratch_shapes=[pltpu.VMEM((2, DN_HEADS, LANES, LANES), F32)],
        compiler_params=_cparams("parallel", "arbitrary"),
        name="dn_scan",
    )(q, k, v, bgc, bgr, q, k, v, bgc, bgr)


def _layer_norm(z, g, b):
    mu = jnp.mean(z, axis=1, keepdims=True)
    zc = z - mu
    var = jnp.mean(zc * zc, axis=1, keepdims=True)
    return zc * lax.rsqrt(var + LN_EPS) * g + b


def _post_kernel(of_ref, ob_ref, gate_ref, xl_ref, xc_ref, mod_ref, nw_ref, wo_ref, lg_ref, lb_ref, rw_ref, rb_ref,
                 *out_refs, n_heads):
    x1_refs, (h2_ref, logit_ref) = out_refs[:-2], out_refs[-2:]
    x_rows = _stream_rows(xl_ref, xc_ref)
    groups = [slice(g * TILE // POST_GROUPS, (g + 1) * TILE // POST_GROUPS) for g in range(POST_GROUPS)]
    on = []
    for rows in groups:
        parts = []
        for h in range(n_heads):
            o = of_ref[h, rows, :].astype(F32) + ob_ref[h, rows, :].astype(F32)
            parts.append(o * lax.rsqrt(jnp.mean(o * o, axis=1, keepdims=True) + RMS_EPS) * nw_ref[...])
        on.append(jnp.concatenate(parts, axis=1) * _silu(gate_ref[rows, :].astype(F32)))
    y = [_mm(_bf(a), wo_ref[...]) for a in on]
    z = [ALPHA * x_rows[rows] + mod_ref[0, 2:3, :] * yg for rows, yg in zip(groups, y)]
    x1 = [_layer_norm(zg, lg_ref[...], lb_ref[...]) for zg in z]
    h2 = [a * (1.0 + mod_ref[0, 4:5, :]) + mod_ref[0, 3:4, :] for a in x1]
    logits = [jnp.dot(a, rw_ref[...], precision=HIGHEST, preferred_element_type=F32) + rb_ref[...] for a in h2]
    _stream_store(x1_refs, jnp.concatenate(x1, axis=0))
    h2_ref[...] = _pack_halves(jnp.concatenate(h2, axis=0))
    logit_ref[...] = jnp.concatenate(logits, axis=0)


POST_GROUPS = 2


def _post(o_f, o_b, gate, xl, xc, mod, norm_w, w_out, ln_g, ln_b, rw, rb, with_ctx):
    n_heads, _, dv = o_f.shape
    n_tiles = ALL_TILES if with_ctx else LAT_TILES
    n_rows = n_tiles * TILE
    tok = lambda w: jax.ShapeDtypeStruct((n_rows, w), F32)
    return pl.pallas_call(
        functools.partial(_post_kernel, n_heads=n_heads),
        grid=(n_tiles,),
        in_specs=[_head_spec(n_heads, dv), _head_spec(n_heads, dv), _tok_spec(D_MODEL)] + _stream_specs() + [
            _mod_spec(), _full_spec(norm_w.shape), _full_spec(w_out.shape), _full_spec(ln_g.shape),
            _full_spec(ln_b.shape), _full_spec(rw.shape), _full_spec(rb.shape)],
        out_specs=_stream_specs()[:2 if with_ctx else 1] + [_tok_spec(PACKED_D), _tok_spec(LANES)],
        out_shape=_stream_shapes(with_ctx) + [tok(PACKED_D), tok(LANES)],
        compiler_params=_cparams("arbitrary"),
        name="post_mixer",
    )(o_f, o_b, gate, xl, xc, mod, norm_w, w_out, ln_g, ln_b, rw, rb)


def _route_kernel(logit_ref, route_ref, slot_ref, count_ref, run_ref, start_ref, table_ref):
    sweep = pl.program_id(0)
    t = pl.program_id(1)
    lane = lax.broadcasted_iota(jnp.int32, (TILE, LANES), 1)
    lanef = lane.astype(F32)

    def add_counts(onehot):
        new_run = run_ref[0:1, :] + jnp.sum(onehot, axis=0, keepdims=True)
        run_ref[...] = jnp.broadcast_to(new_run, run_ref.shape)
        return new_run

    @pl.when(t == 0)
    def _():
        run_ref[...] = jnp.zeros_like(run_ref)

    @pl.when(sweep == 0)
    def _():
        neg = jnp.float32(-jnp.inf)
        l = jnp.where(lane < N_EXPERTS, logit_ref[...], neg)
        vals, picks = [], []
        onehot = jnp.zeros((TILE, LANES), F32)
        for _ in range(TOP_K):
            m = jnp.max(l, axis=1, keepdims=True)
            pick = jnp.min(jnp.where(l == m, lanef, float(LANES)), axis=1, keepdims=True)
            hot = lanef == pick
            l = jnp.where(hot, neg, l)
            onehot = onehot + jnp.where(hot, 1.0, 0.0)
            vals.append(m)
            picks.append(pick)
        es = [jnp.exp(v - vals[0]) for v in vals]
        denom = es[0] + es[1] + es[2] + es[3]
        out = jnp.zeros((TILE, LANES), F32)
        for kk in range(TOP_K):
            out = jnp.where(lane == kk, picks[kk], out)
            out = jnp.where(lane == ROUTE_WEIGHT + kk, es[kk] / denom, out)
        table_ref[t] = out
        new_run = add_counts(onehot)

        @pl.when(t == pl.num_programs(1) - 1)
        def _():
            lane8 = lax.broadcasted_iota(jnp.int32, (SUBLANES, LANES), 1)
            size = jnp.ceil(jnp.broadcast_to(new_run, (SUBLANES, LANES)) * (1.0 / EXPERT_TILE)) * EXPERT_TILE
            size = jnp.where(lane8 < N_EXPERTS, size, 0.0)
            ends = size
            sh = 1
            while sh < N_EXPERTS:
                ends = ends + jnp.where(lane8 >= sh, pltpu.roll(ends, sh, 1), 0.0)
                sh *= 2
            start_ref[...] = ends - size

    @pl.when(sweep == 1)
    def _():
        out = table_ref[t]
        hots = [lanef == out[:, kk:kk + 1] for kk in range(TOP_K)]
        onehot = jnp.zeros((TILE, LANES), F32)
        for hot in hots:
            onehot = onehot + jnp.where(hot, 1.0, 0.0)
        r = lax.broadcasted_iota(jnp.int32, (TILE, TILE), 0)
        c = lax.broadcasted_iota(jnp.int32, (TILE, TILE), 1)
        before = jnp.where(c < r, 1.0, 0.0).astype(BF16)
        slot_of = _mm(before, _bf(onehot)) + run_ref[0:1, :] + start_ref[0:1, :]
        for kk in range(TOP_K):
            slot = jnp.sum(jnp.where(hots[kk], slot_of, 0.0), axis=1, keepdims=True)
            out = jnp.where(lane == ROUTE_SLOT + kk, slot, out)
        route_ref[...] = out
        slot_ref[...] = out.T[ROUTE_WEIGHT:ROUTE_WEIGHT + SUBLANES, :].astype(jnp.int32)
        count_ref[...] = jnp.broadcast_to(add_counts(onehot), count_ref.shape)


ROUTE_WEIGHT = SUBLANES
ROUTE_SLOT = SUBLANES + TOP_K


def _route(logits):
    n_tok = logits.shape[0]
    n_tiles = n_tok // TILE
    return pl.pallas_call(
        _route_kernel,
        grid=(2, n_tiles),
        in_specs=[pl.BlockSpec((TILE, LANES), lambda s, t: (t, 0))],
        out_specs=[pl.BlockSpec((TILE, LANES), lambda s, t: (s * t, 0)),
                   pl.BlockSpec((SUBLANES, TILE), lambda s, t: (0, s * t)),
                   pl.BlockSpec((SUBLANES, LANES), lambda s, t: (0, 0))],
        out_shape=[jax.ShapeDtypeStruct(logits.shape, F32), jax.ShapeDtypeStruct((SUBLANES, n_tok), jnp.int32),
                   jax.ShapeDtypeStruct((SUBLANES, LANES), F32)],
        scratch_shapes=[pltpu.VMEM((SUBLANES, LANES), F32), pltpu.VMEM((SUBLANES, LANES), F32),
                        pltpu.VMEM((n_tiles, TILE, LANES), F32)],
        compiler_params=_cparams("arbitrary", "arbitrary"),
        name="route",
    )(logits)


GATHER_ROWS = 16
GATHER_BUFS = 4


def _sc_gather(table, idx):
    n_rows = idx.shape[0]
    d = table.shape[1]
    info = plsc.get_sparse_core_info()
    n_workers = info.num_cores * info.num_subcores
    rows_per_worker = n_rows // n_workers
    n_chunks = rows_per_worker // GATHER_ROWS
    assert rows_per_worker * n_workers == n_rows and n_chunks * GATHER_ROWS == rows_per_worker
    assert n_chunks % GATHER_BUFS == 0
    mesh = plsc.VectorSubcoreMesh(core_axis_name="c", subcore_axis_name="s")
    scratch = ([pltpu.VMEM((GATHER_ROWS,), jnp.int32)] * GATHER_BUFS
               + [pltpu.VMEM((GATHER_ROWS, d), table.dtype)] * GATHER_BUFS
               + [pltpu.SemaphoreType.DMA] * (2 * GATHER_BUFS))

    def body(table_hbm, idx_hbm, out_hbm, *scr):
        idx_v, rows_v = scr[:GATHER_BUFS], scr[GATHER_BUFS:2 * GATHER_BUFS]
        gather_sem, write_sem = scr[2 * GATHER_BUFS:3 * GATHER_BUFS], scr[3 * GATHER_BUFS:]
        worker = lax.axis_index("s") * info.num_cores + lax.axis_index("c")
        base = worker * rows_per_worker

        @pl.loop(0, n_chunks, step=GATHER_BUFS)
        def _(c0):
            offs = [pl.multiple_of(base + (c0 + b) * GATHER_ROWS, SUBLANES) for b in range(GATHER_BUFS)]
            gathers = []
            for b in range(GATHER_BUFS):
                pltpu.sync_copy(idx_hbm.at[pl.ds(offs[b], GATHER_ROWS)], idx_v[b])
                gathers.append(pltpu.async_copy(table_hbm.at[idx_v[b]], rows_v[b], gather_sem[b]))
            writes = []
            for b in range(GATHER_BUFS):
                gathers[b].wait()
                writes.append(pltpu.async_copy(rows_v[b], out_hbm.at[pl.ds(offs[b], GATHER_ROWS)], write_sem[b]))
            for w in writes:
                w.wait()

    return pl.kernel(body, out_type=jax.ShapeDtypeStruct((n_rows, d), table.dtype), mesh=mesh,
                     scratch_types=scratch, name="sc_gather")(table, idx)


def _sc_scatter(src, idx, n_out):
    n, d = src.shape
    info = plsc.get_sparse_core_info()
    n_workers = info.num_cores * info.num_subcores
    rows_per_worker = n // n_workers
    n_chunks = rows_per_worker // GATHER_ROWS
    assert rows_per_worker * n_workers == n and n_chunks * GATHER_ROWS == rows_per_worker
    assert n_chunks % GATHER_BUFS == 0
    mesh = plsc.VectorSubcoreMesh(core_axis_name="c", subcore_axis_name="s")
    scratch = ([pltpu.VMEM((GATHER_ROWS,), jnp.int32)] * (GATHER_BUFS * TOP_K)
               + [pltpu.VMEM((GATHER_ROWS, d), src.dtype)] * GATHER_BUFS + [pltpu.SemaphoreType.DMA] * GATHER_BUFS)

    def body(src_hbm, idx_hbm, out_hbm, *scr):
        idx_v = scr[:GATHER_BUFS * TOP_K]
        rows_v = scr[GATHER_BUFS * TOP_K:GATHER_BUFS * TOP_K + GATHER_BUFS]
        sem = scr[GATHER_BUFS * TOP_K + GATHER_BUFS:]
        worker = lax.axis_index("s") * info.num_cores + lax.axis_index("c")
        base = worker * rows_per_worker

        @pl.loop(0, n_chunks, step=GATHER_BUFS)
        def _(c0):
            copies = []
            for b in range(GATHER_BUFS):
                off = pl.multiple_of(base + (c0 + b) * GATHER_ROWS, SUBLANES)
                pltpu.sync_copy(src_hbm.at[pl.ds(off, GATHER_ROWS)], rows_v[b])
                for kk in range(TOP_K):
                    pltpu.sync_copy(idx_hbm.at[pl.ds(pl.multiple_of(kk * n + off, SUBLANES), GATHER_ROWS)],
                                    idx_v[b * TOP_K + kk])
                for kk in range(TOP_K):
                    copies.append(pltpu.async_copy(rows_v[b], out_hbm.at[idx_v[b * TOP_K + kk]], sem[b]))
            for c in copies:
                c.wait()

    return pl.kernel(body, out_type=jax.ShapeDtypeStruct((n_out, d), src.dtype), mesh=mesh,
                     scratch_types=scratch, name="sc_scatter")(src, idx)


PAIR_BLOCK = 2 * LANES


def _expert_kernel(te_ref, tv_ref, nt_ref, xs_ref, w1_ref, b1_ref, w2_ref, b2_ref, ys_ref, w1p_ref, w2b_ref):
    i = pl.program_id(0)
    valid = i < nt_ref[0]
    new_expert = jnp.logical_or(i == 0, te_ref[i] != te_ref[jnp.maximum(i - 1, 0)])

    @pl.when(jnp.logical_and(valid, new_expert))
    def _():
        r = lax.broadcasted_iota(jnp.int32, (PAIR_BLOCK, PAIR_BLOCK), 0)
        c = lax.broadcasted_iota(jnp.int32, (PAIR_BLOCK, PAIR_BLOCK), 1)
        src = jnp.where(c < LANES, 2 * c, 2 * (c - LANES) + 1)
        perm = jnp.where(r == src, 1.0, 0.0).astype(BF16)
        for b in range(2 * D_FF // PAIR_BLOCK):
            cols = slice(b * PAIR_BLOCK, (b + 1) * PAIR_BLOCK)
            w1p_ref[:, cols] = _bf(_mm(_bf(w1_ref[0, 0, :, cols]), perm))
        w2b_ref[...] = _bf(w2_ref[0, 0])

    @pl.when(valid)
    def _():
        rowi = lax.broadcasted_iota(jnp.int32, (EXPERT_TILE, D_MODEL), 0)
        x = _bf(jnp.where(rowi < tv_ref[i], _unpack_halves(xs_ref[...]), 0.0))
        u = _mm(x, w1p_ref[...]) + b1_ref[0]
        parts = []
        for b in range(2 * D_FF // PAIR_BLOCK):
            glu = jnp.minimum(u[:, b * PAIR_BLOCK:b * PAIR_BLOCK + LANES], SWIGLU_LIMIT)
            lin = jnp.clip(u[:, b * PAIR_BLOCK + LANES:(b + 1) * PAIR_BLOCK], -SWIGLU_LIMIT, SWIGLU_LIMIT)
            parts.append(glu * _sigmoid(SWIGLU_ALPHA * glu) * (lin + 1.0))
        act = jnp.concatenate(parts, axis=1)
        ys_ref[...] = _pack_halves(_mm(_bf(act), w2b_ref[...]) + b2_ref[0])


def _experts(layer, tile_expert, tile_valid, n_tiles, xs, w1, b1p, w2, b2):
    tile_map = lambda i, te, tv, nt: (jnp.minimum(i, nt[0] - 1), 0)
    wmap = lambda i, te, tv, nt: (te[i], 0, 0)
    lmap = lambda i, te, tv, nt: (layer, te[i], 0, 0)
    grid_spec = pltpu.PrefetchScalarGridSpec(
        num_scalar_prefetch=3,
        grid=(N_SLOT_TILES,),
        in_specs=[pl.BlockSpec((EXPERT_TILE, PACKED_D), tile_map),
                  pl.BlockSpec((1, 1, D_MODEL, 2 * D_FF), lmap), pl.BlockSpec((1, 1, 2 * D_FF), wmap),
                  pl.BlockSpec((1, 1, D_FF, D_MODEL), lmap), pl.BlockSpec((1, 1, D_MODEL), wmap)],
        out_specs=pl.BlockSpec((EXPERT_TILE, PACKED_D), tile_map),
        scratch_shapes=[pltpu.VMEM((D_MODEL, 2 * D_FF), BF16), pltpu.VMEM((D_FF, D_MODEL), BF16)],
    )
    return pl.pallas_call(
        _expert_kernel,
        grid_spec=grid_spec,
        out_shape=jax.ShapeDtypeStruct((N_SLOTS, PACKED_D), F32),
        compiler_params=pltpu.CompilerParams(dimension_semantics=("arbitrary",), vmem_limit_bytes=EXPERT_VMEM_LIMIT),
        name="experts",
    )(tile_expert, tile_valid, n_tiles, xs, w1, b1p, w2, b2)


def _combine_kernel(y_ref, route_ref, x1l_ref, x1c_ref, mod_ref, lg_ref, lb_ref, *x2_refs):
    route = route_ref[...]
    lane = lax.broadcasted_iota(jnp.int32, route.shape, 1)
    f = None
    for kk in range(TOP_K):
        term = _lane_pick(route, lane, ROUTE_WEIGHT + kk) * _unpack_halves(y_ref[kk])
        f = term if f is None else f + term
    z = ALPHA * _stream_rows(x1l_ref, x1c_ref) + mod_ref[0, 5:6, :] * f
    _stream_store(x2_refs, _layer_norm(z, lg_ref[...], lb_ref[...]))


def _combine(y_tok, route, x1l, x1c, mod, ln_g, ln_b, with_ctx):
    n_tiles = ALL_TILES if with_ctx else LAT_TILES
    return pl.pallas_call(
        _combine_kernel,
        grid=(n_tiles,),
        in_specs=[pl.BlockSpec((TOP_K, TILE, PACKED_D), lambda t: (0, t, 0)), _tok_spec(LANES)] + _stream_specs() + [
            _mod_spec(), _full_spec(ln_g.shape), _full_spec(ln_b.shape)],
        out_specs=_stream_specs()[:2 if with_ctx else 1],
        out_shape=_stream_shapes(with_ctx),
        compiler_params=_cparams("arbitrary"),
        name="combine",
    )(y_tok, route, x1l, x1c, mod, ln_g, ln_b)


def _moe(layer, h2, logits, x1l, x1c, mod, ln_g, ln_b, w1, b1p, w2, b2, with_ctx):
    n_tok = h2.shape[0]
    route, slot_rows, counts = _route(logits)
    slot_by_choice = slot_rows[TOP_K:2 * TOP_K].reshape(-1)
    cnt = counts[0, :N_EXPERTS].astype(jnp.int32)
    padded = (cnt + EXPERT_TILE - 1) // EXPERT_TILE * EXPERT_TILE
    ends = jnp.cumsum(padded)
    starts = ends - padded
    tile_start = jnp.arange(N_SLOT_TILES, dtype=jnp.int32) * EXPERT_TILE
    tile_expert = jnp.minimum(jnp.sum(tile_start[:, None] >= ends[None, :], axis=1), N_EXPERTS - 1).astype(jnp.int32)
    tile_valid = jnp.clip(starts[tile_expert] + cnt[tile_expert] - tile_start, 0, EXPERT_TILE).astype(jnp.int32)
    n_tiles = (ends[N_EXPERTS - 1:] // EXPERT_TILE).astype(jnp.int32)
    xs = _sc_scatter(h2, slot_by_choice, N_SLOTS)
    ys = _experts(layer, tile_expert, tile_valid, n_tiles, xs, w1, b1p, w2, b2)
    y_tok = _sc_gather(ys, slot_by_choice).reshape(TOP_K, n_tok, PACKED_D)
    return _combine(y_tok, route, x1l, x1c, mod, ln_g, ln_b, with_ctx)


def _lat_to_col_major(lat):
    return lat.reshape(BATCH, SEQ // GRID_W, GRID_W, D_MODEL).transpose(0, 2, 1, 3).reshape(N_LAT, D_MODEL)


def _lat_to_row_major(lat):
    return lat.reshape(BATCH, GRID_W, SEQ // GRID_W, D_MODEL).transpose(0, 2, 1, 3).reshape(N_LAT, D_MODEL)


def kernel(x, c, ctx, c_ctx, ada_w, ada_b, ln_g, ln_b, dn_w_in, dn_conv_w, dn_a_log, dn_dt_bias, dn_norm_w, dn_w_out,
           hg_w_in, hg_lower_bound, hg_norm_w, hg_w_out, gla_w_in, gla_gk_w, gla_gk_b, gla_norm_w, gla_w_out,
           router_w, router_b, exp_w1, exp_b1, exp_w2, exp_b2):
    xl = x.reshape(N_LAT, D_MODEL)
    xc = ctx.reshape(N_CTX, D_MODEL)
    cc = jnp.zeros((MOD_ROWS, D_MODEL), F32).at[:BATCH].set(c).at[BATCH].set(c_ctx)
    mods = _ada_all(cc, ada_w, ada_b).reshape(DEPTH, MOD_ROWS, ADA_CHUNKS, D_MODEL)
    lb_soft = jax.nn.softmax(hg_lower_bound.astype(F32), axis=0)
    lower_bounds = jnp.cumsum(lb_soft, axis=0) - lb_soft[0]

    for i in range(DEPTH):
        last = i == DEPTH - 1
        kind, j = i % 3, i // 3
        mod = mods[i]
        col_major = i % 2 == 1
        if col_major:
            xl = _lat_to_col_major(xl)

        if kind == 0:
            nq = 2 * D_MODEL + 2 * D_MODEL
            w = dn_w_in[j]
            w_ba = jnp.zeros((D_MODEL, LANES), F32).at[:, :4 * DN_HEADS].set(w[:, nq:])
            a_neg = jnp.zeros((LANES,), F32).at[2 * DN_HEADS:4 * DN_HEADS].set(-jnp.exp(dn_a_log[j].astype(F32)).reshape(-1))
            dtb = jnp.zeros((LANES,), F32).at[2 * DN_HEADS:4 * DN_HEADS].set(dn_dt_bias[j].astype(F32).reshape(-1))
            nb = 4 * DN_HEADS
            p_qkv, gate, bgc, bgr = _dn_proj(xl, xc, mod, _bf(w[:, :nq]), _bf(w_ba), _bf(w[:, nq:].T),
                                            a_neg.reshape(1, LANES), dtb.reshape(1, LANES),
                                            a_neg[:nb].reshape(nb, 1), dtb[:nb].reshape(nb, 1))
            cw = jnp.zeros((SUBLANES, DN_QKV), F32).at[:SHORT_CONV].set(dn_conv_w[j])
            q, k, v = _dn_conv(p_qkv, cw)
            o_f, o_b = _dn_scan(q, k, v, bgc, bgr)
            norm_w, w_out = dn_norm_w[j], dn_w_out[j]
        elif kind == 1:
            q, v, gate, k_f, g_f, k_b, g_b = _hg_proj(xl, xc, mod, _bf(hg_w_in[j]), lower_bounds[i].reshape(1, D_MODEL))
            o_f, o_b = _gla_scan(q, k_f, k_b, v, g_f, g_b)
            norm_w, w_out = hg_norm_w[j], hg_w_out[j]
        else:
            kd = GLA_HEADS * GLA_DK
            n_main = 2 * kd + 2 * D_MODEL
            w = gla_w_in[j]
            w_lr = jnp.zeros((D_MODEL, 2 * LANES), F32)
            w_lr = w_lr.at[:, :GLA_RANK].set(w[:, n_main:n_main + GLA_RANK])
            w_lr = w_lr.at[:, LANES:LANES + GLA_RANK].set(w[:, n_main + GLA_RANK:])
            gk_w = jnp.zeros((2, LANES, kd), F32).at[:, :GLA_RANK].set(gla_gk_w[j])
            q, k, v, gate, g_f, g_b = _gla_proj(xl, xc, mod, _bf(w[:, :n_main]), _bf(w_lr), _bf(gk_w),
                                                gla_gk_b[j].reshape(2, 1, kd))
            o_f, o_b = _gla_scan(q, k, k, v, g_f, g_b)
            norm_w, w_out = gla_norm_w[j], gla_w_out[j]

        with_ctx = not last
        rw = jnp.zeros((D_MODEL, LANES), F32).at[:, :N_EXPERTS].set(router_w[i])
        rb = jnp.zeros((1, LANES), F32).at[0, :N_EXPERTS].set(router_b[i])
        *x1, h2, logits = _post(o_f, o_b, gate, xl, xc, mod, norm_w.reshape(1, -1), _bf(w_out),
                                ln_g[i, 0].reshape(1, D_MODEL), ln_b[i, 0].reshape(1, D_MODEL), rw, rb, with_ctx)
        b1p = exp_b1[i].reshape(N_EXPERTS, 2 * D_FF // PAIR_BLOCK, LANES, 2).transpose(0, 1, 3, 2)
        x2 = _moe(i, h2, logits, x1[0], x1[1] if with_ctx else xc, mod,
                  ln_g[i, 1].reshape(1, D_MODEL), ln_b[i, 1].reshape(1, D_MODEL),
                  exp_w1, b1p.reshape(N_EXPERTS, 1, 2 * D_FF), exp_w2, exp_b2[i][:, None, :], with_ctx)
        xl = _lat_to_row_major(x2[0]) if col_major else x2[0]
        if with_ctx:
            xc = x2[1]
    return xl.reshape(BATCH, SEQ, D_MODEL)
```

```python
import functools

import jax
import jax.numpy as jnp
from jax import lax
from jax.experimental import pallas as pl
from jax.experimental.pallas import tpu as pltpu
from jax.experimental.pallas import tpu_sc as plsc

F32 = jnp.float32
BF16 = jnp.bfloat16
HIGHEST = lax.Precision.HIGHEST

D_MODEL = 1024
BATCH = 8
SEQ = 2048
CTX_LEN = 256
DEPTH = 4
GRID_W = 64
CHUNK = 64
ADA_CHUNKS = 6
N_LAT = BATCH * SEQ
N_CTX = BATCH * CTX_LEN
N_TOK = N_LAT + N_CTX
TILE = 256
LAT_TILES = N_LAT // TILE
ALL_TILES = N_TOK // TILE
TILES_PER_SEQ = SEQ // TILE
MOD_ROWS = 16
LANES = 128
SUBLANES = 8

N_EXPERTS = 32
TOP_K = 4
D_FF = D_MODEL
SWIGLU_LIMIT = 7.0
SWIGLU_ALPHA = 1.702
EXPERT_TILE = 512
N_SLOT_TILES = N_TOK * TOP_K // EXPERT_TILE + N_EXPERTS
N_SLOTS = N_SLOT_TILES * EXPERT_TILE

LN_EPS = 1e-5
RMS_EPS = 1e-6
ALPHA = (2.0 * DEPTH) ** 0.25

VMEM_LIMIT = 48 * 1024 * 1024
EXPERT_VMEM_LIMIT = (2 * 4 * (D_MODEL * 2 * D_FF + D_FF * D_MODEL) + 2 * (D_MODEL * 2 * D_FF + D_FF * D_MODEL)
                     + 4 * 4 * EXPERT_TILE * D_MODEL + 4 * EXPERT_TILE * (2 * D_FF + 2 * D_MODEL) + (4 << 20))


def _cparams(*sem):
    return pltpu.CompilerParams(dimension_semantics=sem, vmem_limit_bytes=VMEM_LIMIT)


def _sigmoid(x):
    return 1.0 / (1.0 + jnp.exp(-x))


def _silu(x):
    return x * _sigmoid(x)


def _softplus(x):
    return jnp.maximum(x, 0.0) + jnp.log(1.0 + jnp.exp(-jnp.abs(x)))


def _log_sigmoid(x):
    return -_softplus(-x)


def _nt(a, b):
    return lax.dot_general(a, b, (((1,), (1,)), ((), ())), preferred_element_type=F32)


def _tn(a, b):
    return lax.dot_general(a, b, (((0,), (0,)), ((), ())), preferred_element_type=F32)


def _mm(a, b):
    return jnp.dot(a, b, preferred_element_type=F32)


def _bf(x):
    return x.astype(BF16)


PACKED_D = D_MODEL // 2
HIGH_HALF = 0xFFFF0000


def _pack_halves(x):
    m = x.shape[1] // 2
    lo = pltpu.bitcast(x[:, :m].astype(BF16).astype(F32), jnp.uint32) >> 16
    hi = pltpu.bitcast(x[:, m:].astype(BF16).astype(F32), jnp.uint32) & jnp.uint32(HIGH_HALF)
    return pltpu.bitcast(hi | lo, F32)


def _unpack_halves(p):
    u = pltpu.bitcast(p, jnp.uint32)
    return jnp.concatenate([pltpu.bitcast(u << 16, F32), pltpu.bitcast(u & jnp.uint32(HIGH_HALF), F32)], axis=1)


def _mod_row(t):
    return jnp.minimum(t // TILES_PER_SEQ, BATCH)


ADA_NBLK = 1536


def _ada_kernel(c_ref, w_ref, b_ref, o_ref):
    s = _silu(c_ref[...])
    o_ref[0] = jnp.dot(s, w_ref[0], precision=HIGHEST, preferred_element_type=F32) + b_ref[0]


def _ada_all(cc, ada_w, ada_b):
    n = ADA_CHUNKS * D_MODEL
    return pl.pallas_call(
        _ada_kernel,
        grid=(DEPTH, n // ADA_NBLK),
        in_specs=[
            pl.BlockSpec((MOD_ROWS, D_MODEL), lambda i, j: (0, 0)),
            pl.BlockSpec((1, D_MODEL, ADA_NBLK), lambda i, j: (i, 0, j)),
            pl.BlockSpec((1, 1, ADA_NBLK), lambda i, j: (i, 0, j)),
        ],
        out_specs=pl.BlockSpec((1, MOD_ROWS, ADA_NBLK), lambda i, j: (i, 0, j)),
        out_shape=jax.ShapeDtypeStruct((DEPTH, MOD_ROWS, n), F32),
        compiler_params=_cparams("parallel", "parallel"),
        name="ada",
    )(cc, ada_w, ada_b.reshape(DEPTH, 1, n))


def _stream_rows(xl_ref, xc_ref):
    return jnp.where(pl.program_id(0) < LAT_TILES, xl_ref[...], xc_ref[...])


def _stream_specs():
    return [pl.BlockSpec((TILE, D_MODEL), lambda t: (jnp.minimum(t, LAT_TILES - 1), 0)),
            pl.BlockSpec((TILE, D_MODEL), lambda t: (jnp.maximum(t - LAT_TILES, 0), 0))]


def _stream_shapes(with_ctx):
    shapes = [jax.ShapeDtypeStruct((N_LAT, D_MODEL), F32)]
    return shapes + [jax.ShapeDtypeStruct((N_CTX, D_MODEL), F32)] if with_ctx else shapes


def _stream_store(refs, value):
    is_lat = pl.program_id(0) < LAT_TILES

    @pl.when(is_lat)
    def _():
        refs[0][...] = value

    if len(refs) > 1:
        @pl.when(jnp.logical_not(is_lat))
        def _():
            refs[1][...] = value


def _modulated(xl_ref, xc_ref, mod_ref, shift, scale):
    return _stream_rows(xl_ref, xc_ref) * (1.0 + mod_ref[0, scale:scale + 1, :]) + mod_ref[0, shift:shift + 1, :]


def _store_heads(ref, val, width):
    for h in range(val.shape[1] // width):
        ref[h] = val[:, h * width:(h + 1) * width].astype(ref.dtype)


def _hg_proj_kernel(xl_ref, xc_ref, mod_ref, w_ref, lb_ref, q_ref, v_ref, gate_ref, kf_ref, gf_ref, kb_ref, gb_ref):
    h = _bf(_modulated(xl_ref, xc_ref, mod_ref, 0, 1))
    d = D_MODEL
    _store_heads(q_ref, _silu(_mm(h, w_ref[:, 0:d])), LANES)
    _store_heads(v_ref, _mm(h, w_ref[:, d:2 * d]), LANES)
    gate_ref[...] = _bf(_mm(h, w_ref[:, 2 * d:3 * d]))
    lb = lb_ref[...]
    for k_ref, g_ref, lo in ((kf_ref, gf_ref, 3 * d), (kb_ref, gb_ref, 4 * d)):
        f = _mm(h, w_ref[:, lo:lo + d])
        _store_heads(k_ref, (1.0 - lb) * _sigmoid(-f), LANES)
        _store_heads(g_ref, jnp.log(lb + (1.0 - lb) * _sigmoid(f)), LANES)


def _head_spec(n_heads, width):
    return pl.BlockSpec((n_heads, TILE, width), lambda t: (0, t, 0))


def _head_shape(n_heads, width, dtype=None):
    return jax.ShapeDtypeStruct((n_heads, N_TOK, width), BF16 if dtype is None else dtype)


def _tok_shape(width):
    return jax.ShapeDtypeStruct((N_TOK, width), BF16)


def _tok_spec(width):
    return pl.BlockSpec((TILE, width), lambda t: (t, 0))


def _mod_spec():
    return pl.BlockSpec((1, ADA_CHUNKS, D_MODEL), lambda t: (_mod_row(t), 0, 0))


def _full_spec(shape):
    return pl.BlockSpec(shape, lambda t: (0,) * len(shape))


def _hg_proj(xl, xc, mod, w_in, lb):
    n = w_in.shape[1]
    hs = _head_shape(8, LANES)
    hg = _head_shape(8, LANES, F32)
    return pl.pallas_call(
        _hg_proj_kernel,
        grid=(ALL_TILES,),
        in_specs=_stream_specs() + [_mod_spec(), _full_spec((D_MODEL, n)), _full_spec((1, D_MODEL))],
        out_specs=[_head_spec(8, LANES), _head_spec(8, LANES), _tok_spec(D_MODEL),
                   _head_spec(8, LANES), _head_spec(8, LANES), _head_spec(8, LANES), _head_spec(8, LANES)],
        out_shape=[hs, hs, _tok_shape(D_MODEL), hs, hg, hs, hg],
        compiler_params=_cparams("parallel"),
        name="hg_proj",
    )(xl, xc, mod, w_in, lb)


GLA_HEADS = 4
GLA_DK = 128
GLA_DV = 256
GLA_RANK = 16
GLA_GATE_NORM = 16.0


def _gla_proj_kernel(xl_ref, xc_ref, mod_ref, w_ref, wlr_ref, gkw_ref, gkb_ref, q_ref, k_ref, v_ref, gate_ref, gf_ref, gb_ref):
    h = _bf(_modulated(xl_ref, xc_ref, mod_ref, 0, 1))
    kd = GLA_HEADS * GLA_DK
    vd = GLA_HEADS * GLA_DV
    _store_heads(q_ref, _mm(h, w_ref[:, 0:kd]) * GLA_DK ** -0.5, GLA_DK)
    _store_heads(k_ref, _mm(h, w_ref[:, kd:2 * kd]), GLA_DK)
    _store_heads(v_ref, _mm(h, w_ref[:, 2 * kd:2 * kd + vd]), GLA_DV)
    gate_ref[...] = _bf(_mm(h, w_ref[:, 2 * kd + vd:2 * kd + 2 * vd]))
    for z, g_ref in ((0, gf_ref), (1, gb_ref)):
        lr = _mm(h, wlr_ref[:, z * LANES:(z + 1) * LANES])
        gk = _mm(_bf(lr), gkw_ref[z]) + gkb_ref[z]
        _store_heads(g_ref, _log_sigmoid(gk) / GLA_GATE_NORM, GLA_DK)


def _gla_proj(xl, xc, mod, w_main, w_lr, gk_w, gk_b):
    kd = GLA_HEADS * GLA_DK
    hk = _head_shape(GLA_HEADS, GLA_DK)
    return pl.pallas_call(
        _gla_proj_kernel,
        grid=(ALL_TILES,),
        in_specs=_stream_specs() + [_mod_spec(), _full_spec(w_main.shape), _full_spec(w_lr.shape),
                  _full_spec(gk_w.shape), _full_spec(gk_b.shape)],
        out_specs=[_head_spec(GLA_HEADS, GLA_DK), _head_spec(GLA_HEADS, GLA_DK), _head_spec(GLA_HEADS, GLA_DV),
                   _tok_spec(D_MODEL), _head_spec(GLA_HEADS, GLA_DK), _head_spec(GLA_HEADS, GLA_DK)],
        out_shape=[hk, hk, _head_shape(GLA_HEADS, GLA_DV), _tok_shape(D_MODEL),
                   _head_shape(GLA_HEADS, GLA_DK, F32), _head_shape(GLA_HEADS, GLA_DK, F32)],
        compiler_params=_cparams("parallel"),
        name="gla_proj",
    )(xl, xc, mod, w_main, w_lr, gk_w, gk_b)


DN_HEADS = 8
DN_QKV = 3 * D_MODEL
SHORT_CONV = 5
CHUNKS_PER_TILE = TILE // CHUNK


def _seg_scan(x, pos, axis, reverse):
    n = x.shape[axis]
    sh = 1
    while sh < CHUNK:
        if reverse:
            x = x + jnp.where(pos < CHUNK - sh, pltpu.roll(x, n - sh, axis), 0.0)
        else:
            x = x + jnp.where(pos >= sh, pltpu.roll(x, sh, axis), 0.0)
        sh *= 2
    return x


def _dn_gates(p, a_neg, dtb, idx, pos, axis):
    beta = _sigmoid(p)
    g = a_neg * _softplus(p + dtb)
    gp = _seg_scan(g, pos, axis, False)
    gs = _seg_scan(g, pos, axis, True)
    return jnp.where(idx < 2 * DN_HEADS, beta, jnp.where(idx < 3 * DN_HEADS, gp, gs))


def _dn_proj_kernel(xl_ref, xc_ref, mod_ref, w_ref, wba_ref, wbat_ref, ac_ref, dc_ref, ar_ref, dr_ref,
                    qkv_ref, gate_ref, bgc_ref, bgr_ref):
    h = _bf(_modulated(xl_ref, xc_ref, mod_ref, 0, 1))
    qkv_ref[...] = _bf(_mm(h, w_ref[:, 0:DN_QKV]))
    gate_ref[...] = _bf(_mm(h, w_ref[:, DN_QKV:DN_QKV + D_MODEL]))
    pc = _mm(h, wba_ref[...])
    lane = lax.broadcasted_iota(jnp.int32, pc.shape, 1)
    rowpos = lax.broadcasted_iota(jnp.int32, pc.shape, 0) % CHUNK
    bgc_ref[...] = _dn_gates(pc, ac_ref[...], dc_ref[...], lane, rowpos, 0)
    pr = _nt(wbat_ref[...], h)
    sub = lax.broadcasted_iota(jnp.int32, pr.shape, 0)
    lanepos = lax.broadcasted_iota(jnp.int32, pr.shape, 1) % CHUNK
    gr = _dn_gates(pr, ar_ref[...], dr_ref[...], sub, lanepos, 1)
    for c in range(CHUNKS_PER_TILE):
        bgr_ref[c] = gr[:, c * CHUNK:(c + 1) * CHUNK]


def _dn_proj(xl, xc, mod, w_main, w_ba, w_bat, a_col, d_col, a_row, d_row):
    nb = 4 * DN_HEADS
    return pl.pallas_call(
        _dn_proj_kernel,
        grid=(ALL_TILES,),
        in_specs=_stream_specs() + [_mod_spec(), _full_spec(w_main.shape), _full_spec(w_ba.shape),
                  _full_spec(w_bat.shape), _full_spec(a_col.shape), _full_spec(d_col.shape),
                  _full_spec(a_row.shape), _full_spec(d_row.shape)],
        out_specs=[_tok_spec(DN_QKV), _tok_spec(D_MODEL), _tok_spec(LANES),
                   pl.BlockSpec((CHUNKS_PER_TILE, nb, CHUNK), lambda t: (t, 0, 0))],
        out_shape=[_tok_shape(DN_QKV), _tok_shape(D_MODEL), jax.ShapeDtypeStruct((N_TOK, LANES), F32),
                   jax.ShapeDtypeStruct((N_TOK // CHUNK, nb, CHUNK), F32)],
        compiler_params=_cparams("parallel"),
        name="dn_proj",
    )(xl, xc, mod, w_main, w_ba, w_bat, a_col, d_col, a_row, d_row)


HALO = 2 * SUBLANES


def _dn_conv_kernel(pm_ref, pp_ref, pn_ref, cw_ref, q_ref, k_ref, v_ref):
    t = pl.program_id(0)
    is_lat = t < LAT_TILES
    first = jnp.logical_or(jnp.logical_not(is_lat), t % TILES_PER_SEQ == 0)
    last = jnp.logical_or(jnp.logical_not(is_lat), t % TILES_PER_SEQ == TILES_PER_SEQ - 1)
    keep_prev = jnp.where(first, 0.0, 1.0)
    keep_next = jnp.where(last, 0.0, 1.0)
    n_ext = TILE + 2 * HALO
    pad = SHORT_CONV // 2
    for s in range(DN_QKV // LANES):
        cols = slice(s * LANES, (s + 1) * LANES)
        ext = jnp.concatenate([pp_ref[:, cols].astype(F32) * keep_prev, pm_ref[:, cols].astype(F32),
                               pn_ref[:, cols].astype(F32) * keep_next], axis=0)
        acc = None
        for kk in range(SHORT_CONV):
            off = HALO - pad + kk
            tap = pltpu.roll(ext, n_ext - off, 0)[0:TILE] * cw_ref[kk:kk + 1, cols]
            acc = tap if acc is None else acc + tap
        a = _silu(acc)
        head = s % DN_HEADS
        if s < 2 * DN_HEADS:
            a = a * lax.rsqrt(jnp.sum(a * a, axis=1, keepdims=True) + 1e-6)
            if s < DN_HEADS:
                q_ref[head] = _bf(a * LANES ** -0.5)
            else:
                k_ref[head] = _bf(a)
        else:
            v_ref[head] = _bf(a)


def _dn_conv(p_qkv, conv_w):
    rows8 = TILE // HALO
    last8 = N_TOK // HALO - 1
    hs = _head_shape(DN_HEADS, LANES)
    return pl.pallas_call(
        _dn_conv_kernel,
        grid=(ALL_TILES,),
        in_specs=[_tok_spec(DN_QKV),
                  pl.BlockSpec((HALO, DN_QKV), lambda t: (jnp.maximum(t * rows8 - 1, 0), 0)),
                  pl.BlockSpec((HALO, DN_QKV), lambda t: (jnp.minimum((t + 1) * rows8, last8), 0)),
                  _full_spec(conv_w.shape)],
        out_specs=[_head_spec(DN_HEADS, LANES)] * 3,
        out_shape=[hs, hs, hs],
        compiler_params=_cparams("parallel"),
        name="dn_conv",
    )(p_qkv, p_qkv, p_qkv, conv_w)


LOG2_E = 1.4426950408889634
SIGN_BIT = 0x80000000


def _chunk_cumsum(x, ones_tri):
    hi = _bf(x)
    rest = x - hi.astype(F32)
    mid = _bf(rest)
    lo = _bf(rest - mid.astype(F32))
    return _mm(ones_tri, hi) + _mm(ones_tri, mid) + _mm(ones_tri, lo)


def _neg_abs(x):
    return pltpu.bitcast(pltpu.bitcast(x, jnp.uint32) | jnp.uint32(SIGN_BIT), F32)


def _gla_chunks(q, k, v, g, st, rev, row, r64, c64):
    n = range(len(q))
    q32 = [q[i].astype(F32) for i in n]
    k32 = [k[i].astype(F32) for i in n]
    ones_tri = {False: jnp.where(r64 >= c64, 1.0, 0.0).astype(BF16), True: jnp.where(r64 <= c64, 1.0, 0.0).astype(BF16)}
    cum = [_chunk_cumsum(g[i] * LOG2_E, ones_tri[rev[i]]) for i in n]
    g_end = [cum[i][0:1, :] if rev[i] else cum[i][CHUNK - 1:CHUNK, :] for i in n]
    att = [jnp.where(r64 == c64, _nt(_bf(q[i]), _bf(k[i])), 0.0) for i in n]
    edge = list(cum)
    s = 1
    while s < CHUNK:
        late = (row & s) != 0
        early = jnp.logical_not(late)
        rlate = (r64 & s) != 0
        clate = (c64 & s) != 0
        pair = {False: jnp.logical_and(rlate, jnp.logical_not(clate)), True: jnp.logical_and(clate, jnp.logical_not(rlate))}
        if 2 * s < CHUNK:
            shift = (2 * s).bit_length() - 1
            same = (r64 >> shift) == (c64 >> shift)
            pair = {d: jnp.logical_and(p, same) for d, p in pair.items()}
        for i in n:
            if rev[i]:
                cut = jnp.where(late, edge[i], pltpu.roll(edge[i], CHUNK - s, 0))
            else:
                cut = jnp.where(late, pltpu.roll(edge[i], s, 0), edge[i])
            e = jnp.exp2(_neg_abs(cum[i] - cut))
            q_side = early if rev[i] else late
            x16 = _bf(jnp.where(q_side, q32[i], k32[i]) * e)
            att[i] = att[i] + jnp.where(pair[rev[i]], _nt(x16, x16), 0.0)
            if rev[i]:
                edge[i] = jnp.where(late, pltpu.roll(edge[i], s, 0), edge[i])
            else:
                edge[i] = jnp.where(late, edge[i], pltpu.roll(edge[i], CHUNK - s, 0))
        s *= 2
    st16 = [_bf(st[i]) for i in n]
    o = [_mm(_bf(att[i]), _bf(v[i])) + _nt(_bf(q32[i] * jnp.exp2(cum[i])), st16[i]) for i in n]
    st_new = [st[i] * jnp.exp2(g_end[i]) + _tn(_bf(v[i]), _bf(k32[i] * jnp.exp2(g_end[i] - cum[i]))) for i in n]
    return o, st_new


def _scan_iotas():
    row = lax.broadcasted_iota(jnp.int32, (CHUNK, LANES), 0)
    r64 = lax.broadcasted_iota(jnp.int32, (CHUNK, CHUNK), 0)
    c64 = lax.broadcasted_iota(jnp.int32, (CHUNK, CHUNK), 1)
    return row, r64, c64


SCAN_BATCH = 2


def _scan_chains(in_refs, per_chain, n_heads):
    kinds = [[] for _ in range(per_chain)]
    rev = []
    for d in range(2):
        for bi in range(SCAN_BATCH):
            group = in_refs[(d * SCAN_BATCH + bi) * per_chain:(d * SCAN_BATCH + bi + 1) * per_chain]
            for h in range(n_heads):
                for kind, ref in zip(kinds, group):
                    kind.append((ref, h))
                rev.append(d == 1)
    return kinds, rev


def _scan_store(o, st, of_ref, ob_ref, st_ref, n_heads):
    i = 0
    for d, o_ref in ((0, of_ref), (1, ob_ref)):
        for bi in range(SCAN_BATCH):
            for h in range(n_heads):
                o_ref[h, bi * CHUNK:(bi + 1) * CHUNK, :] = _bf(o[i])
                st_ref[d, bi, h] = st[i]
                i += 1


def _gla_scan_kernel(*refs, n_heads):
    n_in = 2 * SCAN_BATCH * 4
    of_ref, ob_ref, st_ref = refs[n_in:]

    @pl.when(pl.program_id(1) == 0)
    def _():
        st_ref[...] = jnp.zeros_like(st_ref)

    row, r64, c64 = _scan_iotas()
    (q, k, v, g), rev = _scan_chains(refs[:n_in], 4, n_heads)
    load = lambda items: [ref[h] for ref, h in items]
    st = [st_ref[d, bi, h] for d in range(2) for bi in range(SCAN_BATCH) for h in range(n_heads)]
    o, st = _gla_chunks(load(q), load(k), load(v), load(g), st, rev, row, r64, c64)
    _scan_store(o, st, of_ref, ob_ref, st_ref, n_heads)


CTX_CHUNKS = CTX_LEN // CHUNK
LAT_CHUNKS = SEQ // CHUNK
SCAN_STEPS = CTX_CHUNKS + LAT_CHUNKS


def _fwd_chunk(b, n):
    return jnp.where(n < CTX_CHUNKS, N_LAT // CHUNK + b * CTX_CHUNKS + n, b * LAT_CHUNKS + n - CTX_CHUNKS)


def _bwd_chunk(b, n):
    return jnp.where(n < CTX_CHUNKS, N_LAT // CHUNK + b * CTX_CHUNKS + CTX_CHUNKS - 1 - n,
                     b * LAT_CHUNKS + SCAN_STEPS - 1 - n)


def _scan_in_maps(rank):
    def make(chunk_of, bi):
        def index_map(p, n):
            c = chunk_of(p * SCAN_BATCH + bi, n)
            return {1: (c, 0, 0), 2: (0, c, 0), 3: (c, 0)}[rank]
        return index_map
    return [make(chunk_of, bi) for chunk_of in (_fwd_chunk, _bwd_chunk) for bi in range(SCAN_BATCH)]


def _scan_out_pos(step, reverse):
    if not reverse:
        return step
    return jnp.where(step < CTX_CHUNKS, CTX_CHUNKS - 1 - step, CTX_CHUNKS + SCAN_STEPS - 1 - step)


def _scan_out_specs(n_heads, dv):
    return [pl.BlockSpec((n_heads, SCAN_BATCH * CHUNK, dv),
                         lambda p, n, reverse=reverse: (0, p * SCAN_STEPS + _scan_out_pos(n, reverse), 0))
            for reverse in (False, True)]


def _scan_out_block(t, c):
    is_lat = t < LAT_TILES
    b = jnp.where(is_lat, t // TILES_PER_SEQ, t - LAT_TILES)
    pos = jnp.where(is_lat, CTX_CHUNKS + (t % TILES_PER_SEQ) * CHUNKS_PER_TILE + c, c)
    return ((b // SCAN_BATCH) * SCAN_STEPS + pos) * SCAN_BATCH + b % SCAN_BATCH


def _gla_scan(q, k_f, k_b, v, g_f, g_b):
    n_heads, _, dk = q.shape
    dv = v.shape[2]
    maps = _scan_in_maps(2)
    kspec = lambda m: pl.BlockSpec((n_heads, CHUNK, dk), m)
    vspec = lambda m: pl.BlockSpec((n_heads, CHUNK, dv), m)
    in_specs, args = [], []
    for d, k_arr, g_arr in ((0, k_f, g_f), (1, k_b, g_b)):
        for bi in range(SCAN_BATCH):
            m = maps[d * SCAN_BATCH + bi]
            in_specs += [kspec(m), kspec(m), vspec(m), kspec(m)]
            args += [q, k_arr, v, g_arr]
    oshape = _head_shape(n_heads, dv)
    return pl.pallas_call(
        functools.partial(_gla_scan_kernel, n_heads=n_heads),
        grid=(BATCH // SCAN_BATCH, SCAN_STEPS),
        in_specs=in_specs,
        out_specs=_scan_out_specs(n_heads, dv),
        out_shape=[oshape, oshape],
        scratch_shapes=[pltpu.VMEM((2, SCAN_BATCH, n_heads, dv, dk), F32)],
        compiler_params=_cparams("parallel", "arbitrary"),
        name="gla_scan",
    )(*args)


def _lane_pick(x, lane, idx):
    return jnp.sum(jnp.where(lane == idx, x, 0.0), axis=1, keepdims=True)


def _dn_chunks(q, k, v, beta, g_col, g_row, st, rev, r64, c64):
    n = range(len(q))
    dk = k[0].shape[1]
    g_end = [g_row[i][:, 0:1] if rev[i] else g_row[i][:, CHUNK - 1:CHUNK] for i in n]
    causal = {False: r64 >= c64, True: r64 <= c64}
    gam = [jnp.where(causal[rev[i]], jnp.exp(jnp.minimum(g_col[i] - g_row[i], 0.0)), 0.0) for i in n]
    q32 = [q[i].astype(F32) for i in n]
    k32 = [k[i].astype(F32) for i in n]
    kb = [k32[i] * beta[i] for i in n]
    k16 = [_bf(k[i]) for i in n]
    m = [jnp.where(r64 == c64, 0.0, _nt(_bf(kb[i]), k16[i]) * gam[i]) for i in n]
    att = [_nt(_bf(q[i]), k16[i]) * gam[i] for i in n]
    y = None
    s = 1
    while s < CHUNK:
        shift = (2 * s).bit_length() - 1
        rlate = (r64 & s) != 0
        clate = (c64 & s) != 0
        pair = {False: jnp.logical_and(rlate, jnp.logical_not(clate)), True: jnp.logical_and(clate, jnp.logical_not(rlate))}
        if 2 * s < CHUNK:
            same = (r64 >> shift) == (c64 >> shift)
            pair = {d: jnp.logical_and(p, same) for d, p in pair.items()}
        c = [jnp.where(pair[rev[i]], m[i], 0.0) for i in n]
        if y is None:
            y = [-c[i] for i in n]
        else:
            c16 = [_bf(c[i]) for i in n]
            y16 = [_bf(y[i]) for i in n]
            p = [c[i] + _mm(y16[i], c16[i]) for i in n]
            y = [y[i] - (p[i] + _mm(_bf(p[i]), y16[i])) for i in n]
        s *= 2
    e_col = [jnp.exp(g_col[i]) for i in n]
    rhs = [jnp.concatenate([kb[i] * e_col[i], v[i].astype(F32) * beta[i]], axis=1) for i in n]
    wu = [rhs[i] + _mm(_bf(y[i]), _bf(rhs[i])) for i in n]
    st16 = [_bf(st[i]) for i in n]
    v_new = [wu[i][:, dk:] - _nt(_bf(wu[i][:, :dk]), st16[i]) for i in n]
    o = [_nt(_bf(q32[i] * e_col[i]), st16[i]) + _mm(_bf(att[i]), _bf(v_new[i])) for i in n]
    st_new = [st[i] * jnp.exp(g_end[i]) + _tn(_bf(v_new[i]), _bf(k32[i] * jnp.exp(g_end[i] - g_col[i]))) for i in n]
    return o, st_new


def _dn_scan_kernel(*refs):
    n_in = 2 * SCAN_BATCH * 5
    of_ref, ob_ref, st_ref = refs[n_in:]

    @pl.when(pl.program_id(1) == 0)
    def _():
        st_ref[...] = jnp.zeros_like(st_ref)

    _, r64, c64 = _scan_iotas()
    (q, k, v, gc, gr), rev = _scan_chains(refs[:n_in], 5, DN_HEADS)
    load = lambda items: [ref[h] for ref, h in items]
    tables = {id(ref): ref[...] for ref, _ in gc}
    beta = [tables[id(ref)][:, d * DN_HEADS + h:d * DN_HEADS + h + 1] for (ref, h), d in zip(gc, rev)]
    g_col = [tables[id(ref)][:, (2 + d) * DN_HEADS + h:(2 + d) * DN_HEADS + h + 1] for (ref, h), d in zip(gc, rev)]
    g_row = [ref[0, (2 + d) * DN_HEADS + h:(2 + d) * DN_HEADS + h + 1, :] for (ref, h), d in zip(gr, rev)]
    st = [st_ref[d, bi, h] for d in range(2) for bi in range(SCAN_BATCH) for h in range(DN_HEADS)]
    o, st = _dn_chunks(load(q), load(k), load(v), beta, g_col, g_row, st, rev, r64, c64)
    _scan_store(o, st, of_ref, ob_ref, st_ref, DN_HEADS)


def _dn_scan(q, k, v, bgc, bgr):
    head_maps, col_maps, row_maps = _scan_in_maps(2), _scan_in_maps(3), _scan_in_maps(1)
    in_specs, args = [], []
    for i in range(2 * SCAN_BATCH):
        in_specs += [pl.BlockSpec((DN_HEADS, CHUNK, LANES), head_maps[i])] * 3
        in_specs += [pl.BlockSpec((CHUNK, LANES), col_maps[i]), pl.BlockSpec((1, 4 * DN_HEADS, CHUNK), row_maps[i])]
        args += [q, k, v, bgc, bgr]
    oshape = _head_shape(DN_HEADS, LANES)
    return pl.pallas_call(
        _dn_scan_kernel,
        grid=(BATCH // SCAN_BATCH, SCAN_STEPS),
        in_specs=in_specs,
        out_specs=_scan_out_specs(DN_HEADS, LANES),
        out_shape=[oshape, oshape],
        scratch_shapes=[pltpu.VMEM((2, SCAN_BATCH, DN_HEADS, LANES, LANES), F32)],
        compiler_params=_cparams("parallel", "arbitrary"),
        name="dn_scan",
    )(*args)


def _layer_norm(z, g, b):
    mu = jnp.mean(z, axis=1, keepdims=True)
    zc = z - mu
    var = jnp.mean(zc * zc, axis=1, keepdims=True)
    return zc * lax.rsqrt(var + LN_EPS) * g + b


def _post_kernel(*refs, n_heads):
    o_refs, refs = refs[:2 * CHUNKS_PER_TILE], refs[2 * CHUNKS_PER_TILE:]
    gate_ref, xl_ref, xc_ref, mod_ref, nw_ref, wo_ref, lg_ref, lb_ref, rw_ref, rb_ref = refs[:10]
    out_refs = refs[10:]
    x1_refs, (h2_ref, logit_ref) = out_refs[:-2], out_refs[-2:]
    x_rows = _stream_rows(xl_ref, xc_ref)
    groups = [slice(g * TILE // POST_GROUPS, (g + 1) * TILE // POST_GROUPS) for g in range(POST_GROUPS)]
    chunks_per_group = CHUNKS_PER_TILE // POST_GROUPS
    on = []
    for g, rows in enumerate(groups):
        parts = []
        for h in range(n_heads):
            o = jnp.concatenate([o_refs[c][h].astype(F32) + o_refs[CHUNKS_PER_TILE + c][h].astype(F32)
                                 for c in range(g * chunks_per_group, (g + 1) * chunks_per_group)], axis=0)
            parts.append(o * lax.rsqrt(jnp.mean(o * o, axis=1, keepdims=True) + RMS_EPS) * nw_ref[...])
        on.append(jnp.concatenate(parts, axis=1) * _silu(gate_ref[rows, :].astype(F32)))
    y = [_mm(_bf(a), wo_ref[...]) for a in on]
    z = [ALPHA * x_rows[rows] + mod_ref[0, 2:3, :] * yg for rows, yg in zip(groups, y)]
    x1 = [_layer_norm(zg, lg_ref[...], lb_ref[...]) for zg in z]
    h2 = [a * (1.0 + mod_ref[0, 4:5, :]) + mod_ref[0, 3:4, :] for a in x1]
    logits = [jnp.dot(a, rw_ref[...], precision=HIGHEST, preferred_element_type=F32) + rb_ref[...] for a in h2]
    _stream_store(x1_refs, jnp.concatenate(x1, axis=0))
    h2_ref[...] = _pack_halves(jnp.concatenate(h2, axis=0))
    logit_ref[...] = jnp.concatenate(logits, axis=0)


POST_GROUPS = 2


def _post(o_f, o_b, gate, xl, xc, mod, norm_w, w_out, ln_g, ln_b, rw, rb, with_ctx):
    n_heads, _, dv = o_f.shape
    n_tiles = ALL_TILES if with_ctx else LAT_TILES
    n_rows = n_tiles * TILE
    tok = lambda w: jax.ShapeDtypeStruct((n_rows, w), F32)
    return pl.pallas_call(
        functools.partial(_post_kernel, n_heads=n_heads),
        grid=(n_tiles,),
        in_specs=[pl.BlockSpec((n_heads, CHUNK, dv), lambda t, c=c: (0, _scan_out_block(t, c), 0))
                  for _ in range(2) for c in range(CHUNKS_PER_TILE)] + [_tok_spec(D_MODEL)] + _stream_specs() + [
            _mod_spec(), _full_spec(norm_w.shape), _full_spec(w_out.shape), _full_spec(ln_g.shape),
            _full_spec(ln_b.shape), _full_spec(rw.shape), _full_spec(rb.shape)],
        out_specs=_stream_specs()[:2 if with_ctx else 1] + [_tok_spec(PACKED_D), _tok_spec(LANES)],
        out_shape=_stream_shapes(with_ctx) + [tok(PACKED_D), tok(LANES)],
        compiler_params=_cparams("arbitrary"),
        name="post_mixer",
    )(*([o_f] * CHUNKS_PER_TILE + [o_b] * CHUNKS_PER_TILE), gate, xl, xc, mod, norm_w, w_out, ln_g, ln_b, rw, rb)


def _route_kernel(logit_ref, route_ref, slot_ref, count_ref, run_ref, start_ref, table_ref):
    sweep = pl.program_id(0)
    t = pl.program_id(1)
    lane = lax.broadcasted_iota(jnp.int32, (TILE, LANES), 1)
    lanef = lane.astype(F32)

    def add_counts(onehot):
        new_run = run_ref[0:1, :] + jnp.sum(onehot, axis=0, keepdims=True)
        run_ref[...] = jnp.broadcast_to(new_run, run_ref.shape)
        return new_run

    @pl.when(t == 0)
    def _():
        run_ref[...] = jnp.zeros_like(run_ref)

    @pl.when(sweep == 0)
    def _():
        neg = jnp.float32(-jnp.inf)
        l = jnp.where(lane < N_EXPERTS, logit_ref[...], neg)
        vals, picks = [], []
        onehot = jnp.zeros((TILE, LANES), F32)
        for _ in range(TOP_K):
            m = jnp.max(l, axis=1, keepdims=True)
            pick = jnp.min(jnp.where(l == m, lanef, float(LANES)), axis=1, keepdims=True)
            hot = lanef == pick
            l = jnp.where(hot, neg, l)
            onehot = onehot + jnp.where(hot, 1.0, 0.0)
            vals.append(m)
            picks.append(pick)
        es = [jnp.exp(v - vals[0]) for v in vals]
        denom = es[0] + es[1] + es[2] + es[3]
        out = jnp.zeros((TILE, LANES), F32)
        for kk in range(TOP_K):
            out = jnp.where(lane == kk, picks[kk], out)
            out = jnp.where(lane == ROUTE_WEIGHT + kk, es[kk] / denom, out)
        table_ref[t] = out
        new_run = add_counts(onehot)

        @pl.when(t == pl.num_programs(1) - 1)
        def _():
            lane8 = lax.broadcasted_iota(jnp.int32, (SUBLANES, LANES), 1)
            size = jnp.ceil(jnp.broadcast_to(new_run, (SUBLANES, LANES)) * (1.0 / EXPERT_TILE)) * EXPERT_TILE
            size = jnp.where(lane8 < N_EXPERTS, size, 0.0)
            ends = size
            sh = 1
            while sh < N_EXPERTS:
                ends = ends + jnp.where(lane8 >= sh, pltpu.roll(ends, sh, 1), 0.0)
                sh *= 2
            start_ref[...] = ends - size

    @pl.when(sweep == 1)
    def _():
        out = table_ref[t]
        hots = [lanef == out[:, kk:kk + 1] for kk in range(TOP_K)]
        onehot = jnp.zeros((TILE, LANES), F32)
        for hot in hots:
            onehot = onehot + jnp.where(hot, 1.0, 0.0)
        r = lax.broadcasted_iota(jnp.int32, (TILE, TILE), 0)
        c = lax.broadcasted_iota(jnp.int32, (TILE, TILE), 1)
        before = jnp.where(c < r, 1.0, 0.0).astype(BF16)
        slot_of = _mm(before, _bf(onehot)) + run_ref[0:1, :] + start_ref[0:1, :]
        for kk in range(TOP_K):
            slot = jnp.sum(jnp.where(hots[kk], slot_of, 0.0), axis=1, keepdims=True)
            out = jnp.where(lane == ROUTE_SLOT + kk, slot, out)
        route_ref[...] = out
        slot_ref[...] = out.T[ROUTE_WEIGHT:ROUTE_WEIGHT + SUBLANES, :].astype(jnp.int32)
        count_ref[...] = jnp.broadcast_to(add_counts(onehot), count_ref.shape)


ROUTE_WEIGHT = SUBLANES
ROUTE_SLOT = SUBLANES + TOP_K


def _route(logits):
    n_tok = logits.shape[0]
    n_tiles = n_tok // TILE
    return pl.pallas_call(
        _route_kernel,
        grid=(2, n_tiles),
        in_specs=[pl.BlockSpec((TILE, LANES), lambda s, t: (t, 0))],
        out_specs=[pl.BlockSpec((TILE, LANES), lambda s, t: (s * t, 0)),
                   pl.BlockSpec((SUBLANES, TILE), lambda s, t: (0, s * t)),
                   pl.BlockSpec((SUBLANES, LANES), lambda s, t: (0, 0))],
        out_shape=[jax.ShapeDtypeStruct(logits.shape, F32), jax.ShapeDtypeStruct((SUBLANES, n_tok), jnp.int32),
                   jax.ShapeDtypeStruct((SUBLANES, LANES), F32)],
        scratch_shapes=[pltpu.VMEM((SUBLANES, LANES), F32), pltpu.VMEM((SUBLANES, LANES), F32),
                        pltpu.VMEM((n_tiles, TILE, LANES), F32)],
        compiler_params=_cparams("arbitrary", "arbitrary"),
        name="route",
    )(logits)


GATHER_ROWS = 16
GATHER_BUFS = 4


def _sc_gather(table, idx):
    n_rows = idx.shape[0]
    d = table.shape[1]
    info = plsc.get_sparse_core_info()
    n_workers = info.num_cores * info.num_subcores
    rows_per_worker = n_rows // n_workers
    n_chunks = rows_per_worker // GATHER_ROWS
    assert rows_per_worker * n_workers == n_rows and n_chunks * GATHER_ROWS == rows_per_worker
    assert n_chunks % GATHER_BUFS == 0
    mesh = plsc.VectorSubcoreMesh(core_axis_name="c", subcore_axis_name="s")
    scratch = ([pltpu.VMEM((GATHER_ROWS,), jnp.int32)] * GATHER_BUFS
               + [pltpu.VMEM((GATHER_ROWS, d), table.dtype)] * GATHER_BUFS
               + [pltpu.SemaphoreType.DMA] * (2 * GATHER_BUFS))

    def body(table_hbm, idx_hbm, out_hbm, *scr):
        idx_v, rows_v = scr[:GATHER_BUFS], scr[GATHER_BUFS:2 * GATHER_BUFS]
        gather_sem, write_sem = scr[2 * GATHER_BUFS:3 * GATHER_BUFS], scr[3 * GATHER_BUFS:]
        worker = lax.axis_index("s") * info.num_cores + lax.axis_index("c")
        base = worker * rows_per_worker

        @pl.loop(0, n_chunks, step=GATHER_BUFS)
        def _(c0):
            offs = [pl.multiple_of(base + (c0 + b) * GATHER_ROWS, SUBLANES) for b in range(GATHER_BUFS)]
            gathers = []
            for b in range(GATHER_BUFS):
                pltpu.sync_copy(idx_hbm.at[pl.ds(offs[b], GATHER_ROWS)], idx_v[b])
                gathers.append(pltpu.async_copy(table_hbm.at[idx_v[b]], rows_v[b], gather_sem[b]))
            writes = []
            for b in range(GATHER_BUFS):
                gathers[b].wait()
                writes.append(pltpu.async_copy(rows_v[b], out_hbm.at[pl.ds(offs[b], GATHER_ROWS)], write_sem[b]))
            for w in writes:
                w.wait()

    return pl.kernel(body, out_type=jax.ShapeDtypeStruct((n_rows, d), table.dtype), mesh=mesh,
                     scratch_types=scratch, name="sc_gather")(table, idx)


def _sc_scatter(src, idx, n_out):
    n, d = src.shape
    info = plsc.get_sparse_core_info()
    n_workers = info.num_cores * info.num_subcores
    rows_per_worker = n // n_workers
    n_chunks = rows_per_worker // GATHER_ROWS
    assert rows_per_worker * n_workers == n and n_chunks * GATHER_ROWS == rows_per_worker
    assert n_chunks % GATHER_BUFS == 0
    mesh = plsc.VectorSubcoreMesh(core_axis_name="c", subcore_axis_name="s")
    scratch = ([pltpu.VMEM((GATHER_ROWS,), jnp.int32)] * (GATHER_BUFS * TOP_K)
               + [pltpu.VMEM((GATHER_ROWS, d), src.dtype)] * GATHER_BUFS + [pltpu.SemaphoreType.DMA] * GATHER_BUFS)

    def body(src_hbm, idx_hbm, out_hbm, *scr):
        idx_v = scr[:GATHER_BUFS * TOP_K]
        rows_v = scr[GATHER_BUFS * TOP_K:GATHER_BUFS * TOP_K + GATHER_BUFS]
        sem = scr[GATHER_BUFS * TOP_K + GATHER_BUFS:]
        worker = lax.axis_index("s") * info.num_cores + lax.axis_index("c")
        base = worker * rows_per_worker

        @pl.loop(0, n_chunks, step=GATHER_BUFS)
        def _(c0):
            copies = []
            for b in range(GATHER_BUFS):
                off = pl.multiple_of(base + (c0 + b) * GATHER_ROWS, SUBLANES)
                pltpu.sync_copy(src_hbm.at[pl.ds(off, GATHER_ROWS)], rows_v[b])
                for kk in range(TOP_K):
                    pltpu.sync_copy(idx_hbm.at[pl.ds(pl.multiple_of(kk * n + off, SUBLANES), GATHER_ROWS)],
                                    idx_v[b * TOP_K + kk])
                for kk in range(TOP_K):
                    copies.append(pltpu.async_copy(rows_v[b], out_hbm.at[idx_v[b * TOP_K + kk]], sem[b]))
            for c in copies:
                c.wait()

    return pl.kernel(body, out_type=jax.ShapeDtypeStruct((n_out, d), src.dtype), mesh=mesh,
                     scratch_types=scratch, name="sc_scatter")(src, idx)


PAIR_BLOCK = 2 * LANES


def _expert_kernel(te_ref, tv_ref, nt_ref, xs_ref, w1_ref, b1_ref, w2_ref, b2_ref, ys_ref, w1p_ref, w2b_ref):
    i = pl.program_id(0)
    valid = i < nt_ref[0]
    new_expert = jnp.logical_or(i == 0, te_ref[i] != te_ref[jnp.maximum(i - 1, 0)])

    @pl.when(jnp.logical_and(valid, new_expert))
    def _():
        r = lax.broadcasted_iota(jnp.int32, (PAIR_BLOCK, PAIR_BLOCK), 0)
        c = lax.broadcasted_iota(jnp.int32, (PAIR_BLOCK, PAIR_BLOCK), 1)
        src = jnp.where(c < LANES, 2 * c, 2 * (c - LANES) + 1)
        perm = jnp.where(r == src, 1.0, 0.0).astype(BF16)
        for b in range(2 * D_FF // PAIR_BLOCK):
            cols = slice(b * PAIR_BLOCK, (b + 1) * PAIR_BLOCK)
            w1p_ref[:, cols] = _bf(_mm(_bf(w1_ref[0, 0, :, cols]), perm))
        w2b_ref[...] = _bf(w2_ref[0, 0])

    @pl.when(valid)
    def _():
        rowi = lax.broadcasted_iota(jnp.int32, (EXPERT_TILE, D_MODEL), 0)
        x = _bf(jnp.where(rowi < tv_ref[i], _unpack_halves(xs_ref[...]), 0.0))
        u = _mm(x, w1p_ref[...]) + b1_ref[0]
        parts = []
        for b in range(2 * D_FF // PAIR_BLOCK):
            glu = jnp.minimum(u[:, b * PAIR_BLOCK:b * PAIR_BLOCK + LANES], SWIGLU_LIMIT)
            lin = jnp.clip(u[:, b * PAIR_BLOCK + LANES:(b + 1) * PAIR_BLOCK], -SWIGLU_LIMIT, SWIGLU_LIMIT)
            parts.append(glu * _sigmoid(SWIGLU_ALPHA * glu) * (lin + 1.0))
        act = jnp.concatenate(parts, axis=1)
        ys_ref[...] = _pack_halves(_mm(_bf(act), w2b_ref[...]) + b2_ref[0])


def _experts(layer, tile_expert, tile_valid, n_tiles, xs, w1, b1p, w2, b2):
    tile_map = lambda i, te, tv, nt: (jnp.minimum(i, nt[0] - 1), 0)
    wmap = lambda i, te, tv, nt: (te[i], 0, 0)
    lmap = lambda i, te, tv, nt: (layer, te[i], 0, 0)
    grid_spec = pltpu.PrefetchScalarGridSpec(
        num_scalar_prefetch=3,
        grid=(N_SLOT_TILES,),
        in_specs=[pl.BlockSpec((EXPERT_TILE, PACKED_D), tile_map),
                  pl.BlockSpec((1, 1, D_MODEL, 2 * D_FF), lmap), pl.BlockSpec((1, 1, 2 * D_FF), wmap),
                  pl.BlockSpec((1, 1, D_FF, D_MODEL), lmap), pl.BlockSpec((1, 1, D_MODEL), wmap)],
        out_specs=pl.BlockSpec((EXPERT_TILE, PACKED_D), tile_map),
        scratch_shapes=[pltpu.VMEM((D_MODEL, 2 * D_FF), BF16), pltpu.VMEM((D_FF, D_MODEL), BF16)],
    )
    return pl.pallas_call(
        _expert_kernel,
        grid_spec=grid_spec,
        out_shape=jax.ShapeDtypeStruct((N_SLOTS, PACKED_D), F32),
        compiler_params=pltpu.CompilerParams(dimension_semantics=("arbitrary",), vmem_limit_bytes=EXPERT_VMEM_LIMIT),
        name="experts",
    )(tile_expert, tile_valid, n_tiles, xs, w1, b1p, w2, b2)


def _combine_kernel(y_ref, route_ref, x1l_ref, x1c_ref, mod_ref, lg_ref, lb_ref, *x2_refs):
    route = route_ref[...]
    lane = lax.broadcasted_iota(jnp.int32, route.shape, 1)
    f = None
    for kk in range(TOP_K):
        term = _lane_pick(route, lane, ROUTE_WEIGHT + kk) * _unpack_halves(y_ref[kk])
        f = term if f is None else f + term
    z = ALPHA * _stream_rows(x1l_ref, x1c_ref) + mod_ref[0, 5:6, :] * f
    _stream_store(x2_refs, _layer_norm(z, lg_ref[...], lb_ref[...]))


def _combine(y_tok, route, x1l, x1c, mod, ln_g, ln_b, with_ctx):
    n_tiles = ALL_TILES if with_ctx else LAT_TILES
    return pl.pallas_call(
        _combine_kernel,
        grid=(n_tiles,),
        in_specs=[pl.BlockSpec((TOP_K, TILE, PACKED_D), lambda t: (0, t, 0)), _tok_spec(LANES)] + _stream_specs() + [
            _mod_spec(), _full_spec(ln_g.shape), _full_spec(ln_b.shape)],
        out_specs=_stream_specs()[:2 if with_ctx else 1],
        out_shape=_stream_shapes(with_ctx),
        compiler_params=_cparams("arbitrary"),
        name="combine",
    )(y_tok, route, x1l, x1c, mod, ln_g, ln_b)


def _moe(layer, h2, logits, x1l, x1c, mod, ln_g, ln_b, w1, b1p, w2, b2, with_ctx):
    n_tok = h2.shape[0]
    route, slot_rows, counts = _route(logits)
    slot_by_choice = slot_rows[TOP_K:2 * TOP_K].reshape(-1)
    cnt = counts[0, :N_EXPERTS].astype(jnp.int32)
    padded = (cnt + EXPERT_TILE - 1) // EXPERT_TILE * EXPERT_TILE
    ends = jnp.cumsum(padded)
    starts = ends - padded
    tile_start = jnp.arange(N_SLOT_TILES, dtype=jnp.int32) * EXPERT_TILE
    tile_expert = jnp.minimum(jnp.sum(tile_start[:, None] >= ends[None, :], axis=1), N_EXPERTS - 1).astype(jnp.int32)
    tile_valid = jnp.clip(starts[tile_expert] + cnt[tile_expert] - tile_start, 0, EXPERT_TILE).astype(jnp.int32)
    n_tiles = (ends[N_EXPERTS - 1:] // EXPERT_TILE).astype(jnp.int32)
    xs = _sc_scatter(h2, slot_by_choice, N_SLOTS)
    ys = _experts(layer, tile_expert, tile_valid, n_tiles, xs, w1, b1p, w2, b2)
    y_tok = _sc_gather(ys, slot_by_choice).reshape(TOP_K, n_tok, PACKED_D)
    return _combine(y_tok, route, x1l, x1c, mod, ln_g, ln_b, with_ctx)


def _lat_to_col_major(lat):
    return lat.reshape(BATCH, SEQ // GRID_W, GRID_W, D_MODEL).transpose(0, 2, 1, 3).reshape(N_LAT, D_MODEL)


def _lat_to_row_major(lat):
    return lat.reshape(BATCH, GRID_W, SEQ // GRID_W, D_MODEL).transpose(0, 2, 1, 3).reshape(N_LAT, D_MODEL)


def kernel(x, c, ctx, c_ctx, ada_w, ada_b, ln_g, ln_b, dn_w_in, dn_conv_w, dn_a_log, dn_dt_bias, dn_norm_w, dn_w_out,
           hg_w_in, hg_lower_bound, hg_norm_w, hg_w_out, gla_w_in, gla_gk_w, gla_gk_b, gla_norm_w, gla_w_out,
           router_w, router_b, exp_w1, exp_b1, exp_w2, exp_b2):
    xl = x.reshape(N_LAT, D_MODEL)
    xc = ctx.reshape(N_CTX, D_MODEL)
    cc = jnp.zeros((MOD_ROWS, D_MODEL), F32).at[:BATCH].set(c).at[BATCH].set(c_ctx)
    mods = _ada_all(cc, ada_w, ada_b).reshape(DEPTH, MOD_ROWS, ADA_CHUNKS, D_MODEL)
    lb_soft = jax.nn.softmax(hg_lower_bound.astype(F32), axis=0)
    lower_bounds = jnp.cumsum(lb_soft, axis=0) - lb_soft[0]

    for i in range(DEPTH):
        last = i == DEPTH - 1
        kind, j = i % 3, i // 3
        mod = mods[i]
        col_major = i % 2 == 1
        if col_major:
            xl = _lat_to_col_major(xl)

        if kind == 0:
            nq = 2 * D_MODEL + 2 * D_MODEL
            w = dn_w_in[j]
            w_ba = jnp.zeros((D_MODEL, LANES), F32).at[:, :4 * DN_HEADS].set(w[:, nq:])
            a_neg = jnp.zeros((LANES,), F32).at[2 * DN_HEADS:4 * DN_HEADS].set(-jnp.exp(dn_a_log[j].astype(F32)).reshape(-1))
            dtb = jnp.zeros((LANES,), F32).at[2 * DN_HEADS:4 * DN_HEADS].set(dn_dt_bias[j].astype(F32).reshape(-1))
            nb = 4 * DN_HEADS
            p_qkv, gate, bgc, bgr = _dn_proj(xl, xc, mod, _bf(w[:, :nq]), _bf(w_ba), _bf(w[:, nq:].T),
                                            a_neg.reshape(1, LANES), dtb.reshape(1, LANES),
                                            a_neg[:nb].reshape(nb, 1), dtb[:nb].reshape(nb, 1))
            cw = jnp.zeros((SUBLANES, DN_QKV), F32).at[:SHORT_CONV].set(dn_conv_w[j])
            q, k, v = _dn_conv(p_qkv, cw)
            o_f, o_b = _dn_scan(q, k, v, bgc, bgr)
            norm_w, w_out = dn_norm_w[j], dn_w_out[j]
        elif kind == 1:
            q, v, gate, k_f, g_f, k_b, g_b = _hg_proj(xl, xc, mod, _bf(hg_w_in[j]), lower_bounds[i].reshape(1, D_MODEL))
            o_f, o_b = _gla_scan(q, k_f, k_b, v, g_f, g_b)
            norm_w, w_out = hg_norm_w[j], hg_w_out[j]
        else:
            kd = GLA_HEADS * GLA_DK
            n_main = 2 * kd + 2 * D_MODEL
            w = gla_w_in[j]
            w_lr = jnp.zeros((D_MODEL, 2 * LANES), F32)
            w_lr = w_lr.at[:, :GLA_RANK].set(w[:, n_main:n_main + GLA_RANK])
            w_lr = w_lr.at[:, LANES:LANES + GLA_RANK].set(w[:, n_main + GLA_RANK:])
            gk_w = jnp.zeros((2, LANES, kd), F32).at[:, :GLA_RANK].set(gla_gk_w[j])
            q, k, v, gate, g_f, g_b = _gla_proj(xl, xc, mod, _bf(w[:, :n_main]), _bf(w_lr), _bf(gk_w),
                                                gla_gk_b[j].reshape(2, 1, kd))
            o_f, o_b = _gla_scan(q, k, k, v, g_f, g_b)
            norm_w, w_out = gla_norm_w[j], gla_w_out[j]

        with_ctx = not last
        rw = jnp.zeros((D_MODEL, LANES), F32).at[:, :N_EXPERTS].set(router_w[i])
        rb = jnp.zeros((1, LANES), F32).at[0, :N_EXPERTS].set(router_b[i])
        *x1, h2, logits = _post(o_f, o_b, gate, xl, xc, mod, norm_w.reshape(1, -1), _bf(w_out),
                                ln_g[i, 0].reshape(1, D_MODEL), ln_b[i, 0].reshape(1, D_MODEL), rw, rb, with_ctx)
        b1p = exp_b1[i].reshape(N_EXPERTS, 2 * D_FF // PAIR_BLOCK, LANES, 2).transpose(0, 1, 3, 2)
        x2 = _moe(i, h2, logits, x1[0], x1[1] if with_ctx else xc, mod,
                  ln_g[i, 1].reshape(1, D_MODEL), ln_b[i, 1].reshape(1, D_MODEL),
                  exp_w1, b1p.reshape(N_EXPERTS, 1, 2 * D_FF), exp_w2, exp_b2[i][:, None, :], with_ctx)
        xl = _lat_to_row_major(x2[0]) if col_major else x2[0]
        if with_ctx:
            xc = x2[1]
    return xl.reshape(BATCH, SEQ, D_MODEL)
```

```python
import functools

import jax
import jax.numpy as jnp
from jax import lax
from jax.experimental import pallas as pl
from jax.experimental.pallas import tpu as pltpu
from jax.experimental.pallas import tpu_sc as plsc

F32 = jnp.float32
BF16 = jnp.bfloat16
HIGHEST = lax.Precision.HIGHEST

D_MODEL = 1024
BATCH = 8
SEQ = 2048
CTX_LEN = 256
DEPTH = 4
GRID_W = 64
CHUNK = 64
ADA_CHUNKS = 6
N_LAT = BATCH * SEQ
N_CTX = BATCH * CTX_LEN
N_TOK = N_LAT + N_CTX
TILE = 256
LAT_TILES = N_LAT // TILE
ALL_TILES = N_TOK // TILE
TILES_PER_SEQ = SEQ // TILE
MOD_ROWS = 16
LANES = 128
SUBLANES = 8

N_EXPERTS = 32
TOP_K = 4
D_FF = D_MODEL
SWIGLU_LIMIT = 7.0
SWIGLU_ALPHA = 1.702
EXPERT_TILE = 512
N_SLOT_TILES = N_TOK * TOP_K // EXPERT_TILE + N_EXPERTS
N_SLOTS = N_SLOT_TILES * EXPERT_TILE

LN_EPS = 1e-5
RMS_EPS = 1e-6
ALPHA = (2.0 * DEPTH) ** 0.25

VMEM_LIMIT = 48 * 1024 * 1024
EXPERT_VMEM_LIMIT = (2 * 4 * (D_MODEL * 2 * D_FF + D_FF * D_MODEL) + 2 * (D_MODEL * 2 * D_FF + D_FF * D_MODEL)
                     + 4 * 4 * EXPERT_TILE * D_MODEL + 4 * EXPERT_TILE * (2 * D_FF + 2 * D_MODEL) + (4 << 20))


def _cparams(*sem):
    return pltpu.CompilerParams(dimension_semantics=sem, vmem_limit_bytes=VMEM_LIMIT)


def _sigmoid(x):
    return 1.0 / (1.0 + jnp.exp(-x))


def _silu(x):
    return x * _sigmoid(x)


def _softplus(x):
    return jnp.maximum(x, 0.0) + jnp.log(1.0 + jnp.exp(-jnp.abs(x)))


def _log_sigmoid(x):
    return -_softplus(-x)


def _nt(a, b):
    return lax.dot_general(a, b, (((1,), (1,)), ((), ())), preferred_element_type=F32)


def _tn(a, b):
    return lax.dot_general(a, b, (((0,), (0,)), ((), ())), preferred_element_type=F32)


def _mm(a, b):
    return jnp.dot(a, b, preferred_element_type=F32)


def _bf(x):
    return x.astype(BF16)


PACKED_D = D_MODEL // 2
HIGH_HALF = 0xFFFF0000


def _pack_halves(x):
    m = x.shape[1] // 2
    lo = pltpu.bitcast(x[:, :m].astype(BF16).astype(F32), jnp.uint32) >> 16
    hi = pltpu.bitcast(x[:, m:].astype(BF16).astype(F32), jnp.uint32) & jnp.uint32(HIGH_HALF)
    return pltpu.bitcast(hi | lo, F32)


def _unpack_halves(p):
    u = pltpu.bitcast(p, jnp.uint32)
    return jnp.concatenate([pltpu.bitcast(u << 16, F32), pltpu.bitcast(u & jnp.uint32(HIGH_HALF), F32)], axis=1)


def _mod_row(t):
    return jnp.minimum(t // TILES_PER_SEQ, BATCH)


ADA_NBLK = 1536


def _ada_kernel(c_ref, w_ref, b_ref, o_ref):
    s = _silu(c_ref[...])
    o_ref[0] = jnp.dot(s, w_ref[0], precision=HIGHEST, preferred_element_type=F32) + b_ref[0]


def _ada_all(cc, ada_w, ada_b):
    n = ADA_CHUNKS * D_MODEL
    return pl.pallas_call(
        _ada_kernel,
        grid=(DEPTH, n // ADA_NBLK),
        in_specs=[
            pl.BlockSpec((MOD_ROWS, D_MODEL), lambda i, j: (0, 0)),
            pl.BlockSpec((1, D_MODEL, ADA_NBLK), lambda i, j: (i, 0, j)),
            pl.BlockSpec((1, 1, ADA_NBLK), lambda i, j: (i, 0, j)),
        ],
        out_specs=pl.BlockSpec((1, MOD_ROWS, ADA_NBLK), lambda i, j: (i, 0, j)),
        out_shape=jax.ShapeDtypeStruct((DEPTH, MOD_ROWS, n), F32),
        compiler_params=_cparams("parallel", "parallel"),
        name="ada",
    )(cc, ada_w, ada_b.reshape(DEPTH, 1, n))


def _stream_rows(xl_ref, xc_ref):
    return jnp.where(pl.program_id(0) < LAT_TILES, xl_ref[...], xc_ref[...])


def _stream_specs():
    return [pl.BlockSpec((TILE, D_MODEL), lambda t: (jnp.minimum(t, LAT_TILES - 1), 0)),
            pl.BlockSpec((TILE, D_MODEL), lambda t: (jnp.maximum(t - LAT_TILES, 0), 0))]


def _stream_shapes(with_ctx):
    shapes = [jax.ShapeDtypeStruct((N_LAT, D_MODEL), F32)]
    return shapes + [jax.ShapeDtypeStruct((N_CTX, D_MODEL), F32)] if with_ctx else shapes


def _stream_store(refs, value):
    is_lat = pl.program_id(0) < LAT_TILES

    @pl.when(is_lat)
    def _():
        refs[0][...] = value

    if len(refs) > 1:
        @pl.when(jnp.logical_not(is_lat))
        def _():
            refs[1][...] = value


def _modulated(xl_ref, xc_ref, mod_ref, shift, scale):
    return _stream_rows(xl_ref, xc_ref) * (1.0 + mod_ref[0, scale:scale + 1, :]) + mod_ref[0, shift:shift + 1, :]


def _store_heads(ref, val, width):
    for h in range(val.shape[1] // width):
        ref[h] = val[:, h * width:(h + 1) * width].astype(ref.dtype)


def _hg_proj_kernel(xl_ref, xc_ref, mod_ref, w_ref, lb_ref, q_ref, v_ref, gate_ref, kf_ref, gf_ref, kb_ref, gb_ref):
    h = _bf(_modulated(xl_ref, xc_ref, mod_ref, 0, 1))
    d = D_MODEL
    lb = lb_ref[...]
    for k_ref, g_ref, lo in ((kf_ref, gf_ref, 3 * d), (kb_ref, gb_ref, 4 * d)):
        s = _sigmoid(_mm(h, w_ref[:, lo:lo + d]))
        _store_heads(k_ref, (1.0 - lb) * (1.0 - s), LANES)
        _store_heads(g_ref, jnp.log(lb + (1.0 - lb) * s), LANES)
    _store_heads(q_ref, _silu(_mm(h, w_ref[:, 0:d])), LANES)
    _store_heads(v_ref, _mm(h, w_ref[:, d:2 * d]), LANES)
    gate_ref[...] = _bf(_mm(h, w_ref[:, 2 * d:3 * d]))


def _head_spec(n_heads, width):
    return pl.BlockSpec((n_heads, TILE, width), lambda t: (0, t, 0))


def _head_shape(n_heads, width, dtype=None):
    return jax.ShapeDtypeStruct((n_heads, N_TOK, width), BF16 if dtype is None else dtype)


def _tok_shape(width):
    return jax.ShapeDtypeStruct((N_TOK, width), BF16)


def _tok_spec(width):
    return pl.BlockSpec((TILE, width), lambda t: (t, 0))


def _mod_spec():
    return pl.BlockSpec((1, ADA_CHUNKS, D_MODEL), lambda t: (_mod_row(t), 0, 0))


def _full_spec(shape):
    return pl.BlockSpec(shape, lambda t: (0,) * len(shape))


def _hg_proj(xl, xc, mod, w_in, lb):
    n = w_in.shape[1]
    hs = _head_shape(8, LANES)
    hg = _head_shape(8, LANES, F32)
    return pl.pallas_call(
        _hg_proj_kernel,
        grid=(ALL_TILES,),
        in_specs=_stream_specs() + [_mod_spec(), _full_spec((D_MODEL, n)), _full_spec((1, D_MODEL))],
        out_specs=[_head_spec(8, LANES), _head_spec(8, LANES), _tok_spec(D_MODEL),
                   _head_spec(8, LANES), _head_spec(8, LANES), _head_spec(8, LANES), _head_spec(8, LANES)],
        out_shape=[hs, hs, _tok_shape(D_MODEL), hs, hg, hs, hg],
        compiler_params=_cparams("parallel"),
        name="hg_proj",
    )(xl, xc, mod, w_in, lb)


GLA_HEADS = 4
GLA_DK = 128
GLA_DV = 256
GLA_RANK = 16
GLA_GATE_NORM = 16.0


def _gla_proj_kernel(xl_ref, xc_ref, mod_ref, w_ref, wlr_ref, gkw_ref, gkb_ref, q_ref, k_ref, v_ref, gate_ref, gf_ref, gb_ref):
    h = _bf(_modulated(xl_ref, xc_ref, mod_ref, 0, 1))
    kd = GLA_HEADS * GLA_DK
    vd = GLA_HEADS * GLA_DV
    for z, g_ref in ((0, gf_ref), (1, gb_ref)):
        lr = _mm(h, wlr_ref[:, z * LANES:(z + 1) * LANES])
        gk = _mm(_bf(lr), gkw_ref[z]) + gkb_ref[z]
        _store_heads(g_ref, _log_sigmoid(gk) / GLA_GATE_NORM, GLA_DK)
    _store_heads(q_ref, _mm(h, w_ref[:, 0:kd]) * GLA_DK ** -0.5, GLA_DK)
    _store_heads(k_ref, _mm(h, w_ref[:, kd:2 * kd]), GLA_DK)
    _store_heads(v_ref, _mm(h, w_ref[:, 2 * kd:2 * kd + vd]), GLA_DV)
    gate_ref[...] = _bf(_mm(h, w_ref[:, 2 * kd + vd:2 * kd + 2 * vd]))


def _gla_proj(xl, xc, mod, w_main, w_lr, gk_w, gk_b):
    kd = GLA_HEADS * GLA_DK
    hk = _head_shape(GLA_HEADS, GLA_DK)
    return pl.pallas_call(
        _gla_proj_kernel,
        grid=(ALL_TILES,),
        in_specs=_stream_specs() + [_mod_spec(), _full_spec(w_main.shape), _full_spec(w_lr.shape),
                  _full_spec(gk_w.shape), _full_spec(gk_b.shape)],
        out_specs=[_head_spec(GLA_HEADS, GLA_DK), _head_spec(GLA_HEADS, GLA_DK), _head_spec(GLA_HEADS, GLA_DV),
                   _tok_spec(D_MODEL), _head_spec(GLA_HEADS, GLA_DK), _head_spec(GLA_HEADS, GLA_DK)],
        out_shape=[hk, hk, _head_shape(GLA_HEADS, GLA_DV), _tok_shape(D_MODEL),
                   _head_shape(GLA_HEADS, GLA_DK, F32), _head_shape(GLA_HEADS, GLA_DK, F32)],
        compiler_params=_cparams("parallel"),
        name="gla_proj",
    )(xl, xc, mod, w_main, w_lr, gk_w, gk_b)


DN_HEADS = 8
DN_QKV = 3 * D_MODEL
SHORT_CONV = 5
CHUNKS_PER_TILE = TILE // CHUNK


def _seg_scan(x, pos, axis, reverse):
    n = x.shape[axis]
    sh = 1
    while sh < CHUNK:
        if reverse:
            x = x + jnp.where(pos < CHUNK - sh, pltpu.roll(x, n - sh, axis), 0.0)
        else:
            x = x + jnp.where(pos >= sh, pltpu.roll(x, sh, axis), 0.0)
        sh *= 2
    return x


def _dn_gates(p, a_neg, dtb, idx, pos, axis):
    beta = _sigmoid(p)
    g = a_neg * _softplus(p + dtb)
    gp = _seg_scan(g, pos, axis, False)
    gs = _seg_scan(g, pos, axis, True)
    return jnp.where(idx < 2 * DN_HEADS, beta, jnp.where(idx < 3 * DN_HEADS, gp, gs))


def _dn_proj_kernel(xl_ref, xc_ref, mod_ref, w_ref, wba_ref, wbat_ref, ac_ref, dc_ref, ar_ref, dr_ref,
                    qkv_ref, gate_ref, bgc_ref, bgr_ref):
    h = _bf(_modulated(xl_ref, xc_ref, mod_ref, 0, 1))
    pc = _mm(h, wba_ref[...])
    lane = lax.broadcasted_iota(jnp.int32, pc.shape, 1)
    rowpos = lax.broadcasted_iota(jnp.int32, pc.shape, 0) % CHUNK
    bgc_ref[...] = _dn_gates(pc, ac_ref[...], dc_ref[...], lane, rowpos, 0)
    pr = _nt(wbat_ref[...], h)
    sub = lax.broadcasted_iota(jnp.int32, pr.shape, 0)
    lanepos = lax.broadcasted_iota(jnp.int32, pr.shape, 1) % CHUNK
    gr = _dn_gates(pr, ar_ref[...], dr_ref[...], sub, lanepos, 1)
    for c in range(CHUNKS_PER_TILE):
        bgr_ref[c] = gr[:, c * CHUNK:(c + 1) * CHUNK]
    qkv_ref[...] = _bf(_mm(h, w_ref[:, 0:DN_QKV]))
    gate_ref[...] = _bf(_mm(h, w_ref[:, DN_QKV:DN_QKV + D_MODEL]))


def _dn_proj(xl, xc, mod, w_main, w_ba, w_bat, a_col, d_col, a_row, d_row):
    nb = 4 * DN_HEADS
    return pl.pallas_call(
        _dn_proj_kernel,
        grid=(ALL_TILES,),
        in_specs=_stream_specs() + [_mod_spec(), _full_spec(w_main.shape), _full_spec(w_ba.shape),
                  _full_spec(w_bat.shape), _full_spec(a_col.shape), _full_spec(d_col.shape),
                  _full_spec(a_row.shape), _full_spec(d_row.shape)],
        out_specs=[_tok_spec(DN_QKV), _tok_spec(D_MODEL), _tok_spec(LANES),
                   pl.BlockSpec((CHUNKS_PER_TILE, nb, CHUNK), lambda t: (t, 0, 0))],
        out_shape=[_tok_shape(DN_QKV), _tok_shape(D_MODEL), jax.ShapeDtypeStruct((N_TOK, LANES), F32),
                   jax.ShapeDtypeStruct((N_TOK // CHUNK, nb, CHUNK), F32)],
        compiler_params=_cparams("parallel"),
        name="dn_proj",
    )(xl, xc, mod, w_main, w_ba, w_bat, a_col, d_col, a_row, d_row)


HALO = 2 * SUBLANES


def _dn_conv_kernel(pm_ref, pp_ref, pn_ref, cw_ref, q_ref, k_ref, v_ref):
    t = pl.program_id(0)
    is_lat = t < LAT_TILES
    first = jnp.logical_or(jnp.logical_not(is_lat), t % TILES_PER_SEQ == 0)
    last = jnp.logical_or(jnp.logical_not(is_lat), t % TILES_PER_SEQ == TILES_PER_SEQ - 1)
    keep_prev = jnp.where(first, 0.0, 1.0)
    keep_next = jnp.where(last, 0.0, 1.0)
    n_ext = TILE + 2 * HALO
    pad = SHORT_CONV // 2
    for s in range(DN_QKV // LANES):
        cols = slice(s * LANES, (s + 1) * LANES)
        ext = jnp.concatenate([pp_ref[:, cols].astype(F32) * keep_prev, pm_ref[:, cols].astype(F32),
                               pn_ref[:, cols].astype(F32) * keep_next], axis=0)
        acc = None
        for kk in range(SHORT_CONV):
            off = HALO - pad + kk
            tap = pltpu.roll(ext, n_ext - off, 0)[0:TILE] * cw_ref[kk:kk + 1, cols]
            acc = tap if acc is None else acc + tap
        a = _silu(acc)
        head = s % DN_HEADS
        if s < 2 * DN_HEADS:
            a = a * lax.rsqrt(jnp.sum(a * a, axis=1, keepdims=True) + 1e-6)
            if s < DN_HEADS:
                q_ref[head] = _bf(a * LANES ** -0.5)
            else:
                k_ref[head] = _bf(a)
        else:
            v_ref[head] = _bf(a)


def _dn_conv(p_qkv, conv_w):
    rows8 = TILE // HALO
    last8 = N_TOK // HALO - 1
    hs = _head_shape(DN_HEADS, LANES)
    return pl.pallas_call(
        _dn_conv_kernel,
        grid=(ALL_TILES,),
        in_specs=[_tok_spec(DN_QKV),
                  pl.BlockSpec((HALO, DN_QKV), lambda t: (jnp.maximum(t * rows8 - 1, 0), 0)),
                  pl.BlockSpec((HALO, DN_QKV), lambda t: (jnp.minimum((t + 1) * rows8, last8), 0)),
                  _full_spec(conv_w.shape)],
        out_specs=[_head_spec(DN_HEADS, LANES)] * 3,
        out_shape=[hs, hs, hs],
        compiler_params=_cparams("parallel"),
        name="dn_conv",
    )(p_qkv, p_qkv, p_qkv, conv_w)


LOG2_E = 1.4426950408889634
SIGN_BIT = 0x80000000


def _chunk_cumsum(x, ones_tri):
    hi = _bf(x)
    rest = x - hi.astype(F32)
    mid = _bf(rest)
    lo = _bf(rest - mid.astype(F32))
    return _mm(ones_tri, hi) + _mm(ones_tri, mid) + _mm(ones_tri, lo)


def _neg_abs(x):
    return pltpu.bitcast(pltpu.bitcast(x, jnp.uint32) | jnp.uint32(SIGN_BIT), F32)


def _gla_chunks(q, k, v, g, st, rev, row, r64, c64):
    n = range(len(q))
    q32 = [q[i].astype(F32) for i in n]
    k32 = [k[i].astype(F32) for i in n]
    ones_tri = {False: jnp.where(r64 >= c64, 1.0, 0.0).astype(BF16), True: jnp.where(r64 <= c64, 1.0, 0.0).astype(BF16)}
    cum = [_chunk_cumsum(g[i] * LOG2_E, ones_tri[rev[i]]) for i in n]
    g_end = [cum[i][0:1, :] if rev[i] else cum[i][CHUNK - 1:CHUNK, :] for i in n]
    att = [jnp.where(r64 == c64, _nt(_bf(q[i]), _bf(k[i])), 0.0) for i in n]
    edge = list(cum)
    s = 1
    while s < CHUNK:
        late = (row & s) != 0
        early = jnp.logical_not(late)
        rlate = (r64 & s) != 0
        clate = (c64 & s) != 0
        pair = {False: jnp.logical_and(rlate, jnp.logical_not(clate)), True: jnp.logical_and(clate, jnp.logical_not(rlate))}
        if 2 * s < CHUNK:
            shift = (2 * s).bit_length() - 1
            same = (r64 >> shift) == (c64 >> shift)
            pair = {d: jnp.logical_and(p, same) for d, p in pair.items()}
        for i in n:
            if rev[i]:
                cut = jnp.where(late, edge[i], pltpu.roll(edge[i], CHUNK - s, 0))
            else:
                cut = jnp.where(late, pltpu.roll(edge[i], s, 0), edge[i])
            e = jnp.exp2(_neg_abs(cum[i] - cut))
            q_side = early if rev[i] else late
            x16 = _bf(jnp.where(q_side, q32[i], k32[i]) * e)
            att[i] = att[i] + jnp.where(pair[rev[i]], _nt(x16, x16), 0.0)
            if rev[i]:
                edge[i] = jnp.where(late, pltpu.roll(edge[i], s, 0), edge[i])
            else:
                edge[i] = jnp.where(late, edge[i], pltpu.roll(edge[i], CHUNK - s, 0))
        s *= 2
    st16 = [_bf(st[i]) for i in n]
    o = [_mm(_bf(att[i]), _bf(v[i])) + _nt(_bf(q32[i] * jnp.exp2(cum[i])), st16[i]) for i in n]
    st_new = [st[i] * jnp.exp2(g_end[i]) + _tn(_bf(v[i]), _bf(k32[i] * jnp.exp2(g_end[i] - cum[i]))) for i in n]
    return o, st_new


def _scan_iotas():
    row = lax.broadcasted_iota(jnp.int32, (CHUNK, LANES), 0)
    r64 = lax.broadcasted_iota(jnp.int32, (CHUNK, CHUNK), 0)
    c64 = lax.broadcasted_iota(jnp.int32, (CHUNK, CHUNK), 1)
    return row, r64, c64


SCAN_BATCH = 2


def _scan_chains(in_refs, per_chain, n_heads):
    kinds = [[] for _ in range(per_chain)]
    rev = []
    for d in range(2):
        for bi in range(SCAN_BATCH):
            group = in_refs[(d * SCAN_BATCH + bi) * per_chain:(d * SCAN_BATCH + bi + 1) * per_chain]
            for h in range(n_heads):
                for kind, ref in zip(kinds, group):
                    kind.append((ref, h))
                rev.append(d == 1)
    return kinds, rev


def _scan_store(o, st, of_ref, ob_ref, st_ref, n_heads):
    i = 0
    for d, o_ref in ((0, of_ref), (1, ob_ref)):
        for bi in range(SCAN_BATCH):
            for h in range(n_heads):
                o_ref[h, bi * CHUNK:(bi + 1) * CHUNK, :] = _bf(o[i])
                st_ref[d, bi, h] = st[i]
                i += 1


def _gla_scan_kernel(*refs, n_heads):
    n_in = 2 * SCAN_BATCH * 4
    of_ref, ob_ref, st_ref = refs[n_in:]

    @pl.when(pl.program_id(1) == 0)
    def _():
        st_ref[...] = jnp.zeros_like(st_ref)

    row, r64, c64 = _scan_iotas()
    (q, k, v, g), rev = _scan_chains(refs[:n_in], 4, n_heads)
    load = lambda items: [ref[h] for ref, h in items]
    st = [st_ref[d, bi, h] for d in range(2) for bi in range(SCAN_BATCH) for h in range(n_heads)]
    o, st = _gla_chunks(load(q), load(k), load(v), load(g), st, rev, row, r64, c64)
    _scan_store(o, st, of_ref, ob_ref, st_ref, n_heads)


CTX_CHUNKS = CTX_LEN // CHUNK
LAT_CHUNKS = SEQ // CHUNK
SCAN_STEPS = CTX_CHUNKS + LAT_CHUNKS


def _fwd_chunk(b, n):
    return jnp.where(n < CTX_CHUNKS, N_LAT // CHUNK + b * CTX_CHUNKS + n, b * LAT_CHUNKS + n - CTX_CHUNKS)


def _bwd_chunk(b, n):
    return jnp.where(n < CTX_CHUNKS, N_LAT // CHUNK + b * CTX_CHUNKS + CTX_CHUNKS - 1 - n,
                     b * LAT_CHUNKS + SCAN_STEPS - 1 - n)


def _scan_in_maps(rank):
    def make(chunk_of, bi):
        def index_map(p, n):
            c = chunk_of(p * SCAN_BATCH + bi, n)
            return {1: (c, 0, 0), 2: (0, c, 0), 3: (c, 0)}[rank]
        return index_map
    return [make(chunk_of, bi) for chunk_of in (_fwd_chunk, _bwd_chunk) for bi in range(SCAN_BATCH)]


def _scan_out_pos(step, reverse):
    if not reverse:
        return step
    return jnp.where(step < CTX_CHUNKS, CTX_CHUNKS - 1 - step, CTX_CHUNKS + SCAN_STEPS - 1 - step)


def _scan_out_specs(n_heads, dv):
    return [pl.BlockSpec((n_heads, SCAN_BATCH * CHUNK, dv),
                         lambda p, n, reverse=reverse: (0, p * SCAN_STEPS + _scan_out_pos(n, reverse), 0))
            for reverse in (False, True)]


def _scan_out_block(t, c):
    is_lat = t < LAT_TILES
    b = jnp.where(is_lat, t // TILES_PER_SEQ, t - LAT_TILES)
    pos = jnp.where(is_lat, CTX_CHUNKS + (t % TILES_PER_SEQ) * CHUNKS_PER_TILE + c, c)
    return ((b // SCAN_BATCH) * SCAN_STEPS + pos) * SCAN_BATCH + b % SCAN_BATCH


def _gla_scan(q, k_f, k_b, v, g_f, g_b):
    n_heads, _, dk = q.shape
    dv = v.shape[2]
    maps = _scan_in_maps(2)
    kspec = lambda m: pl.BlockSpec((n_heads, CHUNK, dk), m)
    vspec = lambda m: pl.BlockSpec((n_heads, CHUNK, dv), m)
    in_specs, args = [], []
    for d, k_arr, g_arr in ((0, k_f, g_f), (1, k_b, g_b)):
        for bi in range(SCAN_BATCH):
            m = maps[d * SCAN_BATCH + bi]
            in_specs += [kspec(m), kspec(m), vspec(m), kspec(m)]
            args += [q, k_arr, v, g_arr]
    oshape = _head_shape(n_heads, dv)
    return pl.pallas_call(
        functools.partial(_gla_scan_kernel, n_heads=n_heads),
        grid=(BATCH // SCAN_BATCH, SCAN_STEPS),
        in_specs=in_specs,
        out_specs=_scan_out_specs(n_heads, dv),
        out_shape=[oshape, oshape],
        scratch_shapes=[pltpu.VMEM((2, SCAN_BATCH, n_heads, dv, dk), F32)],
        compiler_params=_cparams("parallel", "arbitrary"),
        name="gla_scan",
    )(*args)


def _lane_pick(x, lane, idx):
    return jnp.sum(jnp.where(lane == idx, x, 0.0), axis=1, keepdims=True)


def _dn_chunks(q, k, v, beta, g_col, g_row, st, rev, r64, c64):
    n = range(len(q))
    dk = k[0].shape[1]
    g_end = [g_row[i][:, 0:1] if rev[i] else g_row[i][:, CHUNK - 1:CHUNK] for i in n]
    causal = {False: r64 >= c64, True: r64 <= c64}
    gam = [jnp.where(causal[rev[i]], jnp.exp(jnp.minimum(g_col[i] - g_row[i], 0.0)), 0.0) for i in n]
    q32 = [q[i].astype(F32) for i in n]
    k32 = [k[i].astype(F32) for i in n]
    kb = [k32[i] * beta[i] for i in n]
    k16 = [_bf(k[i]) for i in n]
    m = [jnp.where(r64 == c64, 0.0, _nt(_bf(kb[i]), k16[i]) * gam[i]) for i in n]
    att = [_nt(_bf(q[i]), k16[i]) * gam[i] for i in n]
    y = None
    s = 1
    while s < CHUNK:
        shift = (2 * s).bit_length() - 1
        rlate = (r64 & s) != 0
        clate = (c64 & s) != 0
        pair = {False: jnp.logical_and(rlate, jnp.logical_not(clate)), True: jnp.logical_and(clate, jnp.logical_not(rlate))}
        if 2 * s < CHUNK:
            same = (r64 >> shift) == (c64 >> shift)
            pair = {d: jnp.logical_and(p, same) for d, p in pair.items()}
        c = [jnp.where(pair[rev[i]], m[i], 0.0) for i in n]
        if y is None:
            y = [-c[i] for i in n]
        else:
            c16 = [_bf(c[i]) for i in n]
            y16 = [_bf(y[i]) for i in n]
            p = [c[i] + _mm(y16[i], c16[i]) for i in n]
            y = [y[i] - (p[i] + _mm(_bf(p[i]), y16[i])) for i in n]
        s *= 2
    e_col = [jnp.exp(g_col[i]) for i in n]
    rhs = [jnp.concatenate([kb[i] * e_col[i], v[i].astype(F32) * beta[i]], axis=1) for i in n]
    wu = [rhs[i] + _mm(_bf(y[i]), _bf(rhs[i])) for i in n]
    st16 = [_bf(st[i]) for i in n]
    v_new = [wu[i][:, dk:] - _nt(_bf(wu[i][:, :dk]), st16[i]) for i in n]
    o = [_nt(_bf(q32[i] * e_col[i]), st16[i]) + _mm(_bf(att[i]), _bf(v_new[i])) for i in n]
    st_new = [st[i] * jnp.exp(g_end[i]) + _tn(_bf(v_new[i]), _bf(k32[i] * jnp.exp(g_end[i] - g_col[i]))) for i in n]
    return o, st_new


def _dn_scan_kernel(*refs):
    n_in = 2 * SCAN_BATCH * 5
    of_ref, ob_ref, st_ref = refs[n_in:]

    @pl.when(pl.program_id(1) == 0)
    def _():
        st_ref[...] = jnp.zeros_like(st_ref)

    _, r64, c64 = _scan_iotas()
    (q, k, v, gc, gr), rev = _scan_chains(refs[:n_in], 5, DN_HEADS)
    load = lambda items: [ref[h] for ref, h in items]
    tables = {id(ref): ref[...] for ref, _ in gc}
    beta = [tables[id(ref)][:, d * DN_HEADS + h:d * DN_HEADS + h + 1] for (ref, h), d in zip(gc, rev)]
    g_col = [tables[id(ref)][:, (2 + d) * DN_HEADS + h:(2 + d) * DN_HEADS + h + 1] for (ref, h), d in zip(gc, rev)]
    g_row = [ref[0, (2 + d) * DN_HEADS + h:(2 + d) * DN_HEADS + h + 1, :] for (ref, h), d in zip(gr, rev)]
    st = [st_ref[d, bi, h] for d in range(2) for bi in range(SCAN_BATCH) for h in range(DN_HEADS)]
    o, st = _dn_chunks(load(q), load(k), load(v), beta, g_col, g_row, st, rev, r64, c64)
    _scan_store(o, st, of_ref, ob_ref, st_ref, DN_HEADS)


def _dn_scan(q, k, v, bgc, bgr):
    head_maps, col_maps, row_maps = _scan_in_maps(2), _scan_in_maps(3), _scan_in_maps(1)
    in_specs, args = [], []
    for i in range(2 * SCAN_BATCH):
        in_specs += [pl.BlockSpec((DN_HEADS, CHUNK, LANES), head_maps[i])] * 3
        in_specs += [pl.BlockSpec((CHUNK, LANES), col_maps[i]), pl.BlockSpec((1, 4 * DN_HEADS, CHUNK), row_maps[i])]
        args += [q, k, v, bgc, bgr]
    oshape = _head_shape(DN_HEADS, LANES)
    return pl.pallas_call(
        _dn_scan_kernel,
        grid=(BATCH // SCAN_BATCH, SCAN_STEPS),
        in_specs=in_specs,
        out_specs=_scan_out_specs(DN_HEADS, LANES),
        out_shape=[oshape, oshape],
        scratch_shapes=[pltpu.VMEM((2, SCAN_BATCH, DN_HEADS, LANES, LANES), F32)],
        compiler_params=_cparams("parallel", "arbitrary"),
        name="dn_scan",
    )(*args)


def _layer_norm(z, g, b):
    mu = jnp.mean(z, axis=1, keepdims=True)
    zc = z - mu
    var = jnp.mean(zc * zc, axis=1, keepdims=True)
    return zc * lax.rsqrt(var + LN_EPS) * g + b


def _post_kernel(*refs, n_heads):
    o_refs, refs = refs[:2 * CHUNKS_PER_TILE], refs[2 * CHUNKS_PER_TILE:]
    gate_ref, xl_ref, xc_ref, mod_ref, nw_ref, wo_ref, lg_ref, lb_ref, rw_ref, rb_ref = refs[:10]
    out_refs = refs[10:]
    x1_refs, (h2_ref, logit_ref) = out_refs[:-2], out_refs[-2:]
    x_rows = _stream_rows(xl_ref, xc_ref)
    groups = [slice(g * TILE // POST_GROUPS, (g + 1) * TILE // POST_GROUPS) for g in range(POST_GROUPS)]
    chunks_per_group = CHUNKS_PER_TILE // POST_GROUPS
    on = []
    for g, rows in enumerate(groups):
        parts = []
        for h in range(n_heads):
            o = jnp.concatenate([o_refs[c][h].astype(F32) + o_refs[CHUNKS_PER_TILE + c][h].astype(F32)
                                 for c in range(g * chunks_per_group, (g + 1) * chunks_per_group)], axis=0)
            parts.append(o * lax.rsqrt(jnp.mean(o * o, axis=1, keepdims=True) + RMS_EPS) * nw_ref[...])
        on.append(jnp.concatenate(parts, axis=1) * _silu(gate_ref[rows, :].astype(F32)))
    y = [_mm(_bf(a), wo_ref[...]) for a in on]
    z = [ALPHA * x_rows[rows] + mod_ref[0, 2:3, :] * yg for rows, yg in zip(groups, y)]
    x1 = [_layer_norm(zg, lg_ref[...], lb_ref[...]) for zg in z]
    h2 = [a * (1.0 + mod_ref[0, 4:5, :]) + mod_ref[0, 3:4, :] for a in x1]
    logits = [jnp.dot(a, rw_ref[...], precision=HIGHEST, preferred_element_type=F32) + rb_ref[...] for a in h2]
    _stream_store(x1_refs, jnp.concatenate(x1, axis=0))
    h2_ref[...] = _pack_halves(jnp.concatenate(h2, axis=0))
    logit_ref[...] = jnp.concatenate(logits, axis=0)


POST_GROUPS = 2


def _post(o_f, o_b, gate, xl, xc, mod, norm_w, w_out, ln_g, ln_b, rw, rb, with_ctx):
    n_heads, _, dv = o_f.shape
    n_tiles = ALL_TILES if with_ctx else LAT_TILES
    n_rows = n_tiles * TILE
    tok = lambda w: jax.ShapeDtypeStruct((n_rows, w), F32)
    return pl.pallas_call(
        functools.partial(_post_kernel, n_heads=n_heads),
        grid=(n_tiles,),
        in_specs=[pl.BlockSpec((n_heads, CHUNK, dv), lambda t, c=c: (0, _scan_out_block(t, c), 0))
                  for _ in range(2) for c in range(CHUNKS_PER_TILE)] + [_tok_spec(D_MODEL)] + _stream_specs() + [
            _mod_spec(), _full_spec(norm_w.shape), _full_spec(w_out.shape), _full_spec(ln_g.shape),
            _full_spec(ln_b.shape), _full_spec(rw.shape), _full_spec(rb.shape)],
        out_specs=_stream_specs()[:2 if with_ctx else 1] + [_tok_spec(PACKED_D), _tok_spec(LANES)],
        out_shape=_stream_shapes(with_ctx) + [tok(PACKED_D), tok(LANES)],
        compiler_params=_cparams("arbitrary"),
        name="post_mixer",
    )(*([o_f] * CHUNKS_PER_TILE + [o_b] * CHUNKS_PER_TILE), gate, xl, xc, mod, norm_w, w_out, ln_g, ln_b, rw, rb)


def _route_kernel(logit_ref, route_ref, slot_ref, tiles_ref, run_ref, start_ref, table_ref):
    sweep = pl.program_id(0)
    t = pl.program_id(1)
    lane = lax.broadcasted_iota(jnp.int32, (TILE, LANES), 1)
    lanef = lane.astype(F32)

    def add_counts(onehot):
        new_run = run_ref[0:1, :] + jnp.sum(onehot, axis=0, keepdims=True)
        run_ref[...] = jnp.broadcast_to(new_run, run_ref.shape)
        return new_run

    @pl.when(t == 0)
    def _():
        run_ref[...] = jnp.zeros_like(run_ref)

    @pl.when(sweep == 0)
    def _():
        neg = jnp.float32(-jnp.inf)
        l = jnp.where(lane < N_EXPERTS, logit_ref[...], neg)
        vals, picks = [], []
        onehot = jnp.zeros((TILE, LANES), F32)
        for _ in range(TOP_K):
            m = jnp.max(l, axis=1, keepdims=True)
            pick = jnp.min(jnp.where(l == m, lanef, float(LANES)), axis=1, keepdims=True)
            hot = lanef == pick
            l = jnp.where(hot, neg, l)
            onehot = onehot + jnp.where(hot, 1.0, 0.0)
            vals.append(m)
            picks.append(pick)
        es = [jnp.exp(v - vals[0]) for v in vals]
        denom = es[0] + es[1] + es[2] + es[3]
        out = jnp.zeros((TILE, LANES), F32)
        for kk in range(TOP_K):
            out = jnp.where(lane == kk, picks[kk], out)
            out = jnp.where(lane == ROUTE_WEIGHT + kk, es[kk] / denom, out)
        table_ref[t] = out
        new_run = add_counts(onehot)

        @pl.when(t == pl.num_programs(1) - 1)
        def _():
            lane8 = lax.broadcasted_iota(jnp.int32, (SUBLANES, LANES), 1)
            size = jnp.ceil(jnp.broadcast_to(new_run, (SUBLANES, LANES)) * (1.0 / EXPERT_TILE)) * EXPERT_TILE
            size = jnp.where(lane8 < N_EXPERTS, size, 0.0)
            ends = size
            sh = 1
            while sh < N_EXPERTS:
                ends = ends + jnp.where(lane8 >= sh, pltpu.roll(ends, sh, 1), 0.0)
                sh *= 2
            start_ref[...] = ends - size
            tile_start = lax.broadcasted_iota(jnp.int32, (TILE, 1), 0).astype(F32) * EXPERT_TILE
            ends_t = jnp.broadcast_to(ends[0:1, :], (TILE, LANES))
            group_end_t = jnp.broadcast_to((ends - size)[0:1, :] + new_run, (TILE, LANES))
            expert = jnp.sum(jnp.where(jnp.logical_and(lane < N_EXPERTS, ends_t <= tile_start), 1.0, 0.0),
                             axis=1, keepdims=True)
            expert = jnp.minimum(expert, N_EXPERTS - 1.0)
            group_end = jnp.sum(jnp.where(lanef == expert, group_end_t, 0.0), axis=1, keepdims=True)
            valid = jnp.clip(group_end - tile_start, 0.0, float(EXPERT_TILE))
            n_tiles = jnp.max(ends_t, axis=1, keepdims=True) * (1.0 / EXPERT_TILE)
            tiles_ref[...] = jnp.where(lane == 0, expert, jnp.where(lane == 1, valid, jnp.where(lane == 2, n_tiles, 0.0)))

    @pl.when(sweep == 1)
    def _():
        out = table_ref[t]
        hots = [lanef == out[:, kk:kk + 1] for kk in range(TOP_K)]
        onehot = jnp.zeros((TILE, LANES), F32)
        for hot in hots:
            onehot = onehot + jnp.where(hot, 1.0, 0.0)
        r = lax.broadcasted_iota(jnp.int32, (TILE, TILE), 0)
        c = lax.broadcasted_iota(jnp.int32, (TILE, TILE), 1)
        before = jnp.where(c < r, 1.0, 0.0).astype(BF16)
        slot_of = _mm(before, _bf(onehot)) + run_ref[0:1, :] + start_ref[0:1, :]
        for kk in range(TOP_K):
            slot = jnp.sum(jnp.where(hots[kk], slot_of, 0.0), axis=1, keepdims=True)
            out = jnp.where(lane == ROUTE_SLOT + kk, slot, out)
        route_ref[...] = out
        slot_ref[...] = out.T[ROUTE_WEIGHT:ROUTE_WEIGHT + SUBLANES, :].astype(jnp.int32)
        add_counts(onehot)


ROUTE_WEIGHT = SUBLANES
ROUTE_SLOT = SUBLANES + TOP_K


def _route(logits):
    n_tok = logits.shape[0]
    n_tiles = n_tok // TILE
    return pl.pallas_call(
        _route_kernel,
        grid=(2, n_tiles),
        in_specs=[pl.BlockSpec((TILE, LANES), lambda s, t: (t, 0))],
        out_specs=[pl.BlockSpec((TILE, LANES), lambda s, t: (s * t, 0)),
                   pl.BlockSpec((SUBLANES, TILE), lambda s, t: (0, s * t)),
                   pl.BlockSpec((TILE, LANES), lambda s, t: (0, 0))],
        out_shape=[jax.ShapeDtypeStruct(logits.shape, F32), jax.ShapeDtypeStruct((SUBLANES, n_tok), jnp.int32),
                   jax.ShapeDtypeStruct((TILE, LANES), F32)],
        scratch_shapes=[pltpu.VMEM((SUBLANES, LANES), F32), pltpu.VMEM((SUBLANES, LANES), F32),
                        pltpu.VMEM((n_tiles, TILE, LANES), F32)],
        compiler_params=_cparams("arbitrary", "arbitrary"),
        name="route",
    )(logits)


GATHER_ROWS = 16
GATHER_BUFS = 4


def _sc_gather(table, idx):
    n_rows = idx.shape[0]
    d = table.shape[1]
    info = plsc.get_sparse_core_info()
    n_workers = info.num_cores * info.num_subcores
    rows_per_worker = n_rows // n_workers
    n_chunks = rows_per_worker // GATHER_ROWS
    assert rows_per_worker * n_workers == n_rows and n_chunks * GATHER_ROWS == rows_per_worker
    assert n_chunks % GATHER_BUFS == 0
    mesh = plsc.VectorSubcoreMesh(core_axis_name="c", subcore_axis_name="s")
    scratch = ([pltpu.VMEM((GATHER_ROWS,), jnp.int32)] * GATHER_BUFS
               + [pltpu.VMEM((GATHER_ROWS, d), table.dtype)] * GATHER_BUFS
               + [pltpu.SemaphoreType.DMA] * (2 * GATHER_BUFS))

    def body(table_hbm, idx_hbm, out_hbm, *scr):
        idx_v, rows_v = scr[:GATHER_BUFS], scr[GATHER_BUFS:2 * GATHER_BUFS]
        gather_sem, write_sem = scr[2 * GATHER_BUFS:3 * GATHER_BUFS], scr[3 * GATHER_BUFS:]
        worker = lax.axis_index("s") * info.num_cores + lax.axis_index("c")
        base = worker * rows_per_worker

        @pl.loop(0, n_chunks, step=GATHER_BUFS)
        def _(c0):
            offs = [pl.multiple_of(base + (c0 + b) * GATHER_ROWS, SUBLANES) for b in range(GATHER_BUFS)]
            gathers = []
            for b in range(GATHER_BUFS):
                pltpu.sync_copy(idx_hbm.at[pl.ds(offs[b], GATHER_ROWS)], idx_v[b])
                gathers.append(pltpu.async_copy(table_hbm.at[idx_v[b]], rows_v[b], gather_sem[b]))
            writes = []
            for b in range(GATHER_BUFS):
                gathers[b].wait()
                writes.append(pltpu.async_copy(rows_v[b], out_hbm.at[pl.ds(offs[b], GATHER_ROWS)], write_sem[b]))
            for w in writes:
                w.wait()

    return pl.kernel(body, out_type=jax.ShapeDtypeStruct((n_rows, d), table.dtype), mesh=mesh,
                     scratch_types=scratch, name="sc_gather")(table, idx)


def _sc_scatter(src, idx, n_out):
    n, d = src.shape
    info = plsc.get_sparse_core_info()
    n_workers = info.num_cores * info.num_subcores
    rows_per_worker = n // n_workers
    n_chunks = rows_per_worker // GATHER_ROWS
    assert rows_per_worker * n_workers == n and n_chunks * GATHER_ROWS == rows_per_worker
    assert n_chunks % GATHER_BUFS == 0
    mesh = plsc.VectorSubcoreMesh(core_axis_name="c", subcore_axis_name="s")
    scratch = ([pltpu.VMEM((GATHER_ROWS,), jnp.int32)] * (GATHER_BUFS * TOP_K)
               + [pltpu.VMEM((GATHER_ROWS, d), src.dtype)] * GATHER_BUFS + [pltpu.SemaphoreType.DMA] * GATHER_BUFS)

    def body(src_hbm, idx_hbm, out_hbm, *scr):
        idx_v = scr[:GATHER_BUFS * TOP_K]
        rows_v = scr[GATHER_BUFS * TOP_K:GATHER_BUFS * TOP_K + GATHER_BUFS]
        sem = scr[GATHER_BUFS * TOP_K + GATHER_BUFS:]
        worker = lax.axis_index("s") * info.num_cores + lax.axis_index("c")
        base = worker * rows_per_worker

        @pl.loop(0, n_chunks, step=GATHER_BUFS)
        def _(c0):
            copies = []
            for b in range(GATHER_BUFS):
                off = pl.multiple_of(base + (c0 + b) * GATHER_ROWS, SUBLANES)
                pltpu.sync_copy(src_hbm.at[pl.ds(off, GATHER_ROWS)], rows_v[b])
                for kk in range(TOP_K):
                    pltpu.sync_copy(idx_hbm.at[pl.ds(pl.multiple_of(kk * n + off, SUBLANES), GATHER_ROWS)],
                                    idx_v[b * TOP_K + kk])
                for kk in range(TOP_K):
                    copies.append(pltpu.async_copy(rows_v[b], out_hbm.at[idx_v[b * TOP_K + kk]], sem[b]))
            for c in copies:
                c.wait()

    return pl.kernel(body, out_type=jax.ShapeDtypeStruct((n_out, d), src.dtype), mesh=mesh,
                     scratch_types=scratch, name="sc_scatter")(src, idx)


PAIR_BLOCK = 2 * LANES


def _expert_kernel(te_ref, tv_ref, nt_ref, xs_ref, w1_ref, b1_ref, w2_ref, b2_ref, ys_ref, w1p_ref, w2b_ref):
    i = pl.program_id(0)
    valid = i < nt_ref[0]
    new_expert = jnp.logical_or(i == 0, te_ref[i] != te_ref[jnp.maximum(i - 1, 0)])

    @pl.when(jnp.logical_and(valid, new_expert))
    def _():
        r = lax.broadcasted_iota(jnp.int32, (PAIR_BLOCK, PAIR_BLOCK), 0)
        c = lax.broadcasted_iota(jnp.int32, (PAIR_BLOCK, PAIR_BLOCK), 1)
        src = jnp.where(c < LANES, 2 * c, 2 * (c - LANES) + 1)
        perm = jnp.where(r == src, 1.0, 0.0).astype(BF16)
        for b in range(2 * D_FF // PAIR_BLOCK):
            cols = slice(b * PAIR_BLOCK, (b + 1) * PAIR_BLOCK)
            w1p_ref[:, cols] = _bf(_mm(_bf(w1_ref[0, 0, :, cols]), perm))
        w2b_ref[...] = _bf(w2_ref[0, 0])

    @pl.when(valid)
    def _():
        rowi = lax.broadcasted_iota(jnp.int32, (EXPERT_TILE, D_MODEL), 0)
        x = _bf(jnp.where(rowi < tv_ref[i], _unpack_halves(xs_ref[...]), 0.0))
        u = _mm(x, w1p_ref[...]) + b1_ref[0]
        parts = []
        for b in range(2 * D_FF // PAIR_BLOCK):
            glu = jnp.minimum(u[:, b * PAIR_BLOCK:b * PAIR_BLOCK + LANES], SWIGLU_LIMIT)
            lin = jnp.clip(u[:, b * PAIR_BLOCK + LANES:(b + 1) * PAIR_BLOCK], -SWIGLU_LIMIT, SWIGLU_LIMIT)
            parts.append(glu * _sigmoid(SWIGLU_ALPHA * glu) * (lin + 1.0))
        act = jnp.concatenate(parts, axis=1)
        ys_ref[...] = _pack_halves(_mm(_bf(act), w2b_ref[...]) + b2_ref[0])


def _experts(layer, tile_expert, tile_valid, n_tiles, xs, w1, b1p, w2, b2):
    tile_map = lambda i, te, tv, nt: (jnp.minimum(i, nt[0] - 1), 0)
    wmap = lambda i, te, tv, nt: (te[i], 0, 0)
    lmap = lambda i, te, tv, nt: (layer, te[i], 0, 0)
    grid_spec = pltpu.PrefetchScalarGridSpec(
        num_scalar_prefetch=3,
        grid=(N_SLOT_TILES,),
        in_specs=[pl.BlockSpec((EXPERT_TILE, PACKED_D), tile_map),
                  pl.BlockSpec((1, 1, D_MODEL, 2 * D_FF), lmap), pl.BlockSpec((1, 1, 2 * D_FF), wmap),
                  pl.BlockSpec((1, 1, D_FF, D_MODEL), lmap), pl.BlockSpec((1, 1, D_MODEL), wmap)],
        out_specs=pl.BlockSpec((EXPERT_TILE, PACKED_D), tile_map),
        scratch_shapes=[pltpu.VMEM((D_MODEL, 2 * D_FF), BF16), pltpu.VMEM((D_FF, D_MODEL), BF16)],
    )
    return pl.pallas_call(
        _expert_kernel,
        grid_spec=grid_spec,
        out_shape=jax.ShapeDtypeStruct((N_SLOTS, PACKED_D), F32),
        compiler_params=pltpu.CompilerParams(dimension_semantics=("arbitrary",), vmem_limit_bytes=EXPERT_VMEM_LIMIT),
        name="experts",
    )(tile_expert, tile_valid, n_tiles, xs, w1, b1p, w2, b2)


def _combine_kernel(y_ref, route_ref, x1l_ref, x1c_ref, mod_ref, lg_ref, lb_ref, *x2_refs):
    route = route_ref[...]
    lane = lax.broadcasted_iota(jnp.int32, route.shape, 1)
    f = None
    for kk in range(TOP_K):
        term = _lane_pick(route, lane, ROUTE_WEIGHT + kk) * _unpack_halves(y_ref[kk])
        f = term if f is None else f + term
    z = ALPHA * _stream_rows(x1l_ref, x1c_ref) + mod_ref[0, 5:6, :] * f
    _stream_store(x2_refs, _layer_norm(z, lg_ref[...], lb_ref[...]))


def _combine(y_tok, route, x1l, x1c, mod, ln_g, ln_b, with_ctx):
    n_tiles = ALL_TILES if with_ctx else LAT_TILES
    return pl.pallas_call(
        _combine_kernel,
        grid=(n_tiles,),
        in_specs=[pl.BlockSpec((TOP_K, TILE, PACKED_D), lambda t: (0, t, 0)), _tok_spec(LANES)] + _stream_specs() + [
            _mod_spec(), _full_spec(ln_g.shape), _full_spec(ln_b.shape)],
        out_specs=_stream_specs()[:2 if with_ctx else 1],
        out_shape=_stream_shapes(with_ctx),
        compiler_params=_cparams("arbitrary"),
        name="combine",
    )(y_tok, route, x1l, x1c, mod, ln_g, ln_b)


def _moe(layer, h2, logits, x1l, x1c, mod, ln_g, ln_b, w1, b1p, w2, b2, with_ctx):
    n_tok = h2.shape[0]
    assert N_SLOT_TILES <= TILE
    route, slot_rows, tiles = _route(logits)
    slot_by_choice = slot_rows[TOP_K:2 * TOP_K].reshape(-1)
    tiles = tiles[:N_SLOT_TILES, :SUBLANES].astype(jnp.int32)
    tile_expert, tile_valid, n_tiles = tiles[:, 0], tiles[:, 1], tiles[:1, 2]
    xs = _sc_scatter(h2, slot_by_choice, N_SLOTS)
    ys = _experts(layer, tile_expert, tile_valid, n_tiles, xs, w1, b1p, w2, b2)
    y_tok = _sc_gather(ys, slot_by_choice).reshape(TOP_K, n_tok, PACKED_D)
    return _combine(y_tok, route, x1l, x1c, mod, ln_g, ln_b, with_ctx)


def _lat_to_col_major(lat):
    return lat.reshape(BATCH, SEQ // GRID_W, GRID_W, D_MODEL).transpose(0, 2, 1, 3).reshape(N_LAT, D_MODEL)


def _lat_to_row_major(lat):
    return lat.reshape(BATCH, GRID_W, SEQ // GRID_W, D_MODEL).transpose(0, 2, 1, 3).reshape(N_LAT, D_MODEL)


def kernel(x, c, ctx, c_ctx, ada_w, ada_b, ln_g, ln_b, dn_w_in, dn_conv_w, dn_a_log, dn_dt_bias, dn_norm_w, dn_w_out,
           hg_w_in, hg_lower_bound, hg_norm_w, hg_w_out, gla_w_in, gla_gk_w, gla_gk_b, gla_norm_w, gla_w_out,
           router_w, router_b, exp_w1, exp_b1, exp_w2, exp_b2):
    xl = x.reshape(N_LAT, D_MODEL)
    xc = ctx.reshape(N_CTX, D_MODEL)
    cc = jnp.zeros((MOD_ROWS, D_MODEL), F32).at[:BATCH].set(c).at[BATCH].set(c_ctx)
    mods = _ada_all(cc, ada_w, ada_b).reshape(DEPTH, MOD_ROWS, ADA_CHUNKS, D_MODEL)
    lb_soft = jax.nn.softmax(hg_lower_bound.astype(F32), axis=0)
    lower_bounds = jnp.cumsum(lb_soft, axis=0) - lb_soft[0]

    for i in range(DEPTH):
        last = i == DEPTH - 1
        kind, j = i % 3, i // 3
        mod = mods[i]
        col_major = i % 2 == 1
        if col_major:
            xl = _lat_to_col_major(xl)

        if kind == 0:
            nq = 2 * D_MODEL + 2 * D_MODEL
            w = dn_w_in[j]
            w_ba = jnp.zeros((D_MODEL, LANES), F32).at[:, :4 * DN_HEADS].set(w[:, nq:])
            a_neg = jnp.zeros((LANES,), F32).at[2 * DN_HEADS:4 * DN_HEADS].set(-jnp.exp(dn_a_log[j].astype(F32)).reshape(-1))
            dtb = jnp.zeros((LANES,), F32).at[2 * DN_HEADS:4 * DN_HEADS].set(dn_dt_bias[j].astype(F32).reshape(-1))
            nb = 4 * DN_HEADS
            p_qkv, gate, bgc, bgr = _dn_proj(xl, xc, mod, _bf(w), _bf(w_ba), _bf(w[:, nq:].T),
                                            a_neg.reshape(1, LANES), dtb.reshape(1, LANES),
                                            a_neg[:nb].reshape(nb, 1), dtb[:nb].reshape(nb, 1))
            cw = jnp.zeros((SUBLANES, DN_QKV), F32).at[:SHORT_CONV].set(dn_conv_w[j])
            q, k, v = _dn_conv(p_qkv, cw)
            o_f, o_b = _dn_scan(q, k, v, bgc, bgr)
            norm_w, w_out = dn_norm_w[j], dn_w_out[j]
        elif kind == 1:
            q, v, gate, k_f, g_f, k_b, g_b = _hg_proj(xl, xc, mod, _bf(hg_w_in[j]), lower_bounds[i].reshape(1, D_MODEL))
            o_f, o_b = _gla_scan(q, k_f, k_b, v, g_f, g_b)
            norm_w, w_out = hg_norm_w[j], hg_w_out[j]
        else:
            kd = GLA_HEADS * GLA_DK
            n_main = 2 * kd + 2 * D_MODEL
            w = gla_w_in[j]
            w_lr = jnp.zeros((D_MODEL, 2 * LANES), F32)
            w_lr = w_lr.at[:, :GLA_RANK].set(w[:, n_main:n_main + GLA_RANK])
            w_lr = w_lr.at[:, LANES:LANES + GLA_RANK].set(w[:, n_main + GLA_RANK:])
            gk_w = jnp.zeros((2, LANES, kd), F32).at[:, :GLA_RANK].set(gla_gk_w[j])
            q, k, v, gate, g_f, g_b = _gla_proj(xl, xc, mod, _bf(w), _bf(w_lr), _bf(gk_w),
                                                gla_gk_b[j].reshape(2, 1, kd))
            o_f, o_b = _gla_scan(q, k, k, v, g_f, g_b)
            norm_w, w_out = gla_norm_w[j], gla_w_out[j]

        with_ctx = not last
        rw = jnp.zeros((D_MODEL, LANES), F32).at[:, :N_EXPERTS].set(router_w[i])
        rb = jnp.zeros((1, LANES), F32).at[0, :N_EXPERTS].set(router_b[i])
        *x1, h2, logits = _post(o_f, o_b, gate, xl, xc, mod, norm_w.reshape(1, -1), _bf(w_out),
                                ln_g[i, 0].reshape(1, D_MODEL), ln_b[i, 0].reshape(1, D_MODEL), rw, rb, with_ctx)
        b1p = exp_b1[i].reshape(N_EXPERTS, 2 * D_FF // PAIR_BLOCK, LANES, 2).transpose(0, 1, 3, 2)
        x2 = _moe(i, h2, logits, x1[0], x1[1] if with_ctx else xc, mod,
                  ln_g[i, 1].reshape(1, D_MODEL), ln_b[i, 1].reshape(1, D_MODEL),
                  exp_w1, b1p.reshape(N_EXPERTS, 1, 2 * D_FF), exp_w2, exp_b2[i][:, None, :], with_ctx)
        xl = _lat_to_row_major(x2[0]) if col_major else x2[0]
        if with_ctx:
            xc = x2[1]
    return xl.reshape(BATCH, SEQ, D_MODEL)
```

```python
import functools

import jax
import jax.numpy as jnp
from jax import lax
from jax.experimental import pallas as pl
from jax.experimental.pallas import tpu as pltpu
from jax.experimental.pallas import tpu_sc as plsc

F32 = jnp.float32
BF16 = jnp.bfloat16
HIGHEST = lax.Precision.HIGHEST

D_MODEL = 1024
BATCH = 8
SEQ = 2048
CTX_LEN = 256
DEPTH = 4
GRID_W = 64
CHUNK = 64
ADA_CHUNKS = 6
N_LAT = BATCH * SEQ
N_CTX = BATCH * CTX_LEN
N_TOK = N_LAT + N_CTX
TILE = 256
LAT_TILES = N_LAT // TILE
ALL_TILES = N_TOK // TILE
TILES_PER_SEQ = SEQ // TILE
MOD_ROWS = 16
LANES = 128
SUBLANES = 8

N_EXPERTS = 32
TOP_K = 4
D_FF = D_MODEL
SWIGLU_LIMIT = 7.0
SWIGLU_ALPHA = 1.702
EXPERT_TILE = 512
N_SLOT_TILES = N_TOK * TOP_K // EXPERT_TILE + N_EXPERTS
N_SLOTS = N_SLOT_TILES * EXPERT_TILE

LN_EPS = 1e-5
RMS_EPS = 1e-6
ALPHA = (2.0 * DEPTH) ** 0.25

VMEM_LIMIT = 48 * 1024 * 1024
EXPERT_VMEM_LIMIT = (2 * 4 * (D_MODEL * 2 * D_FF + D_FF * D_MODEL) + 2 * (D_MODEL * 2 * D_FF + D_FF * D_MODEL)
                     + 4 * 4 * EXPERT_TILE * D_MODEL + 4 * EXPERT_TILE * (2 * D_FF + 2 * D_MODEL) + (4 << 20))


def _cparams(*sem):
    return pltpu.CompilerParams(dimension_semantics=sem, vmem_limit_bytes=VMEM_LIMIT)


def _sigmoid(x):
    return 1.0 / (1.0 + jnp.exp(-x))


def _silu(x):
    return x * _sigmoid(x)


def _softplus(x):
    return jnp.maximum(x, 0.0) + jnp.log(1.0 + jnp.exp(-jnp.abs(x)))


def _log_sigmoid(x):
    return -_softplus(-x)


def _nt(a, b):
    return lax.dot_general(a, b, (((1,), (1,)), ((), ())), preferred_element_type=F32)


def _tn(a, b):
    return lax.dot_general(a, b, (((0,), (0,)), ((), ())), preferred_element_type=F32)


def _mm(a, b):
    return jnp.dot(a, b, preferred_element_type=F32)


def _bf(x):
    return x.astype(BF16)


PACKED_D = D_MODEL // 2
HIGH_HALF = 0xFFFF0000


def _pack_halves(x):
    m = x.shape[1] // 2
    lo = pltpu.bitcast(x[:, :m].astype(BF16).astype(F32), jnp.uint32) >> 16
    hi = pltpu.bitcast(x[:, m:].astype(BF16).astype(F32), jnp.uint32) & jnp.uint32(HIGH_HALF)
    return pltpu.bitcast(hi | lo, F32)


def _unpack_halves(p):
    u = pltpu.bitcast(p, jnp.uint32)
    return jnp.concatenate([pltpu.bitcast(u << 16, F32), pltpu.bitcast(u & jnp.uint32(HIGH_HALF), F32)], axis=1)


def _mod_row(t):
    return jnp.minimum(t // TILES_PER_SEQ, BATCH)


ADA_NBLK = 1536


def _ada_kernel(c_ref, w_ref, b_ref, o_ref):
    s = _silu(c_ref[...])
    o_ref[0] = jnp.dot(s, w_ref[0], precision=HIGHEST, preferred_element_type=F32) + b_ref[0]


def _ada_all(cc, ada_w, ada_b):
    n = ADA_CHUNKS * D_MODEL
    return pl.pallas_call(
        _ada_kernel,
        grid=(DEPTH, n // ADA_NBLK),
        in_specs=[
            pl.BlockSpec((MOD_ROWS, D_MODEL), lambda i, j: (0, 0)),
            pl.BlockSpec((1, D_MODEL, ADA_NBLK), lambda i, j: (i, 0, j)),
            pl.BlockSpec((1, 1, ADA_NBLK), lambda i, j: (i, 0, j)),
        ],
        out_specs=pl.BlockSpec((1, MOD_ROWS, ADA_NBLK), lambda i, j: (i, 0, j)),
        out_shape=jax.ShapeDtypeStruct((DEPTH, MOD_ROWS, n), F32),
        compiler_params=_cparams("parallel", "parallel"),
        name="ada",
    )(cc, ada_w, ada_b.reshape(DEPTH, 1, n))


def _stream_rows(xl_ref, xc_ref):
    return jnp.where(pl.program_id(0) < LAT_TILES, xl_ref[...], xc_ref[...])


def _stream_specs():
    return [pl.BlockSpec((TILE, D_MODEL), lambda t: (jnp.minimum(t, LAT_TILES - 1), 0)),
            pl.BlockSpec((TILE, D_MODEL), lambda t: (jnp.maximum(t - LAT_TILES, 0), 0))]


def _stream_shapes(with_ctx):
    shapes = [jax.ShapeDtypeStruct((N_LAT, D_MODEL), F32)]
    return shapes + [jax.ShapeDtypeStruct((N_CTX, D_MODEL), F32)] if with_ctx else shapes


def _stream_store(refs, value):
    is_lat = pl.program_id(0) < LAT_TILES

    @pl.when(is_lat)
    def _():
        refs[0][...] = value

    if len(refs) > 1:
        @pl.when(jnp.logical_not(is_lat))
        def _():
            refs[1][...] = value


def _modulated(xl_ref, xc_ref, mod_ref, shift, scale):
    return _stream_rows(xl_ref, xc_ref) * (1.0 + mod_ref[0, scale:scale + 1, :]) + mod_ref[0, shift:shift + 1, :]


def _store_heads(ref, val, width):
    for h in range(val.shape[1] // width):
        ref[h] = val[:, h * width:(h + 1) * width].astype(ref.dtype)


def _hg_proj_kernel(xl_ref, xc_ref, mod_ref, w_ref, lb_ref, q_ref, v_ref, gate_ref, kf_ref, gf_ref, kb_ref, gb_ref):
    h = _bf(_modulated(xl_ref, xc_ref, mod_ref, 0, 1))
    d = D_MODEL
    lb = lb_ref[...]
    for k_ref, g_ref, lo in ((kf_ref, gf_ref, 3 * d), (kb_ref, gb_ref, 4 * d)):
        s = _sigmoid(_mm(h, w_ref[:, lo:lo + d]))
        _store_heads(k_ref, (1.0 - lb) * (1.0 - s), LANES)
        _store_heads(g_ref, jnp.log(lb + (1.0 - lb) * s), LANES)
    _store_heads(q_ref, _silu(_mm(h, w_ref[:, 0:d])), LANES)
    _store_heads(v_ref, _mm(h, w_ref[:, d:2 * d]), LANES)
    gate_ref[...] = _bf(_mm(h, w_ref[:, 2 * d:3 * d]))


def _head_spec(n_heads, width):
    return pl.BlockSpec((n_heads, TILE, width), lambda t: (0, t, 0))


def _head_shape(n_heads, width, dtype=None):
    return jax.ShapeDtypeStruct((n_heads, N_TOK, width), BF16 if dtype is None else dtype)


def _tok_shape(width):
    return jax.ShapeDtypeStruct((N_TOK, width), BF16)


def _tok_spec(width):
    return pl.BlockSpec((TILE, width), lambda t: (t, 0))


def _mod_spec():
    return pl.BlockSpec((1, ADA_CHUNKS, D_MODEL), lambda t: (_mod_row(t), 0, 0))


def _full_spec(shape):
    return pl.BlockSpec(shape, lambda t: (0,) * len(shape))


def _hg_proj(xl, xc, mod, w_in, lb):
    n = w_in.shape[1]
    hs = _head_shape(8, LANES)
    hg = _head_shape(8, LANES, F32)
    return pl.pallas_call(
        _hg_proj_kernel,
        grid=(ALL_TILES,),
        in_specs=_stream_specs() + [_mod_spec(), _full_spec((D_MODEL, n)), _full_spec((1, D_MODEL))],
        out_specs=[_head_spec(8, LANES), _head_spec(8, LANES), _tok_spec(D_MODEL),
                   _head_spec(8, LANES), _head_spec(8, LANES), _head_spec(8, LANES), _head_spec(8, LANES)],
        out_shape=[hs, hs, _tok_shape(D_MODEL), hs, hg, hs, hg],
        compiler_params=_cparams("parallel"),
        name="hg_proj",
    )(xl, xc, mod, w_in, lb)


GLA_HEADS = 4
GLA_DK = 128
GLA_DV = 256
GLA_RANK = 16
GLA_GATE_NORM = 16.0


def _gla_proj_kernel(xl_ref, xc_ref, mod_ref, w_ref, wlr_ref, gkw_ref, gkb_ref, q_ref, k_ref, v_ref, gate_ref, gf_ref, gb_ref):
    h = _bf(_modulated(xl_ref, xc_ref, mod_ref, 0, 1))
    kd = GLA_HEADS * GLA_DK
    vd = GLA_HEADS * GLA_DV
    for z, g_ref in ((0, gf_ref), (1, gb_ref)):
        lr = _mm(h, wlr_ref[:, z * LANES:(z + 1) * LANES])
        gk = _mm(_bf(lr), gkw_ref[z]) + gkb_ref[z]
        _store_heads(g_ref, _log_sigmoid(gk) / GLA_GATE_NORM, GLA_DK)
    _store_heads(q_ref, _mm(h, w_ref[:, 0:kd]) * GLA_DK ** -0.5, GLA_DK)
    _store_heads(k_ref, _mm(h, w_ref[:, kd:2 * kd]), GLA_DK)
    _store_heads(v_ref, _mm(h, w_ref[:, 2 * kd:2 * kd + vd]), GLA_DV)
    gate_ref[...] = _bf(_mm(h, w_ref[:, 2 * kd + vd:2 * kd + 2 * vd]))


def _gla_proj(xl, xc, mod, w_main, w_lr, gk_w, gk_b):
    kd = GLA_HEADS * GLA_DK
    hk = _head_shape(GLA_HEADS, GLA_DK)
    return pl.pallas_call(
        _gla_proj_kernel,
        grid=(ALL_TILES,),
        in_specs=_stream_specs() + [_mod_spec(), _full_spec(w_main.shape), _full_spec(w_lr.shape),
                  _full_spec(gk_w.shape), _full_spec(gk_b.shape)],
        out_specs=[_head_spec(GLA_HEADS, GLA_DK), _head_spec(GLA_HEADS, GLA_DK), _head_spec(GLA_HEADS, GLA_DV),
                   _tok_spec(D_MODEL), _head_spec(GLA_HEADS, GLA_DK), _head_spec(GLA_HEADS, GLA_DK)],
        out_shape=[hk, hk, _head_shape(GLA_HEADS, GLA_DV), _tok_shape(D_MODEL),
                   _head_shape(GLA_HEADS, GLA_DK, F32), _head_shape(GLA_HEADS, GLA_DK, F32)],
        compiler_params=_cparams("parallel"),
        name="gla_proj",
    )(xl, xc, mod, w_main, w_lr, gk_w, gk_b)


DN_HEADS = 8
DN_QKV = 3 * D_MODEL
SHORT_CONV = 5
CHUNKS_PER_TILE = TILE // CHUNK


def _seg_scan(x, pos, axis, reverse):
    n = x.shape[axis]
    sh = 1
    while sh < CHUNK:
        if reverse:
            x = x + jnp.where(pos < CHUNK - sh, pltpu.roll(x, n - sh, axis), 0.0)
        else:
            x = x + jnp.where(pos >= sh, pltpu.roll(x, sh, axis), 0.0)
        sh *= 2
    return x


def _dn_gates(p, a_neg, dtb, idx, pos, axis):
    beta = _sigmoid(p)
    g = a_neg * _softplus(p + dtb)
    gp = _seg_scan(g, pos, axis, False)
    gs = _seg_scan(g, pos, axis, True)
    return jnp.where(idx < 2 * DN_HEADS, beta, jnp.where(idx < 3 * DN_HEADS, gp, gs))


def _dn_proj_kernel(xl_ref, xc_ref, mod_ref, w_ref, wba_ref, wbat_ref, ac_ref, dc_ref, ar_ref, dr_ref,
                    qkv_ref, gate_ref, bgc_ref, bgr_ref):
    h = _bf(_modulated(xl_ref, xc_ref, mod_ref, 0, 1))
    pc = _mm(h, wba_ref[...])
    lane = lax.broadcasted_iota(jnp.int32, pc.shape, 1)
    rowpos = lax.broadcasted_iota(jnp.int32, pc.shape, 0) % CHUNK
    bgc_ref[...] = _dn_gates(pc, ac_ref[...], dc_ref[...], lane, rowpos, 0)
    pr = _nt(wbat_ref[...], h)
    sub = lax.broadcasted_iota(jnp.int32, pr.shape, 0)
    lanepos = lax.broadcasted_iota(jnp.int32, pr.shape, 1) % CHUNK
    gr = _dn_gates(pr, ar_ref[...], dr_ref[...], sub, lanepos, 1)
    for c in range(CHUNKS_PER_TILE):
        bgr_ref[c] = gr[:, c * CHUNK:(c + 1) * CHUNK]
    qkv_ref[...] = _bf(_mm(h, w_ref[:, 0:DN_QKV]))
    gate_ref[...] = _bf(_mm(h, w_ref[:, DN_QKV:DN_QKV + D_MODEL]))


def _dn_proj(xl, xc, mod, w_main, w_ba, w_bat, a_col, d_col, a_row, d_row):
    nb = 4 * DN_HEADS
    return pl.pallas_call(
        _dn_proj_kernel,
        grid=(ALL_TILES,),
        in_specs=_stream_specs() + [_mod_spec(), _full_spec(w_main.shape), _full_spec(w_ba.shape),
                  _full_spec(w_bat.shape), _full_spec(a_col.shape), _full_spec(d_col.shape),
                  _full_spec(a_row.shape), _full_spec(d_row.shape)],
        out_specs=[_tok_spec(DN_QKV), _tok_spec(D_MODEL), _tok_spec(LANES),
                   pl.BlockSpec((CHUNKS_PER_TILE, nb, CHUNK), lambda t: (t, 0, 0))],
        out_shape=[_tok_shape(DN_QKV), _tok_shape(D_MODEL), jax.ShapeDtypeStruct((N_TOK, LANES), F32),
                   jax.ShapeDtypeStruct((N_TOK // CHUNK, nb, CHUNK), F32)],
        compiler_params=_cparams("parallel"),
        name="dn_proj",
    )(xl, xc, mod, w_main, w_ba, w_bat, a_col, d_col, a_row, d_row)


HALO = 2 * SUBLANES


def _dn_conv_kernel(pm_ref, pp_ref, pn_ref, cw_ref, q_ref, k_ref, v_ref):
    t = pl.program_id(0)
    is_lat = t < LAT_TILES
    first = jnp.logical_or(jnp.logical_not(is_lat), t % TILES_PER_SEQ == 0)
    last = jnp.logical_or(jnp.logical_not(is_lat), t % TILES_PER_SEQ == TILES_PER_SEQ - 1)
    keep_prev = jnp.where(first, 0.0, 1.0)
    keep_next = jnp.where(last, 0.0, 1.0)
    n_ext = TILE + 2 * HALO
    pad = SHORT_CONV // 2
    for s in range(DN_QKV // LANES):
        cols = slice(s * LANES, (s + 1) * LANES)
        ext = jnp.concatenate([pp_ref[:, cols].astype(F32) * keep_prev, pm_ref[:, cols].astype(F32),
                               pn_ref[:, cols].astype(F32) * keep_next], axis=0)
        acc = None
        for kk in range(SHORT_CONV):
            off = HALO - pad + kk
            tap = pltpu.roll(ext, n_ext - off, 0)[0:TILE] * cw_ref[kk:kk + 1, cols]
            acc = tap if acc is None else acc + tap
        a = _silu(acc)
        head = s % DN_HEADS
        if s < 2 * DN_HEADS:
            a = a * lax.rsqrt(jnp.sum(a * a, axis=1, keepdims=True) + 1e-6)
            if s < DN_HEADS:
                q_ref[head] = _bf(a * LANES ** -0.5)
            else:
                k_ref[head] = _bf(a)
        else:
            v_ref[head] = _bf(a)


def _dn_conv(p_qkv, conv_w):
    rows8 = TILE // HALO
    last8 = N_TOK // HALO - 1
    hs = _head_shape(DN_HEADS, LANES)
    return pl.pallas_call(
        _dn_conv_kernel,
        grid=(ALL_TILES,),
        in_specs=[_tok_spec(DN_QKV),
                  pl.BlockSpec((HALO, DN_QKV), lambda t: (jnp.maximum(t * rows8 - 1, 0), 0)),
                  pl.BlockSpec((HALO, DN_QKV), lambda t: (jnp.minimum((t + 1) * rows8, last8), 0)),
                  _full_spec(conv_w.shape)],
        out_specs=[_head_spec(DN_HEADS, LANES)] * 3,
        out_shape=[hs, hs, hs],
        compiler_params=_cparams("parallel"),
        name="dn_conv",
    )(p_qkv, p_qkv, p_qkv, conv_w)


LOG2_E = 1.4426950408889634
SIGN_BIT = 0x80000000


def _chunk_cumsum(x, ones_tri):
    hi = _bf(x)
    rest = x - hi.astype(F32)
    mid = _bf(rest)
    lo = _bf(rest - mid.astype(F32))
    return _mm(ones_tri, hi) + _mm(ones_tri, mid) + _mm(ones_tri, lo)


def _neg_abs(x):
    return pltpu.bitcast(pltpu.bitcast(x, jnp.uint32) | jnp.uint32(SIGN_BIT), F32)


def _gla_chunks(q, k, v, g, st, rev, row, r64, c64):
    n = range(len(q))
    q32 = [q[i].astype(F32) for i in n]
    k32 = [k[i].astype(F32) for i in n]
    ones_tri = {False: jnp.where(r64 >= c64, 1.0, 0.0).astype(BF16), True: jnp.where(r64 <= c64, 1.0, 0.0).astype(BF16)}
    cum = [_chunk_cumsum(g[i] * LOG2_E, ones_tri[rev[i]]) for i in n]
    g_end = [cum[i][0:1, :] if rev[i] else cum[i][CHUNK - 1:CHUNK, :] for i in n]
    att = [jnp.where(r64 == c64, _nt(_bf(q[i]), _bf(k[i])), 0.0) for i in n]
    edge = list(cum)
    s = 1
    while s < CHUNK:
        late = (row & s) != 0
        early = jnp.logical_not(late)
        rlate = (r64 & s) != 0
        clate = (c64 & s) != 0
        pair = {False: jnp.logical_and(rlate, jnp.logical_not(clate)), True: jnp.logical_and(clate, jnp.logical_not(rlate))}
        if 2 * s < CHUNK:
            shift = (2 * s).bit_length() - 1
            same = (r64 >> shift) == (c64 >> shift)
            pair = {d: jnp.logical_and(p, same) for d, p in pair.items()}
        for i in n:
            if rev[i]:
                cut = jnp.where(late, edge[i], pltpu.roll(edge[i], CHUNK - s, 0))
            else:
                cut = jnp.where(late, pltpu.roll(edge[i], s, 0), edge[i])
            e = jnp.exp2(_neg_abs(cum[i] - cut))
            q_side = early if rev[i] else late
            x16 = _bf(jnp.where(q_side, q32[i], k32[i]) * e)
            att[i] = att[i] + jnp.where(pair[rev[i]], _nt(x16, x16), 0.0)
            if rev[i]:
                edge[i] = jnp.where(late, pltpu.roll(edge[i], s, 0), edge[i])
            else:
                edge[i] = jnp.where(late, edge[i], pltpu.roll(edge[i], CHUNK - s, 0))
        s *= 2
    st16 = [_bf(st[i]) for i in n]
    o = [_mm(_bf(att[i]), _bf(v[i])) + _nt(_bf(q32[i] * jnp.exp2(cum[i])), st16[i]) for i in n]
    st_new = [st[i] * jnp.exp2(g_end[i]) + _tn(_bf(v[i]), _bf(k32[i] * jnp.exp2(g_end[i] - cum[i]))) for i in n]
    return o, st_new


def _scan_iotas():
    row = lax.broadcasted_iota(jnp.int32, (CHUNK, LANES), 0)
    r64 = lax.broadcasted_iota(jnp.int32, (CHUNK, CHUNK), 0)
    c64 = lax.broadcasted_iota(jnp.int32, (CHUNK, CHUNK), 1)
    return row, r64, c64


SCAN_BATCH = 2


def _scan_chains(in_refs, per_chain, n_heads):
    kinds = [[] for _ in range(per_chain)]
    rev = []
    for d in range(2):
        for bi in range(SCAN_BATCH):
            group = in_refs[(d * SCAN_BATCH + bi) * per_chain:(d * SCAN_BATCH + bi + 1) * per_chain]
            for h in range(n_heads):
                for kind, ref in zip(kinds, group):
                    kind.append((ref, h))
                rev.append(d == 1)
    return kinds, rev


def _scan_store(o, st, of_ref, ob_ref, st_ref, n_heads):
    i = 0
    for d, o_ref in ((0, of_ref), (1, ob_ref)):
        for bi in range(SCAN_BATCH):
            for h in range(n_heads):
                o_ref[h, bi * CHUNK:(bi + 1) * CHUNK, :] = _bf(o[i])
                st_ref[d, bi, h] = st[i]
                i += 1


def _gla_scan_kernel(*refs, n_heads):
    n_in = 2 * SCAN_BATCH * 4
    of_ref, ob_ref, st_ref = refs[n_in:]

    @pl.when(pl.program_id(1) == 0)
    def _():
        st_ref[...] = jnp.zeros_like(st_ref)

    row, r64, c64 = _scan_iotas()
    (q, k, v, g), rev = _scan_chains(refs[:n_in], 4, n_heads)
    load = lambda items: [ref[h] for ref, h in items]
    st = [st_ref[d, bi, h] for d in range(2) for bi in range(SCAN_BATCH) for h in range(n_heads)]
    o, st = _gla_chunks(load(q), load(k), load(v), load(g), st, rev, row, r64, c64)
    _scan_store(o, st, of_ref, ob_ref, st_ref, n_heads)


CTX_CHUNKS = CTX_LEN // CHUNK
LAT_CHUNKS = SEQ // CHUNK
SCAN_STEPS = CTX_CHUNKS + LAT_CHUNKS


def _fwd_chunk(b, n):
    return jnp.where(n < CTX_CHUNKS, N_LAT // CHUNK + b * CTX_CHUNKS + n, b * LAT_CHUNKS + n - CTX_CHUNKS)


def _bwd_chunk(b, n):
    return jnp.where(n < CTX_CHUNKS, N_LAT // CHUNK + b * CTX_CHUNKS + CTX_CHUNKS - 1 - n,
                     b * LAT_CHUNKS + SCAN_STEPS - 1 - n)


def _scan_in_maps(rank):
    def make(chunk_of, bi):
        def index_map(p, n):
            c = chunk_of(p * SCAN_BATCH + bi, n)
            return {1: (c, 0, 0), 2: (0, c, 0), 3: (c, 0)}[rank]
        return index_map
    return [make(chunk_of, bi) for chunk_of in (_fwd_chunk, _bwd_chunk) for bi in range(SCAN_BATCH)]


def _scan_out_pos(step, reverse):
    if not reverse:
        return step
    return jnp.where(step < CTX_CHUNKS, CTX_CHUNKS - 1 - step, CTX_CHUNKS + SCAN_STEPS - 1 - step)


def _scan_out_specs(n_heads, dv):
    return [pl.BlockSpec((n_heads, SCAN_BATCH * CHUNK, dv),
                         lambda p, n, reverse=reverse: (0, p * SCAN_STEPS + _scan_out_pos(n, reverse), 0))
            for reverse in (False, True)]


def _scan_out_block(t, c):
    is_lat = t < LAT_TILES
    b = jnp.where(is_lat, t // TILES_PER_SEQ, t - LAT_TILES)
    pos = jnp.where(is_lat, CTX_CHUNKS + (t % TILES_PER_SEQ) * CHUNKS_PER_TILE + c, c)
    return ((b // SCAN_BATCH) * SCAN_STEPS + pos) * SCAN_BATCH + b % SCAN_BATCH


def _gla_scan(q, k_f, k_b, v, g_f, g_b):
    n_heads, _, dk = q.shape
    dv = v.shape[2]
    maps = _scan_in_maps(2)
    kspec = lambda m: pl.BlockSpec((n_heads, CHUNK, dk), m)
    vspec = lambda m: pl.BlockSpec((n_heads, CHUNK, dv), m)
    in_specs, args = [], []
    for d, k_arr, g_arr in ((0, k_f, g_f), (1, k_b, g_b)):
        for bi in range(SCAN_BATCH):
            m = maps[d * SCAN_BATCH + bi]
            in_specs += [kspec(m), kspec(m), vspec(m), kspec(m)]
            args += [q, k_arr, v, g_arr]
    oshape = _head_shape(n_heads, dv)
    return pl.pallas_call(
        functools.partial(_gla_scan_kernel, n_heads=n_heads),
        grid=(BATCH // SCAN_BATCH, SCAN_STEPS),
        in_specs=in_specs,
        out_specs=_scan_out_specs(n_heads, dv),
        out_shape=[oshape, oshape],
        scratch_shapes=[pltpu.VMEM((2, SCAN_BATCH, n_heads, dv, dk), F32)],
        compiler_params=_cparams("parallel", "arbitrary"),
        name="gla_scan",
    )(*args)


def _lane_pick(x, lane, idx):
    return jnp.sum(jnp.where(lane == idx, x, 0.0), axis=1, keepdims=True)


def _dn_chunks(q, k, v, beta, g_col, g_row, st, rev, r64, c64):
    n = range(len(q))
    dk = k[0].shape[1]
    g_end = [g_row[i][:, 0:1] if rev[i] else g_row[i][:, CHUNK - 1:CHUNK] for i in n]
    causal = {False: r64 >= c64, True: r64 <= c64}
    gam = [jnp.where(causal[rev[i]], jnp.exp(jnp.minimum(g_col[i] - g_row[i], 0.0)), 0.0) for i in n]
    q32 = [q[i].astype(F32) for i in n]
    k32 = [k[i].astype(F32) for i in n]
    kb = [k32[i] * beta[i] for i in n]
    k16 = [_bf(k[i]) for i in n]
    m = [jnp.where(r64 == c64, 0.0, _nt(_bf(kb[i]), k16[i]) * gam[i]) for i in n]
    att = [_nt(_bf(q[i]), k16[i]) * gam[i] for i in n]
    y = None
    s = 1
    while s < CHUNK:
        shift = (2 * s).bit_length() - 1
        rlate = (r64 & s) != 0
        clate = (c64 & s) != 0
        pair = {False: jnp.logical_and(rlate, jnp.logical_not(clate)), True: jnp.logical_and(clate, jnp.logical_not(rlate))}
        if 2 * s < CHUNK:
            same = (r64 >> shift) == (c64 >> shift)
            pair = {d: jnp.logical_and(p, same) for d, p in pair.items()}
        c = [jnp.where(pair[rev[i]], m[i], 0.0) for i in n]
        if y is None:
            y = [-c[i] for i in n]
        else:
            c16 = [_bf(c[i]) for i in n]
            y16 = [_bf(y[i]) for i in n]
            p = [c[i] + _mm(y16[i], c16[i]) for i in n]
            y = [y[i] - (p[i] + _mm(_bf(p[i]), y16[i])) for i in n]
        s *= 2
    e_col = [jnp.exp(g_col[i]) for i in n]
    rhs = [jnp.concatenate([kb[i] * e_col[i], v[i].astype(F32) * beta[i]], axis=1) for i in n]
    wu = [rhs[i] + _mm(_bf(y[i]), _bf(rhs[i])) for i in n]
    st16 = [_bf(st[i]) for i in n]
    v_new = [wu[i][:, dk:] - _nt(_bf(wu[i][:, :dk]), st16[i]) for i in n]
    o = [_nt(_bf(q32[i] * e_col[i]), st16[i]) + _mm(_bf(att[i]), _bf(v_new[i])) for i in n]
    st_new = [st[i] * jnp.exp(g_end[i]) + _tn(_bf(v_new[i]), _bf(k32[i] * jnp.exp(g_end[i] - g_col[i]))) for i in n]
    return o, st_new


def _dn_scan_kernel(*refs):
    n_in = 2 * SCAN_BATCH * 5
    of_ref, ob_ref, st_ref = refs[n_in:]

    @pl.when(pl.program_id(1) == 0)
    def _():
        st_ref[...] = jnp.zeros_like(st_ref)

    _, r64, c64 = _scan_iotas()
    (q, k, v, gc, gr), rev = _scan_chains(refs[:n_in], 5, DN_HEADS)
    load = lambda items: [ref[h] for ref, h in items]
    tables = {id(ref): ref[...] for ref, _ in gc}
    beta = [tables[id(ref)][:, d * DN_HEADS + h:d * DN_HEADS + h + 1] for (ref, h), d in zip(gc, rev)]
    g_col = [tables[id(ref)][:, (2 + d) * DN_HEADS + h:(2 + d) * DN_HEADS + h + 1] for (ref, h), d in zip(gc, rev)]
    g_row = [ref[0, (2 + d) * DN_HEADS + h:(2 + d) * DN_HEADS + h + 1, :] for (ref, h), d in zip(gr, rev)]
    st = [st_ref[d, bi, h] for d in range(2) for bi in range(SCAN_BATCH) for h in range(DN_HEADS)]
    o, st = _dn_chunks(load(q), load(k), load(v), beta, g_col, g_row, st, rev, r64, c64)
    _scan_store(o, st, of_ref, ob_ref, st_ref, DN_HEADS)


def _dn_scan(q, k, v, bgc, bgr):
    head_maps, col_maps, row_maps = _scan_in_maps(2), _scan_in_maps(3), _scan_in_maps(1)
    in_specs, args = [], []
    for i in range(2 * SCAN_BATCH):
        in_specs += [pl.BlockSpec((DN_HEADS, CHUNK, LANES), head_maps[i])] * 3
        in_specs += [pl.BlockSpec((CHUNK, LANES), col_maps[i]), pl.BlockSpec((1, 4 * DN_HEADS, CHUNK), row_maps[i])]
        args += [q, k, v, bgc, bgr]
    oshape = _head_shape(DN_HEADS, LANES)
    return pl.pallas_call(
        _dn_scan_kernel,
        grid=(BATCH // SCAN_BATCH, SCAN_STEPS),
        in_specs=in_specs,
        out_specs=_scan_out_specs(DN_HEADS, LANES),
        out_shape=[oshape, oshape],
        scratch_shapes=[pltpu.VMEM((2, SCAN_BATCH, DN_HEADS, LANES, LANES), F32)],
        compiler_params=_cparams("parallel", "arbitrary"),
        name="dn_scan",
    )(*args)


def _layer_norm(z, g, b):
    mu = jnp.mean(z, axis=1, keepdims=True)
    zc = z - mu
    var = jnp.mean(zc * zc, axis=1, keepdims=True)
    return zc * lax.rsqrt(var + LN_EPS) * g + b


def _post_kernel(*refs, n_heads):
    o_refs, refs = refs[:2 * CHUNKS_PER_TILE], refs[2 * CHUNKS_PER_TILE:]
    gate_ref, xl_ref, xc_ref, mod_ref, nw_ref, wo_ref, lg_ref, lb_ref, rw_ref, rb_ref = refs[:10]
    out_refs = refs[10:]
    x1_refs, (h2_ref, logit_ref) = out_refs[:-2], out_refs[-2:]
    x_rows = _stream_rows(xl_ref, xc_ref)
    groups = [slice(g * TILE // POST_GROUPS, (g + 1) * TILE // POST_GROUPS) for g in range(POST_GROUPS)]
    chunks_per_group = CHUNKS_PER_TILE // POST_GROUPS
    on = []
    for g, rows in enumerate(groups):
        parts = []
        for h in range(n_heads):
            o = jnp.concatenate([o_refs[c][h].astype(F32) + o_refs[CHUNKS_PER_TILE + c][h].astype(F32)
                                 for c in range(g * chunks_per_group, (g + 1) * chunks_per_group)], axis=0)
            parts.append(o * lax.rsqrt(jnp.mean(o * o, axis=1, keepdims=True) + RMS_EPS) * nw_ref[...])
        on.append(jnp.concatenate(parts, axis=1) * _silu(gate_ref[rows, :].astype(F32)))
    y = [_mm(_bf(a), wo_ref[...]) for a in on]
    z = [ALPHA * x_rows[rows] + mod_ref[0, 2:3, :] * yg for rows, yg in zip(groups, y)]
    x1 = [_layer_norm(zg, lg_ref[...], lb_ref[...]) for zg in z]
    h2 = [a * (1.0 + mod_ref[0, 4:5, :]) + mod_ref[0, 3:4, :] for a in x1]
    logits = [jnp.dot(a, rw_ref[...], precision=HIGHEST, preferred_element_type=F32) + rb_ref[...] for a in h2]
    _stream_store(x1_refs, jnp.concatenate(x1, axis=0))
    h2_ref[...] = _pack_halves(jnp.concatenate(h2, axis=0))
    logit_ref[...] = jnp.concatenate(logits, axis=0)


POST_GROUPS = 2


def _post(o_f, o_b, gate, xl, xc, mod, norm_w, w_out, ln_g, ln_b, rw, rb, with_ctx):
    n_heads, _, dv = o_f.shape
    n_tiles = ALL_TILES if with_ctx else LAT_TILES
    n_rows = n_tiles * TILE
    tok = lambda w: jax.ShapeDtypeStruct((n_rows, w), F32)
    return pl.pallas_call(
        functools.partial(_post_kernel, n_heads=n_heads),
        grid=(n_tiles,),
        in_specs=[pl.BlockSpec((n_heads, CHUNK, dv), lambda t, c=c: (0, _scan_out_block(t, c), 0))
                  for _ in range(2) for c in range(CHUNKS_PER_TILE)] + [_tok_spec(D_MODEL)] + _stream_specs() + [
            _mod_spec(), _full_spec(norm_w.shape), _full_spec(w_out.shape), _full_spec(ln_g.shape),
            _full_spec(ln_b.shape), _full_spec(rw.shape), _full_spec(rb.shape)],
        out_specs=_stream_specs()[:2 if with_ctx else 1] + [_tok_spec(PACKED_D), _tok_spec(LANES)],
        out_shape=_stream_shapes(with_ctx) + [tok(PACKED_D), tok(LANES)],
        compiler_params=_cparams("arbitrary"),
        name="post_mixer",
    )(*([o_f] * CHUNKS_PER_TILE + [o_b] * CHUNKS_PER_TILE), gate, xl, xc, mod, norm_w, w_out, ln_g, ln_b, rw, rb)


def _route_kernel(logit_ref, route_ref, slot_ref, tiles_ref, run_ref, start_ref, table_ref):
    sweep = pl.program_id(0)
    t = pl.program_id(1)
    lane = lax.broadcasted_iota(jnp.int32, (TILE, LANES), 1)
    lanef = lane.astype(F32)

    def add_counts(onehot):
        new_run = run_ref[0:1, :] + jnp.sum(onehot, axis=0, keepdims=True)
        run_ref[...] = jnp.broadcast_to(new_run, run_ref.shape)
        return new_run

    @pl.when(t == 0)
    def _():
        run_ref[...] = jnp.zeros_like(run_ref)

    @pl.when(sweep == 0)
    def _():
        neg = jnp.float32(-jnp.inf)
        l = jnp.where(lane < N_EXPERTS, logit_ref[...], neg)
        vals, picks = [], []
        onehot = jnp.zeros((TILE, LANES), F32)
        for _ in range(TOP_K):
            m = jnp.max(l, axis=1, keepdims=True)
            pick = jnp.min(jnp.where(l == m, lanef, float(LANES)), axis=1, keepdims=True)
            hot = lanef == pick
            l = jnp.where(hot, neg, l)
            onehot = onehot + jnp.where(hot, 1.0, 0.0)
            vals.append(m)
            picks.append(pick)
        es = [jnp.exp(v - vals[0]) for v in vals]
        denom = es[0] + es[1] + es[2] + es[3]
        out = jnp.zeros((TILE, LANES), F32)
        for kk in range(TOP_K):
            out = jnp.where(lane == kk, picks[kk], out)
            out = jnp.where(lane == ROUTE_WEIGHT + kk, es[kk] / denom, out)
        table_ref[t] = out
        new_run = add_counts(onehot)

        @pl.when(t == pl.num_programs(1) - 1)
        def _():
            lane8 = lax.broadcasted_iota(jnp.int32, (SUBLANES, LANES), 1)
            size = jnp.ceil(jnp.broadcast_to(new_run, (SUBLANES, LANES)) * (1.0 / EXPERT_TILE)) * EXPERT_TILE
            size = jnp.where(lane8 < N_EXPERTS, size, 0.0)
            ends = size
            sh = 1
            while sh < N_EXPERTS:
                ends = ends + jnp.where(lane8 >= sh, pltpu.roll(ends, sh, 1), 0.0)
                sh *= 2
            start_ref[...] = ends - size
            tile_start = lax.broadcasted_iota(jnp.int32, (TILE, 1), 0).astype(F32) * EXPERT_TILE
            ends_t = jnp.broadcast_to(ends[0:1, :], (TILE, LANES))
            group_end_t = jnp.broadcast_to((ends - size)[0:1, :] + new_run, (TILE, LANES))
            expert = jnp.sum(jnp.where(jnp.logical_and(lane < N_EXPERTS, ends_t <= tile_start), 1.0, 0.0),
                             axis=1, keepdims=True)
            expert = jnp.minimum(expert, N_EXPERTS - 1.0)
            group_end = jnp.sum(jnp.where(lanef == expert, group_end_t, 0.0), axis=1, keepdims=True)
            valid = jnp.clip(group_end - tile_start, 0.0, float(EXPERT_TILE))
            total = jnp.max(ends_t, axis=1, keepdims=True)
            own_end = jnp.sum(jnp.where(lanef == expert, ends_t, 0.0), axis=1, keepdims=True)
            nxt = jnp.sum(jnp.where(jnp.logical_and(lane < N_EXPERTS, ends_t <= own_end), 1.0, 0.0), axis=1, keepdims=True)
            nxt = jnp.where(own_end < total, nxt, -1.0)
            tiles_ref[...] = jnp.where(lane == 0, expert, jnp.where(lane == 1, valid, jnp.where(
                lane == 2, total * (1.0 / EXPERT_TILE), jnp.where(lane == 3, nxt, 0.0))))

    @pl.when(sweep == 1)
    def _():
        out = table_ref[t]
        hots = [lanef == out[:, kk:kk + 1] for kk in range(TOP_K)]
        onehot = jnp.zeros((TILE, LANES), F32)
        for hot in hots:
            onehot = onehot + jnp.where(hot, 1.0, 0.0)
        r = lax.broadcasted_iota(jnp.int32, (TILE, TILE), 0)
        c = lax.broadcasted_iota(jnp.int32, (TILE, TILE), 1)
        before = jnp.where(c < r, 1.0, 0.0).astype(BF16)
        slot_of = _mm(before, _bf(onehot)) + run_ref[0:1, :] + start_ref[0:1, :]
        for kk in range(TOP_K):
            slot = jnp.sum(jnp.where(hots[kk], slot_of, 0.0), axis=1, keepdims=True)
            out = jnp.where(lane == ROUTE_SLOT + kk, slot, out)
        route_ref[...] = out
        slot_ref[...] = out.T[ROUTE_WEIGHT:ROUTE_WEIGHT + SUBLANES, :].astype(jnp.int32)
        add_counts(onehot)


ROUTE_WEIGHT = SUBLANES
ROUTE_SLOT = SUBLANES + TOP_K


def _route(logits):
    n_tok = logits.shape[0]
    n_tiles = n_tok // TILE
    return pl.pallas_call(
        _route_kernel,
        grid=(2, n_tiles),
        in_specs=[pl.BlockSpec((TILE, LANES), lambda s, t: (t, 0))],
        out_specs=[pl.BlockSpec((TILE, LANES), lambda s, t: (s * t, 0)),
                   pl.BlockSpec((SUBLANES, TILE), lambda s, t: (0, s * t)),
                   pl.BlockSpec((TILE, LANES), lambda s, t: (0, 0))],
        out_shape=[jax.ShapeDtypeStruct(logits.shape, F32), jax.ShapeDtypeStruct((SUBLANES, n_tok), jnp.int32),
                   jax.ShapeDtypeStruct((TILE, LANES), F32)],
        scratch_shapes=[pltpu.VMEM((SUBLANES, LANES), F32), pltpu.VMEM((SUBLANES, LANES), F32),
                        pltpu.VMEM((n_tiles, TILE, LANES), F32)],
        compiler_params=_cparams("arbitrary", "arbitrary"),
        name="route",
    )(logits)


GATHER_ROWS = 16
GATHER_BUFS = 4


def _sc_gather(table, idx):
    n_rows = idx.shape[0]
    d = table.shape[1]
    info = plsc.get_sparse_core_info()
    n_workers = info.num_cores * info.num_subcores
    rows_per_worker = n_rows // n_workers
    n_chunks = rows_per_worker // GATHER_ROWS
    assert rows_per_worker * n_workers == n_rows and n_chunks * GATHER_ROWS == rows_per_worker
    assert n_chunks % GATHER_BUFS == 0
    mesh = plsc.VectorSubcoreMesh(core_axis_name="c", subcore_axis_name="s")
    scratch = ([pltpu.VMEM((GATHER_ROWS,), jnp.int32)] * GATHER_BUFS
               + [pltpu.VMEM((GATHER_ROWS, d), table.dtype)] * GATHER_BUFS
               + [pltpu.SemaphoreType.DMA] * (2 * GATHER_BUFS))

    def body(table_hbm, idx_hbm, out_hbm, *scr):
        idx_v, rows_v = scr[:GATHER_BUFS], scr[GATHER_BUFS:2 * GATHER_BUFS]
        gather_sem, write_sem = scr[2 * GATHER_BUFS:3 * GATHER_BUFS], scr[3 * GATHER_BUFS:]
        worker = lax.axis_index("s") * info.num_cores + lax.axis_index("c")
        base = worker * rows_per_worker

        @pl.loop(0, n_chunks, step=GATHER_BUFS)
        def _(c0):
            offs = [pl.multiple_of(base + (c0 + b) * GATHER_ROWS, SUBLANES) for b in range(GATHER_BUFS)]
            gathers = []
            for b in range(GATHER_BUFS):
                pltpu.sync_copy(idx_hbm.at[pl.ds(offs[b], GATHER_ROWS)], idx_v[b])
                gathers.append(pltpu.async_copy(table_hbm.at[idx_v[b]], rows_v[b], gather_sem[b]))
            writes = []
            for b in range(GATHER_BUFS):
                gathers[b].wait()
                writes.append(pltpu.async_copy(rows_v[b], out_hbm.at[pl.ds(offs[b], GATHER_ROWS)], write_sem[b]))
            for w in writes:
                w.wait()

    return pl.kernel(body, out_type=jax.ShapeDtypeStruct((n_rows, d), table.dtype), mesh=mesh,
                     scratch_types=scratch, name="sc_gather")(table, idx)


def _sc_scatter(src, idx, n_out):
    n, d = src.shape
    info = plsc.get_sparse_core_info()
    n_workers = info.num_cores * info.num_subcores
    rows_per_worker = n // n_workers
    n_chunks = rows_per_worker // GATHER_ROWS
    assert rows_per_worker * n_workers == n and n_chunks * GATHER_ROWS == rows_per_worker
    assert n_chunks % GATHER_BUFS == 0
    mesh = plsc.VectorSubcoreMesh(core_axis_name="c", subcore_axis_name="s")
    scratch = ([pltpu.VMEM((GATHER_ROWS,), jnp.int32)] * (GATHER_BUFS * TOP_K)
               + [pltpu.VMEM((GATHER_ROWS, d), src.dtype)] * GATHER_BUFS + [pltpu.SemaphoreType.DMA] * GATHER_BUFS)

    def body(src_hbm, idx_hbm, out_hbm, *scr):
        idx_v = scr[:GATHER_BUFS * TOP_K]
        rows_v = scr[GATHER_BUFS * TOP_K:GATHER_BUFS * TOP_K + GATHER_BUFS]
        sem = scr[GATHER_BUFS * TOP_K + GATHER_BUFS:]
        worker = lax.axis_index("s") * info.num_cores + lax.axis_index("c")
        base = worker * rows_per_worker

        @pl.loop(0, n_chunks, step=GATHER_BUFS)
        def _(c0):
            copies = []
            for b in range(GATHER_BUFS):
                off = pl.multiple_of(base + (c0 + b) * GATHER_ROWS, SUBLANES)
                pltpu.sync_copy(src_hbm.at[pl.ds(off, GATHER_ROWS)], rows_v[b])
                for kk in range(TOP_K):
                    pltpu.sync_copy(idx_hbm.at[pl.ds(pl.multiple_of(kk * n + off, SUBLANES), GATHER_ROWS)],
                                    idx_v[b * TOP_K + kk])
                for kk in range(TOP_K):
                    copies.append(pltpu.async_copy(rows_v[b], out_hbm.at[idx_v[b * TOP_K + kk]], sem[b]))
            for c in copies:
                c.wait()

    return pl.kernel(body, out_type=jax.ShapeDtypeStruct((n_out, d), src.dtype), mesh=mesh,
                     scratch_types=scratch, name="sc_scatter")(src, idx)


PAIR_BLOCK = 2 * LANES


def _expert_kernel(te_ref, tv_ref, nt_ref, nx_ref, xs_ref, w1_hbm, b1_ref, w2_hbm, b2_ref, ys_ref,
                   w1p_ref, w2b_ref, w1_buf, w2_buf, sem, slot_ref, *, layer):
    i = pl.program_id(0)
    valid = i < nt_ref[0]
    new_expert = jnp.logical_or(i == 0, te_ref[i] != te_ref[jnp.maximum(i - 1, 0)])

    def weight_copies(expert, slot):
        return (pltpu.make_async_copy(w1_hbm.at[layer, expert], w1_buf.at[slot], sem.at[slot, 0]),
                pltpu.make_async_copy(w2_hbm.at[layer, expert], w2_buf.at[slot], sem.at[slot, 1]))

    @pl.when(i == 0)
    def _():
        slot_ref[0] = 0
        for cp in weight_copies(te_ref[0], 0):
            cp.start()

    @pl.when(jnp.logical_and(valid, new_expert))
    def _():
        slot = slot_ref[0]
        for cp in weight_copies(te_ref[i], slot):
            cp.wait()

        @pl.when(nx_ref[i] >= 0)
        def _():
            for cp in weight_copies(nx_ref[i], 1 - slot):
                cp.start()

        slot_ref[0] = 1 - slot
        r = lax.broadcasted_iota(jnp.int32, (PAIR_BLOCK, PAIR_BLOCK), 0)
        c = lax.broadcasted_iota(jnp.int32, (PAIR_BLOCK, PAIR_BLOCK), 1)
        src = jnp.where(c < LANES, 2 * c, 2 * (c - LANES) + 1)
        perm = jnp.where(r == src, 1.0, 0.0).astype(BF16)
        for b in range(2 * D_FF // PAIR_BLOCK):
            cols = slice(b * PAIR_BLOCK, (b + 1) * PAIR_BLOCK)
            w1p_ref[:, cols] = _bf(_mm(_bf(w1_buf[slot, :, cols]), perm))
        w2b_ref[...] = _bf(w2_buf[slot])

    @pl.when(valid)
    def _():
        rowi = lax.broadcasted_iota(jnp.int32, (EXPERT_TILE, D_MODEL), 0)
        x = _bf(jnp.where(rowi < tv_ref[i], _unpack_halves(xs_ref[...]), 0.0))
        u = _mm(x, w1p_ref[...]) + b1_ref[0]
        parts = []
        for b in range(2 * D_FF // PAIR_BLOCK):
            glu = jnp.minimum(u[:, b * PAIR_BLOCK:b * PAIR_BLOCK + LANES], SWIGLU_LIMIT)
            lin = jnp.clip(u[:, b * PAIR_BLOCK + LANES:(b + 1) * PAIR_BLOCK], -SWIGLU_LIMIT, SWIGLU_LIMIT)
            parts.append(glu * _sigmoid(SWIGLU_ALPHA * glu) * (lin + 1.0))
        act = jnp.concatenate(parts, axis=1)
        ys_ref[...] = _pack_halves(_mm(_bf(act), w2b_ref[...]) + b2_ref[0])


def _experts(layer, tile_expert, tile_valid, n_tiles, next_expert, xs, w1, b1p, w2, b2):
    tile_map = lambda i, te, tv, nt, nx: (jnp.minimum(i, nt[0] - 1), 0)
    wmap = lambda i, te, tv, nt, nx: (te[i], 0, 0)
    grid_spec = pltpu.PrefetchScalarGridSpec(
        num_scalar_prefetch=4,
        grid=(N_SLOT_TILES,),
        in_specs=[pl.BlockSpec((EXPERT_TILE, PACKED_D), tile_map),
                  pl.BlockSpec(memory_space=pl.ANY), pl.BlockSpec((1, 1, 2 * D_FF), wmap),
                  pl.BlockSpec(memory_space=pl.ANY), pl.BlockSpec((1, 1, D_MODEL), wmap)],
        out_specs=pl.BlockSpec((EXPERT_TILE, PACKED_D), tile_map),
        scratch_shapes=[pltpu.VMEM((D_MODEL, 2 * D_FF), BF16), pltpu.VMEM((D_FF, D_MODEL), BF16),
                        pltpu.VMEM((2, D_MODEL, 2 * D_FF), F32), pltpu.VMEM((2, D_FF, D_MODEL), F32),
                        pltpu.SemaphoreType.DMA((2, 2)), pltpu.SMEM((1,), jnp.int32)],
    )
    return pl.pallas_call(
        functools.partial(_expert_kernel, layer=layer),
        grid_spec=grid_spec,
        out_shape=jax.ShapeDtypeStruct((N_SLOTS, PACKED_D), F32),
        compiler_params=pltpu.CompilerParams(dimension_semantics=("arbitrary",), vmem_limit_bytes=EXPERT_VMEM_LIMIT),
        name="experts",
    )(tile_expert, tile_valid, n_tiles, next_expert, xs, w1, b1p, w2, b2)


def _combine_kernel(y_ref, route_ref, x1l_ref, x1c_ref, mod_ref, lg_ref, lb_ref, *x2_refs):
    route = route_ref[...]
    lane = lax.broadcasted_iota(jnp.int32, route.shape, 1)
    f = None
    for kk in range(TOP_K):
        term = _lane_pick(route, lane, ROUTE_WEIGHT + kk) * _unpack_halves(y_ref[kk])
        f = term if f is None else f + term
    z = ALPHA * _stream_rows(x1l_ref, x1c_ref) + mod_ref[0, 5:6, :] * f
    _stream_store(x2_refs, _layer_norm(z, lg_ref[...], lb_ref[...]))


def _combine(y_tok, route, x1l, x1c, mod, ln_g, ln_b, with_ctx):
    n_tiles = ALL_TILES if with_ctx else LAT_TILES
    return pl.pallas_call(
        _combine_kernel,
        grid=(n_tiles,),
        in_specs=[pl.BlockSpec((TOP_K, TILE, PACKED_D), lambda t: (0, t, 0)), _tok_spec(LANES)] + _stream_specs() + [
            _mod_spec(), _full_spec(ln_g.shape), _full_spec(ln_b.shape)],
        out_specs=_stream_specs()[:2 if with_ctx else 1],
        out_shape=_stream_shapes(with_ctx),
        compiler_params=_cparams("arbitrary"),
        name="combine",
    )(y_tok, route, x1l, x1c, mod, ln_g, ln_b)


def _moe(layer, h2, logits, x1l, x1c, mod, ln_g, ln_b, w1, b1p, w2, b2, with_ctx):
    n_tok = h2.shape[0]
    assert N_SLOT_TILES <= TILE
    route, slot_rows, tiles = _route(logits)
    slot_by_choice = slot_rows[TOP_K:2 * TOP_K].reshape(-1)
    tiles = tiles[:N_SLOT_TILES, :SUBLANES].astype(jnp.int32)
    tile_expert, tile_valid, n_tiles, next_expert = tiles[:, 0], tiles[:, 1], tiles[:1, 2], tiles[:, 3]
    xs = _sc_scatter(h2, slot_by_choice, N_SLOTS)
    ys = _experts(layer, tile_expert, tile_valid, n_tiles, next_expert, xs, w1, b1p, w2, b2)
    y_tok = _sc_gather(ys, slot_by_choice).reshape(TOP_K, n_tok, PACKED_D)
    return _combine(y_tok, route, x1l, x1c, mod, ln_g, ln_b, with_ctx)


def _lat_to_col_major(lat):
    return lat.reshape(BATCH, SEQ // GRID_W, GRID_W, D_MODEL).transpose(0, 2, 1, 3).reshape(N_LAT, D_MODEL)


def _lat_to_row_major(lat):
    return lat.reshape(BATCH, GRID_W, SEQ // GRID_W, D_MODEL).transpose(0, 2, 1, 3).reshape(N_LAT, D_MODEL)


def kernel(x, c, ctx, c_ctx, ada_w, ada_b, ln_g, ln_b, dn_w_in, dn_conv_w, dn_a_log, dn_dt_bias, dn_norm_w, dn_w_out,
           hg_w_in, hg_lower_bound, hg_norm_w, hg_w_out, gla_w_in, gla_gk_w, gla_gk_b, gla_norm_w, gla_w_out,
           router_w, router_b, exp_w1, exp_b1, exp_w2, exp_b2):
    xl = x.reshape(N_LAT, D_MODEL)
    xc = ctx.reshape(N_CTX, D_MODEL)
    cc = jnp.zeros((MOD_ROWS, D_MODEL), F32).at[:BATCH].set(c).at[BATCH].set(c_ctx)
    mods = _ada_all(cc, ada_w, ada_b).reshape(DEPTH, MOD_ROWS, ADA_CHUNKS, D_MODEL)
    lb_soft = jax.nn.softmax(hg_lower_bound.astype(F32), axis=0)
    lower_bounds = jnp.cumsum(lb_soft, axis=0) - lb_soft[0]

    for i in range(DEPTH):
        last = i == DEPTH - 1
        kind, j = i % 3, i // 3
        mod = mods[i]
        col_major = i % 2 == 1
        if col_major:
            xl = _lat_to_col_major(xl)

        if kind == 0:
            nq = 2 * D_MODEL + 2 * D_MODEL
            w = dn_w_in[j]
            w_ba = jnp.zeros((D_MODEL, LANES), F32).at[:, :4 * DN_HEADS].set(w[:, nq:])
            a_neg = jnp.zeros((LANES,), F32).at[2 * DN_HEADS:4 * DN_HEADS].set(-jnp.exp(dn_a_log[j].astype(F32)).reshape(-1))
            dtb = jnp.zeros((LANES,), F32).at[2 * DN_HEADS:4 * DN_HEADS].set(dn_dt_bias[j].astype(F32).reshape(-1))
            nb = 4 * DN_HEADS
            p_qkv, gate, bgc, bgr = _dn_proj(xl, xc, mod, _bf(w), _bf(w_ba), _bf(w[:, nq:].T),
                                            a_neg.reshape(1, LANES), dtb.reshape(1, LANES),
                                            a_neg[:nb].reshape(nb, 1), dtb[:nb].reshape(nb, 1))
            cw = jnp.zeros((SUBLANES, DN_QKV), F32).at[:SHORT_CONV].set(dn_conv_w[j])
            q, k, v = _dn_conv(p_qkv, cw)
            o_f, o_b = _dn_scan(q, k, v, bgc, bgr)
            norm_w, w_out = dn_norm_w[j], dn_w_out[j]
        elif kind == 1:
            q, v, gate, k_f, g_f, k_b, g_b = _hg_proj(xl, xc, mod, _bf(hg_w_in[j]), lower_bounds[i].reshape(1, D_MODEL))
            o_f, o_b = _gla_scan(q, k_f, k_b, v, g_f, g_b)
            norm_w, w_out = hg_norm_w[j], hg_w_out[j]
        else:
            kd = GLA_HEADS * GLA_DK
            n_main = 2 * kd + 2 * D_MODEL
            w = gla_w_in[j]
            w_lr = jnp.zeros((D_MODEL, 2 * LANES), F32)
            w_lr = w_lr.at[:, :GLA_RANK].set(w[:, n_main:n_main + GLA_RANK])
            w_lr = w_lr.at[:, LANES:LANES + GLA_RANK].set(w[:, n_main + GLA_RANK:])
            gk_w = jnp.zeros((2, LANES, kd), F32).at[:, :GLA_RANK].set(gla_gk_w[j])
            q, k, v, gate, g_f, g_b = _gla_proj(xl, xc, mod, _bf(w), _bf(w_lr), _bf(gk_w),
                                                gla_gk_b[j].reshape(2, 1, kd))
            o_f, o_b = _gla_scan(q, k, k, v, g_f, g_b)
            norm_w, w_out = gla_norm_w[j], gla_w_out[j]

        with_ctx = not last
        rw = jnp.zeros((D_MODEL, LANES), F32).at[:, :N_EXPERTS].set(router_w[i])
        rb = jnp.zeros((1, LANES), F32).at[0, :N_EXPERTS].set(router_b[i])
        *x1, h2, logits = _post(o_f, o_b, gate, xl, xc, mod, norm_w.reshape(1, -1), _bf(w_out),
                                ln_g[i, 0].reshape(1, D_MODEL), ln_b[i, 0].reshape(1, D_MODEL), rw, rb, with_ctx)
        b1p = exp_b1[i].reshape(N_EXPERTS, 2 * D_FF // PAIR_BLOCK, LANES, 2).transpose(0, 1, 3, 2)
        x2 = _moe(i, h2, logits, x1[0], x1[1] if with_ctx else xc, mod,
                  ln_g[i, 1].reshape(1, D_MODEL), ln_b[i, 1].reshape(1, D_MODEL),
                  exp_w1, b1p.reshape(N_EXPERTS, 1, 2 * D_FF), exp_w2, exp_b2[i][:, None, :], with_ctx)
        xl = _lat_to_row_major(x2[0]) if col_major else x2[0]
        if with_ctx:
            xc = x2[1]
    return xl.reshape(BATCH, SEQ, D_MODEL)
```

```python
import functools

import jax
import jax.numpy as jnp
from jax import lax
from jax.experimental import pallas as pl
from jax.experimental.pallas import tpu as pltpu
from jax.experimental.pallas import tpu_sc as plsc

F32 = jnp.float32
BF16 = jnp.bfloat16
HIGHEST = lax.Precision.HIGHEST

D_MODEL = 1024
BATCH = 8
SEQ = 2048
CTX_LEN = 256
DEPTH = 4
GRID_W = 64
CHUNK = 64
ADA_CHUNKS = 6
N_LAT = BATCH * SEQ
N_CTX = BATCH * CTX_LEN
N_TOK = N_LAT + N_CTX
TILE = 256
LAT_TILES = N_LAT // TILE
ALL_TILES = N_TOK // TILE
TILES_PER_SEQ = SEQ // TILE
MOD_ROWS = 16
LANES = 128
SUBLANES = 8

N_EXPERTS = 32
TOP_K = 4
D_FF = D_MODEL
SWIGLU_LIMIT = 7.0
SWIGLU_ALPHA = 1.702
EXPERT_TILE = 512
N_SLOT_TILES = N_TOK * TOP_K // EXPERT_TILE + N_EXPERTS
N_SLOTS = N_SLOT_TILES * EXPERT_TILE

LN_EPS = 1e-5
RMS_EPS = 1e-6
L2_EPS = 1e-6
ALPHA = (2.0 * DEPTH) ** 0.25

VMEM_LIMIT = 48 * 1024 * 1024
EXPERT_VMEM_LIMIT = (2 * 4 * (D_MODEL * 2 * D_FF + D_FF * D_MODEL) + 2 * (D_MODEL * 2 * D_FF + D_FF * D_MODEL)
                     + 4 * 4 * EXPERT_TILE * D_MODEL + 4 * EXPERT_TILE * (2 * D_FF + 2 * D_MODEL) + (4 << 20))


def _cparams(*sem):
    return pltpu.CompilerParams(dimension_semantics=sem, vmem_limit_bytes=VMEM_LIMIT)


def _sigmoid(x):
    return 1.0 / (1.0 + jnp.exp(-x))


def _silu(x):
    return x * _sigmoid(x)


def _softplus(x):
    return jnp.maximum(x, 0.0) + jnp.log(1.0 + jnp.exp(-jnp.abs(x)))


def _log_sigmoid(x):
    return -_softplus(-x)


def _nt(a, b):
    return lax.dot_general(a, b, (((1,), (1,)), ((), ())), preferred_element_type=F32)


def _tn(a, b):
    return lax.dot_general(a, b, (((0,), (0,)), ((), ())), preferred_element_type=F32)


def _mm(a, b):
    return jnp.dot(a, b, preferred_element_type=F32)


def _bf(x):
    return x.astype(BF16)


PACKED_D = D_MODEL // 2
HIGH_HALF = 0xFFFF0000


def _pack_halves(x):
    m = x.shape[1] // 2
    lo = pltpu.bitcast(x[:, :m].astype(BF16).astype(F32), jnp.uint32) >> 16
    hi = pltpu.bitcast(x[:, m:].astype(BF16).astype(F32), jnp.uint32) & jnp.uint32(HIGH_HALF)
    return pltpu.bitcast(hi | lo, F32)


def _unpack_halves(p):
    u = pltpu.bitcast(p, jnp.uint32)
    return jnp.concatenate([pltpu.bitcast(u << 16, F32), pltpu.bitcast(u & jnp.uint32(HIGH_HALF), F32)], axis=1)


def _mod_row(t):
    return jnp.minimum(t // TILES_PER_SEQ, BATCH)


ADA_NBLK = 1536


def _ada_kernel(c_ref, w_ref, b_ref, o_ref):
    s = _silu(c_ref[...])
    o_ref[0] = jnp.dot(s, w_ref[0], precision=HIGHEST, preferred_element_type=F32) + b_ref[0]


def _ada_all(cc, ada_w, ada_b):
    n = ADA_CHUNKS * D_MODEL
    return pl.pallas_call(
        _ada_kernel,
        grid=(DEPTH, n // ADA_NBLK),
        in_specs=[
            pl.BlockSpec((MOD_ROWS, D_MODEL), lambda i, j: (0, 0)),
            pl.BlockSpec((1, D_MODEL, ADA_NBLK), lambda i, j: (i, 0, j)),
            pl.BlockSpec((1, 1, ADA_NBLK), lambda i, j: (i, 0, j)),
        ],
        out_specs=pl.BlockSpec((1, MOD_ROWS, ADA_NBLK), lambda i, j: (i, 0, j)),
        out_shape=jax.ShapeDtypeStruct((DEPTH, MOD_ROWS, n), F32),
        compiler_params=_cparams("parallel", "parallel"),
        name="ada",
    )(cc, ada_w, ada_b.reshape(DEPTH, 1, n))


def _stream_rows(xl_ref, xc_ref):
    return jnp.where(pl.program_id(0) < LAT_TILES, xl_ref[...], xc_ref[...])


def _stream_specs():
    return [pl.BlockSpec((TILE, D_MODEL), lambda t: (jnp.minimum(t, LAT_TILES - 1), 0)),
            pl.BlockSpec((TILE, D_MODEL), lambda t: (jnp.maximum(t - LAT_TILES, 0), 0))]


def _stream_shapes(with_ctx):
    shapes = [jax.ShapeDtypeStruct((N_LAT, D_MODEL), F32)]
    return shapes + [jax.ShapeDtypeStruct((N_CTX, D_MODEL), F32)] if with_ctx else shapes


def _stream_store(refs, value):
    is_lat = pl.program_id(0) < LAT_TILES

    @pl.when(is_lat)
    def _():
        refs[0][...] = value

    if len(refs) > 1:
        @pl.when(jnp.logical_not(is_lat))
        def _():
            refs[1][...] = value


def _modulated(xl_ref, xc_ref, mod_ref, shift, scale):
    return _stream_rows(xl_ref, xc_ref) * (1.0 + mod_ref[0, scale:scale + 1, :]) + mod_ref[0, shift:shift + 1, :]


def _store_heads(ref, val, width):
    for h in range(val.shape[1] // width):
        ref[h] = val[:, h * width:(h + 1) * width].astype(ref.dtype)


def _hg_proj_kernel(xl_ref, xc_ref, mod_ref, w_ref, lb_ref, q_ref, v_ref, gate_ref, kf_ref, gf_ref, kb_ref, gb_ref):
    h = _bf(_modulated(xl_ref, xc_ref, mod_ref, 0, 1))
    d = D_MODEL
    lb = lb_ref[...]
    for k_ref, g_ref, lo in ((kf_ref, gf_ref, 3 * d), (kb_ref, gb_ref, 4 * d)):
        s = _sigmoid(_mm(h, w_ref[:, lo:lo + d]))
        _store_heads(k_ref, (1.0 - lb) * (1.0 - s), LANES)
        _store_heads(g_ref, jnp.log(lb + (1.0 - lb) * s), LANES)
    _store_heads(q_ref, _silu(_mm(h, w_ref[:, 0:d])), LANES)
    _store_heads(v_ref, _mm(h, w_ref[:, d:2 * d]), LANES)
    gate_ref[...] = _bf(_mm(h, w_ref[:, 2 * d:3 * d]))


def _head_spec(n_heads, width):
    return pl.BlockSpec((n_heads, TILE, width), lambda t: (0, t, 0))


def _head_shape(n_heads, width, dtype=None):
    return jax.ShapeDtypeStruct((n_heads, N_TOK, width), BF16 if dtype is None else dtype)


def _tok_shape(width):
    return jax.ShapeDtypeStruct((N_TOK, width), BF16)


def _tok_spec(width):
    return pl.BlockSpec((TILE, width), lambda t: (t, 0))


def _mod_spec():
    return pl.BlockSpec((1, ADA_CHUNKS, D_MODEL), lambda t: (_mod_row(t), 0, 0))


def _full_spec(shape):
    return pl.BlockSpec(shape, lambda t: (0,) * len(shape))


HG_HEADS = 8


def _hg_proj(xl, xc, mod, w_in, lb):
    n = w_in.shape[1]
    hs = _head_shape(HG_HEADS, LANES)
    hg = _head_shape(HG_HEADS, LANES, F32)
    spec = _head_spec(HG_HEADS, LANES)
    return pl.pallas_call(
        _hg_proj_kernel,
        grid=(ALL_TILES,),
        in_specs=_stream_specs() + [_mod_spec(), _full_spec((D_MODEL, n)), _full_spec((1, D_MODEL))],
        out_specs=[spec, spec, _tok_spec(D_MODEL), spec, spec, spec, spec],
        out_shape=[hs, hs, _tok_shape(D_MODEL), hs, hg, hs, hg],
        compiler_params=_cparams("parallel"),
        name="hg_proj",
    )(xl, xc, mod, w_in, lb)


GLA_HEADS = 4
GLA_DK = 128
GLA_DV = 256
GLA_RANK = 16
GLA_GATE_NORM = 16.0


def _gla_proj_kernel(xl_ref, xc_ref, mod_ref, w_ref, wlr_ref, gkw_ref, gkb_ref, q_ref, k_ref, v_ref, gate_ref, gf_ref, gb_ref):
    h = _bf(_modulated(xl_ref, xc_ref, mod_ref, 0, 1))
    kd = GLA_HEADS * GLA_DK
    vd = GLA_HEADS * GLA_DV
    for z, g_ref in ((0, gf_ref), (1, gb_ref)):
        lr = _mm(h, wlr_ref[:, z * LANES:(z + 1) * LANES])
        gk = _mm(_bf(lr), gkw_ref[z]) + gkb_ref[z]
        _store_heads(g_ref, _log_sigmoid(gk) / GLA_GATE_NORM, GLA_DK)
    _store_heads(q_ref, _mm(h, w_ref[:, 0:kd]) * GLA_DK ** -0.5, GLA_DK)
    _store_heads(k_ref, _mm(h, w_ref[:, kd:2 * kd]), GLA_DK)
    _store_heads(v_ref, _mm(h, w_ref[:, 2 * kd:2 * kd + vd]), GLA_DV)
    gate_ref[...] = _bf(_mm(h, w_ref[:, 2 * kd + vd:2 * kd + 2 * vd]))


def _gla_proj(xl, xc, mod, w_main, w_lr, gk_w, gk_b):
    kd = GLA_HEADS * GLA_DK
    hk = _head_shape(GLA_HEADS, GLA_DK)
    return pl.pallas_call(
        _gla_proj_kernel,
        grid=(ALL_TILES,),
        in_specs=_stream_specs() + [_mod_spec(), _full_spec(w_main.shape), _full_spec(w_lr.shape),
                  _full_spec(gk_w.shape), _full_spec(gk_b.shape)],
        out_specs=[_head_spec(GLA_HEADS, GLA_DK), _head_spec(GLA_HEADS, GLA_DK), _head_spec(GLA_HEADS, GLA_DV),
                   _tok_spec(D_MODEL), _head_spec(GLA_HEADS, GLA_DK), _head_spec(GLA_HEADS, GLA_DK)],
        out_shape=[hk, hk, _head_shape(GLA_HEADS, GLA_DV), _tok_shape(D_MODEL),
                   _head_shape(GLA_HEADS, GLA_DK, F32), _head_shape(GLA_HEADS, GLA_DK, F32)],
        compiler_params=_cparams("parallel"),
        name="gla_proj",
    )(xl, xc, mod, w_main, w_lr, gk_w, gk_b)


DN_HEADS = 8
DN_QKV = 3 * D_MODEL
SHORT_CONV = 5
CHUNKS_PER_TILE = TILE // CHUNK


def _seg_scan(x, pos, axis, reverse):
    n = x.shape[axis]
    sh = 1
    while sh < CHUNK:
        if reverse:
            x = x + jnp.where(pos < CHUNK - sh, pltpu.roll(x, n - sh, axis), 0.0)
        else:
            x = x + jnp.where(pos >= sh, pltpu.roll(x, sh, axis), 0.0)
        sh *= 2
    return x


def _dn_gates(p, a_neg, dtb, idx, pos, axis):
    beta = _sigmoid(p)
    g = a_neg * _softplus(p + dtb)
    gp = _seg_scan(g, pos, axis, False)
    gs = _seg_scan(g, pos, axis, True)
    return jnp.where(idx < 2 * DN_HEADS, beta, jnp.where(idx < 3 * DN_HEADS, gp, gs))


def _dn_proj_kernel(xl_ref, xc_ref, mod_ref, w_ref, wba_ref, wbat_ref, ac_ref, dc_ref, ar_ref, dr_ref,
                    qkv_ref, gate_ref, bgc_ref, bgr_ref):
    h = _bf(_modulated(xl_ref, xc_ref, mod_ref, 0, 1))
    pc = _mm(h, wba_ref[...])
    lane = lax.broadcasted_iota(jnp.int32, pc.shape, 1)
    rowpos = lax.broadcasted_iota(jnp.int32, pc.shape, 0) % CHUNK
    bgc_ref[...] = _dn_gates(pc, ac_ref[...], dc_ref[...], lane, rowpos, 0)
    pr = _nt(wbat_ref[...], h)
    sub = lax.broadcasted_iota(jnp.int32, pr.shape, 0)
    lanepos = lax.broadcasted_iota(jnp.int32, pr.shape, 1) % CHUNK
    gr = _dn_gates(pr, ar_ref[...], dr_ref[...], sub, lanepos, 1)
    for c in range(CHUNKS_PER_TILE):
        bgr_ref[c] = gr[:, c * CHUNK:(c + 1) * CHUNK]
    qkv_ref[...] = _bf(_mm(h, w_ref[:, 0:DN_QKV]))
    gate_ref[...] = _bf(_mm(h, w_ref[:, DN_QKV:DN_QKV + D_MODEL]))


def _dn_proj(xl, xc, mod, w_main, w_ba, w_bat, a_col, d_col, a_row, d_row):
    nb = 4 * DN_HEADS
    return pl.pallas_call(
        _dn_proj_kernel,
        grid=(ALL_TILES,),
        in_specs=_stream_specs() + [_mod_spec(), _full_spec(w_main.shape), _full_spec(w_ba.shape),
                  _full_spec(w_bat.shape), _full_spec(a_col.shape), _full_spec(d_col.shape),
                  _full_spec(a_row.shape), _full_spec(d_row.shape)],
        out_specs=[_tok_spec(DN_QKV), _tok_spec(D_MODEL), _tok_spec(LANES),
                   pl.BlockSpec((CHUNKS_PER_TILE, nb, CHUNK), lambda t: (t, 0, 0))],
        out_shape=[_tok_shape(DN_QKV), _tok_shape(D_MODEL), jax.ShapeDtypeStruct((N_TOK, LANES), F32),
                   jax.ShapeDtypeStruct((N_TOK // CHUNK, nb, CHUNK), F32)],
        compiler_params=_cparams("parallel"),
        name="dn_proj",
    )(xl, xc, mod, w_main, w_ba, w_bat, a_col, d_col, a_row, d_row)


HALO = 2 * SUBLANES


def _dn_conv_kernel(pm_ref, pp_ref, pn_ref, cw_ref, q_ref, k_ref, v_ref):
    t = pl.program_id(0)
    is_lat = t < LAT_TILES
    first = jnp.logical_or(jnp.logical_not(is_lat), t % TILES_PER_SEQ == 0)
    last = jnp.logical_or(jnp.logical_not(is_lat), t % TILES_PER_SEQ == TILES_PER_SEQ - 1)
    keep_prev = jnp.where(first, 0.0, 1.0)
    keep_next = jnp.where(last, 0.0, 1.0)
    n_ext = TILE + 2 * HALO
    pad = SHORT_CONV // 2
    for s in range(DN_QKV // LANES):
        cols = slice(s * LANES, (s + 1) * LANES)
        ext = jnp.concatenate([pp_ref[:, cols].astype(F32) * keep_prev, pm_ref[:, cols].astype(F32),
                               pn_ref[:, cols].astype(F32) * keep_next], axis=0)
        acc = None
        for kk in range(SHORT_CONV):
            off = HALO - pad + kk
            tap = pltpu.roll(ext, n_ext - off, 0)[0:TILE] * cw_ref[kk:kk + 1, cols]
            acc = tap if acc is None else acc + tap
        a = _silu(acc)
        head = s % DN_HEADS
        if s < 2 * DN_HEADS:
            a = a * lax.rsqrt(jnp.sum(a * a, axis=1, keepdims=True) + L2_EPS)
            if s < DN_HEADS:
                q_ref[head] = _bf(a * LANES ** -0.5)
            else:
                k_ref[head] = _bf(a)
        else:
            v_ref[head] = _bf(a)


def _dn_conv(p_qkv, conv_w):
    rows8 = TILE // HALO
    last8 = N_TOK // HALO - 1
    hs = _head_shape(DN_HEADS, LANES)
    return pl.pallas_call(
        _dn_conv_kernel,
        grid=(ALL_TILES,),
        in_specs=[_tok_spec(DN_QKV),
                  pl.BlockSpec((HALO, DN_QKV), lambda t: (jnp.maximum(t * rows8 - 1, 0), 0)),
                  pl.BlockSpec((HALO, DN_QKV), lambda t: (jnp.minimum((t + 1) * rows8, last8), 0)),
                  _full_spec(conv_w.shape)],
        out_specs=[_head_spec(DN_HEADS, LANES)] * 3,
        out_shape=[hs, hs, hs],
        compiler_params=_cparams("parallel"),
        name="dn_conv",
    )(p_qkv, p_qkv, p_qkv, conv_w)


LOG2_E = 1.4426950408889634
SIGN_BIT = 0x80000000


def _chunk_cumsum(x, ones_tri):
    hi = _bf(x)
    rest = x - hi.astype(F32)
    mid = _bf(rest)
    lo = _bf(rest - mid.astype(F32))
    return _mm(ones_tri, hi) + _mm(ones_tri, mid) + _mm(ones_tri, lo)


def _neg_abs(x):
    return pltpu.bitcast(pltpu.bitcast(x, jnp.uint32) | jnp.uint32(SIGN_BIT), F32)


def _gla_chunks(q, k, v, g, st, rev, row, r64, c64):
    n = range(len(q))
    q32 = [q[i].astype(F32) for i in n]
    k32 = [k[i].astype(F32) for i in n]
    ones_tri = {False: jnp.where(r64 >= c64, 1.0, 0.0).astype(BF16), True: jnp.where(r64 <= c64, 1.0, 0.0).astype(BF16)}
    cum = [_chunk_cumsum(g[i] * LOG2_E, ones_tri[rev[i]]) for i in n]
    g_end = [cum[i][0:1, :] if rev[i] else cum[i][CHUNK - 1:CHUNK, :] for i in n]
    att = [jnp.where(r64 == c64, _nt(_bf(q[i]), _bf(k[i])), 0.0) for i in n]
    edge = list(cum)
    s = 1
    while s < CHUNK:
        late = (row & s) != 0
        early = jnp.logical_not(late)
        rlate = (r64 & s) != 0
        clate = (c64 & s) != 0
        pair = {False: jnp.logical_and(rlate, jnp.logical_not(clate)), True: jnp.logical_and(clate, jnp.logical_not(rlate))}
        if 2 * s < CHUNK:
            shift = (2 * s).bit_length() - 1
            same = (r64 >> shift) == (c64 >> shift)
            pair = {d: jnp.logical_and(p, same) for d, p in pair.items()}
        for i in n:
            if rev[i]:
                cut = jnp.where(late, edge[i], pltpu.roll(edge[i], CHUNK - s, 0))
            else:
                cut = jnp.where(late, pltpu.roll(edge[i], s, 0), edge[i])
            e = jnp.exp2(_neg_abs(cum[i] - cut))
            q_side = early if rev[i] else late
            x16 = _bf(jnp.where(q_side, q32[i], k32[i]) * e)
            att[i] = att[i] + jnp.where(pair[rev[i]], _nt(x16, x16), 0.0)
            if rev[i]:
                edge[i] = jnp.where(late, pltpu.roll(edge[i], s, 0), edge[i])
            else:
                edge[i] = jnp.where(late, edge[i], pltpu.roll(edge[i], CHUNK - s, 0))
        s *= 2
    st16 = [_bf(st[i]) for i in n]
    o = [_mm(_bf(att[i]), _bf(v[i])) + _nt(_bf(q32[i] * jnp.exp2(cum[i])), st16[i]) for i in n]
    st_new = [st[i] * jnp.exp2(g_end[i]) + _tn(_bf(v[i]), _bf(k32[i] * jnp.exp2(g_end[i] - cum[i]))) for i in n]
    return o, st_new


def _scan_iotas():
    row = lax.broadcasted_iota(jnp.int32, (CHUNK, LANES), 0)
    r64 = lax.broadcasted_iota(jnp.int32, (CHUNK, CHUNK), 0)
    c64 = lax.broadcasted_iota(jnp.int32, (CHUNK, CHUNK), 1)
    return row, r64, c64


SCAN_BATCH = 2


def _scan_chains(in_refs, per_chain, n_heads):
    kinds = [[] for _ in range(per_chain)]
    rev = []
    for d in range(2):
        for bi in range(SCAN_BATCH):
            group = in_refs[(d * SCAN_BATCH + bi) * per_chain:(d * SCAN_BATCH + bi + 1) * per_chain]
            for h in range(n_heads):
                for kind, ref in zip(kinds, group):
                    kind.append((ref, h))
                rev.append(d == 1)
    return kinds, rev


def _scan_store(o, st, of_ref, ob_ref, st_ref, n_heads):
    i = 0
    for d, o_ref in ((0, of_ref), (1, ob_ref)):
        for bi in range(SCAN_BATCH):
            for h in range(n_heads):
                o_ref[h, bi * CHUNK:(bi + 1) * CHUNK, :] = _bf(o[i])
                st_ref[d, bi, h] = st[i]
                i += 1


def _gla_scan_kernel(*refs, n_heads):
    n_in = 2 * SCAN_BATCH * 4
    of_ref, ob_ref, st_ref = refs[n_in:]

    @pl.when(pl.program_id(1) == 0)
    def _():
        st_ref[...] = jnp.zeros_like(st_ref)

    row, r64, c64 = _scan_iotas()
    (q, k, v, g), rev = _scan_chains(refs[:n_in], 4, n_heads)
    load = lambda items: [ref[h] for ref, h in items]
    st = [st_ref[d, bi, h] for d in range(2) for bi in range(SCAN_BATCH) for h in range(n_heads)]
    o, st = _gla_chunks(load(q), load(k), load(v), load(g), st, rev, row, r64, c64)
    _scan_store(o, st, of_ref, ob_ref, st_ref, n_heads)


CTX_CHUNKS = CTX_LEN // CHUNK
LAT_CHUNKS = SEQ // CHUNK
SCAN_STEPS = CTX_CHUNKS + LAT_CHUNKS


def _fwd_chunk(b, n):
    return jnp.where(n < CTX_CHUNKS, N_LAT // CHUNK + b * CTX_CHUNKS + n, b * LAT_CHUNKS + n - CTX_CHUNKS)


def _bwd_chunk(b, n):
    return jnp.where(n < CTX_CHUNKS, N_LAT // CHUNK + b * CTX_CHUNKS + CTX_CHUNKS - 1 - n,
                     b * LAT_CHUNKS + SCAN_STEPS - 1 - n)


def _scan_in_maps(rank):
    def make(chunk_of, bi):
        def index_map(p, n):
            c = chunk_of(p * SCAN_BATCH + bi, n)
            return {1: (c, 0, 0), 2: (0, c, 0), 3: (c, 0)}[rank]
        return index_map
    return [make(chunk_of, bi) for chunk_of in (_fwd_chunk, _bwd_chunk) for bi in range(SCAN_BATCH)]


def _scan_out_pos(step, reverse):
    if not reverse:
        return step
    return jnp.where(step < CTX_CHUNKS, CTX_CHUNKS - 1 - step, CTX_CHUNKS + SCAN_STEPS - 1 - step)


def _scan_out_specs(n_heads, dv):
    return [pl.BlockSpec((n_heads, SCAN_BATCH * CHUNK, dv),
                         lambda p, n, reverse=reverse: (0, p * SCAN_STEPS + _scan_out_pos(n, reverse), 0))
            for reverse in (False, True)]


def _scan_out_block(t, c):
    is_lat = t < LAT_TILES
    b = jnp.where(is_lat, t // TILES_PER_SEQ, t - LAT_TILES)
    pos = jnp.where(is_lat, CTX_CHUNKS + (t % TILES_PER_SEQ) * CHUNKS_PER_TILE + c, c)
    return ((b // SCAN_BATCH) * SCAN_STEPS + pos) * SCAN_BATCH + b % SCAN_BATCH


def _gla_scan(q, k_f, k_b, v, g_f, g_b):
    n_heads, _, dk = q.shape
    dv = v.shape[2]
    maps = _scan_in_maps(2)
    kspec = lambda m: pl.BlockSpec((n_heads, CHUNK, dk), m)
    vspec = lambda m: pl.BlockSpec((n_heads, CHUNK, dv), m)
    in_specs, args = [], []
    for d, k_arr, g_arr in ((0, k_f, g_f), (1, k_b, g_b)):
        for bi in range(SCAN_BATCH):
            m = maps[d * SCAN_BATCH + bi]
            in_specs += [kspec(m), kspec(m), vspec(m), kspec(m)]
            args += [q, k_arr, v, g_arr]
    oshape = _head_shape(n_heads, dv)
    return pl.pallas_call(
        functools.partial(_gla_scan_kernel, n_heads=n_heads),
        grid=(BATCH // SCAN_BATCH, SCAN_STEPS),
        in_specs=in_specs,
        out_specs=_scan_out_specs(n_heads, dv),
        out_shape=[oshape, oshape],
        scratch_shapes=[pltpu.VMEM((2, SCAN_BATCH, n_heads, dv, dk), F32)],
        compiler_params=_cparams("parallel", "arbitrary"),
        name="gla_scan",
    )(*args)


def _lane_pick(x, lane, idx):
    return jnp.sum(jnp.where(lane == idx, x, 0.0), axis=1, keepdims=True)


def _dn_chunks(q, k, v, beta, g_col, g_row, st, rev, r64, c64):
    n = range(len(q))
    dk = k[0].shape[1]
    g_end = [g_row[i][:, 0:1] if rev[i] else g_row[i][:, CHUNK - 1:CHUNK] for i in n]
    causal = {False: r64 >= c64, True: r64 <= c64}
    gam = [jnp.where(causal[rev[i]], jnp.exp(jnp.minimum(g_col[i] - g_row[i], 0.0)), 0.0) for i in n]
    q32 = [q[i].astype(F32) for i in n]
    k32 = [k[i].astype(F32) for i in n]
    kb = [k32[i] * beta[i] for i in n]
    k16 = [_bf(k[i]) for i in n]
    m = [jnp.where(r64 == c64, 0.0, _nt(_bf(kb[i]), k16[i]) * gam[i]) for i in n]
    att = [_nt(_bf(q[i]), k16[i]) * gam[i] for i in n]
    y = None
    s = 1
    while s < CHUNK:
        shift = (2 * s).bit_length() - 1
        rlate = (r64 & s) != 0
        clate = (c64 & s) != 0
        pair = {False: jnp.logical_and(rlate, jnp.logical_not(clate)), True: jnp.logical_and(clate, jnp.logical_not(rlate))}
        if 2 * s < CHUNK:
            same = (r64 >> shift) == (c64 >> shift)
            pair = {d: jnp.logical_and(p, same) for d, p in pair.items()}
        c = [jnp.where(pair[rev[i]], m[i], 0.0) for i in n]
        if y is None:
            y = [-c[i] for i in n]
        else:
            c16 = [_bf(c[i]) for i in n]
            y16 = [_bf(y[i]) for i in n]
            p = [c[i] + _mm(y16[i], c16[i]) for i in n]
            y = [y[i] - (p[i] + _mm(_bf(p[i]), y16[i])) for i in n]
        s *= 2
    e_col = [jnp.exp(g_col[i]) for i in n]
    rhs = [jnp.concatenate([kb[i] * e_col[i], v[i].astype(F32) * beta[i]], axis=1) for i in n]
    wu = [rhs[i] + _mm(_bf(y[i]), _bf(rhs[i])) for i in n]
    st16 = [_bf(st[i]) for i in n]
    v_new = [wu[i][:, dk:] - _nt(_bf(wu[i][:, :dk]), st16[i]) for i in n]
    o = [_nt(_bf(q32[i] * e_col[i]), st16[i]) + _mm(_bf(att[i]), _bf(v_new[i])) for i in n]
    st_new = [st[i] * jnp.exp(g_end[i]) + _tn(_bf(v_new[i]), _bf(k32[i] * jnp.exp(g_end[i] - g_col[i]))) for i in n]
    return o, st_new


def _dn_scan_kernel(*refs):
    n_in = 2 * SCAN_BATCH * 5
    of_ref, ob_ref, st_ref = refs[n_in:]

    @pl.when(pl.program_id(1) == 0)
    def _():
        st_ref[...] = jnp.zeros_like(st_ref)

    _, r64, c64 = _scan_iotas()
    (q, k, v, gc, gr), rev = _scan_chains(refs[:n_in], 5, DN_HEADS)
    load = lambda items: [ref[h] for ref, h in items]
    tables = {id(ref): ref[...] for ref, _ in gc}
    beta = [tables[id(ref)][:, d * DN_HEADS + h:d * DN_HEADS + h + 1] for (ref, h), d in zip(gc, rev)]
    g_col = [tables[id(ref)][:, (2 + d) * DN_HEADS + h:(2 + d) * DN_HEADS + h + 1] for (ref, h), d in zip(gc, rev)]
    g_row = [ref[0, (2 + d) * DN_HEADS + h:(2 + d) * DN_HEADS + h + 1, :] for (ref, h), d in zip(gr, rev)]
    st = [st_ref[d, bi, h] for d in range(2) for bi in range(SCAN_BATCH) for h in range(DN_HEADS)]
    o, st = _dn_chunks(load(q), load(k), load(v), beta, g_col, g_row, st, rev, r64, c64)
    _scan_store(o, st, of_ref, ob_ref, st_ref, DN_HEADS)


def _dn_scan(q, k, v, bgc, bgr):
    head_maps, col_maps, row_maps = _scan_in_maps(2), _scan_in_maps(3), _scan_in_maps(1)
    in_specs, args = [], []
    for i in range(2 * SCAN_BATCH):
        in_specs += [pl.BlockSpec((DN_HEADS, CHUNK, LANES), head_maps[i])] * 3
        in_specs += [pl.BlockSpec((CHUNK, LANES), col_maps[i]), pl.BlockSpec((1, 4 * DN_HEADS, CHUNK), row_maps[i])]
        args += [q, k, v, bgc, bgr]
    oshape = _head_shape(DN_HEADS, LANES)
    return pl.pallas_call(
        _dn_scan_kernel,
        grid=(BATCH // SCAN_BATCH, SCAN_STEPS),
        in_specs=in_specs,
        out_specs=_scan_out_specs(DN_HEADS, LANES),
        out_shape=[oshape, oshape],
        scratch_shapes=[pltpu.VMEM((2, SCAN_BATCH, DN_HEADS, LANES, LANES), F32)],
        compiler_params=_cparams("parallel", "arbitrary"),
        name="dn_scan",
    )(*args)


def _layer_norm(z, g, b):
    mu = jnp.mean(z, axis=1, keepdims=True)
    zc = z - mu
    var = jnp.mean(zc * zc, axis=1, keepdims=True)
    return zc * lax.rsqrt(var + LN_EPS) * g + b


def _post_kernel(*refs, n_heads):
    o_refs, refs = refs[:2 * CHUNKS_PER_TILE], refs[2 * CHUNKS_PER_TILE:]
    gate_ref, xl_ref, xc_ref, mod_ref, nw_ref, wo_ref, lg_ref, lb_ref, rw_ref, rb_ref = refs[:10]
    out_refs = refs[10:]
    x1_refs, (h2_ref, logit_ref) = out_refs[:-2], out_refs[-2:]
    x_rows = _stream_rows(xl_ref, xc_ref)
    groups = [slice(g * TILE // POST_GROUPS, (g + 1) * TILE // POST_GROUPS) for g in range(POST_GROUPS)]
    chunks_per_group = CHUNKS_PER_TILE // POST_GROUPS
    on = []
    for g, rows in enumerate(groups):
        parts = []
        for h in range(n_heads):
            o = jnp.concatenate([o_refs[c][h].astype(F32) + o_refs[CHUNKS_PER_TILE + c][h].astype(F32)
                                 for c in range(g * chunks_per_group, (g + 1) * chunks_per_group)], axis=0)
            parts.append(o * lax.rsqrt(jnp.mean(o * o, axis=1, keepdims=True) + RMS_EPS) * nw_ref[...])
        on.append(jnp.concatenate(parts, axis=1) * _silu(gate_ref[rows, :].astype(F32)))
    y = [_mm(_bf(a), wo_ref[...]) for a in on]
    z = [ALPHA * x_rows[rows] + mod_ref[0, 2:3, :] * yg for rows, yg in zip(groups, y)]
    x1 = [_layer_norm(zg, lg_ref[...], lb_ref[...]) for zg in z]
    h2 = [a * (1.0 + mod_ref[0, 4:5, :]) + mod_ref[0, 3:4, :] for a in x1]
    logits = [jnp.dot(a, rw_ref[...], precision=HIGHEST, preferred_element_type=F32) + rb_ref[...] for a in h2]
    _stream_store(x1_refs, jnp.concatenate(x1, axis=0))
    h2_ref[...] = _pack_halves(jnp.concatenate(h2, axis=0))
    logit_ref[...] = jnp.concatenate(logits, axis=0)


POST_GROUPS = 2


def _post(o_f, o_b, gate, xl, xc, mod, norm_w, w_out, ln_g, ln_b, rw, rb, with_ctx):
    n_heads, _, dv = o_f.shape
    n_tiles = ALL_TILES if with_ctx else LAT_TILES
    n_rows = n_tiles * TILE
    tok = lambda w: jax.ShapeDtypeStruct((n_rows, w), F32)
    return pl.pallas_call(
        functools.partial(_post_kernel, n_heads=n_heads),
        grid=(n_tiles,),
        in_specs=[pl.BlockSpec((n_heads, CHUNK, dv), lambda t, c=c: (0, _scan_out_block(t, c), 0))
                  for _ in range(2) for c in range(CHUNKS_PER_TILE)] + [_tok_spec(D_MODEL)] + _stream_specs() + [
            _mod_spec(), _full_spec(norm_w.shape), _full_spec(w_out.shape), _full_spec(ln_g.shape),
            _full_spec(ln_b.shape), _full_spec(rw.shape), _full_spec(rb.shape)],
        out_specs=_stream_specs()[:2 if with_ctx else 1] + [_tok_spec(PACKED_D), _tok_spec(LANES)],
        out_shape=_stream_shapes(with_ctx) + [tok(PACKED_D), tok(LANES)],
        compiler_params=_cparams("arbitrary"),
        name="post_mixer",
    )(*([o_f] * CHUNKS_PER_TILE + [o_b] * CHUNKS_PER_TILE), gate, xl, xc, mod, norm_w, w_out, ln_g, ln_b, rw, rb)


def _route_kernel(logit_ref, route_ref, slot_ref, tiles_ref, run_ref, start_ref, table_ref):
    sweep = pl.program_id(0)
    t = pl.program_id(1)
    lane = lax.broadcasted_iota(jnp.int32, (TILE, LANES), 1)
    lanef = lane.astype(F32)

    def add_counts(onehot):
        new_run = run_ref[0:1, :] + jnp.sum(onehot, axis=0, keepdims=True)
        run_ref[...] = jnp.broadcast_to(new_run, run_ref.shape)
        return new_run

    @pl.when(t == 0)
    def _():
        run_ref[...] = jnp.zeros_like(run_ref)

    @pl.when(sweep == 0)
    def _():
        neg = jnp.float32(-jnp.inf)
        l = jnp.where(lane < N_EXPERTS, logit_ref[...], neg)
        vals, picks = [], []
        onehot = jnp.zeros((TILE, LANES), F32)
        for _ in range(TOP_K):
            m = jnp.max(l, axis=1, keepdims=True)
            pick = jnp.min(jnp.where(l == m, lanef, float(LANES)), axis=1, keepdims=True)
            hot = lanef == pick
            l = jnp.where(hot, neg, l)
            onehot = onehot + jnp.where(hot, 1.0, 0.0)
            vals.append(m)
            picks.append(pick)
        es = [jnp.exp(v - vals[0]) for v in vals]
        denom = es[0] + es[1] + es[2] + es[3]
        out = jnp.zeros((TILE, LANES), F32)
        for kk in range(TOP_K):
            out = jnp.where(lane == kk, picks[kk], out)
            out = jnp.where(lane == ROUTE_WEIGHT + kk, es[kk] / denom, out)
        table_ref[t] = out
        new_run = add_counts(onehot)

        @pl.when(t == pl.num_programs(1) - 1)
        def _():
            lane8 = lax.broadcasted_iota(jnp.int32, (SUBLANES, LANES), 1)
            size = jnp.ceil(jnp.broadcast_to(new_run, (SUBLANES, LANES)) * (1.0 / EXPERT_TILE)) * EXPERT_TILE
            size = jnp.where(lane8 < N_EXPERTS, size, 0.0)
            ends = size
            sh = 1
            while sh < N_EXPERTS:
                ends = ends + jnp.where(lane8 >= sh, pltpu.roll(ends, sh, 1), 0.0)
                sh *= 2
            start_ref[...] = ends - size
            tile_start = lax.broadcasted_iota(jnp.int32, (TILE, 1), 0).astype(F32) * EXPERT_TILE
            ends_t = jnp.broadcast_to(ends[0:1, :], (TILE, LANES))
            group_end_t = jnp.broadcast_to((ends - size)[0:1, :] + new_run, (TILE, LANES))
            expert = jnp.sum(jnp.where(jnp.logical_and(lane < N_EXPERTS, ends_t <= tile_start), 1.0, 0.0),
                             axis=1, keepdims=True)
            expert = jnp.minimum(expert, N_EXPERTS - 1.0)
            group_end = jnp.sum(jnp.where(lanef == expert, group_end_t, 0.0), axis=1, keepdims=True)
            valid = jnp.clip(group_end - tile_start, 0.0, float(EXPERT_TILE))
            total = jnp.max(ends_t, axis=1, keepdims=True)
            own_end = jnp.sum(jnp.where(lanef == expert, ends_t, 0.0), axis=1, keepdims=True)
            nxt = jnp.sum(jnp.where(jnp.logical_and(lane < N_EXPERTS, ends_t <= own_end), 1.0, 0.0), axis=1, keepdims=True)
            nxt = jnp.where(own_end < total, nxt, -1.0)
            tiles_ref[...] = jnp.where(lane == 0, expert, jnp.where(lane == 1, valid, jnp.where(
                lane == 2, total * (1.0 / EXPERT_TILE), jnp.where(lane == 3, nxt, 0.0))))

    @pl.when(sweep == 1)
    def _():
        out = table_ref[t]
        hots = [lanef == out[:, kk:kk + 1] for kk in range(TOP_K)]
        onehot = jnp.zeros((TILE, LANES), F32)
        for hot in hots:
            onehot = onehot + jnp.where(hot, 1.0, 0.0)
        r = lax.broadcasted_iota(jnp.int32, (TILE, TILE), 0)
        c = lax.broadcasted_iota(jnp.int32, (TILE, TILE), 1)
        before = jnp.where(c < r, 1.0, 0.0).astype(BF16)
        slot_of = _mm(before, _bf(onehot)) + run_ref[0:1, :] + start_ref[0:1, :]
        for kk in range(TOP_K):
            slot = jnp.sum(jnp.where(hots[kk], slot_of, 0.0), axis=1, keepdims=True)
            out = jnp.where(lane == ROUTE_SLOT + kk, slot, out)
        route_ref[...] = out
        slot_ref[...] = out.T[ROUTE_WEIGHT:ROUTE_WEIGHT + SUBLANES, :].astype(jnp.int32)
        add_counts(onehot)


ROUTE_WEIGHT = SUBLANES
ROUTE_SLOT = SUBLANES + TOP_K


def _route(logits):
    n_tok = logits.shape[0]
    n_tiles = n_tok // TILE
    return pl.pallas_call(
        _route_kernel,
        grid=(2, n_tiles),
        in_specs=[pl.BlockSpec((TILE, LANES), lambda s, t: (t, 0))],
        out_specs=[pl.BlockSpec((TILE, LANES), lambda s, t: (s * t, 0)),
                   pl.BlockSpec((SUBLANES, TILE), lambda s, t: (0, s * t)),
                   pl.BlockSpec((TILE, LANES), lambda s, t: (0, 0))],
        out_shape=[jax.ShapeDtypeStruct(logits.shape, F32), jax.ShapeDtypeStruct((SUBLANES, n_tok), jnp.int32),
                   jax.ShapeDtypeStruct((TILE, LANES), F32)],
        scratch_shapes=[pltpu.VMEM((SUBLANES, LANES), F32), pltpu.VMEM((SUBLANES, LANES), F32),
                        pltpu.VMEM((n_tiles, TILE, LANES), F32)],
        compiler_params=_cparams("arbitrary", "arbitrary"),
        name="route",
    )(logits)


GATHER_ROWS = 16
GATHER_BUFS = 4
SCATTER_BUFS = 2


def _sc_gather(table, idx):
    n_rows = idx.shape[0]
    d = table.shape[1]
    info = plsc.get_sparse_core_info()
    n_workers = info.num_cores * info.num_subcores
    rows_per_worker = n_rows // n_workers
    n_chunks = rows_per_worker // GATHER_ROWS
    assert rows_per_worker * n_workers == n_rows and n_chunks * GATHER_ROWS == rows_per_worker
    assert n_chunks % GATHER_BUFS == 0
    mesh = plsc.VectorSubcoreMesh(core_axis_name="c", subcore_axis_name="s")
    scratch = ([pltpu.VMEM((GATHER_ROWS,), jnp.int32)] * GATHER_BUFS
               + [pltpu.VMEM((GATHER_ROWS, d), table.dtype)] * GATHER_BUFS
               + [pltpu.SemaphoreType.DMA] * (2 * GATHER_BUFS))

    def body(table_hbm, idx_hbm, out_hbm, *scr):
        idx_v, rows_v = scr[:GATHER_BUFS], scr[GATHER_BUFS:2 * GATHER_BUFS]
        gather_sem, write_sem = scr[2 * GATHER_BUFS:3 * GATHER_BUFS], scr[3 * GATHER_BUFS:]
        worker = lax.axis_index("s") * info.num_cores + lax.axis_index("c")
        base = worker * rows_per_worker

        @pl.loop(0, n_chunks, step=GATHER_BUFS)
        def _(c0):
            offs = [pl.multiple_of(base + (c0 + b) * GATHER_ROWS, SUBLANES) for b in range(GATHER_BUFS)]
            gathers = []
            for b in range(GATHER_BUFS):
                pltpu.sync_copy(idx_hbm.at[pl.ds(offs[b], GATHER_ROWS)], idx_v[b])
                gathers.append(pltpu.async_copy(table_hbm.at[idx_v[b]], rows_v[b], gather_sem[b]))
            writes = []
            for b in range(GATHER_BUFS):
                gathers[b].wait()
                writes.append(pltpu.async_copy(rows_v[b], out_hbm.at[pl.ds(offs[b], GATHER_ROWS)], write_sem[b]))
            for w in writes:
                w.wait()

    return pl.kernel(body, out_type=jax.ShapeDtypeStruct((n_rows, d), table.dtype), mesh=mesh,
                     scratch_types=scratch, name="sc_gather")(table, idx)


def _sc_scatter(src, idx, n_out):
    n, d = src.shape
    info = plsc.get_sparse_core_info()
    n_workers = info.num_cores * info.num_subcores
    rows_per_worker = n // n_workers
    n_chunks = rows_per_worker // GATHER_ROWS
    assert rows_per_worker * n_workers == n and n_chunks * GATHER_ROWS == rows_per_worker
    assert n_chunks % SCATTER_BUFS == 0
    mesh = plsc.VectorSubcoreMesh(core_axis_name="c", subcore_axis_name="s")
    scratch = ([pltpu.VMEM((GATHER_ROWS,), jnp.int32)] * (SCATTER_BUFS * TOP_K)
               + [pltpu.VMEM((GATHER_ROWS, d), src.dtype)] * SCATTER_BUFS + [pltpu.SemaphoreType.DMA] * SCATTER_BUFS)

    def body(src_hbm, idx_hbm, out_hbm, *scr):
        idx_v = scr[:SCATTER_BUFS * TOP_K]
        rows_v = scr[SCATTER_BUFS * TOP_K:SCATTER_BUFS * TOP_K + SCATTER_BUFS]
        sem = scr[SCATTER_BUFS * TOP_K + SCATTER_BUFS:]
        worker = lax.axis_index("s") * info.num_cores + lax.axis_index("c")
        base = worker * rows_per_worker

        @pl.loop(0, n_chunks, step=SCATTER_BUFS)
        def _(c0):
            copies = []
            for b in range(SCATTER_BUFS):
                off = pl.multiple_of(base + (c0 + b) * GATHER_ROWS, SUBLANES)
                pltpu.sync_copy(src_hbm.at[pl.ds(off, GATHER_ROWS)], rows_v[b])
                for kk in range(TOP_K):
                    pltpu.sync_copy(idx_hbm.at[pl.ds(pl.multiple_of(kk * n + off, SUBLANES), GATHER_ROWS)],
                                    idx_v[b * TOP_K + kk])
                for kk in range(TOP_K):
                    copies.append(pltpu.async_copy(rows_v[b], out_hbm.at[idx_v[b * TOP_K + kk]], sem[b]))
            for c in copies:
                c.wait()

    return pl.kernel(body, out_type=jax.ShapeDtypeStruct((n_out, d), src.dtype), mesh=mesh,
                     scratch_types=scratch, name="sc_scatter")(src, idx)


PAIR_BLOCK = 2 * LANES


def _expert_kernel(te_ref, tv_ref, nt_ref, nx_ref, xs_ref, w1_hbm, b1_ref, w2_hbm, b2_ref, ys_ref,
                   w1p_ref, w2b_ref, w1_buf, w2_buf, sem, slot_ref, *, layer):
    i = pl.program_id(0)
    valid = i < nt_ref[0]
    new_expert = jnp.logical_or(i == 0, te_ref[i] != te_ref[jnp.maximum(i - 1, 0)])

    def weight_copies(expert, slot):
        return (pltpu.make_async_copy(w1_hbm.at[layer, expert], w1_buf.at[slot], sem.at[slot, 0]),
                pltpu.make_async_copy(w2_hbm.at[layer, expert], w2_buf.at[slot], sem.at[slot, 1]))

    @pl.when(i == 0)
    def _():
        slot_ref[0] = 0
        for cp in weight_copies(te_ref[0], 0):
            cp.start()

    @pl.when(jnp.logical_and(valid, new_expert))
    def _():
        slot = slot_ref[0]
        for cp in weight_copies(te_ref[i], slot):
            cp.wait()

        @pl.when(nx_ref[i] >= 0)
        def _():
            for cp in weight_copies(nx_ref[i], 1 - slot):
                cp.start()

        slot_ref[0] = 1 - slot
        r = lax.broadcasted_iota(jnp.int32, (PAIR_BLOCK, PAIR_BLOCK), 0)
        c = lax.broadcasted_iota(jnp.int32, (PAIR_BLOCK, PAIR_BLOCK), 1)
        src = jnp.where(c < LANES, 2 * c, 2 * (c - LANES) + 1)
        perm = jnp.where(r == src, 1.0, 0.0).astype(BF16)
        for b in range(2 * D_FF // PAIR_BLOCK):
            cols = slice(b * PAIR_BLOCK, (b + 1) * PAIR_BLOCK)
            w1p_ref[:, cols] = _bf(_mm(_bf(w1_buf[slot, :, cols]), perm))
        w2b_ref[...] = _bf(w2_buf[slot])

    @pl.when(valid)
    def _():
        rowi = lax.broadcasted_iota(jnp.int32, (EXPERT_TILE, D_MODEL), 0)
        x = _bf(jnp.where(rowi < tv_ref[i], _unpack_halves(xs_ref[...]), 0.0))
        u = _mm(x, w1p_ref[...]) + b1_ref[0]
        parts = []
        for b in range(2 * D_FF // PAIR_BLOCK):
            glu = jnp.minimum(u[:, b * PAIR_BLOCK:b * PAIR_BLOCK + LANES], SWIGLU_LIMIT)
            lin = jnp.clip(u[:, b * PAIR_BLOCK + LANES:(b + 1) * PAIR_BLOCK], -SWIGLU_LIMIT, SWIGLU_LIMIT)
            parts.append(glu * _sigmoid(SWIGLU_ALPHA * glu) * (lin + 1.0))
        act = jnp.concatenate(parts, axis=1)
        ys_ref[...] = _pack_halves(_mm(_bf(act), w2b_ref[...]) + b2_ref[0])


def _experts(layer, tile_expert, tile_valid, n_tiles, next_expert, xs, w1, b1p, w2, b2):
    tile_map = lambda i, te, tv, nt, nx: (jnp.minimum(i, nt[0] - 1), 0)
    wmap = lambda i, te, tv, nt, nx: (te[i], 0, 0)
    grid_spec = pltpu.PrefetchScalarGridSpec(
        num_scalar_prefetch=4,
        grid=(N_SLOT_TILES,),
        in_specs=[pl.BlockSpec((EXPERT_TILE, PACKED_D), tile_map),
                  pl.BlockSpec(memory_space=pl.ANY), pl.BlockSpec((1, 1, 2 * D_FF), wmap),
                  pl.BlockSpec(memory_space=pl.ANY), pl.BlockSpec((1, 1, D_MODEL), wmap)],
        out_specs=pl.BlockSpec((EXPERT_TILE, PACKED_D), tile_map),
        scratch_shapes=[pltpu.VMEM((D_MODEL, 2 * D_FF), BF16), pltpu.VMEM((D_FF, D_MODEL), BF16),
                        pltpu.VMEM((2, D_MODEL, 2 * D_FF), F32), pltpu.VMEM((2, D_FF, D_MODEL), F32),
                        pltpu.SemaphoreType.DMA((2, 2)), pltpu.SMEM((1,), jnp.int32)],
    )
    return pl.pallas_call(
        functools.partial(_expert_kernel, layer=layer),
        grid_spec=grid_spec,
        out_shape=jax.ShapeDtypeStruct((N_SLOTS, PACKED_D), F32),
        compiler_params=pltpu.CompilerParams(dimension_semantics=("arbitrary",), vmem_limit_bytes=EXPERT_VMEM_LIMIT),
        name="experts",
    )(tile_expert, tile_valid, n_tiles, next_expert, xs, w1, b1p, w2, b2)


def _combine_kernel(y_ref, route_ref, x1l_ref, x1c_ref, mod_ref, lg_ref, lb_ref, *x2_refs):
    route = route_ref[...]
    lane = lax.broadcasted_iota(jnp.int32, route.shape, 1)
    f = None
    for kk in range(TOP_K):
        term = _lane_pick(route, lane, ROUTE_WEIGHT + kk) * _unpack_halves(y_ref[kk])
        f = term if f is None else f + term
    z = ALPHA * _stream_rows(x1l_ref, x1c_ref) + mod_ref[0, 5:6, :] * f
    _stream_store(x2_refs, _layer_norm(z, lg_ref[...], lb_ref[...]))


def _combine(y_tok, route, x1l, x1c, mod, ln_g, ln_b, with_ctx):
    n_tiles = ALL_TILES if with_ctx else LAT_TILES
    return pl.pallas_call(
        _combine_kernel,
        grid=(n_tiles,),
        in_specs=[pl.BlockSpec((TOP_K, TILE, PACKED_D), lambda t: (0, t, 0)), _tok_spec(LANES)] + _stream_specs() + [
            _mod_spec(), _full_spec(ln_g.shape), _full_spec(ln_b.shape)],
        out_specs=_stream_specs()[:2 if with_ctx else 1],
        out_shape=_stream_shapes(with_ctx),
        compiler_params=_cparams("arbitrary"),
        name="combine",
    )(y_tok, route, x1l, x1c, mod, ln_g, ln_b)


def _moe(layer, h2, logits, x1l, x1c, mod, ln_g, ln_b, w1, b1p, w2, b2, with_ctx):
    n_tok = h2.shape[0]
    assert N_SLOT_TILES <= TILE
    route, slot_rows, tiles = _route(logits)
    slot_by_choice = slot_rows[TOP_K:2 * TOP_K].reshape(-1)
    tiles = tiles[:N_SLOT_TILES, :SUBLANES].astype(jnp.int32)
    tile_expert, tile_valid, n_tiles, next_expert = tiles[:, 0], tiles[:, 1], tiles[:1, 2], tiles[:, 3]
    xs = _sc_scatter(h2, slot_by_choice, N_SLOTS)
    ys = _experts(layer, tile_expert, tile_valid, n_tiles, next_expert, xs, w1, b1p, w2, b2)
    y_tok = _sc_gather(ys, slot_by_choice).reshape(TOP_K, n_tok, PACKED_D)
    return _combine(y_tok, route, x1l, x1c, mod, ln_g, ln_b, with_ctx)


def _lat_to_col_major(lat):
    return lat.reshape(BATCH, SEQ // GRID_W, GRID_W, D_MODEL).transpose(0, 2, 1, 3).reshape(N_LAT, D_MODEL)


def _lat_to_row_major(lat):
    return lat.reshape(BATCH, GRID_W, SEQ // GRID_W, D_MODEL).transpose(0, 2, 1, 3).reshape(N_LAT, D_MODEL)


def kernel(x, c, ctx, c_ctx, ada_w, ada_b, ln_g, ln_b, dn_w_in, dn_conv_w, dn_a_log, dn_dt_bias, dn_norm_w, dn_w_out,
           hg_w_in, hg_lower_bound, hg_norm_w, hg_w_out, gla_w_in, gla_gk_w, gla_gk_b, gla_norm_w, gla_w_out,
           router_w, router_b, exp_w1, exp_b1, exp_w2, exp_b2):
    xl = x.reshape(N_LAT, D_MODEL)
    xc = ctx.reshape(N_CTX, D_MODEL)
    cc = jnp.zeros((MOD_ROWS, D_MODEL), F32).at[:BATCH].set(c).at[BATCH].set(c_ctx)
    mods = _ada_all(cc, ada_w, ada_b).reshape(DEPTH, MOD_ROWS, ADA_CHUNKS, D_MODEL)
    lb_soft = jax.nn.softmax(hg_lower_bound.astype(F32), axis=0)
    lower_bounds = jnp.cumsum(lb_soft, axis=0) - lb_soft[0]

    for i in range(DEPTH):
        last = i == DEPTH - 1
        kind, j = i % 3, i // 3
        mod = mods[i]
        col_major = i % 2 == 1
        if col_major:
            xl = _lat_to_col_major(xl)

        if kind == 0:
            nq = 2 * D_MODEL + 2 * D_MODEL
            w = dn_w_in[j]
            w_ba = jnp.zeros((D_MODEL, LANES), F32).at[:, :4 * DN_HEADS].set(w[:, nq:])
            a_neg = jnp.zeros((LANES,), F32).at[2 * DN_HEADS:4 * DN_HEADS].set(-jnp.exp(dn_a_log[j].astype(F32)).reshape(-1))
            dtb = jnp.zeros((LANES,), F32).at[2 * DN_HEADS:4 * DN_HEADS].set(dn_dt_bias[j].astype(F32).reshape(-1))
            nb = 4 * DN_HEADS
            p_qkv, gate, bgc, bgr = _dn_proj(xl, xc, mod, _bf(w), _bf(w_ba), _bf(w[:, nq:].T),
                                            a_neg.reshape(1, LANES), dtb.reshape(1, LANES),
                                            a_neg[:nb].reshape(nb, 1), dtb[:nb].reshape(nb, 1))
            cw = jnp.zeros((SUBLANES, DN_QKV), F32).at[:SHORT_CONV].set(dn_conv_w[j])
            q, k, v = _dn_conv(p_qkv, cw)
            o_f, o_b = _dn_scan(q, k, v, bgc, bgr)
            norm_w, w_out = dn_norm_w[j], dn_w_out[j]
        elif kind == 1:
            q, v, gate, k_f, g_f, k_b, g_b = _hg_proj(xl, xc, mod, _bf(hg_w_in[j]), lower_bounds[i].reshape(1, D_MODEL))
            o_f, o_b = _gla_scan(q, k_f, k_b, v, g_f, g_b)
            norm_w, w_out = hg_norm_w[j], hg_w_out[j]
        else:
            kd = GLA_HEADS * GLA_DK
            n_main = 2 * kd + 2 * D_MODEL
            w = gla_w_in[j]
            w_lr = jnp.zeros((D_MODEL, 2 * LANES), F32)
            w_lr = w_lr.at[:, :GLA_RANK].set(w[:, n_main:n_main + GLA_RANK])
            w_lr = w_lr.at[:, LANES:LANES + GLA_RANK].set(w[:, n_main + GLA_RANK:])
            gk_w = jnp.zeros((2, LANES, kd), F32).at[:, :GLA_RANK].set(gla_gk_w[j])
            q, k, v, gate, g_f, g_b = _gla_proj(xl, xc, mod, _bf(w), _bf(w_lr), _bf(gk_w),
                                                gla_gk_b[j].reshape(2, 1, kd))
            o_f, o_b = _gla_scan(q, k, k, v, g_f, g_b)
            norm_w, w_out = gla_norm_w[j], gla_w_out[j]

        with_ctx = not last
        rw = jnp.zeros((D_MODEL, LANES), F32).at[:, :N_EXPERTS].set(router_w[i])
        rb = jnp.zeros((1, LANES), F32).at[0, :N_EXPERTS].set(router_b[i])
        *x1, h2, logits = _post(o_f, o_b, gate, xl, xc, mod, norm_w.reshape(1, -1), _bf(w_out),
                                ln_g[i, 0].reshape(1, D_MODEL), ln_b[i, 0].reshape(1, D_MODEL), rw, rb, with_ctx)
        b1p = exp_b1[i].reshape(N_EXPERTS, 2 * D_FF // PAIR_BLOCK, LANES, 2).transpose(0, 1, 3, 2)
        x2 = _moe(i, h2, logits, x1[0], x1[1] if with_ctx else xc, mod,
                  ln_g[i, 1].reshape(1, D_MODEL), ln_b[i, 1].reshape(1, D_MODEL),
                  exp_w1, b1p.reshape(N_EXPERTS, 1, 2 * D_FF), exp_w2, exp_b2[i][:, None, :], with_ctx)
        xl = _lat_to_row_major(x2[0]) if col_major else x2[0]
        if with_ctx:
            xc = x2[1]
    return xl.reshape(BATCH, SEQ, D_MODEL)
```

```python
import functools

import jax
import jax.numpy as jnp
from jax import lax
from jax.experimental import pallas as pl
from jax.experimental.pallas import tpu as pltpu
from jax.experimental.pallas import tpu_sc as plsc

F32 = jnp.float32
BF16 = jnp.bfloat16
HIGHEST = lax.Precision.HIGHEST

D_MODEL = 1024
BATCH = 8
SEQ = 2048
CTX_LEN = 256
DEPTH = 4
GRID_W = 64
CHUNK = 64
ADA_CHUNKS = 6
N_LAT = BATCH * SEQ
N_CTX = BATCH * CTX_LEN
N_TOK = N_LAT + N_CTX
TILE = 256
LAT_TILES = N_LAT // TILE
ALL_TILES = N_TOK // TILE
TILES_PER_SEQ = SEQ // TILE
MOD_ROWS = 16
LANES = 128
SUBLANES = 8

N_EXPERTS = 32
TOP_K = 4
D_FF = D_MODEL
SWIGLU_LIMIT = 7.0
SWIGLU_ALPHA = 1.702
EXPERT_TILE = 512
N_SLOT_TILES = N_TOK * TOP_K // EXPERT_TILE + N_EXPERTS
N_SLOTS = N_SLOT_TILES * EXPERT_TILE

LN_EPS = 1e-5
RMS_EPS = 1e-6
L2_EPS = 1e-6
ALPHA = (2.0 * DEPTH) ** 0.25

VMEM_LIMIT = 48 * 1024 * 1024
EXPERT_VMEM_LIMIT = (2 * 4 * (D_MODEL * 2 * D_FF + D_FF * D_MODEL) + 2 * (D_MODEL * 2 * D_FF + D_FF * D_MODEL)
                     + 4 * 4 * EXPERT_TILE * D_MODEL + 4 * EXPERT_TILE * (2 * D_FF + 2 * D_MODEL) + (4 << 20))


def _cparams(*sem):
    return pltpu.CompilerParams(dimension_semantics=sem, vmem_limit_bytes=VMEM_LIMIT)


def _sigmoid(x):
    return 1.0 / (1.0 + jnp.exp(-x))


def _silu(x):
    return x * _sigmoid(x)


def _softplus(x):
    return jnp.maximum(x, 0.0) + jnp.log(1.0 + jnp.exp(-jnp.abs(x)))


def _log_sigmoid(x):
    return -_softplus(-x)


def _nt(a, b):
    return lax.dot_general(a, b, (((1,), (1,)), ((), ())), preferred_element_type=F32)


def _tn(a, b):
    return lax.dot_general(a, b, (((0,), (0,)), ((), ())), preferred_element_type=F32)


def _mm(a, b):
    return jnp.dot(a, b, preferred_element_type=F32)


def _bf(x):
    return x.astype(BF16)


PACKED_D = D_MODEL // 2
HIGH_HALF = 0xFFFF0000


def _pack_halves(x):
    m = x.shape[1] // 2
    lo = pltpu.bitcast(x[:, :m].astype(BF16).astype(F32), jnp.uint32) >> 16
    hi = pltpu.bitcast(x[:, m:].astype(BF16).astype(F32), jnp.uint32) & jnp.uint32(HIGH_HALF)
    return pltpu.bitcast(hi | lo, F32)


def _unpack_halves(p):
    u = pltpu.bitcast(p, jnp.uint32)
    return jnp.concatenate([pltpu.bitcast(u << 16, F32), pltpu.bitcast(u & jnp.uint32(HIGH_HALF), F32)], axis=1)


def _mod_row(t):
    return jnp.minimum(t // TILES_PER_SEQ, BATCH)


ADA_NBLK = 1536


def _ada_kernel(c_ref, w_ref, b_ref, o_ref):
    s = _silu(c_ref[...])
    o_ref[0] = jnp.dot(s, w_ref[0], precision=HIGHEST, preferred_element_type=F32) + b_ref[0]


def _ada_all(cc, ada_w, ada_b):
    n = ADA_CHUNKS * D_MODEL
    return pl.pallas_call(
        _ada_kernel,
        grid=(DEPTH, n // ADA_NBLK),
        in_specs=[
            pl.BlockSpec((MOD_ROWS, D_MODEL), lambda i, j: (0, 0)),
            pl.BlockSpec((1, D_MODEL, ADA_NBLK), lambda i, j: (i, 0, j)),
            pl.BlockSpec((1, 1, ADA_NBLK), lambda i, j: (i, 0, j)),
        ],
        out_specs=pl.BlockSpec((1, MOD_ROWS, ADA_NBLK), lambda i, j: (i, 0, j)),
        out_shape=jax.ShapeDtypeStruct((DEPTH, MOD_ROWS, n), F32),
        compiler_params=_cparams("parallel", "parallel"),
        name="ada",
    )(cc, ada_w, ada_b.reshape(DEPTH, 1, n))


def _stream_rows(xl_ref, xc_ref):
    return jnp.where(pl.program_id(0) < LAT_TILES, xl_ref[...], xc_ref[...])


def _stream_specs():
    return [pl.BlockSpec((TILE, D_MODEL), lambda t: (jnp.minimum(t, LAT_TILES - 1), 0)),
            pl.BlockSpec((TILE, D_MODEL), lambda t: (jnp.maximum(t - LAT_TILES, 0), 0))]


def _stream_shapes(with_ctx):
    shapes = [jax.ShapeDtypeStruct((N_LAT, D_MODEL), F32)]
    return shapes + [jax.ShapeDtypeStruct((N_CTX, D_MODEL), F32)] if with_ctx else shapes


def _stream_store(refs, value):
    is_lat = pl.program_id(0) < LAT_TILES

    @pl.when(is_lat)
    def _():
        refs[0][...] = value

    if len(refs) > 1:
        @pl.when(jnp.logical_not(is_lat))
        def _():
            refs[1][...] = value


def _modulated(xl_ref, xc_ref, mod_ref, shift, scale):
    return _stream_rows(xl_ref, xc_ref) * (1.0 + mod_ref[0, scale:scale + 1, :]) + mod_ref[0, shift:shift + 1, :]


def _store_heads(ref, val, width):
    for h in range(val.shape[1] // width):
        ref[h] = val[:, h * width:(h + 1) * width].astype(ref.dtype)


def _hg_proj_kernel(xl_ref, xc_ref, mod_ref, w_ref, lb_ref, q_ref, v_ref, gate_ref, kf_ref, gf_ref, kb_ref, gb_ref):
    h = _bf(_modulated(xl_ref, xc_ref, mod_ref, 0, 1))
    d = D_MODEL
    lb = lb_ref[...]
    for k_ref, g_ref, lo in ((kf_ref, gf_ref, 3 * d), (kb_ref, gb_ref, 4 * d)):
        s = _sigmoid(_mm(h, w_ref[:, lo:lo + d]))
        _store_heads(k_ref, (1.0 - lb) * (1.0 - s), LANES)
        _store_heads(g_ref, jnp.log(lb + (1.0 - lb) * s), LANES)
    _store_heads(q_ref, _silu(_mm(h, w_ref[:, 0:d])), LANES)
    _store_heads(v_ref, _mm(h, w_ref[:, d:2 * d]), LANES)
    gate_ref[...] = _bf(_mm(h, w_ref[:, 2 * d:3 * d]))


def _head_spec(n_heads, width):
    return pl.BlockSpec((n_heads, TILE, width), lambda t: (0, t, 0))


def _head_shape(n_heads, width, dtype=None):
    return jax.ShapeDtypeStruct((n_heads, N_TOK, width), BF16 if dtype is None else dtype)


def _tok_shape(width):
    return jax.ShapeDtypeStruct((N_TOK, width), BF16)


def _tok_spec(width):
    return pl.BlockSpec((TILE, width), lambda t: (t, 0))


def _mod_spec():
    return pl.BlockSpec((1, ADA_CHUNKS, D_MODEL), lambda t: (_mod_row(t), 0, 0))


def _full_spec(shape):
    return pl.BlockSpec(shape, lambda t: (0,) * len(shape))


HG_HEADS = 8


def _hg_proj(xl, xc, mod, w_in, lb):
    n = w_in.shape[1]
    hs = _head_shape(HG_HEADS, LANES)
    hg = _head_shape(HG_HEADS, LANES, F32)
    spec = _head_spec(HG_HEADS, LANES)
    return pl.pallas_call(
        _hg_proj_kernel,
        grid=(ALL_TILES,),
        in_specs=_stream_specs() + [_mod_spec(), _full_spec((D_MODEL, n)), _full_spec((1, D_MODEL))],
        out_specs=[spec, spec, _tok_spec(D_MODEL), spec, spec, spec, spec],
        out_shape=[hs, hs, _tok_shape(D_MODEL), hs, hg, hs, hg],
        compiler_params=_cparams("parallel"),
        name="hg_proj",
    )(xl, xc, mod, w_in, lb)


GLA_HEADS = 4
GLA_DK = 128
GLA_DV = 256
GLA_RANK = 16
GLA_GATE_NORM = 16.0


def _gla_proj_kernel(xl_ref, xc_ref, mod_ref, w_ref, wlr_ref, gkw_ref, gkb_ref, q_ref, k_ref, v_ref, gate_ref, gf_ref, gb_ref):
    h = _bf(_modulated(xl_ref, xc_ref, mod_ref, 0, 1))
    kd = GLA_HEADS * GLA_DK
    vd = GLA_HEADS * GLA_DV
    for z, g_ref in ((0, gf_ref), (1, gb_ref)):
        lr = _mm(h, wlr_ref[:, z * LANES:(z + 1) * LANES])
        gk = _mm(_bf(lr), gkw_ref[z]) + gkb_ref[z]
        _store_heads(g_ref, _log_sigmoid(gk) / GLA_GATE_NORM, GLA_DK)
    _store_heads(q_ref, _mm(h, w_ref[:, 0:kd]) * GLA_DK ** -0.5, GLA_DK)
    _store_heads(k_ref, _mm(h, w_ref[:, kd:2 * kd]), GLA_DK)
    _store_heads(v_ref, _mm(h, w_ref[:, 2 * kd:2 * kd + vd]), GLA_DV)
    gate_ref[...] = _bf(_mm(h, w_ref[:, 2 * kd + vd:2 * kd + 2 * vd]))


def _gla_proj(xl, xc, mod, w_main, w_lr, gk_w, gk_b):
    kd = GLA_HEADS * GLA_DK
    hk = _head_shape(GLA_HEADS, GLA_DK)
    return pl.pallas_call(
        _gla_proj_kernel,
        grid=(ALL_TILES,),
        in_specs=_stream_specs() + [_mod_spec(), _full_spec(w_main.shape), _full_spec(w_lr.shape),
                  _full_spec(gk_w.shape), _full_spec(gk_b.shape)],
        out_specs=[_head_spec(GLA_HEADS, GLA_DK), _head_spec(GLA_HEADS, GLA_DK), _head_spec(GLA_HEADS, GLA_DV),
                   _tok_spec(D_MODEL), _head_spec(GLA_HEADS, GLA_DK), _head_spec(GLA_HEADS, GLA_DK)],
        out_shape=[hk, hk, _head_shape(GLA_HEADS, GLA_DV), _tok_shape(D_MODEL),
                   _head_shape(GLA_HEADS, GLA_DK, F32), _head_shape(GLA_HEADS, GLA_DK, F32)],
        compiler_params=_cparams("parallel"),
        name="gla_proj",
    )(xl, xc, mod, w_main, w_lr, gk_w, gk_b)


DN_HEADS = 8
DN_QKV = 3 * D_MODEL
SHORT_CONV = 5
CHUNKS_PER_TILE = TILE // CHUNK


def _seg_scan(x, pos, axis, reverse):
    n = x.shape[axis]
    sh = 1
    while sh < CHUNK:
        if reverse:
            x = x + jnp.where(pos < CHUNK - sh, pltpu.roll(x, n - sh, axis), 0.0)
        else:
            x = x + jnp.where(pos >= sh, pltpu.roll(x, sh, axis), 0.0)
        sh *= 2
    return x


def _dn_gates(p, a_neg, dtb, idx, pos, axis):
    beta = _sigmoid(p)
    g = a_neg * _softplus(p + dtb)
    gp = _seg_scan(g, pos, axis, False)
    gs = _seg_scan(g, pos, axis, True)
    return jnp.where(idx < 2 * DN_HEADS, beta, jnp.where(idx < 3 * DN_HEADS, gp, gs))


def _dn_proj_kernel(xl_ref, xc_ref, mod_ref, w_ref, wba_ref, wbat_ref, ac_ref, dc_ref, ar_ref, dr_ref,
                    qkv_ref, gate_ref, bgc_ref, bgr_ref):
    h = _bf(_modulated(xl_ref, xc_ref, mod_ref, 0, 1))
    pc = _mm(h, wba_ref[...])
    lane = lax.broadcasted_iota(jnp.int32, pc.shape, 1)
    rowpos = lax.broadcasted_iota(jnp.int32, pc.shape, 0) % CHUNK
    bgc_ref[...] = _dn_gates(pc, ac_ref[...], dc_ref[...], lane, rowpos, 0)
    pr = _nt(wbat_ref[...], h)
    sub = lax.broadcasted_iota(jnp.int32, pr.shape, 0)
    lanepos = lax.broadcasted_iota(jnp.int32, pr.shape, 1) % CHUNK
    gr = _dn_gates(pr, ar_ref[...], dr_ref[...], sub, lanepos, 1)
    for c in range(CHUNKS_PER_TILE):
        bgr_ref[c] = gr[:, c * CHUNK:(c + 1) * CHUNK]
    qkv_ref[...] = _bf(_mm(h, w_ref[:, 0:DN_QKV]))
    gate_ref[...] = _bf(_mm(h, w_ref[:, DN_QKV:DN_QKV + D_MODEL]))


def _dn_proj(xl, xc, mod, w_main, w_ba, w_bat, a_col, d_col, a_row, d_row):
    nb = 4 * DN_HEADS
    return pl.pallas_call(
        _dn_proj_kernel,
        grid=(ALL_TILES,),
        in_specs=_stream_specs() + [_mod_spec(), _full_spec(w_main.shape), _full_spec(w_ba.shape),
                  _full_spec(w_bat.shape), _full_spec(a_col.shape), _full_spec(d_col.shape),
                  _full_spec(a_row.shape), _full_spec(d_row.shape)],
        out_specs=[_tok_spec(DN_QKV), _tok_spec(D_MODEL), _tok_spec(LANES),
                   pl.BlockSpec((CHUNKS_PER_TILE, nb, CHUNK), lambda t: (t, 0, 0))],
        out_shape=[_tok_shape(DN_QKV), _tok_shape(D_MODEL), jax.ShapeDtypeStruct((N_TOK, LANES), F32),
                   jax.ShapeDtypeStruct((N_TOK // CHUNK, nb, CHUNK), F32)],
        compiler_params=_cparams("parallel"),
        name="dn_proj",
    )(xl, xc, mod, w_main, w_ba, w_bat, a_col, d_col, a_row, d_row)


HALO = 2 * SUBLANES


def _dn_conv_kernel(pm_ref, pp_ref, pn_ref, cw_ref, q_ref, k_ref, v_ref):
    t = pl.program_id(0)
    is_lat = t < LAT_TILES
    first = jnp.logical_or(jnp.logical_not(is_lat), t % TILES_PER_SEQ == 0)
    last = jnp.logical_or(jnp.logical_not(is_lat), t % TILES_PER_SEQ == TILES_PER_SEQ - 1)
    keep_prev = jnp.where(first, 0.0, 1.0)
    keep_next = jnp.where(last, 0.0, 1.0)
    n_ext = TILE + 2 * HALO
    pad = SHORT_CONV // 2
    for s in range(DN_QKV // LANES):
        cols = slice(s * LANES, (s + 1) * LANES)
        ext = jnp.concatenate([pp_ref[:, cols].astype(F32) * keep_prev, pm_ref[:, cols].astype(F32),
                               pn_ref[:, cols].astype(F32) * keep_next], axis=0)
        acc = None
        for kk in range(SHORT_CONV):
            off = HALO - pad + kk
            tap = pltpu.roll(ext, n_ext - off, 0)[0:TILE] * cw_ref[kk:kk + 1, cols]
            acc = tap if acc is None else acc + tap
        a = _silu(acc)
        head = s % DN_HEADS
        if s < 2 * DN_HEADS:
            a = a * lax.rsqrt(jnp.sum(a * a, axis=1, keepdims=True) + L2_EPS)
            if s < DN_HEADS:
                q_ref[head] = _bf(a * LANES ** -0.5)
            else:
                k_ref[head] = _bf(a)
        else:
            v_ref[head] = _bf(a)


def _dn_conv(p_qkv, conv_w):
    rows8 = TILE // HALO
    last8 = N_TOK // HALO - 1
    hs = _head_shape(DN_HEADS, LANES)
    return pl.pallas_call(
        _dn_conv_kernel,
        grid=(ALL_TILES,),
        in_specs=[_tok_spec(DN_QKV),
                  pl.BlockSpec((HALO, DN_QKV), lambda t: (jnp.maximum(t * rows8 - 1, 0), 0)),
                  pl.BlockSpec((HALO, DN_QKV), lambda t: (jnp.minimum((t + 1) * rows8, last8), 0)),
                  _full_spec(conv_w.shape)],
        out_specs=[_head_spec(DN_HEADS, LANES)] * 3,
        out_shape=[hs, hs, hs],
        compiler_params=_cparams("parallel"),
        name="dn_conv",
    )(p_qkv, p_qkv, p_qkv, conv_w)


LOG2_E = 1.4426950408889634
SIGN_BIT = 0x80000000


def _chunk_cumsum(x, ones_tri):
    hi = _bf(x)
    rest = x - hi.astype(F32)
    mid = _bf(rest)
    lo = _bf(rest - mid.astype(F32))
    return _mm(ones_tri, hi) + _mm(ones_tri, mid) + _mm(ones_tri, lo)


def _neg_abs(x):
    return pltpu.bitcast(pltpu.bitcast(x, jnp.uint32) | jnp.uint32(SIGN_BIT), F32)


def _gla_chunks(q, k, v, g, st, rev, row, r64, c64):
    n = range(len(q))
    q32 = [q[i].astype(F32) for i in n]
    k32 = [k[i].astype(F32) for i in n]
    ones_tri = {False: jnp.where(r64 >= c64, 1.0, 0.0).astype(BF16), True: jnp.where(r64 <= c64, 1.0, 0.0).astype(BF16)}
    cum = [_chunk_cumsum(g[i] * LOG2_E, ones_tri[rev[i]]) for i in n]
    g_end = [cum[i][0:1, :] if rev[i] else cum[i][CHUNK - 1:CHUNK, :] for i in n]
    att = [jnp.where(r64 == c64, _nt(_bf(q[i]), _bf(k[i])), 0.0) for i in n]
    edge = list(cum)
    s = 1
    while s < CHUNK:
        late = (row & s) != 0
        early = jnp.logical_not(late)
        rlate = (r64 & s) != 0
        clate = (c64 & s) != 0
        pair = {False: jnp.logical_and(rlate, jnp.logical_not(clate)), True: jnp.logical_and(clate, jnp.logical_not(rlate))}
        if 2 * s < CHUNK:
            shift = (2 * s).bit_length() - 1
            same = (r64 >> shift) == (c64 >> shift)
            pair = {d: jnp.logical_and(p, same) for d, p in pair.items()}
        for i in n:
            if rev[i]:
                cut = jnp.where(late, edge[i], pltpu.roll(edge[i], CHUNK - s, 0))
            else:
                cut = jnp.where(late, pltpu.roll(edge[i], s, 0), edge[i])
            e = jnp.exp2(_neg_abs(cum[i] - cut))
            q_side = early if rev[i] else late
            x16 = _bf(jnp.where(q_side, q32[i], k32[i]) * e)
            att[i] = att[i] + jnp.where(pair[rev[i]], _nt(x16, x16), 0.0)
            if rev[i]:
                edge[i] = jnp.where(late, pltpu.roll(edge[i], s, 0), edge[i])
            else:
                edge[i] = jnp.where(late, edge[i], pltpu.roll(edge[i], CHUNK - s, 0))
        s *= 2
    st16 = [_bf(st[i]) for i in n]
    o = [_mm(_bf(att[i]), _bf(v[i])) + _nt(_bf(q32[i] * jnp.exp2(cum[i])), st16[i]) for i in n]
    st_new = [st[i] * jnp.exp2(g_end[i]) + _tn(_bf(v[i]), _bf(k32[i] * jnp.exp2(g_end[i] - cum[i]))) for i in n]
    return o, st_new


def _scan_iotas():
    row = lax.broadcasted_iota(jnp.int32, (CHUNK, LANES), 0)
    r64 = lax.broadcasted_iota(jnp.int32, (CHUNK, CHUNK), 0)
    c64 = lax.broadcasted_iota(jnp.int32, (CHUNK, CHUNK), 1)
    return row, r64, c64


SCAN_BATCH = 2


def _scan_chains(in_refs, per_chain, n_heads):
    kinds = [[] for _ in range(per_chain)]
    rev = []
    for d in range(2):
        for bi in range(SCAN_BATCH):
            group = in_refs[(d * SCAN_BATCH + bi) * per_chain:(d * SCAN_BATCH + bi + 1) * per_chain]
            for h in range(n_heads):
                for kind, ref in zip(kinds, group):
                    kind.append((ref, h))
                rev.append(d == 1)
    return kinds, rev


def _scan_store(o, st, of_ref, ob_ref, st_ref, n_heads):
    i = 0
    for d, o_ref in ((0, of_ref), (1, ob_ref)):
        for bi in range(SCAN_BATCH):
            for h in range(n_heads):
                o_ref[h, bi * CHUNK:(bi + 1) * CHUNK, :] = _bf(o[i])
                st_ref[d, bi, h] = st[i]
                i += 1


def _gla_scan_kernel(*refs, n_heads):
    n_in = 2 * SCAN_BATCH * 4
    of_ref, ob_ref, st_ref = refs[n_in:]

    @pl.when(pl.program_id(1) == 0)
    def _():
        st_ref[...] = jnp.zeros_like(st_ref)

    row, r64, c64 = _scan_iotas()
    (q, k, v, g), rev = _scan_chains(refs[:n_in], 4, n_heads)
    load = lambda items: [ref[h] for ref, h in items]
    st = [st_ref[d, bi, h] for d in range(2) for bi in range(SCAN_BATCH) for h in range(n_heads)]
    o, st = _gla_chunks(load(q), load(k), load(v), load(g), st, rev, row, r64, c64)
    _scan_store(o, st, of_ref, ob_ref, st_ref, n_heads)


CTX_CHUNKS = CTX_LEN // CHUNK
LAT_CHUNKS = SEQ // CHUNK
SCAN_STEPS = CTX_CHUNKS + LAT_CHUNKS


def _fwd_chunk(b, n):
    return jnp.where(n < CTX_CHUNKS, N_LAT // CHUNK + b * CTX_CHUNKS + n, b * LAT_CHUNKS + n - CTX_CHUNKS)


def _bwd_chunk(b, n):
    return jnp.where(n < CTX_CHUNKS, N_LAT // CHUNK + b * CTX_CHUNKS + CTX_CHUNKS - 1 - n,
                     b * LAT_CHUNKS + SCAN_STEPS - 1 - n)


def _scan_in_maps(rank):
    def make(chunk_of, bi):
        def index_map(p, n):
            c = chunk_of(p * SCAN_BATCH + bi, n)
            return {1: (c, 0, 0), 2: (0, c, 0), 3: (c, 0)}[rank]
        return index_map
    return [make(chunk_of, bi) for chunk_of in (_fwd_chunk, _bwd_chunk) for bi in range(SCAN_BATCH)]


def _scan_out_pos(step, reverse):
    if not reverse:
        return step
    return jnp.where(step < CTX_CHUNKS, CTX_CHUNKS - 1 - step, CTX_CHUNKS + SCAN_STEPS - 1 - step)


def _scan_out_specs(n_heads, dv):
    return [pl.BlockSpec((n_heads, SCAN_BATCH * CHUNK, dv),
                         lambda p, n, reverse=reverse: (0, p * SCAN_STEPS + _scan_out_pos(n, reverse), 0))
            for reverse in (False, True)]


def _scan_out_block(t, c):
    is_lat = t < LAT_TILES
    b = jnp.where(is_lat, t // TILES_PER_SEQ, t - LAT_TILES)
    pos = jnp.where(is_lat, CTX_CHUNKS + (t % TILES_PER_SEQ) * CHUNKS_PER_TILE + c, c)
    return ((b // SCAN_BATCH) * SCAN_STEPS + pos) * SCAN_BATCH + b % SCAN_BATCH


def _gla_scan(q, k_f, k_b, v, g_f, g_b):
    n_heads, _, dk = q.shape
    dv = v.shape[2]
    maps = _scan_in_maps(2)
    kspec = lambda m: pl.BlockSpec((n_heads, CHUNK, dk), m)
    vspec = lambda m: pl.BlockSpec((n_heads, CHUNK, dv), m)
    in_specs, args = [], []
    for d, k_arr, g_arr in ((0, k_f, g_f), (1, k_b, g_b)):
        for bi in range(SCAN_BATCH):
            m = maps[d * SCAN_BATCH + bi]
            in_specs += [kspec(m), kspec(m), vspec(m), kspec(m)]
            args += [q, k_arr, v, g_arr]
    oshape = _head_shape(n_heads, dv)
    return pl.pallas_call(
        functools.partial(_gla_scan_kernel, n_heads=n_heads),
        grid=(BATCH // SCAN_BATCH, SCAN_STEPS),
        in_specs=in_specs,
        out_specs=_scan_out_specs(n_heads, dv),
        out_shape=[oshape, oshape],
        scratch_shapes=[pltpu.VMEM((2, SCAN_BATCH, n_heads, dv, dk), F32)],
        compiler_params=_cparams("parallel", "arbitrary"),
        name="gla_scan",
    )(*args)


def _lane_pick(x, lane, idx):
    return jnp.sum(jnp.where(lane == idx, x, 0.0), axis=1, keepdims=True)


def _dn_chunks(q, k, v, beta, g_col, g_row, st, rev, r64, c64):
    n = range(len(q))
    dk = k[0].shape[1]
    g_end = [g_row[i][:, 0:1] if rev[i] else g_row[i][:, CHUNK - 1:CHUNK] for i in n]
    causal = {False: r64 >= c64, True: r64 <= c64}
    gam = [jnp.where(causal[rev[i]], jnp.exp(jnp.minimum(g_col[i] - g_row[i], 0.0)), 0.0) for i in n]
    q32 = [q[i].astype(F32) for i in n]
    k32 = [k[i].astype(F32) for i in n]
    kb = [k32[i] * beta[i] for i in n]
    k16 = [_bf(k[i]) for i in n]
    m = [jnp.where(r64 == c64, 0.0, _nt(_bf(kb[i]), k16[i]) * gam[i]) for i in n]
    att = [_nt(_bf(q[i]), k16[i]) * gam[i] for i in n]
    y = None
    s = 1
    while s < CHUNK:
        shift = (2 * s).bit_length() - 1
        rlate = (r64 & s) != 0
        clate = (c64 & s) != 0
        pair = {False: jnp.logical_and(rlate, jnp.logical_not(clate)), True: jnp.logical_and(clate, jnp.logical_not(rlate))}
        if 2 * s < CHUNK:
            same = (r64 >> shift) == (c64 >> shift)
            pair = {d: jnp.logical_and(p, same) for d, p in pair.items()}
        c = [jnp.where(pair[rev[i]], m[i], 0.0) for i in n]
        if y is None:
            y = [-c[i] for i in n]
        else:
            c16 = [_bf(c[i]) for i in n]
            y16 = [_bf(y[i]) for i in n]
            p = [c[i] + _mm(y16[i], c16[i]) for i in n]
            y = [y[i] - (p[i] + _mm(_bf(p[i]), y16[i])) for i in n]
        s *= 2
    e_col = [jnp.exp(g_col[i]) for i in n]
    rhs = [jnp.concatenate([kb[i] * e_col[i], v[i].astype(F32) * beta[i]], axis=1) for i in n]
    wu = [rhs[i] + _mm(_bf(y[i]), _bf(rhs[i])) for i in n]
    st16 = [_bf(st[i]) for i in n]
    v_new = [wu[i][:, dk:] - _nt(_bf(wu[i][:, :dk]), st16[i]) for i in n]
    o = [_nt(_bf(q32[i] * e_col[i]), st16[i]) + _mm(_bf(att[i]), _bf(v_new[i])) for i in n]
    st_new = [st[i] * jnp.exp(g_end[i]) + _tn(_bf(v_new[i]), _bf(k32[i] * jnp.exp(g_end[i] - g_col[i]))) for i in n]
    return o, st_new


def _dn_scan_kernel(*refs):
    n_in = 2 * SCAN_BATCH * 5
    of_ref, ob_ref, st_ref = refs[n_in:]

    @pl.when(pl.program_id(1) == 0)
    def _():
        st_ref[...] = jnp.zeros_like(st_ref)

    _, r64, c64 = _scan_iotas()
    (q, k, v, gc, gr), rev = _scan_chains(refs[:n_in], 5, DN_HEADS)
    load = lambda items: [ref[h] for ref, h in items]
    tables = {id(ref): ref[...] for ref, _ in gc}
    beta = [tables[id(ref)][:, d * DN_HEADS + h:d * DN_HEADS + h + 1] for (ref, h), d in zip(gc, rev)]
    g_col = [tables[id(ref)][:, (2 + d) * DN_HEADS + h:(2 + d) * DN_HEADS + h + 1] for (ref, h), d in zip(gc, rev)]
    g_row = [ref[0, (2 + d) * DN_HEADS + h:(2 + d) * DN_HEADS + h + 1, :] for (ref, h), d in zip(gr, rev)]
    st = [st_ref[d, bi, h] for d in range(2) for bi in range(SCAN_BATCH) for h in range(DN_HEADS)]
    o, st = _dn_chunks(load(q), load(k), load(v), beta, g_col, g_row, st, rev, r64, c64)
    _scan_store(o, st, of_ref, ob_ref, st_ref, DN_HEADS)


def _dn_scan(q, k, v, bgc, bgr):
    head_maps, col_maps, row_maps = _scan_in_maps(2), _scan_in_maps(3), _scan_in_maps(1)
    in_specs, args = [], []
    for i in range(2 * SCAN_BATCH):
        in_specs += [pl.BlockSpec((DN_HEADS, CHUNK, LANES), head_maps[i])] * 3
        in_specs += [pl.BlockSpec((CHUNK, LANES), col_maps[i]), pl.BlockSpec((1, 4 * DN_HEADS, CHUNK), row_maps[i])]
        args += [q, k, v, bgc, bgr]
    oshape = _head_shape(DN_HEADS, LANES)
    return pl.pallas_call(
        _dn_scan_kernel,
        grid=(BATCH // SCAN_BATCH, SCAN_STEPS),
        in_specs=in_specs,
        out_specs=_scan_out_specs(DN_HEADS, LANES),
        out_shape=[oshape, oshape],
        scratch_shapes=[pltpu.VMEM((2, SCAN_BATCH, DN_HEADS, LANES, LANES), F32)],
        compiler_params=_cparams("parallel", "arbitrary"),
        name="dn_scan",
    )(*args)


def _layer_norm(z, g, b):
    mu = jnp.mean(z, axis=1, keepdims=True)
    zc = z - mu
    var = jnp.mean(zc * zc, axis=1, keepdims=True)
    return zc * lax.rsqrt(var + LN_EPS) * g + b


def _post_kernel(*refs, n_heads):
    o_refs, refs = refs[:2 * CHUNKS_PER_TILE], refs[2 * CHUNKS_PER_TILE:]
    gate_ref, xl_ref, xc_ref, mod_ref, nw_ref, wo_ref, lg_ref, lb_ref, rw_ref, rb_ref = refs[:10]
    out_refs = refs[10:]
    x1_refs, (h2_ref, logit_ref) = out_refs[:-2], out_refs[-2:]
    x_rows = _stream_rows(xl_ref, xc_ref)
    groups = [slice(g * TILE // POST_GROUPS, (g + 1) * TILE // POST_GROUPS) for g in range(POST_GROUPS)]
    chunks_per_group = CHUNKS_PER_TILE // POST_GROUPS
    on = []
    for g, rows in enumerate(groups):
        parts = []
        for h in range(n_heads):
            o = jnp.concatenate([o_refs[c][h].astype(F32) + o_refs[CHUNKS_PER_TILE + c][h].astype(F32)
                                 for c in range(g * chunks_per_group, (g + 1) * chunks_per_group)], axis=0)
            parts.append(o * lax.rsqrt(jnp.mean(o * o, axis=1, keepdims=True) + RMS_EPS) * nw_ref[...])
        on.append(jnp.concatenate(parts, axis=1) * _silu(gate_ref[rows, :].astype(F32)))
    y = [_mm(_bf(a), wo_ref[...]) for a in on]
    z = [ALPHA * x_rows[rows] + mod_ref[0, 2:3, :] * yg for rows, yg in zip(groups, y)]
    x1 = [_layer_norm(zg, lg_ref[...], lb_ref[...]) for zg in z]
    h2 = [a * (1.0 + mod_ref[0, 4:5, :]) + mod_ref[0, 3:4, :] for a in x1]
    logits = [jnp.dot(a, rw_ref[...], precision=HIGHEST, preferred_element_type=F32) + rb_ref[...] for a in h2]
    _stream_store(x1_refs, jnp.concatenate(x1, axis=0))
    h2_ref[...] = _pack_halves(jnp.concatenate(h2, axis=0))
    logit_ref[...] = jnp.concatenate(logits, axis=0)


POST_GROUPS = 2


def _post(o_f, o_b, gate, xl, xc, mod, norm_w, w_out, ln_g, ln_b, rw, rb, with_ctx):
    n_heads, _, dv = o_f.shape
    n_tiles = ALL_TILES if with_ctx else LAT_TILES
    n_rows = n_tiles * TILE
    tok = lambda w: jax.ShapeDtypeStruct((n_rows, w), F32)
    return pl.pallas_call(
        functools.partial(_post_kernel, n_heads=n_heads),
        grid=(n_tiles,),
        in_specs=[pl.BlockSpec((n_heads, CHUNK, dv), lambda t, c=c: (0, _scan_out_block(t, c), 0))
                  for _ in range(2) for c in range(CHUNKS_PER_TILE)] + [_tok_spec(D_MODEL)] + _stream_specs() + [
            _mod_spec(), _full_spec(norm_w.shape), _full_spec(w_out.shape), _full_spec(ln_g.shape),
            _full_spec(ln_b.shape), _full_spec(rw.shape), _full_spec(rb.shape)],
        out_specs=_stream_specs()[:2 if with_ctx else 1] + [_tok_spec(PACKED_D), _tok_spec(LANES)],
        out_shape=_stream_shapes(with_ctx) + [tok(PACKED_D), tok(LANES)],
        compiler_params=_cparams("arbitrary"),
        name="post_mixer",
    )(*([o_f] * CHUNKS_PER_TILE + [o_b] * CHUNKS_PER_TILE), gate, xl, xc, mod, norm_w, w_out, ln_g, ln_b, rw, rb)


def _route_kernel(logit_ref, route_ref, slot_ref, tiles_ref, run_ref, start_ref, table_ref):
    sweep = pl.program_id(0)
    t = pl.program_id(1)
    lane = lax.broadcasted_iota(jnp.int32, (TILE, LANES), 1)
    lanef = lane.astype(F32)

    def add_counts(onehot):
        new_run = run_ref[0:1, :] + jnp.sum(onehot, axis=0, keepdims=True)
        run_ref[...] = jnp.broadcast_to(new_run, run_ref.shape)
        return new_run

    @pl.when(t == 0)
    def _():
        run_ref[...] = jnp.zeros_like(run_ref)

    @pl.when(sweep == 0)
    def _():
        neg = jnp.float32(-jnp.inf)
        l = jnp.where(lane < N_EXPERTS, logit_ref[...], neg)
        vals, picks = [], []
        onehot = jnp.zeros((TILE, LANES), F32)
        for _ in range(TOP_K):
            m = jnp.max(l, axis=1, keepdims=True)
            pick = jnp.min(jnp.where(l == m, lanef, float(LANES)), axis=1, keepdims=True)
            hot = lanef == pick
            l = jnp.where(hot, neg, l)
            onehot = onehot + jnp.where(hot, 1.0, 0.0)
            vals.append(m)
            picks.append(pick)
        es = [jnp.exp(v - vals[0]) for v in vals]
        denom = es[0] + es[1] + es[2] + es[3]
        out = jnp.zeros((TILE, LANES), F32)
        for kk in range(TOP_K):
            out = jnp.where(lane == kk, picks[kk], out)
            out = jnp.where(lane == ROUTE_WEIGHT + kk, es[kk] / denom, out)
        table_ref[t] = out
        new_run = add_counts(onehot)

        @pl.when(t == pl.num_programs(1) - 1)
        def _():
            lane8 = lax.broadcasted_iota(jnp.int32, (SUBLANES, LANES), 1)
            size = jnp.ceil(jnp.broadcast_to(new_run, (SUBLANES, LANES)) * (1.0 / EXPERT_TILE)) * EXPERT_TILE
            size = jnp.where(lane8 < N_EXPERTS, size, 0.0)
            ends = size
            sh = 1
            while sh < N_EXPERTS:
                ends = ends + jnp.where(lane8 >= sh, pltpu.roll(ends, sh, 1), 0.0)
                sh *= 2
            start_ref[...] = ends - size
            tile_start = lax.broadcasted_iota(jnp.int32, (TILE, 1), 0).astype(F32) * EXPERT_TILE
            ends_t = jnp.broadcast_to(ends[0:1, :], (TILE, LANES))
            group_end_t = jnp.broadcast_to((ends - size)[0:1, :] + new_run, (TILE, LANES))
            expert = jnp.sum(jnp.where(jnp.logical_and(lane < N_EXPERTS, ends_t <= tile_start), 1.0, 0.0),
                             axis=1, keepdims=True)
            expert = jnp.minimum(expert, N_EXPERTS - 1.0)
            group_end = jnp.sum(jnp.where(lanef == expert, group_end_t, 0.0), axis=1, keepdims=True)
            valid = jnp.clip(group_end - tile_start, 0.0, float(EXPERT_TILE))
            total = jnp.max(ends_t, axis=1, keepdims=True)
            own_end = jnp.sum(jnp.where(lanef == expert, ends_t, 0.0), axis=1, keepdims=True)
            nxt = jnp.sum(jnp.where(jnp.logical_and(lane < N_EXPERTS, ends_t <= own_end), 1.0, 0.0), axis=1, keepdims=True)
            nxt = jnp.where(own_end < total, nxt, -1.0)
            tiles_ref[...] = jnp.where(lane == 0, expert, jnp.where(lane == 1, valid, jnp.where(
                lane == 2, total * (1.0 / EXPERT_TILE), jnp.where(lane == 3, nxt, 0.0))))

    @pl.when(sweep == 1)
    def _():
        out = table_ref[t]
        hots = [lanef == out[:, kk:kk + 1] for kk in range(TOP_K)]
        onehot = jnp.zeros((TILE, LANES), F32)
        for hot in hots:
            onehot = onehot + jnp.where(hot, 1.0, 0.0)
        r = lax.broadcasted_iota(jnp.int32, (TILE, TILE), 0)
        c = lax.broadcasted_iota(jnp.int32, (TILE, TILE), 1)
        before = jnp.where(c < r, 1.0, 0.0).astype(BF16)
        slot_of = _mm(before, _bf(onehot)) + run_ref[0:1, :] + start_ref[0:1, :]
        for kk in range(TOP_K):
            slot = jnp.sum(jnp.where(hots[kk], slot_of, 0.0), axis=1, keepdims=True)
            out = jnp.where(lane == ROUTE_SLOT + kk, slot, out)
        route_ref[...] = out
        slot_ref[...] = out.T[ROUTE_WEIGHT:ROUTE_WEIGHT + SUBLANES, :].astype(jnp.int32)
        add_counts(onehot)


ROUTE_WEIGHT = SUBLANES
ROUTE_SLOT = SUBLANES + TOP_K


def _route(logits):
    n_tok = logits.shape[0]
    n_tiles = n_tok // TILE
    return pl.pallas_call(
        _route_kernel,
        grid=(2, n_tiles),
        in_specs=[pl.BlockSpec((TILE, LANES), lambda s, t: (t, 0))],
        out_specs=[pl.BlockSpec((TILE, LANES), lambda s, t: (s * t, 0)),
                   pl.BlockSpec((SUBLANES, TILE), lambda s, t: (0, s * t)),
                   pl.BlockSpec((TILE, LANES), lambda s, t: (0, 0))],
        out_shape=[jax.ShapeDtypeStruct(logits.shape, F32), jax.ShapeDtypeStruct((SUBLANES, n_tok), jnp.int32),
                   jax.ShapeDtypeStruct((TILE, LANES), F32)],
        scratch_shapes=[pltpu.VMEM((SUBLANES, LANES), F32), pltpu.VMEM((SUBLANES, LANES), F32),
                        pltpu.VMEM((n_tiles, TILE, LANES), F32)],
        compiler_params=_cparams("arbitrary", "arbitrary"),
        name="route",
    )(logits)


GATHER_ROWS = 16
GATHER_BUFS = 4
SCATTER_BUFS = 4


def _sc_gather(table, idx):
    n_rows = idx.shape[0]
    d = table.shape[1]
    info = plsc.get_sparse_core_info()
    n_workers = info.num_cores * info.num_subcores
    rows_per_worker = n_rows // n_workers
    n_chunks = rows_per_worker // GATHER_ROWS
    assert rows_per_worker * n_workers == n_rows and n_chunks * GATHER_ROWS == rows_per_worker
    assert n_chunks % GATHER_BUFS == 0
    mesh = plsc.VectorSubcoreMesh(core_axis_name="c", subcore_axis_name="s")
    scratch = ([pltpu.VMEM((GATHER_ROWS,), jnp.int32)] * GATHER_BUFS
               + [pltpu.VMEM((GATHER_ROWS, d), table.dtype)] * GATHER_BUFS
               + [pltpu.SemaphoreType.DMA] * (2 * GATHER_BUFS))

    def body(table_hbm, idx_hbm, out_hbm, *scr):
        idx_v, rows_v = scr[:GATHER_BUFS], scr[GATHER_BUFS:2 * GATHER_BUFS]
        gather_sem, write_sem = scr[2 * GATHER_BUFS:3 * GATHER_BUFS], scr[3 * GATHER_BUFS:]
        worker = lax.axis_index("s") * info.num_cores + lax.axis_index("c")
        base = worker * rows_per_worker

        @pl.loop(0, n_chunks, step=GATHER_BUFS)
        def _(c0):
            offs = [pl.multiple_of(base + (c0 + b) * GATHER_ROWS, SUBLANES) for b in range(GATHER_BUFS)]
            gathers = []
            for b in range(GATHER_BUFS):
                pltpu.sync_copy(idx_hbm.at[pl.ds(offs[b], GATHER_ROWS)], idx_v[b])
                gathers.append(pltpu.async_copy(table_hbm.at[idx_v[b]], rows_v[b], gather_sem[b]))
            writes = []
            for b in range(GATHER_BUFS):
                gathers[b].wait()
                writes.append(pltpu.async_copy(rows_v[b], out_hbm.at[pl.ds(offs[b], GATHER_ROWS)], write_sem[b]))
            for w in writes:
                w.wait()

    return pl.kernel(body, out_type=jax.ShapeDtypeStruct((n_rows, d), table.dtype), mesh=mesh,
                     scratch_types=scratch, name="sc_gather")(table, idx)


def _sc_scatter(src, idx, n_out):
    n, d = src.shape
    info = plsc.get_sparse_core_info()
    n_workers = info.num_cores * info.num_subcores
    rows_per_worker = n // n_workers
    n_chunks = rows_per_worker // GATHER_ROWS
    assert rows_per_worker * n_workers == n and n_chunks * GATHER_ROWS == rows_per_worker
    assert n_chunks % SCATTER_BUFS == 0
    mesh = plsc.VectorSubcoreMesh(core_axis_name="c", subcore_axis_name="s")
    scratch = ([pltpu.VMEM((GATHER_ROWS,), jnp.int32)] * (SCATTER_BUFS * TOP_K)
               + [pltpu.VMEM((GATHER_ROWS, d), src.dtype)] * SCATTER_BUFS + [pltpu.SemaphoreType.DMA] * SCATTER_BUFS)

    def body(src_hbm, idx_hbm, out_hbm, *scr):
        idx_v = scr[:SCATTER_BUFS * TOP_K]
        rows_v = scr[SCATTER_BUFS * TOP_K:SCATTER_BUFS * TOP_K + SCATTER_BUFS]
        sem = scr[SCATTER_BUFS * TOP_K + SCATTER_BUFS:]
        worker = lax.axis_index("s") * info.num_cores + lax.axis_index("c")
        base = worker * rows_per_worker

        @pl.loop(0, n_chunks, step=SCATTER_BUFS)
        def _(c0):
            copies = []
            for b in range(SCATTER_BUFS):
                off = pl.multiple_of(base + (c0 + b) * GATHER_ROWS, SUBLANES)
                pltpu.sync_copy(src_hbm.at[pl.ds(off, GATHER_ROWS)], rows_v[b])
                for kk in range(TOP_K):
                    pltpu.sync_copy(idx_hbm.at[pl.ds(pl.multiple_of(kk * n + off, SUBLANES), GATHER_ROWS)],
                                    idx_v[b * TOP_K + kk])
                for kk in range(TOP_K):
                    copies.append(pltpu.async_copy(rows_v[b], out_hbm.at[idx_v[b * TOP_K + kk]], sem[b]))
            for c in copies:
                c.wait()

    return pl.kernel(body, out_type=jax.ShapeDtypeStruct((n_out, d), src.dtype), mesh=mesh,
                     scratch_types=scratch, name="sc_scatter")(src, idx)


PAIR_BLOCK = 2 * LANES


def _expert_kernel(te_ref, tv_ref, nt_ref, nx_ref, xs_ref, w1_hbm, b1_ref, w2_hbm, b2_ref, ys_ref,
                   w1p_ref, w2b_ref, w1_buf, w2_buf, sem, slot_ref, *, layer):
    i = pl.program_id(0)
    valid = i < nt_ref[0]
    new_expert = jnp.logical_or(i == 0, te_ref[i] != te_ref[jnp.maximum(i - 1, 0)])

    def weight_copies(expert, slot):
        return (pltpu.make_async_copy(w1_hbm.at[layer, expert], w1_buf.at[slot], sem.at[slot, 0]),
                pltpu.make_async_copy(w2_hbm.at[layer, expert], w2_buf.at[slot], sem.at[slot, 1]))

    @pl.when(i == 0)
    def _():
        slot_ref[0] = 0
        for cp in weight_copies(te_ref[0], 0):
            cp.start()

    @pl.when(jnp.logical_and(valid, new_expert))
    def _():
        slot = slot_ref[0]
        for cp in weight_copies(te_ref[i], slot):
            cp.wait()

        @pl.when(nx_ref[i] >= 0)
        def _():
            for cp in weight_copies(nx_ref[i], 1 - slot):
                cp.start()

        slot_ref[0] = 1 - slot
        r = lax.broadcasted_iota(jnp.int32, (PAIR_BLOCK, PAIR_BLOCK), 0)
        c = lax.broadcasted_iota(jnp.int32, (PAIR_BLOCK, PAIR_BLOCK), 1)
        src = jnp.where(c < LANES, 2 * c, 2 * (c - LANES) + 1)
        perm = jnp.where(r == src, 1.0, 0.0).astype(BF16)
        for b in range(2 * D_FF // PAIR_BLOCK):
            cols = slice(b * PAIR_BLOCK, (b + 1) * PAIR_BLOCK)
            w1p_ref[:, cols] = _bf(_mm(_bf(w1_buf[slot, :, cols]), perm))
        w2b_ref[...] = _bf(w2_buf[slot])

    @pl.when(valid)
    def _():
        rowi = lax.broadcasted_iota(jnp.int32, (EXPERT_TILE, D_MODEL), 0)
        x = _bf(jnp.where(rowi < tv_ref[i], _unpack_halves(xs_ref[...]), 0.0))
        u = _mm(x, w1p_ref[...]) + b1_ref[0]
        parts = []
        for b in range(2 * D_FF // PAIR_BLOCK):
            glu = jnp.minimum(u[:, b * PAIR_BLOCK:b * PAIR_BLOCK + LANES], SWIGLU_LIMIT)
            lin = jnp.clip(u[:, b * PAIR_BLOCK + LANES:(b + 1) * PAIR_BLOCK], -SWIGLU_LIMIT, SWIGLU_LIMIT)
            parts.append(glu * _sigmoid(SWIGLU_ALPHA * glu) * (lin + 1.0))
        act = jnp.concatenate(parts, axis=1)
        ys_ref[...] = _pack_halves(_mm(_bf(act), w2b_ref[...]) + b2_ref[0])


def _experts(layer, tile_expert, tile_valid, n_tiles, next_expert, xs, w1, b1p, w2, b2):
    tile_map = lambda i, te, tv, nt, nx: (jnp.minimum(i, nt[0] - 1), 0)
    wmap = lambda i, te, tv, nt, nx: (te[i], 0, 0)
    grid_spec = pltpu.PrefetchScalarGridSpec(
        num_scalar_prefetch=4,
        grid=(N_SLOT_TILES,),
        in_specs=[pl.BlockSpec((EXPERT_TILE, PACKED_D), tile_map),
                  pl.BlockSpec(memory_space=pl.ANY), pl.BlockSpec((1, 1, 2 * D_FF), wmap),
                  pl.BlockSpec(memory_space=pl.ANY), pl.BlockSpec((1, 1, D_MODEL), wmap)],
        out_specs=pl.BlockSpec((EXPERT_TILE, PACKED_D), tile_map),
        scratch_shapes=[pltpu.VMEM((D_MODEL, 2 * D_FF), BF16), pltpu.VMEM((D_FF, D_MODEL), BF16),
                        pltpu.VMEM((2, D_MODEL, 2 * D_FF), F32), pltpu.VMEM((2, D_FF, D_MODEL), F32),
                        pltpu.SemaphoreType.DMA((2, 2)), pltpu.SMEM((1,), jnp.int32)],
    )
    return pl.pallas_call(
        functools.partial(_expert_kernel, layer=layer),
        grid_spec=grid_spec,
        out_shape=jax.ShapeDtypeStruct((N_SLOTS, PACKED_D), F32),
        compiler_params=pltpu.CompilerParams(dimension_semantics=("arbitrary",), vmem_limit_bytes=EXPERT_VMEM_LIMIT),
        name="experts",
    )(tile_expert, tile_valid, n_tiles, next_expert, xs, w1, b1p, w2, b2)


def _combine_kernel(y_ref, route_ref, x1l_ref, x1c_ref, mod_ref, lg_ref, lb_ref, *x2_refs):
    route = route_ref[...]
    lane = lax.broadcasted_iota(jnp.int32, route.shape, 1)
    f = None
    for kk in range(TOP_K):
        term = _lane_pick(route, lane, ROUTE_WEIGHT + kk) * _unpack_halves(y_ref[kk])
        f = term if f is None else f + term
    z = ALPHA * _stream_rows(x1l_ref, x1c_ref) + mod_ref[0, 5:6, :] * f
    _stream_store(x2_refs, _layer_norm(z, lg_ref[...], lb_ref[...]))


def _combine(y_tok, route, x1l, x1c, mod, ln_g, ln_b, with_ctx):
    n_tiles = ALL_TILES if with_ctx else LAT_TILES
    return pl.pallas_call(
        _combine_kernel,
        grid=(n_tiles,),
        in_specs=[pl.BlockSpec((TOP_K, TILE, PACKED_D), lambda t: (0, t, 0)), _tok_spec(LANES)] + _stream_specs() + [
            _mod_spec(), _full_spec(ln_g.shape), _full_spec(ln_b.shape)],
        out_specs=_stream_specs()[:2 if with_ctx else 1],
        out_shape=_stream_shapes(with_ctx),
        compiler_params=_cparams("arbitrary"),
        name="combine",
    )(y_tok, route, x1l, x1c, mod, ln_g, ln_b)


def _moe(layer, h2, logits, x1l, x1c, mod, ln_g, ln_b, w1, b1p, w2, b2, with_ctx):
    n_tok = h2.shape[0]
    assert N_SLOT_TILES <= TILE
    route, slot_rows, tiles = _route(logits)
    slot_by_choice = slot_rows[TOP_K:2 * TOP_K].reshape(-1)
    tiles = tiles[:N_SLOT_TILES, :SUBLANES].astype(jnp.int32)
    tile_expert, tile_valid, n_tiles, next_expert = tiles[:, 0], tiles[:, 1], tiles[:1, 2], tiles[:, 3]
    xs = _sc_scatter(h2, slot_by_choice, N_SLOTS)
    ys = _experts(layer, tile_expert, tile_valid, n_tiles, next_expert, xs, w1, b1p, w2, b2)
    y_tok = _sc_gather(ys, slot_by_choice).reshape(TOP_K, n_tok, PACKED_D)
    return _combine(y_tok, route, x1l, x1c, mod, ln_g, ln_b, with_ctx)


def _lat_to_col_major(lat):
    return lat.reshape(BATCH, SEQ // GRID_W, GRID_W, D_MODEL).transpose(0, 2, 1, 3).reshape(N_LAT, D_MODEL)


def _lat_to_row_major(lat):
    return lat.reshape(BATCH, GRID_W, SEQ // GRID_W, D_MODEL).transpose(0, 2, 1, 3).reshape(N_LAT, D_MODEL)


def kernel(x, c, ctx, c_ctx, ada_w, ada_b, ln_g, ln_b, dn_w_in, dn_conv_w, dn_a_log, dn_dt_bias, dn_norm_w, dn_w_out,
           hg_w_in, hg_lower_bound, hg_norm_w, hg_w_out, gla_w_in, gla_gk_w, gla_gk_b, gla_norm_w, gla_w_out,
           router_w, router_b, exp_w1, exp_b1, exp_w2, exp_b2):
    xl = x.reshape(N_LAT, D_MODEL)
    xc = ctx.reshape(N_CTX, D_MODEL)
    cc = jnp.zeros((MOD_ROWS, D_MODEL), F32).at[:BATCH].set(c).at[BATCH].set(c_ctx)
    mods = _ada_all(cc, ada_w, ada_b).reshape(DEPTH, MOD_ROWS, ADA_CHUNKS, D_MODEL)
    lb_soft = jax.nn.softmax(hg_lower_bound.astype(F32), axis=0)
    lower_bounds = jnp.cumsum(lb_soft, axis=0) - lb_soft[0]

    for i in range(DEPTH):
        last = i == DEPTH - 1
        kind, j = i % 3, i // 3
        mod = mods[i]
        col_major = i % 2 == 1
        if col_major:
            xl = _lat_to_col_major(xl)

        if kind == 0:
            nq = 2 * D_MODEL + 2 * D_MODEL
            w = dn_w_in[j]
            w_ba = jnp.zeros((D_MODEL, LANES), F32).at[:, :4 * DN_HEADS].set(w[:, nq:])
            a_neg = jnp.zeros((LANES,), F32).at[2 * DN_HEADS:4 * DN_HEADS].set(-jnp.exp(dn_a_log[j].astype(F32)).reshape(-1))
            dtb = jnp.zeros((LANES,), F32).at[2 * DN_HEADS:4 * DN_HEADS].set(dn_dt_bias[j].astype(F32).reshape(-1))
            nb = 4 * DN_HEADS
            p_qkv, gate, bgc, bgr = _dn_proj(xl, xc, mod, _bf(w), _bf(w_ba), _bf(w[:, nq:].T),
                                            a_neg.reshape(1, LANES), dtb.reshape(1, LANES),
                                            a_neg[:nb].reshape(nb, 1), dtb[:nb].reshape(nb, 1))
            cw = jnp.zeros((SUBLANES, DN_QKV), F32).at[:SHORT_CONV].set(dn_conv_w[j])
            q, k, v = _dn_conv(p_qkv, cw)
            o_f, o_b = _dn_scan(q, k, v, bgc, bgr)
            norm_w, w_out = dn_norm_w[j], dn_w_out[j]
        elif kind == 1:
            q, v, gate, k_f, g_f, k_b, g_b = _hg_proj(xl, xc, mod, _bf(hg_w_in[j]), lower_bounds[i].reshape(1, D_MODEL))
            o_f, o_b = _gla_scan(q, k_f, k_b, v, g_f, g_b)
            norm_w, w_out = hg_norm_w[j], hg_w_out[j]
        else:
            kd = GLA_HEADS * GLA_DK
            n_main = 2 * kd + 2 * D_MODEL
            w = gla_w_in[j]
            w_lr = jnp.zeros((D_MODEL, 2 * LANES), F32)
            w_lr = w_lr.at[:, :GLA_RANK].set(w[:, n_main:n_main + GLA_RANK])
            w_lr = w_lr.at[:, LANES:LANES + GLA_RANK].set(w[:, n_main + GLA_RANK:])
            gk_w = jnp.zeros((2, LANES, kd), F32).at[:, :GLA_RANK].set(gla_gk_w[j])
            q, k, v, gate, g_f, g_b = _gla_proj(xl, xc, mod, _bf(w), _bf(w_lr), _bf(gk_w),
                                                gla_gk_b[j].reshape(2, 1, kd))
            o_f, o_b = _gla_scan(q, k, k, v, g_f, g_b)
            norm_w, w_out = gla_norm_w[j], gla_w_out[j]

        with_ctx = not last
        rw = jnp.zeros((D_MODEL, LANES), F32).at[:, :N_EXPERTS].set(router_w[i])
        rb = jnp.zeros((1, LANES), F32).at[0, :N_EXPERTS].set(router_b[i])
        *x1, h2, logits = _post(o_f, o_b, gate, xl, xc, mod, norm_w.reshape(1, -1), _bf(w_out),
                                ln_g[i, 0].reshape(1, D_MODEL), ln_b[i, 0].reshape(1, D_MODEL), rw, rb, with_ctx)
        b1p = exp_b1[i].reshape(N_EXPERTS, 2 * D_FF // PAIR_BLOCK, LANES, 2).transpose(0, 1, 3, 2)
        x2 = _moe(i, h2, logits, x1[0], x1[1] if with_ctx else xc, mod,
                  ln_g[i, 1].reshape(1, D_MODEL), ln_b[i, 1].reshape(1, D_MODEL),
                  exp_w1, b1p.reshape(N_EXPERTS, 1, 2 * D_FF), exp_w2, exp_b2[i][:, None, :], with_ctx)
        xl = _lat_to_row_major(x2[0]) if col_major else x2[0]
        if with_ctx:
            xc = x2[1]
    return xl.reshape(BATCH, SEQ, D_MODEL)
```

```python
import functools

import jax
import jax.numpy as jnp
from jax import lax
from jax.experimental import pallas as pl
from jax.experimental.pallas import tpu as pltpu
from jax.experimental.pallas import tpu_sc as plsc

F32 = jnp.float32
BF16 = jnp.bfloat16
HIGHEST = lax.Precision.HIGHEST

D_MODEL = 1024
BATCH = 8
SEQ = 2048
CTX_LEN = 256
DEPTH = 4
GRID_W = 64
CHUNK = 64
ADA_CHUNKS = 6
N_LAT = BATCH * SEQ
N_CTX = BATCH * CTX_LEN
N_TOK = N_LAT + N_CTX
TILE = 256
LAT_TILES = N_LAT // TILE
ALL_TILES = N_TOK // TILE
TILES_PER_SEQ = SEQ // TILE
MOD_ROWS = 16
LANES = 128
SUBLANES = 8

N_EXPERTS = 32
TOP_K = 4
D_FF = D_MODEL
SWIGLU_LIMIT = 7.0
SWIGLU_ALPHA = 1.702
EXPERT_TILE = 512
N_SLOT_TILES = N_TOK * TOP_K // EXPERT_TILE + N_EXPERTS
N_SLOTS = N_SLOT_TILES * EXPERT_TILE

LN_EPS = 1e-5
RMS_EPS = 1e-6
L2_EPS = 1e-6
ALPHA = (2.0 * DEPTH) ** 0.25

VMEM_LIMIT = 48 * 1024 * 1024
EXPERT_VMEM_LIMIT = (2 * 4 * (D_MODEL * 2 * D_FF + D_FF * D_MODEL) + 2 * (D_MODEL * 2 * D_FF + D_FF * D_MODEL)
                     + 4 * 4 * EXPERT_TILE * D_MODEL + 4 * EXPERT_TILE * (2 * D_FF + 2 * D_MODEL) + (4 << 20))


def _cparams(*sem):
    return pltpu.CompilerParams(dimension_semantics=sem, vmem_limit_bytes=VMEM_LIMIT)


def _sigmoid(x):
    return 1.0 / (1.0 + jnp.exp(-x))


def _silu(x):
    return x * _sigmoid(x)


def _softplus(x):
    return jnp.maximum(x, 0.0) + jnp.log(1.0 + jnp.exp(-jnp.abs(x)))


def _log_sigmoid(x):
    return -_softplus(-x)


def _nt(a, b):
    return lax.dot_general(a, b, (((1,), (1,)), ((), ())), preferred_element_type=F32)


def _tn(a, b):
    return lax.dot_general(a, b, (((0,), (0,)), ((), ())), preferred_element_type=F32)


def _mm(a, b):
    return jnp.dot(a, b, preferred_element_type=F32)


def _bf(x):
    return x.astype(BF16)


PACKED_D = D_MODEL // 2
HIGH_HALF = 0xFFFF0000


def _pack_halves(x):
    m = x.shape[1] // 2
    lo = pltpu.bitcast(x[:, :m].astype(BF16).astype(F32), jnp.uint32) >> 16
    hi = pltpu.bitcast(x[:, m:].astype(BF16).astype(F32), jnp.uint32) & jnp.uint32(HIGH_HALF)
    return pltpu.bitcast(hi | lo, F32)


def _unpack_halves(p):
    u = pltpu.bitcast(p, jnp.uint32)
    return jnp.concatenate([pltpu.bitcast(u << 16, F32), pltpu.bitcast(u & jnp.uint32(HIGH_HALF), F32)], axis=1)


def _mod_row(t):
    return jnp.minimum(t // TILES_PER_SEQ, BATCH)


ADA_NBLK = 1536


def _ada_kernel(c_ref, w_ref, b_ref, o_ref):
    s = _silu(c_ref[...])
    o_ref[0] = jnp.dot(s, w_ref[0], precision=HIGHEST, preferred_element_type=F32) + b_ref[0]


def _ada_all(cc, ada_w, ada_b):
    n = ADA_CHUNKS * D_MODEL
    return pl.pallas_call(
        _ada_kernel,
        grid=(DEPTH, n // ADA_NBLK),
        in_specs=[
            pl.BlockSpec((MOD_ROWS, D_MODEL), lambda i, j: (0, 0)),
            pl.BlockSpec((1, D_MODEL, ADA_NBLK), lambda i, j: (i, 0, j)),
            pl.BlockSpec((1, 1, ADA_NBLK), lambda i, j: (i, 0, j)),
        ],
        out_specs=pl.BlockSpec((1, MOD_ROWS, ADA_NBLK), lambda i, j: (i, 0, j)),
        out_shape=jax.ShapeDtypeStruct((DEPTH, MOD_ROWS, n), F32),
        compiler_params=_cparams("parallel", "parallel"),
        name="ada",
    )(cc, ada_w, ada_b.reshape(DEPTH, 1, n))


def _stream_rows(xl_ref, xc_ref):
    return jnp.where(pl.program_id(0) < LAT_TILES, xl_ref[...], xc_ref[...])


def _stream_specs():
    return [pl.BlockSpec((TILE, D_MODEL), lambda t: (jnp.minimum(t, LAT_TILES - 1), 0)),
            pl.BlockSpec((TILE, D_MODEL), lambda t: (jnp.maximum(t - LAT_TILES, 0), 0))]


def _stream_shapes(with_ctx):
    shapes = [jax.ShapeDtypeStruct((N_LAT, D_MODEL), F32)]
    return shapes + [jax.ShapeDtypeStruct((N_CTX, D_MODEL), F32)] if with_ctx else shapes


def _stream_store(refs, value):
    is_lat = pl.program_id(0) < LAT_TILES

    @pl.when(is_lat)
    def _():
        refs[0][...] = value

    if len(refs) > 1:
        @pl.when(jnp.logical_not(is_lat))
        def _():
            refs[1][...] = value


def _modulated(xl_ref, xc_ref, mod_ref, shift, scale):
    return _stream_rows(xl_ref, xc_ref) * (1.0 + mod_ref[0, scale:scale + 1, :]) + mod_ref[0, shift:shift + 1, :]


def _store_heads(ref, val, width):
    for h in range(val.shape[1] // width):
        ref[h] = val[:, h * width:(h + 1) * width].astype(ref.dtype)


def _hg_proj_kernel(xl_ref, xc_ref, mod_ref, w_ref, lb_ref, q_ref, v_ref, gate_ref, kf_ref, gf_ref, kb_ref, gb_ref):
    h = _bf(_modulated(xl_ref, xc_ref, mod_ref, 0, 1))
    d = D_MODEL
    lb = lb_ref[...]
    for k_ref, g_ref, lo in ((kf_ref, gf_ref, 3 * d), (kb_ref, gb_ref, 4 * d)):
        s = _sigmoid(_mm(h, w_ref[:, lo:lo + d]))
        _store_heads(k_ref, (1.0 - lb) * (1.0 - s), LANES)
        _store_heads(g_ref, jnp.log(lb + (1.0 - lb) * s), LANES)
    _store_heads(q_ref, _silu(_mm(h, w_ref[:, 0:d])), LANES)
    _store_heads(v_ref, _mm(h, w_ref[:, d:2 * d]), LANES)
    gate_ref[...] = _bf(_mm(h, w_ref[:, 2 * d:3 * d]))


def _head_spec(n_heads, width):
    return pl.BlockSpec((n_heads, TILE, width), lambda t: (0, t, 0))


def _head_shape(n_heads, width, dtype=None):
    return jax.ShapeDtypeStruct((n_heads, N_TOK, width), BF16 if dtype is None else dtype)


def _tok_shape(width):
    return jax.ShapeDtypeStruct((N_TOK, width), BF16)


def _tok_spec(width):
    return pl.BlockSpec((TILE, width), lambda t: (t, 0))


def _mod_spec():
    return pl.BlockSpec((1, ADA_CHUNKS, D_MODEL), lambda t: (_mod_row(t), 0, 0))


def _full_spec(shape):
    return pl.BlockSpec(shape, lambda t: (0,) * len(shape))


HG_HEADS = 8


def _hg_proj(xl, xc, mod, w_in, lb):
    n = w_in.shape[1]
    hs = _head_shape(HG_HEADS, LANES)
    hg = _head_shape(HG_HEADS, LANES, F32)
    spec = _head_spec(HG_HEADS, LANES)
    return pl.pallas_call(
        _hg_proj_kernel,
        grid=(ALL_TILES,),
        in_specs=_stream_specs() + [_mod_spec(), _full_spec((D_MODEL, n)), _full_spec((1, D_MODEL))],
        out_specs=[spec, spec, _tok_spec(D_MODEL), spec, spec, spec, spec],
        out_shape=[hs, hs, _tok_shape(D_MODEL), hs, hg, hs, hg],
        compiler_params=_cparams("parallel"),
        name="hg_proj",
    )(xl, xc, mod, w_in, lb)


GLA_HEADS = 4
GLA_DK = 128
GLA_DV = 256
GLA_RANK = 16
GLA_GATE_NORM = 16.0


def _gla_proj_kernel(xl_ref, xc_ref, mod_ref, w_ref, wlr_ref, gkw_ref, gkb_ref, q_ref, k_ref, v_ref, gate_ref, gf_ref, gb_ref):
    h = _bf(_modulated(xl_ref, xc_ref, mod_ref, 0, 1))
    kd = GLA_HEADS * GLA_DK
    vd = GLA_HEADS * GLA_DV
    for z, g_ref in ((0, gf_ref), (1, gb_ref)):
        lr = _mm(h, wlr_ref[:, z * LANES:(z + 1) * LANES])
        gk = _mm(_bf(lr), gkw_ref[z]) + gkb_ref[z]
        _store_heads(g_ref, _log_sigmoid(gk) / GLA_GATE_NORM, GLA_DK)
    _store_heads(q_ref, _mm(h, w_ref[:, 0:kd]) * GLA_DK ** -0.5, GLA_DK)
    _store_heads(k_ref, _mm(h, w_ref[:, kd:2 * kd]), GLA_DK)
    _store_heads(v_ref, _mm(h, w_ref[:, 2 * kd:2 * kd + vd]), GLA_DV)
    gate_ref[...] = _bf(_mm(h, w_ref[:, 2 * kd + vd:2 * kd + 2 * vd]))


def _gla_proj(xl, xc, mod, w_main, w_lr, gk_w, gk_b):
    kd = GLA_HEADS * GLA_DK
    hk = _head_shape(GLA_HEADS, GLA_DK)
    return pl.pallas_call(
        _gla_proj_kernel,
        grid=(ALL_TILES,),
        in_specs=_stream_specs() + [_mod_spec(), _full_spec(w_main.shape), _full_spec(w_lr.shape),
                  _full_spec(gk_w.shape), _full_spec(gk_b.shape)],
        out_specs=[_head_spec(GLA_HEADS, GLA_DK), _head_spec(GLA_HEADS, GLA_DK), _head_spec(GLA_HEADS, GLA_DV),
                   _tok_spec(D_MODEL), _head_spec(GLA_HEADS, GLA_DK), _head_spec(GLA_HEADS, GLA_DK)],
        out_shape=[hk, hk, _head_shape(GLA_HEADS, GLA_DV), _tok_shape(D_MODEL),
                   _head_shape(GLA_HEADS, GLA_DK, F32), _head_shape(GLA_HEADS, GLA_DK, F32)],
        compiler_params=_cparams("parallel"),
        name="gla_proj",
    )(xl, xc, mod, w_main, w_lr, gk_w, gk_b)


DN_HEADS = 8
DN_QKV = 3 * D_MODEL
SHORT_CONV = 5
CHUNKS_PER_TILE = TILE // CHUNK


def _seg_scan(x, pos, axis, reverse):
    n = x.shape[axis]
    sh = 1
    while sh < CHUNK:
        if reverse:
            x = x + jnp.where(pos < CHUNK - sh, pltpu.roll(x, n - sh, axis), 0.0)
        else:
            x = x + jnp.where(pos >= sh, pltpu.roll(x, sh, axis), 0.0)
        sh *= 2
    return x


def _dn_gates(p, a_neg, dtb, idx, pos, axis):
    beta = _sigmoid(p)
    g = a_neg * _softplus(p + dtb)
    gp = _seg_scan(g, pos, axis, False)
    gs = _seg_scan(g, pos, axis, True)
    return jnp.where(idx < 2 * DN_HEADS, beta, jnp.where(idx < 3 * DN_HEADS, gp, gs))


def _dn_proj_kernel(xl_ref, xc_ref, mod_ref, w_ref, wba_ref, wbat_ref, ac_ref, dc_ref, ar_ref, dr_ref,
                    qkv_ref, gate_ref, bgc_ref, bgr_ref):
    h = _bf(_modulated(xl_ref, xc_ref, mod_ref, 0, 1))
    pc = _mm(h, wba_ref[...])
    lane = lax.broadcasted_iota(jnp.int32, pc.shape, 1)
    rowpos = lax.broadcasted_iota(jnp.int32, pc.shape, 0) % CHUNK
    bgc_ref[...] = _dn_gates(pc, ac_ref[...], dc_ref[...], lane, rowpos, 0)
    pr = _nt(wbat_ref[...], h)
    sub = lax.broadcasted_iota(jnp.int32, pr.shape, 0)
    lanepos = lax.broadcasted_iota(jnp.int32, pr.shape, 1) % CHUNK
    gr = _dn_gates(pr, ar_ref[...], dr_ref[...], sub, lanepos, 1)
    for c in range(CHUNKS_PER_TILE):
        bgr_ref[c] = gr[:, c * CHUNK:(c + 1) * CHUNK]
    qkv_ref[...] = _bf(_mm(h, w_ref[:, 0:DN_QKV]))
    gate_ref[...] = _bf(_mm(h, w_ref[:, DN_QKV:DN_QKV + D_MODEL]))


def _dn_proj(xl, xc, mod, w_main, w_ba, w_bat, a_col, d_col, a_row, d_row):
    nb = 4 * DN_HEADS
    return pl.pallas_call(
        _dn_proj_kernel,
        grid=(ALL_TILES,),
        in_specs=_stream_specs() + [_mod_spec(), _full_spec(w_main.shape), _full_spec(w_ba.shape),
                  _full_spec(w_bat.shape), _full_spec(a_col.shape), _full_spec(d_col.shape),
                  _full_spec(a_row.shape), _full_spec(d_row.shape)],
        out_specs=[_tok_spec(DN_QKV), _tok_spec(D_MODEL), _tok_spec(LANES),
                   pl.BlockSpec((CHUNKS_PER_TILE, nb, CHUNK), lambda t: (t, 0, 0))],
        out_shape=[_tok_shape(DN_QKV), _tok_shape(D_MODEL), jax.ShapeDtypeStruct((N_TOK, LANES), F32),
                   jax.ShapeDtypeStruct((N_TOK // CHUNK, nb, CHUNK), F32)],
        compiler_params=_cparams("parallel"),
        name="dn_proj",
    )(xl, xc, mod, w_main, w_ba, w_bat, a_col, d_col, a_row, d_row)


HALO = 2 * SUBLANES


def _dn_conv_kernel(pm_ref, pp_ref, pn_ref, cw_ref, q_ref, k_ref, v_ref):
    t = pl.program_id(0)
    is_lat = t < LAT_TILES
    first = jnp.logical_or(jnp.logical_not(is_lat), t % TILES_PER_SEQ == 0)
    last = jnp.logical_or(jnp.logical_not(is_lat), t % TILES_PER_SEQ == TILES_PER_SEQ - 1)
    keep_prev = jnp.where(first, 0.0, 1.0)
    keep_next = jnp.where(last, 0.0, 1.0)
    n_ext = TILE + 2 * HALO
    pad = SHORT_CONV // 2
    for s in range(DN_QKV // LANES):
        cols = slice(s * LANES, (s + 1) * LANES)
        ext = jnp.concatenate([pp_ref[:, cols].astype(F32) * keep_prev, pm_ref[:, cols].astype(F32),
                               pn_ref[:, cols].astype(F32) * keep_next], axis=0)
        acc = None
        for kk in range(SHORT_CONV):
            off = HALO - pad + kk
            tap = pltpu.roll(ext, n_ext - off, 0)[0:TILE] * cw_ref[kk:kk + 1, cols]
            acc = tap if acc is None else acc + tap
        a = _silu(acc)
        head = s % DN_HEADS
        if s < 2 * DN_HEADS:
            a = a * lax.rsqrt(jnp.sum(a * a, axis=1, keepdims=True) + L2_EPS)
            if s < DN_HEADS:
                q_ref[head] = _bf(a * LANES ** -0.5)
            else:
                k_ref[head] = _bf(a)
        else:
            v_ref[head] = _bf(a)


def _dn_conv(p_qkv, conv_w):
    rows8 = TILE // HALO
    last8 = N_TOK // HALO - 1
    hs = _head_shape(DN_HEADS, LANES)
    return pl.pallas_call(
        _dn_conv_kernel,
        grid=(ALL_TILES,),
        in_specs=[_tok_spec(DN_QKV),
                  pl.BlockSpec((HALO, DN_QKV), lambda t: (jnp.maximum(t * rows8 - 1, 0), 0)),
                  pl.BlockSpec((HALO, DN_QKV), lambda t: (jnp.minimum((t + 1) * rows8, last8), 0)),
                  _full_spec(conv_w.shape)],
        out_specs=[_head_spec(DN_HEADS, LANES)] * 3,
        out_shape=[hs, hs, hs],
        compiler_params=_cparams("parallel"),
        name="dn_conv",
    )(p_qkv, p_qkv, p_qkv, conv_w)


LOG2_E = 1.4426950408889634
SIGN_BIT = 0x80000000


def _chunk_cumsum(x, ones_tri):
    hi = _bf(x)
    rest = x - hi.astype(F32)
    mid = _bf(rest)
    lo = _bf(rest - mid.astype(F32))
    return _mm(ones_tri, hi) + _mm(ones_tri, mid) + _mm(ones_tri, lo)


def _neg_abs(x):
    return pltpu.bitcast(pltpu.bitcast(x, jnp.uint32) | jnp.uint32(SIGN_BIT), F32)


def _gla_chunks(q, k, v, g, st, rev, row, r64, c64):
    n = range(len(q))
    q32 = [q[i].astype(F32) for i in n]
    k32 = [k[i].astype(F32) for i in n]
    ones_tri = {False: jnp.where(r64 >= c64, 1.0, 0.0).astype(BF16), True: jnp.where(r64 <= c64, 1.0, 0.0).astype(BF16)}
    cum = [_chunk_cumsum(g[i] * LOG2_E, ones_tri[rev[i]]) for i in n]
    g_end = [cum[i][0:1, :] if rev[i] else cum[i][CHUNK - 1:CHUNK, :] for i in n]
    att = [jnp.where(r64 == c64, _nt(_bf(q[i]), _bf(k[i])), 0.0) for i in n]
    edge = list(cum)
    s = 1
    while s < CHUNK:
        late = (row & s) != 0
        early = jnp.logical_not(late)
        rlate = (r64 & s) != 0
        clate = (c64 & s) != 0
        pair = {False: jnp.logical_and(rlate, jnp.logical_not(clate)), True: jnp.logical_and(clate, jnp.logical_not(rlate))}
        if 2 * s < CHUNK:
            shift = (2 * s).bit_length() - 1
            same = (r64 >> shift) == (c64 >> shift)
            pair = {d: jnp.logical_and(p, same) for d, p in pair.items()}
        for i in n:
            if rev[i]:
                cut = jnp.where(late, edge[i], pltpu.roll(edge[i], CHUNK - s, 0))
            else:
                cut = jnp.where(late, pltpu.roll(edge[i], s, 0), edge[i])
            e = jnp.exp2(_neg_abs(cum[i] - cut))
            q_side = early if rev[i] else late
            x16 = _bf(jnp.where(q_side, q32[i], k32[i]) * e)
            att[i] = att[i] + jnp.where(pair[rev[i]], _nt(x16, x16), 0.0)
            if rev[i]:
                edge[i] = jnp.where(late, pltpu.roll(edge[i], s, 0), edge[i])
            else:
                edge[i] = jnp.where(late, edge[i], pltpu.roll(edge[i], CHUNK - s, 0))
        s *= 2
    st16 = [_bf(st[i]) for i in n]
    o = [_mm(_bf(att[i]), _bf(v[i])) + _nt(_bf(q32[i] * jnp.exp2(cum[i])), st16[i]) for i in n]
    st_new = [st[i] * jnp.exp2(g_end[i]) + _tn(_bf(v[i]), _bf(k32[i] * jnp.exp2(g_end[i] - cum[i]))) for i in n]
    return o, st_new


def _scan_iotas():
    row = lax.broadcasted_iota(jnp.int32, (CHUNK, LANES), 0)
    r64 = lax.broadcasted_iota(jnp.int32, (CHUNK, CHUNK), 0)
    c64 = lax.broadcasted_iota(jnp.int32, (CHUNK, CHUNK), 1)
    return row, r64, c64


SCAN_BATCH = 2


def _scan_chains(in_refs, per_chain, n_heads):
    kinds = [[] for _ in range(per_chain)]
    rev = []
    for d in range(2):
        for bi in range(SCAN_BATCH):
            group = in_refs[(d * SCAN_BATCH + bi) * per_chain:(d * SCAN_BATCH + bi + 1) * per_chain]
            for h in range(n_heads):
                for kind, ref in zip(kinds, group):
                    kind.append((ref, h))
                rev.append(d == 1)
    return kinds, rev


def _scan_store(o, st, of_ref, ob_ref, st_ref, n_heads):
    i = 0
    for d, o_ref in ((0, of_ref), (1, ob_ref)):
        for bi in range(SCAN_BATCH):
            for h in range(n_heads):
                o_ref[h, bi * CHUNK:(bi + 1) * CHUNK, :] = _bf(o[i])
                st_ref[d, bi, h] = st[i]
                i += 1


def _gla_scan_kernel(*refs, n_heads):
    n_in = 2 * SCAN_BATCH * 4
    of_ref, ob_ref, st_ref = refs[n_in:]

    @pl.when(pl.program_id(1) == 0)
    def _():
        st_ref[...] = jnp.zeros_like(st_ref)

    row, r64, c64 = _scan_iotas()
    (q, k, v, g), rev = _scan_chains(refs[:n_in], 4, n_heads)
    load = lambda items: [ref[h] for ref, h in items]
    st = [st_ref[d, bi, h] for d in range(2) for bi in range(SCAN_BATCH) for h in range(n_heads)]
    o, st = _gla_chunks(load(q), load(k), load(v), load(g), st, rev, row, r64, c64)
    _scan_store(o, st, of_ref, ob_ref, st_ref, n_heads)


CTX_CHUNKS = CTX_LEN // CHUNK
LAT_CHUNKS = SEQ // CHUNK
SCAN_STEPS = CTX_CHUNKS + LAT_CHUNKS


def _fwd_chunk(b, n):
    return jnp.where(n < CTX_CHUNKS, N_LAT // CHUNK + b * CTX_CHUNKS + n, b * LAT_CHUNKS + n - CTX_CHUNKS)


def _bwd_chunk(b, n):
    return jnp.where(n < CTX_CHUNKS, N_LAT // CHUNK + b * CTX_CHUNKS + CTX_CHUNKS - 1 - n,
                     b * LAT_CHUNKS + SCAN_STEPS - 1 - n)


def _scan_in_maps(rank):
    def make(chunk_of, bi):
        def index_map(p, n):
            c = chunk_of(p * SCAN_BATCH + bi, n)
            return {1: (c, 0, 0), 2: (0, c, 0), 3: (c, 0)}[rank]
        return index_map
    return [make(chunk_of, bi) for chunk_of in (_fwd_chunk, _bwd_chunk) for bi in range(SCAN_BATCH)]


def _scan_out_pos(step, reverse):
    if not reverse:
        return step
    return jnp.where(step < CTX_CHUNKS, CTX_CHUNKS - 1 - step, CTX_CHUNKS + SCAN_STEPS - 1 - step)


def _scan_out_specs(n_heads, dv):
    return [pl.BlockSpec((n_heads, SCAN_BATCH * CHUNK, dv),
                         lambda p, n, reverse=reverse: (0, p * SCAN_STEPS + _scan_out_pos(n, reverse), 0))
            for reverse in (False, True)]


def _scan_out_block(t, c):
    is_lat = t < LAT_TILES
    b = jnp.where(is_lat, t // TILES_PER_SEQ, t - LAT_TILES)
    pos = jnp.where(is_lat, CTX_CHUNKS + (t % TILES_PER_SEQ) * CHUNKS_PER_TILE + c, c)
    return ((b // SCAN_BATCH) * SCAN_STEPS + pos) * SCAN_BATCH + b % SCAN_BATCH


def _gla_scan(q, k_f, k_b, v, g_f, g_b):
    n_heads, _, dk = q.shape
    dv = v.shape[2]
    maps = _scan_in_maps(2)
    kspec = lambda m: pl.BlockSpec((n_heads, CHUNK, dk), m)
    vspec = lambda m: pl.BlockSpec((n_heads, CHUNK, dv), m)
    in_specs, args = [], []
    for d, k_arr, g_arr in ((0, k_f, g_f), (1, k_b, g_b)):
        for bi in range(SCAN_BATCH):
            m = maps[d * SCAN_BATCH + bi]
            in_specs += [kspec(m), kspec(m), vspec(m), kspec(m)]
            args += [q, k_arr, v, g_arr]
    oshape = _head_shape(n_heads, dv)
    return pl.pallas_call(
        functools.partial(_gla_scan_kernel, n_heads=n_heads),
        grid=(BATCH // SCAN_BATCH, SCAN_STEPS),
        in_specs=in_specs,
        out_specs=_scan_out_specs(n_heads, dv),
        out_shape=[oshape, oshape],
        scratch_shapes=[pltpu.VMEM((2, SCAN_BATCH, n_heads, dv, dk), F32)],
        compiler_params=_cparams("parallel", "arbitrary"),
        name="gla_scan",
    )(*args)


def _lane_pick(x, lane, idx):
    return jnp.sum(jnp.where(lane == idx, x, 0.0), axis=1, keepdims=True)


def _dn_chunks(q, k, v, beta, g_col, g_row, st, rev, r64, c64):
    n = range(len(q))
    dk = k[0].shape[1]
    g_end = [g_row[i][:, 0:1] if rev[i] else g_row[i][:, CHUNK - 1:CHUNK] for i in n]
    causal = {False: r64 >= c64, True: r64 <= c64}
    gam = [jnp.where(causal[rev[i]], jnp.exp(jnp.minimum(g_col[i] - g_row[i], 0.0)), 0.0) for i in n]
    q32 = [q[i].astype(F32) for i in n]
    k32 = [k[i].astype(F32) for i in n]
    kb = [k32[i] * beta[i] for i in n]
    k16 = [_bf(k[i]) for i in n]
    m = [jnp.where(r64 == c64, 0.0, _nt(_bf(kb[i]), k16[i]) * gam[i]) for i in n]
    att = [_nt(_bf(q[i]), k16[i]) * gam[i] for i in n]
    y = None
    s = 1
    while s < CHUNK:
        shift = (2 * s).bit_length() - 1
        rlate = (r64 & s) != 0
        clate = (c64 & s) != 0
        pair = {False: jnp.logical_and(rlate, jnp.logical_not(clate)), True: jnp.logical_and(clate, jnp.logical_not(rlate))}
        if 2 * s < CHUNK:
            same = (r64 >> shift) == (c64 >> shift)
            pair = {d: jnp.logical_and(p, same) for d, p in pair.items()}
        c = [jnp.where(pair[rev[i]], m[i], 0.0) for i in n]
        if y is None:
            y = [-c[i] for i in n]
        else:
            c16 = [_bf(c[i]) for i in n]
            y16 = [_bf(y[i]) for i in n]
            p = [c[i] + _mm(y16[i], c16[i]) for i in n]
            y = [y[i] - (p[i] + _mm(_bf(p[i]), y16[i])) for i in n]
        s *= 2
    e_col = [jnp.exp(g_col[i]) for i in n]
    rhs = [jnp.concatenate([kb[i] * e_col[i], v[i].astype(F32) * beta[i]], axis=1) for i in n]
    wu = [rhs[i] + _mm(_bf(y[i]), _bf(rhs[i])) for i in n]
    st16 = [_bf(st[i]) for i in n]
    v_new = [wu[i][:, dk:] - _nt(_bf(wu[i][:, :dk]), st16[i]) for i in n]
    o = [_nt(_bf(q32[i] * e_col[i]), st16[i]) + _mm(_bf(att[i]), _bf(v_new[i])) for i in n]
    st_new = [st[i] * jnp.exp(g_end[i]) + _tn(_bf(v_new[i]), _bf(k32[i] * jnp.exp(g_end[i] - g_col[i]))) for i in n]
    return o, st_new


def _dn_scan_kernel(*refs):
    n_in = 2 * SCAN_BATCH * 5
    of_ref, ob_ref, st_ref = refs[n_in:]

    @pl.when(pl.program_id(1) == 0)
    def _():
        st_ref[...] = jnp.zeros_like(st_ref)

    _, r64, c64 = _scan_iotas()
    (q, k, v, gc, gr), rev = _scan_chains(refs[:n_in], 5, DN_HEADS)
    load = lambda items: [ref[h] for ref, h in items]
    tables = {id(ref): ref[...] for ref, _ in gc}
    beta = [tables[id(ref)][:, d * DN_HEADS + h:d * DN_HEADS + h + 1] for (ref, h), d in zip(gc, rev)]
    g_col = [tables[id(ref)][:, (2 + d) * DN_HEADS + h:(2 + d) * DN_HEADS + h + 1] for (ref, h), d in zip(gc, rev)]
    g_row = [ref[0, (2 + d) * DN_HEADS + h:(2 + d) * DN_HEADS + h + 1, :] for (ref, h), d in zip(gr, rev)]
    st = [st_ref[d, bi, h] for d in range(2) for bi in range(SCAN_BATCH) for h in range(DN_HEADS)]
    o, st = _dn_chunks(load(q), load(k), load(v), beta, g_col, g_row, st, rev, r64, c64)
    _scan_store(o, st, of_ref, ob_ref, st_ref, DN_HEADS)


def _dn_scan(q, k, v, bgc, bgr):
    head_maps, col_maps, row_maps = _scan_in_maps(2), _scan_in_maps(3), _scan_in_maps(1)
    in_specs, args = [], []
    for i in range(2 * SCAN_BATCH):
        in_specs += [pl.BlockSpec((DN_HEADS, CHUNK, LANES), head_maps[i])] * 3
        in_specs += [pl.BlockSpec((CHUNK, LANES), col_maps[i]), pl.BlockSpec((1, 4 * DN_HEADS, CHUNK), row_maps[i])]
        args += [q, k, v, bgc, bgr]
    oshape = _head_shape(DN_HEADS, LANES)
    return pl.pallas_call(
        _dn_scan_kernel,
        grid=(BATCH // SCAN_BATCH, SCAN_STEPS),
        in_specs=in_specs,
        out_specs=_scan_out_specs(DN_HEADS, LANES),
        out_shape=[oshape, oshape],
        scratch_shapes=[pltpu.VMEM((2, SCAN_BATCH, DN_HEADS, LANES, LANES), F32)],
        compiler_params=_cparams("parallel", "arbitrary"),
        name="dn_scan",
    )(*args)


def _layer_norm(z, g, b):
    mu = jnp.mean(z, axis=1, keepdims=True)
    zc = z - mu
    var = jnp.mean(zc * zc, axis=1, keepdims=True)
    return zc * lax.rsqrt(var + LN_EPS) * g + b


def _post_kernel(*refs, n_heads):
    o_refs, refs = refs[:2 * CHUNKS_PER_TILE], refs[2 * CHUNKS_PER_TILE:]
    gate_ref, xl_ref, xc_ref, mod_ref, nw_ref, wo_ref, lg_ref, lb_ref, rw_ref, rb_ref = refs[:10]
    out_refs = refs[10:]
    x1_refs, (h2_ref, logit_ref) = out_refs[:-2], out_refs[-2:]
    x_rows = _stream_rows(xl_ref, xc_ref)
    groups = [slice(g * TILE // POST_GROUPS, (g + 1) * TILE // POST_GROUPS) for g in range(POST_GROUPS)]
    chunks_per_group = CHUNKS_PER_TILE // POST_GROUPS
    on = []
    for g, rows in enumerate(groups):
        parts = []
        for h in range(n_heads):
            o = jnp.concatenate([o_refs[c][h].astype(F32) + o_refs[CHUNKS_PER_TILE + c][h].astype(F32)
                                 for c in range(g * chunks_per_group, (g + 1) * chunks_per_group)], axis=0)
            parts.append(o * lax.rsqrt(jnp.mean(o * o, axis=1, keepdims=True) + RMS_EPS) * nw_ref[...])
        on.append(jnp.concatenate(parts, axis=1) * _silu(gate_ref[rows, :].astype(F32)))
    y = [_mm(_bf(a), wo_ref[...]) for a in on]
    z = [ALPHA * x_rows[rows] + mod_ref[0, 2:3, :] * yg for rows, yg in zip(groups, y)]
    x1 = [_layer_norm(zg, lg_ref[...], lb_ref[...]) for zg in z]
    h2 = [a * (1.0 + mod_ref[0, 4:5, :]) + mod_ref[0, 3:4, :] for a in x1]
    logits = [jnp.dot(a, rw_ref[...], precision=HIGHEST, preferred_element_type=F32) + rb_ref[...] for a in h2]
    _stream_store(x1_refs, jnp.concatenate(x1, axis=0))
    h2_ref[...] = _pack_halves(jnp.concatenate(h2, axis=0))
    logit_ref[...] = jnp.concatenate(logits, axis=0)


POST_GROUPS = 2


def _post(o_f, o_b, gate, xl, xc, mod, norm_w, w_out, ln_g, ln_b, rw, rb, with_ctx):
    n_heads, _, dv = o_f.shape
    n_tiles = ALL_TILES if with_ctx else LAT_TILES
    n_rows = n_tiles * TILE
    tok = lambda w: jax.ShapeDtypeStruct((n_rows, w), F32)
    return pl.pallas_call(
        functools.partial(_post_kernel, n_heads=n_heads),
        grid=(n_tiles,),
        in_specs=[pl.BlockSpec((n_heads, CHUNK, dv), lambda t, c=c: (0, _scan_out_block(t, c), 0))
                  for _ in range(2) for c in range(CHUNKS_PER_TILE)] + [_tok_spec(D_MODEL)] + _stream_specs() + [
            _mod_spec(), _full_spec(norm_w.shape), _full_spec(w_out.shape), _full_spec(ln_g.shape),
            _full_spec(ln_b.shape), _full_spec(rw.shape), _full_spec(rb.shape)],
        out_specs=_stream_specs()[:2 if with_ctx else 1] + [_tok_spec(PACKED_D), _tok_spec(LANES)],
        out_shape=_stream_shapes(with_ctx) + [tok(PACKED_D), tok(LANES)],
        compiler_params=_cparams("arbitrary"),
        name="post_mixer",
    )(*([o_f] * CHUNKS_PER_TILE + [o_b] * CHUNKS_PER_TILE), gate, xl, xc, mod, norm_w, w_out, ln_g, ln_b, rw, rb)


def _route_kernel(logit_ref, route_ref, slot_ref, tiles_ref, run_ref, start_ref, table_ref):
    sweep = pl.program_id(0)
    t = pl.program_id(1)
    lane = lax.broadcasted_iota(jnp.int32, (TILE, LANES), 1)
    lanef = lane.astype(F32)

    def add_counts(onehot):
        new_run = run_ref[0:1, :] + jnp.sum(onehot, axis=0, keepdims=True)
        run_ref[...] = jnp.broadcast_to(new_run, run_ref.shape)
        return new_run

    @pl.when(t == 0)
    def _():
        run_ref[...] = jnp.zeros_like(run_ref)

    @pl.when(sweep == 0)
    def _():
        neg = jnp.float32(-jnp.inf)
        l = jnp.where(lane < N_EXPERTS, logit_ref[...], neg)
        vals, picks = [], []
        onehot = jnp.zeros((TILE, LANES), F32)
        for _ in range(TOP_K):
            m = jnp.max(l, axis=1, keepdims=True)
            pick = jnp.min(jnp.where(l == m, lanef, float(LANES)), axis=1, keepdims=True)
            hot = lanef == pick
            l = jnp.where(hot, neg, l)
            onehot = onehot + jnp.where(hot, 1.0, 0.0)
            vals.append(m)
            picks.append(pick)
        es = [jnp.exp(v - vals[0]) for v in vals]
        denom = es[0] + es[1] + es[2] + es[3]
        out = jnp.zeros((TILE, LANES), F32)
        for kk in range(TOP_K):
            out = jnp.where(lane == kk, picks[kk], out)
            out = jnp.where(lane == ROUTE_WEIGHT + kk, es[kk] / denom, out)
        table_ref[t] = out
        new_run = add_counts(onehot)

        @pl.when(t == pl.num_programs(1) - 1)
        def _():
            lane8 = lax.broadcasted_iota(jnp.int32, (SUBLANES, LANES), 1)
            size = jnp.ceil(jnp.broadcast_to(new_run, (SUBLANES, LANES)) * (1.0 / EXPERT_TILE)) * EXPERT_TILE
            size = jnp.where(lane8 < N_EXPERTS, size, 0.0)
            ends = size
            sh = 1
            while sh < N_EXPERTS:
                ends = ends + jnp.where(lane8 >= sh, pltpu.roll(ends, sh, 1), 0.0)
                sh *= 2
            start_ref[...] = ends - size
            tile_start = lax.broadcasted_iota(jnp.int32, (TILE, 1), 0).astype(F32) * EXPERT_TILE
            ends_t = jnp.broadcast_to(ends[0:1, :], (TILE, LANES))
            group_end_t = jnp.broadcast_to((ends - size)[0:1, :] + new_run, (TILE, LANES))
            expert = jnp.sum(jnp.where(jnp.logical_and(lane < N_EXPERTS, ends_t <= tile_start), 1.0, 0.0),
                             axis=1, keepdims=True)
            expert = jnp.minimum(expert, N_EXPERTS - 1.0)
            group_end = jnp.sum(jnp.where(lanef == expert, group_end_t, 0.0), axis=1, keepdims=True)
            valid = jnp.clip(group_end - tile_start, 0.0, float(EXPERT_TILE))
            total = jnp.max(ends_t, axis=1, keepdims=True)
            own_end = jnp.sum(jnp.where(lanef == expert, ends_t, 0.0), axis=1, keepdims=True)
            nxt = jnp.sum(jnp.where(jnp.logical_and(lane < N_EXPERTS, ends_t <= own_end), 1.0, 0.0), axis=1, keepdims=True)
            nxt = jnp.where(own_end < total, nxt, -1.0)
            tiles_ref[...] = jnp.where(lane == 0, expert, jnp.where(lane == 1, valid, jnp.where(
                lane == 2, total * (1.0 / EXPERT_TILE), jnp.where(lane == 3, nxt, 0.0))))

    @pl.when(sweep == 1)
    def _():
        out = table_ref[t]
        hots = [lanef == out[:, kk:kk + 1] for kk in range(TOP_K)]
        onehot = jnp.zeros((TILE, LANES), F32)
        for hot in hots:
            onehot = onehot + jnp.where(hot, 1.0, 0.0)
        r = lax.broadcasted_iota(jnp.int32, (TILE, TILE), 0)
        c = lax.broadcasted_iota(jnp.int32, (TILE, TILE), 1)
        before = jnp.where(c < r, 1.0, 0.0).astype(BF16)
        slot_of = _mm(before, _bf(onehot)) + run_ref[0:1, :] + start_ref[0:1, :]
        for kk in range(TOP_K):
            slot = jnp.sum(jnp.where(hots[kk], slot_of, 0.0), axis=1, keepdims=True)
            out = jnp.where(lane == ROUTE_SLOT + kk, slot, out)
        route_ref[...] = out
        slot_ref[...] = out.T[ROUTE_WEIGHT:ROUTE_WEIGHT + SUBLANES, :].astype(jnp.int32)
        add_counts(onehot)


ROUTE_WEIGHT = SUBLANES
ROUTE_SLOT = SUBLANES + TOP_K


def _route(logits):
    n_tok = logits.shape[0]
    n_tiles = n_tok // TILE
    return pl.pallas_call(
        _route_kernel,
        grid=(2, n_tiles),
        in_specs=[pl.BlockSpec((TILE, LANES), lambda s, t: (t, 0))],
        out_specs=[pl.BlockSpec((TILE, LANES), lambda s, t: (s * t, 0)),
                   pl.BlockSpec((SUBLANES, TILE), lambda s, t: (0, s * t)),
                   pl.BlockSpec((TILE, LANES), lambda s, t: (0, 0))],
        out_shape=[jax.ShapeDtypeStruct(logits.shape, F32), jax.ShapeDtypeStruct((SUBLANES, n_tok), jnp.int32),
                   jax.ShapeDtypeStruct((TILE, LANES), F32)],
        scratch_shapes=[pltpu.VMEM((SUBLANES, LANES), F32), pltpu.VMEM((SUBLANES, LANES), F32),
                        pltpu.VMEM((n_tiles, TILE, LANES), F32)],
        compiler_params=_cparams("arbitrary", "arbitrary"),
        name="route",
    )(logits)


GATHER_ROWS = 16
GATHER_BUFS = 4
SCATTER_BUFS = 4


def _sc_gather(table, idx):
    n_rows = idx.shape[0]
    d = table.shape[1]
    info = plsc.get_sparse_core_info()
    n_workers = info.num_cores * info.num_subcores
    rows_per_worker = n_rows // n_workers
    n_chunks = rows_per_worker // GATHER_ROWS
    assert rows_per_worker * n_workers == n_rows and n_chunks * GATHER_ROWS == rows_per_worker
    assert n_chunks % GATHER_BUFS == 0
    mesh = plsc.VectorSubcoreMesh(core_axis_name="c", subcore_axis_name="s")
    scratch = ([pltpu.VMEM((GATHER_ROWS,), jnp.int32)] * GATHER_BUFS
               + [pltpu.VMEM((GATHER_ROWS, d), table.dtype)] * GATHER_BUFS
               + [pltpu.SemaphoreType.DMA] * (2 * GATHER_BUFS))

    def body(table_hbm, idx_hbm, out_hbm, *scr):
        idx_v, rows_v = scr[:GATHER_BUFS], scr[GATHER_BUFS:2 * GATHER_BUFS]
        gather_sem, write_sem = scr[2 * GATHER_BUFS:3 * GATHER_BUFS], scr[3 * GATHER_BUFS:]
        worker = lax.axis_index("s") * info.num_cores + lax.axis_index("c")
        base = worker * rows_per_worker

        @pl.loop(0, n_chunks, step=GATHER_BUFS)
        def _(c0):
            offs = [pl.multiple_of(base + (c0 + b) * GATHER_ROWS, SUBLANES) for b in range(GATHER_BUFS)]
            gathers = []
            for b in range(GATHER_BUFS):
                pltpu.sync_copy(idx_hbm.at[pl.ds(offs[b], GATHER_ROWS)], idx_v[b])
                gathers.append(pltpu.async_copy(table_hbm.at[idx_v[b]], rows_v[b], gather_sem[b]))
            writes = []
            for b in range(GATHER_BUFS):
                gathers[b].wait()
                writes.append(pltpu.async_copy(rows_v[b], out_hbm.at[pl.ds(offs[b], GATHER_ROWS)], write_sem[b]))
            for w in writes:
                w.wait()

    return pl.kernel(body, out_type=jax.ShapeDtypeStruct((n_rows, d), table.dtype), mesh=mesh,
                     scratch_types=scratch, name="sc_gather")(table, idx)


def _sc_scatter(src, idx, n_out):
    n, d = src.shape
    info = plsc.get_sparse_core_info()
    n_workers = info.num_cores * info.num_subcores
    rows_per_worker = n // n_workers
    n_chunks = rows_per_worker // GATHER_ROWS
    assert rows_per_worker * n_workers == n and n_chunks * GATHER_ROWS == rows_per_worker
    assert n_chunks % SCATTER_BUFS == 0
    mesh = plsc.VectorSubcoreMesh(core_axis_name="c", subcore_axis_name="s")
    scratch = ([pltpu.VMEM((GATHER_ROWS,), jnp.int32)] * (SCATTER_BUFS * TOP_K)
               + [pltpu.VMEM((GATHER_ROWS, d), src.dtype)] * SCATTER_BUFS + [pltpu.SemaphoreType.DMA] * SCATTER_BUFS)

    def body(src_hbm, idx_hbm, out_hbm, *scr):
        idx_v = scr[:SCATTER_BUFS * TOP_K]
        rows_v = scr[SCATTER_BUFS * TOP_K:SCATTER_BUFS * TOP_K + SCATTER_BUFS]
        sem = scr[SCATTER_BUFS * TOP_K + SCATTER_BUFS:]
        worker = lax.axis_index("s") * info.num_cores + lax.axis_index("c")
        base = worker * rows_per_worker

        @pl.loop(0, n_chunks, step=SCATTER_BUFS)
        def _(c0):
            copies = []
            for b in range(SCATTER_BUFS):
                off = pl.multiple_of(base + (c0 + b) * GATHER_ROWS, SUBLANES)
                pltpu.sync_copy(src_hbm.at[pl.ds(off, GATHER_ROWS)], rows_v[b])
                for kk in range(TOP_K):
                    pltpu.sync_copy(idx_hbm.at[pl.ds(pl.multiple_of(kk * n + off, SUBLANES), GATHER_ROWS)],
                                    idx_v[b * TOP_K + kk])
                for kk in range(TOP_K):
                    copies.append(pltpu.async_copy(rows_v[b], out_hbm.at[idx_v[b * TOP_K + kk]], sem[b]))
            for c in copies:
                c.wait()

    return pl.kernel(body, out_type=jax.ShapeDtypeStruct((n_out, d), src.dtype), mesh=mesh,
                     scratch_types=scratch, name="sc_scatter")(src, idx)


PAIR_BLOCK = 2 * LANES


def _expert_kernel(te_ref, tv_ref, nt_ref, nx_ref, xs_ref, w1_hbm, b1_ref, w2_hbm, b2_ref, ys_ref,
                   w1p_ref, w2b_ref, w1_buf, w2_buf, sem, slot_ref, *, layer):
    i = pl.program_id(0)
    valid = i < nt_ref[0]
    new_expert = jnp.logical_or(i == 0, te_ref[i] != te_ref[jnp.maximum(i - 1, 0)])

    def weight_copies(expert, slot):
        return (pltpu.make_async_copy(w1_hbm.at[layer, expert], w1_buf.at[slot], sem.at[slot, 0]),
                pltpu.make_async_copy(w2_hbm.at[layer, expert], w2_buf.at[slot], sem.at[slot, 1]))

    @pl.when(i == 0)
    def _():
        slot_ref[0] = 0
        for cp in weight_copies(te_ref[0], 0):
            cp.start()

    @pl.when(jnp.logical_and(valid, new_expert))
    def _():
        slot = slot_ref[0]
        for cp in weight_copies(te_ref[i], slot):
            cp.wait()

        @pl.when(nx_ref[i] >= 0)
        def _():
            for cp in weight_copies(nx_ref[i], 1 - slot):
                cp.start()

        slot_ref[0] = 1 - slot
        r = lax.broadcasted_iota(jnp.int32, (PAIR_BLOCK, PAIR_BLOCK), 0)
        c = lax.broadcasted_iota(jnp.int32, (PAIR_BLOCK, PAIR_BLOCK), 1)
        src = jnp.where(c < LANES, 2 * c, 2 * (c - LANES) + 1)
        perm = jnp.where(r == src, 1.0, 0.0).astype(BF16)
        for b in range(2 * D_FF // PAIR_BLOCK):
            cols = slice(b * PAIR_BLOCK, (b + 1) * PAIR_BLOCK)
            w1p_ref[:, cols] = _bf(_mm(_bf(w1_buf[slot, :, cols]), perm))
        w2b_ref[...] = _bf(w2_buf[slot])

    def ffn(n_rows):
        rowi = lax.broadcasted_iota(jnp.int32, (n_rows, D_MODEL), 0)
        x = _bf(jnp.where(rowi < tv_ref[i], _unpack_halves(xs_ref[0:n_rows, :]), 0.0))
        u = _mm(x, w1p_ref[...]) + b1_ref[0]
        parts = []
        for b in range(2 * D_FF // PAIR_BLOCK):
            glu = jnp.minimum(u[:, b * PAIR_BLOCK:b * PAIR_BLOCK + LANES], SWIGLU_LIMIT)
            lin = jnp.clip(u[:, b * PAIR_BLOCK + LANES:(b + 1) * PAIR_BLOCK], -SWIGLU_LIMIT, SWIGLU_LIMIT)
            parts.append(glu * _sigmoid(SWIGLU_ALPHA * glu) * (lin + 1.0))
        act = jnp.concatenate(parts, axis=1)
        ys_ref[0:n_rows, :] = _pack_halves(_mm(_bf(act), w2b_ref[...]) + b2_ref[0])

    half = EXPERT_TILE // 2

    @pl.when(jnp.logical_and(valid, tv_ref[i] > half))
    def _():
        ffn(EXPERT_TILE)

    @pl.when(jnp.logical_and(valid, tv_ref[i] <= half))
    def _():
        ffn(half)


def _experts(layer, tile_expert, tile_valid, n_tiles, next_expert, xs, w1, b1p, w2, b2):
    tile_map = lambda i, te, tv, nt, nx: (jnp.minimum(i, nt[0] - 1), 0)
    wmap = lambda i, te, tv, nt, nx: (te[i], 0, 0)
    grid_spec = pltpu.PrefetchScalarGridSpec(
        num_scalar_prefetch=4,
        grid=(N_SLOT_TILES,),
        in_specs=[pl.BlockSpec((EXPERT_TILE, PACKED_D), tile_map),
                  pl.BlockSpec(memory_space=pl.ANY), pl.BlockSpec((1, 1, 2 * D_FF), wmap),
                  pl.BlockSpec(memory_space=pl.ANY), pl.BlockSpec((1, 1, D_MODEL), wmap)],
        out_specs=pl.BlockSpec((EXPERT_TILE, PACKED_D), tile_map),
        scratch_shapes=[pltpu.VMEM((D_MODEL, 2 * D_FF), BF16), pltpu.VMEM((D_FF, D_MODEL), BF16),
                        pltpu.VMEM((2, D_MODEL, 2 * D_FF), F32), pltpu.VMEM((2, D_FF, D_MODEL), F32),
                        pltpu.SemaphoreType.DMA((2, 2)), pltpu.SMEM((1,), jnp.int32)],
    )
    return pl.pallas_call(
        functools.partial(_expert_kernel, layer=layer),
        grid_spec=grid_spec,
        out_shape=jax.ShapeDtypeStruct((N_SLOTS, PACKED_D), F32),
        compiler_params=pltpu.CompilerParams(dimension_semantics=("arbitrary",), vmem_limit_bytes=EXPERT_VMEM_LIMIT),
        name="experts",
    )(tile_expert, tile_valid, n_tiles, next_expert, xs, w1, b1p, w2, b2)


def _combine_kernel(y_ref, route_ref, x1l_ref, x1c_ref, mod_ref, lg_ref, lb_ref, *x2_refs):
    route = route_ref[...]
    lane = lax.broadcasted_iota(jnp.int32, route.shape, 1)
    f = None
    for kk in range(TOP_K):
        term = _lane_pick(route, lane, ROUTE_WEIGHT + kk) * _unpack_halves(y_ref[kk])
        f = term if f is None else f + term
    z = ALPHA * _stream_rows(x1l_ref, x1c_ref) + mod_ref[0, 5:6, :] * f
    _stream_store(x2_refs, _layer_norm(z, lg_ref[...], lb_ref[...]))


def _combine(y_tok, route, x1l, x1c, mod, ln_g, ln_b, with_ctx):
    n_tiles = ALL_TILES if with_ctx else LAT_TILES
    return pl.pallas_call(
        _combine_kernel,
        grid=(n_tiles,),
        in_specs=[pl.BlockSpec((TOP_K, TILE, PACKED_D), lambda t: (0, t, 0)), _tok_spec(LANES)] + _stream_specs() + [
            _mod_spec(), _full_spec(ln_g.shape), _full_spec(ln_b.shape)],
        out_specs=_stream_specs()[:2 if with_ctx else 1],
        out_shape=_stream_shapes(with_ctx),
        compiler_params=_cparams("arbitrary"),
        name="combine",
    )(y_tok, route, x1l, x1c, mod, ln_g, ln_b)


def _moe(layer, h2, logits, x1l, x1c, mod, ln_g, ln_b, w1, b1p, w2, b2, with_ctx):
    n_tok = h2.shape[0]
    assert N_SLOT_TILES <= TILE
    route, slot_rows, tiles = _route(logits)
    slot_by_choice = slot_rows[TOP_K:2 * TOP_K].reshape(-1)
    tiles = tiles[:N_SLOT_TILES, :SUBLANES].astype(jnp.int32)
    tile_expert, tile_valid, n_tiles, next_expert = tiles[:, 0], tiles[:, 1], tiles[:1, 2], tiles[:, 3]
    xs = _sc_scatter(h2, slot_by_choice, N_SLOTS)
    ys = _experts(layer, tile_expert, tile_valid, n_tiles, next_expert, xs, w1, b1p, w2, b2)
    y_tok = _sc_gather(ys, slot_by_choice).reshape(TOP_K, n_tok, PACKED_D)
    return _combine(y_tok, route, x1l, x1c, mod, ln_g, ln_b, with_ctx)


def _lat_to_col_major(lat):
    return lat.reshape(BATCH, SEQ // GRID_W, GRID_W, D_MODEL).transpose(0, 2, 1, 3).reshape(N_LAT, D_MODEL)


def _lat_to_row_major(lat):
    return lat.reshape(BATCH, GRID_W, SEQ // GRID_W, D_MODEL).transpose(0, 2, 1, 3).reshape(N_LAT, D_MODEL)


def kernel(x, c, ctx, c_ctx, ada_w, ada_b, ln_g, ln_b, dn_w_in, dn_conv_w, dn_a_log, dn_dt_bias, dn_norm_w, dn_w_out,
           hg_w_in, hg_lower_bound, hg_norm_w, hg_w_out, gla_w_in, gla_gk_w, gla_gk_b, gla_norm_w, gla_w_out,
           router_w, router_b, exp_w1, exp_b1, exp_w2, exp_b2):
    xl = x.reshape(N_LAT, D_MODEL)
    xc = ctx.reshape(N_CTX, D_MODEL)
    cc = jnp.zeros((MOD_ROWS, D_MODEL), F32).at[:BATCH].set(c).at[BATCH].set(c_ctx)
    mods = _ada_all(cc, ada_w, ada_b).reshape(DEPTH, MOD_ROWS, ADA_CHUNKS, D_MODEL)
    lb_soft = jax.nn.softmax(hg_lower_bound.astype(F32), axis=0)
    lower_bounds = jnp.cumsum(lb_soft, axis=0) - lb_soft[0]

    for i in range(DEPTH):
        last = i == DEPTH - 1
        kind, j = i % 3, i // 3
        mod = mods[i]
        col_major = i % 2 == 1
        if col_major:
            xl = _lat_to_col_major(xl)

        if kind == 0:
            nq = 2 * D_MODEL + 2 * D_MODEL
            w = dn_w_in[j]
            w_ba = jnp.zeros((D_MODEL, LANES), F32).at[:, :4 * DN_HEADS].set(w[:, nq:])
            a_neg = jnp.zeros((LANES,), F32).at[2 * DN_HEADS:4 * DN_HEADS].set(-jnp.exp(dn_a_log[j].astype(F32)).reshape(-1))
            dtb = jnp.zeros((LANES,), F32).at[2 * DN_HEADS:4 * DN_HEADS].set(dn_dt_bias[j].astype(F32).reshape(-1))
            nb = 4 * DN_HEADS
            p_qkv, gate, bgc, bgr = _dn_proj(xl, xc, mod, _bf(w), _bf(w_ba), _bf(w[:, nq:].T),
                                            a_neg.reshape(1, LANES), dtb.reshape(1, LANES),
                                            a_neg[:nb].reshape(nb, 1), dtb[:nb].reshape(nb, 1))
            cw = jnp.zeros((SUBLANES, DN_QKV), F32).at[:SHORT_CONV].set(dn_conv_w[j])
            q, k, v = _dn_conv(p_qkv, cw)
            o_f, o_b = _dn_scan(q, k, v, bgc, bgr)
            norm_w, w_out = dn_norm_w[j], dn_w_out[j]
        elif kind == 1:
            q, v, gate, k_f, g_f, k_b, g_b = _hg_proj(xl, xc, mod, _bf(hg_w_in[j]), lower_bounds[i].reshape(1, D_MODEL))
            o_f, o_b = _gla_scan(q, k_f, k_b, v, g_f, g_b)
            norm_w, w_out = hg_norm_w[j], hg_w_out[j]
        else:
            kd = GLA_HEADS * GLA_DK
            n_main = 2 * kd + 2 * D_MODEL
            w = gla_w_in[j]
            w_lr = jnp.zeros((D_MODEL, 2 * LANES), F32)
            w_lr = w_lr.at[:, :GLA_RANK].set(w[:, n_main:n_main + GLA_RANK])
            w_lr = w_lr.at[:, LANES:LANES + GLA_RANK].set(w[:, n_main + GLA_RANK:])
            gk_w = jnp.zeros((2, LANES, kd), F32).at[:, :GLA_RANK].set(gla_gk_w[j])
            q, k, v, gate, g_f, g_b = _gla_proj(xl, xc, mod, _bf(w), _bf(w_lr), _bf(gk_w),
                                                gla_gk_b[j].reshape(2, 1, kd))
            o_f, o_b = _gla_scan(q, k, k, v, g_f, g_b)
            norm_w, w_out = gla_norm_w[j], gla_w_out[j]

        with_ctx = not last
        rw = jnp.zeros((D_MODEL, LANES), F32).at[:, :N_EXPERTS].set(router_w[i])
        rb = jnp.zeros((1, LANES), F32).at[0, :N_EXPERTS].set(router_b[i])
        *x1, h2, logits = _post(o_f, o_b, gate, xl, xc, mod, norm_w.reshape(1, -1), _bf(w_out),
                                ln_g[i, 0].reshape(1, D_MODEL), ln_b[i, 0].reshape(1, D_MODEL), rw, rb, with_ctx)
        b1p = exp_b1[i].reshape(N_EXPERTS, 2 * D_FF // PAIR_BLOCK, LANES, 2).transpose(0, 1, 3, 2)
        x2 = _moe(i, h2, logits, x1[0], x1[1] if with_ctx else xc, mod,
                  ln_g[i, 1].reshape(1, D_MODEL), ln_b[i, 1].reshape(1, D_MODEL),
                  exp_w1, b1p.reshape(N_EXPERTS, 1, 2 * D_FF), exp_w2, exp_b2[i][:, None, :], with_ctx)
        xl = _lat_to_row_major(x2[0]) if col_major else x2[0]
        if with_ctx:
            xc = x2[1]
    return xl.reshape(BATCH, SEQ, D_MODEL)
```

```python
import functools

import jax
import jax.numpy as jnp
from jax import lax
from jax.experimental import pallas as pl
from jax.experimental.pallas import tpu as pltpu
from jax.experimental.pallas import tpu_sc as plsc

F32 = jnp.float32
BF16 = jnp.bfloat16
HIGHEST = lax.Precision.HIGHEST

D_MODEL = 1024
BATCH = 8
SEQ = 2048
CTX_LEN = 256
DEPTH = 4
GRID_W = 64
CHUNK = 64
ADA_CHUNKS = 6
N_LAT = BATCH * SEQ
N_CTX = BATCH * CTX_LEN
N_TOK = N_LAT + N_CTX
TILE = 256
LAT_TILES = N_LAT // TILE
ALL_TILES = N_TOK // TILE
TILES_PER_SEQ = SEQ // TILE
MOD_ROWS = 16
LANES = 128
SUBLANES = 8

N_EXPERTS = 32
TOP_K = 4
D_FF = D_MODEL
SWIGLU_LIMIT = 7.0
SWIGLU_ALPHA = 1.702
EXPERT_TILE = 512
N_SLOT_TILES = N_TOK * TOP_K // EXPERT_TILE + N_EXPERTS
N_SLOTS = N_SLOT_TILES * EXPERT_TILE

LN_EPS = 1e-5
RMS_EPS = 1e-6
L2_EPS = 1e-6
ALPHA = (2.0 * DEPTH) ** 0.25

VMEM_LIMIT = 48 * 1024 * 1024
EXPERT_VMEM_LIMIT = (2 * 4 * (D_MODEL * 2 * D_FF + D_FF * D_MODEL) + 2 * (D_MODEL * 2 * D_FF + D_FF * D_MODEL)
                     + 4 * 4 * EXPERT_TILE * D_MODEL + 4 * EXPERT_TILE * (2 * D_FF + 2 * D_MODEL) + (4 << 20))


def _cparams(*sem):
    return pltpu.CompilerParams(dimension_semantics=sem, vmem_limit_bytes=VMEM_LIMIT)


def _sigmoid(x):
    return 1.0 / (1.0 + jnp.exp(-x))


def _silu(x):
    return x * _sigmoid(x)


def _softplus(x):
    return jnp.maximum(x, 0.0) + jnp.log(1.0 + jnp.exp(-jnp.abs(x)))


def _log_sigmoid(x):
    return -_softplus(-x)


def _nt(a, b):
    return lax.dot_general(a, b, (((1,), (1,)), ((), ())), preferred_element_type=F32)


def _tn(a, b):
    return lax.dot_general(a, b, (((0,), (0,)), ((), ())), preferred_element_type=F32)


def _mm(a, b):
    return jnp.dot(a, b, preferred_element_type=F32)


def _bf(x):
    return x.astype(BF16)


PACKED_D = D_MODEL // 2
HIGH_HALF = 0xFFFF0000


def _pack_halves(x):
    m = x.shape[1] // 2
    lo = pltpu.bitcast(x[:, :m].astype(BF16).astype(F32), jnp.uint32) >> 16
    hi = pltpu.bitcast(x[:, m:].astype(BF16).astype(F32), jnp.uint32) & jnp.uint32(HIGH_HALF)
    return pltpu.bitcast(hi | lo, F32)


def _unpack_halves(p):
    u = pltpu.bitcast(p, jnp.uint32)
    return jnp.concatenate([pltpu.bitcast(u << 16, F32), pltpu.bitcast(u & jnp.uint32(HIGH_HALF), F32)], axis=1)


def _mod_row(t):
    return jnp.minimum(t // TILES_PER_SEQ, BATCH)


ADA_NBLK = 1536


def _ada_kernel(c_ref, w_ref, b_ref, o_ref):
    s = _silu(c_ref[...])
    o_ref[0] = jnp.dot(s, w_ref[0], precision=HIGHEST, preferred_element_type=F32) + b_ref[0]


def _ada_all(cc, ada_w, ada_b):
    n = ADA_CHUNKS * D_MODEL
    return pl.pallas_call(
        _ada_kernel,
        grid=(DEPTH, n // ADA_NBLK),
        in_specs=[
            pl.BlockSpec((MOD_ROWS, D_MODEL), lambda i, j: (0, 0)),
            pl.BlockSpec((1, D_MODEL, ADA_NBLK), lambda i, j: (i, 0, j)),
            pl.BlockSpec((1, 1, ADA_NBLK), lambda i, j: (i, 0, j)),
        ],
        out_specs=pl.BlockSpec((1, MOD_ROWS, ADA_NBLK), lambda i, j: (i, 0, j)),
        out_shape=jax.ShapeDtypeStruct((DEPTH, MOD_ROWS, n), F32),
        compiler_params=_cparams("parallel", "parallel"),
        name="ada",
    )(cc, ada_w, ada_b.reshape(DEPTH, 1, n))


def _stream_rows(xl_ref, xc_ref, first_tile=0):
    return jnp.where(pl.program_id(0) + first_tile < LAT_TILES, xl_ref[...], xc_ref[...])


def _stream_specs(first_tile=0):
    return [pl.BlockSpec((TILE, D_MODEL), lambda t: (jnp.minimum(t + first_tile, LAT_TILES - 1), 0)),
            pl.BlockSpec((TILE, D_MODEL), lambda t: (jnp.maximum(t + first_tile - LAT_TILES, 0), 0))]


def _stream_shapes(with_ctx):
    shapes = [jax.ShapeDtypeStruct((N_LAT, D_MODEL), F32)]
    return shapes + [jax.ShapeDtypeStruct((N_CTX, D_MODEL), F32)] if with_ctx else shapes


def _stream_store(refs, value, first_tile=0):
    is_lat = pl.program_id(0) + first_tile < LAT_TILES

    @pl.when(is_lat)
    def _():
        refs[0][...] = value

    if len(refs) > 1:
        @pl.when(jnp.logical_not(is_lat))
        def _():
            refs[1][...] = value


def _modulated(xl_ref, xc_ref, mod_ref, shift, scale):
    return _stream_rows(xl_ref, xc_ref) * (1.0 + mod_ref[0, scale:scale + 1, :]) + mod_ref[0, shift:shift + 1, :]


def _store_heads(ref, val, width):
    for h in range(val.shape[1] // width):
        ref[h] = val[:, h * width:(h + 1) * width].astype(ref.dtype)


def _hg_proj_kernel(xl_ref, xc_ref, mod_ref, w_ref, lb_ref, q_ref, v_ref, gate_ref, kf_ref, gf_ref, kb_ref, gb_ref):
    h = _bf(_modulated(xl_ref, xc_ref, mod_ref, 0, 1))
    d = D_MODEL
    lb = lb_ref[...]
    for k_ref, g_ref, lo in ((kf_ref, gf_ref, 3 * d), (kb_ref, gb_ref, 4 * d)):
        s = _sigmoid(_mm(h, w_ref[:, lo:lo + d]))
        _store_heads(k_ref, (1.0 - lb) * (1.0 - s), LANES)
        _store_heads(g_ref, jnp.log(lb + (1.0 - lb) * s), LANES)
    _store_heads(q_ref, _silu(_mm(h, w_ref[:, 0:d])), LANES)
    _store_heads(v_ref, _mm(h, w_ref[:, d:2 * d]), LANES)
    gate_ref[...] = _bf(_mm(h, w_ref[:, 2 * d:3 * d]))


def _head_spec(n_heads, width):
    return pl.BlockSpec((n_heads, TILE, width), lambda t: (0, t, 0))


def _head_shape(n_heads, width, dtype=None):
    return jax.ShapeDtypeStruct((n_heads, N_TOK, width), BF16 if dtype is None else dtype)


def _tok_shape(width):
    return jax.ShapeDtypeStruct((N_TOK, width), BF16)


def _tok_spec(width, first_tile=0):
    return pl.BlockSpec((TILE, width), lambda t: (t + first_tile, 0))


def _mod_spec(first_tile=0):
    return pl.BlockSpec((1, ADA_CHUNKS, D_MODEL), lambda t: (_mod_row(t + first_tile), 0, 0))


def _full_spec(shape):
    return pl.BlockSpec(shape, lambda t: (0,) * len(shape))


HG_HEADS = 8


def _hg_proj(xl, xc, mod, w_in, lb):
    n = w_in.shape[1]
    hs = _head_shape(HG_HEADS, LANES)
    hg = _head_shape(HG_HEADS, LANES, F32)
    spec = _head_spec(HG_HEADS, LANES)
    return pl.pallas_call(
        _hg_proj_kernel,
        grid=(ALL_TILES,),
        in_specs=_stream_specs() + [_mod_spec(), _full_spec((D_MODEL, n)), _full_spec((1, D_MODEL))],
        out_specs=[spec, spec, _tok_spec(D_MODEL), spec, spec, spec, spec],
        out_shape=[hs, hs, _tok_shape(D_MODEL), hs, hg, hs, hg],
        compiler_params=_cparams("parallel"),
        name="hg_proj",
    )(xl, xc, mod, w_in, lb)


GLA_HEADS = 4
GLA_DK = 128
GLA_DV = 256
GLA_RANK = 16
GLA_GATE_NORM = 16.0


def _gla_proj_kernel(xl_ref, xc_ref, mod_ref, w_ref, wlr_ref, gkw_ref, gkb_ref, q_ref, k_ref, v_ref, gate_ref, gf_ref, gb_ref):
    h = _bf(_modulated(xl_ref, xc_ref, mod_ref, 0, 1))
    kd = GLA_HEADS * GLA_DK
    vd = GLA_HEADS * GLA_DV
    for z, g_ref in ((0, gf_ref), (1, gb_ref)):
        lr = _mm(h, wlr_ref[:, z * LANES:(z + 1) * LANES])
        gk = _mm(_bf(lr), gkw_ref[z]) + gkb_ref[z]
        _store_heads(g_ref, _log_sigmoid(gk) / GLA_GATE_NORM, GLA_DK)
    _store_heads(q_ref, _mm(h, w_ref[:, 0:kd]) * GLA_DK ** -0.5, GLA_DK)
    _store_heads(k_ref, _mm(h, w_ref[:, kd:2 * kd]), GLA_DK)
    _store_heads(v_ref, _mm(h, w_ref[:, 2 * kd:2 * kd + vd]), GLA_DV)
    gate_ref[...] = _bf(_mm(h, w_ref[:, 2 * kd + vd:2 * kd + 2 * vd]))


def _gla_proj(xl, xc, mod, w_main, w_lr, gk_w, gk_b):
    kd = GLA_HEADS * GLA_DK
    hk = _head_shape(GLA_HEADS, GLA_DK)
    return pl.pallas_call(
        _gla_proj_kernel,
        grid=(ALL_TILES,),
        in_specs=_stream_specs() + [_mod_spec(), _full_spec(w_main.shape), _full_spec(w_lr.shape),
                  _full_spec(gk_w.shape), _full_spec(gk_b.shape)],
        out_specs=[_head_spec(GLA_HEADS, GLA_DK), _head_spec(GLA_HEADS, GLA_DK), _head_spec(GLA_HEADS, GLA_DV),
                   _tok_spec(D_MODEL), _head_spec(GLA_HEADS, GLA_DK), _head_spec(GLA_HEADS, GLA_DK)],
        out_shape=[hk, hk, _head_shape(GLA_HEADS, GLA_DV), _tok_shape(D_MODEL),
                   _head_shape(GLA_HEADS, GLA_DK, F32), _head_shape(GLA_HEADS, GLA_DK, F32)],
        compiler_params=_cparams("parallel"),
        name="gla_proj",
    )(xl, xc, mod, w_main, w_lr, gk_w, gk_b)


DN_HEADS = 8
DN_QKV = 3 * D_MODEL
SHORT_CONV = 5
CHUNKS_PER_TILE = TILE // CHUNK


def _seg_scan(x, pos, axis, reverse):
    n = x.shape[axis]
    sh = 1
    while sh < CHUNK:
        if reverse:
            x = x + jnp.where(pos < CHUNK - sh, pltpu.roll(x, n - sh, axis), 0.0)
        else:
            x = x + jnp.where(pos >= sh, pltpu.roll(x, sh, axis), 0.0)
        sh *= 2
    return x


def _dn_gates(p, a_neg, dtb, idx, pos, axis):
    beta = _sigmoid(p)
    g = a_neg * _softplus(p + dtb)
    gp = _seg_scan(g, pos, axis, False)
    gs = _seg_scan(g, pos, axis, True)
    return jnp.where(idx < 2 * DN_HEADS, beta, jnp.where(idx < 3 * DN_HEADS, gp, gs))


def _dn_proj_kernel(xl_ref, xc_ref, mod_ref, w_ref, wba_ref, wbat_ref, ac_ref, dc_ref, ar_ref, dr_ref,
                    qkv_ref, gate_ref, bgc_ref, bgr_ref):
    h = _bf(_modulated(xl_ref, xc_ref, mod_ref, 0, 1))
    pc = _mm(h, wba_ref[...])
    lane = lax.broadcasted_iota(jnp.int32, pc.shape, 1)
    rowpos = lax.broadcasted_iota(jnp.int32, pc.shape, 0) % CHUNK
    bgc_ref[...] = _dn_gates(pc, ac_ref[...], dc_ref[...], lane, rowpos, 0)
    pr = _nt(wbat_ref[...], h)
    sub = lax.broadcasted_iota(jnp.int32, pr.shape, 0)
    lanepos = lax.broadcasted_iota(jnp.int32, pr.shape, 1) % CHUNK
    gr = _dn_gates(pr, ar_ref[...], dr_ref[...], sub, lanepos, 1)
    for c in range(CHUNKS_PER_TILE):
        bgr_ref[c] = gr[:, c * CHUNK:(c + 1) * CHUNK]
    qkv_ref[...] = _bf(_mm(h, w_ref[:, 0:DN_QKV]))
    gate_ref[...] = _bf(_mm(h, w_ref[:, DN_QKV:DN_QKV + D_MODEL]))


def _dn_proj(xl, xc, mod, w_main, w_ba, w_bat, a_col, d_col, a_row, d_row):
    nb = 4 * DN_HEADS
    return pl.pallas_call(
        _dn_proj_kernel,
        grid=(ALL_TILES,),
        in_specs=_stream_specs() + [_mod_spec(), _full_spec(w_main.shape), _full_spec(w_ba.shape),
                  _full_spec(w_bat.shape), _full_spec(a_col.shape), _full_spec(d_col.shape),
                  _full_spec(a_row.shape), _full_spec(d_row.shape)],
        out_specs=[_tok_spec(DN_QKV), _tok_spec(D_MODEL), _tok_spec(LANES),
                   pl.BlockSpec((CHUNKS_PER_TILE, nb, CHUNK), lambda t: (t, 0, 0))],
        out_shape=[_tok_shape(DN_QKV), _tok_shape(D_MODEL), jax.ShapeDtypeStruct((N_TOK, LANES), F32),
                   jax.ShapeDtypeStruct((N_TOK // CHUNK, nb, CHUNK), F32)],
        compiler_params=_cparams("parallel"),
        name="dn_proj",
    )(xl, xc, mod, w_main, w_ba, w_bat, a_col, d_col, a_row, d_row)


HALO = 2 * SUBLANES


def _dn_conv_kernel(pm_ref, pp_ref, pn_ref, cw_ref, q_ref, k_ref, v_ref):
    t = pl.program_id(0)
    is_lat = t < LAT_TILES
    first = jnp.logical_or(jnp.logical_not(is_lat), t % TILES_PER_SEQ == 0)
    last = jnp.logical_or(jnp.logical_not(is_lat), t % TILES_PER_SEQ == TILES_PER_SEQ - 1)
    keep_prev = jnp.where(first, 0.0, 1.0)
    keep_next = jnp.where(last, 0.0, 1.0)
    n_ext = TILE + 2 * HALO
    pad = SHORT_CONV // 2
    for s in range(DN_QKV // LANES):
        cols = slice(s * LANES, (s + 1) * LANES)
        ext = jnp.concatenate([pp_ref[:, cols].astype(F32) * keep_prev, pm_ref[:, cols].astype(F32),
                               pn_ref[:, cols].astype(F32) * keep_next], axis=0)
        acc = None
        for kk in range(SHORT_CONV):
            off = HALO - pad + kk
            tap = pltpu.roll(ext, n_ext - off, 0)[0:TILE] * cw_ref[kk:kk + 1, cols]
            acc = tap if acc is None else acc + tap
        a = _silu(acc)
        head = s % DN_HEADS
        if s < 2 * DN_HEADS:
            a = a * lax.rsqrt(jnp.sum(a * a, axis=1, keepdims=True) + L2_EPS)
            if s < DN_HEADS:
                q_ref[head] = _bf(a * LANES ** -0.5)
            else:
                k_ref[head] = _bf(a)
        else:
            v_ref[head] = _bf(a)


def _dn_conv(p_qkv, conv_w):
    rows8 = TILE // HALO
    last8 = N_TOK // HALO - 1
    hs = _head_shape(DN_HEADS, LANES)
    return pl.pallas_call(
        _dn_conv_kernel,
        grid=(ALL_TILES,),
        in_specs=[_tok_spec(DN_QKV),
                  pl.BlockSpec((HALO, DN_QKV), lambda t: (jnp.maximum(t * rows8 - 1, 0), 0)),
                  pl.BlockSpec((HALO, DN_QKV), lambda t: (jnp.minimum((t + 1) * rows8, last8), 0)),
                  _full_spec(conv_w.shape)],
        out_specs=[_head_spec(DN_HEADS, LANES)] * 3,
        out_shape=[hs, hs, hs],
        compiler_params=_cparams("parallel"),
        name="dn_conv",
    )(p_qkv, p_qkv, p_qkv, conv_w)


LOG2_E = 1.4426950408889634
SIGN_BIT = 0x80000000


def _chunk_cumsum(x, ones_tri):
    hi = _bf(x)
    rest = x - hi.astype(F32)
    mid = _bf(rest)
    lo = _bf(rest - mid.astype(F32))
    return _mm(ones_tri, hi) + _mm(ones_tri, mid) + _mm(ones_tri, lo)


def _neg_abs(x):
    return pltpu.bitcast(pltpu.bitcast(x, jnp.uint32) | jnp.uint32(SIGN_BIT), F32)


def _gla_chunks(q, k, v, g, st, rev, row, r64, c64):
    n = range(len(q))
    q32 = [q[i].astype(F32) for i in n]
    k32 = [k[i].astype(F32) for i in n]
    ones_tri = {False: jnp.where(r64 >= c64, 1.0, 0.0).astype(BF16), True: jnp.where(r64 <= c64, 1.0, 0.0).astype(BF16)}
    cum = [_chunk_cumsum(g[i] * LOG2_E, ones_tri[rev[i]]) for i in n]
    g_end = [cum[i][0:1, :] if rev[i] else cum[i][CHUNK - 1:CHUNK, :] for i in n]
    att = [jnp.where(r64 == c64, _nt(_bf(q[i]), _bf(k[i])), 0.0) for i in n]
    edge = list(cum)
    s = 1
    while s < CHUNK:
        late = (row & s) != 0
        early = jnp.logical_not(late)
        rlate = (r64 & s) != 0
        clate = (c64 & s) != 0
        pair = {False: jnp.logical_and(rlate, jnp.logical_not(clate)), True: jnp.logical_and(clate, jnp.logical_not(rlate))}
        if 2 * s < CHUNK:
            shift = (2 * s).bit_length() - 1
            same = (r64 >> shift) == (c64 >> shift)
            pair = {d: jnp.logical_and(p, same) for d, p in pair.items()}
        for i in n:
            if rev[i]:
                cut = jnp.where(late, edge[i], pltpu.roll(edge[i], CHUNK - s, 0))
            else:
                cut = jnp.where(late, pltpu.roll(edge[i], s, 0), edge[i])
            e = jnp.exp2(_neg_abs(cum[i] - cut))
            q_side = early if rev[i] else late
            x16 = _bf(jnp.where(q_side, q32[i], k32[i]) * e)
            att[i] = att[i] + jnp.where(pair[rev[i]], _nt(x16, x16), 0.0)
            if rev[i]:
                edge[i] = jnp.where(late, pltpu.roll(edge[i], s, 0), edge[i])
            else:
                edge[i] = jnp.where(late, edge[i], pltpu.roll(edge[i], CHUNK - s, 0))
        s *= 2
    st16 = [_bf(st[i]) for i in n]
    o = [_mm(_bf(att[i]), _bf(v[i])) + _nt(_bf(q32[i] * jnp.exp2(cum[i])), st16[i]) for i in n]
    st_new = [st[i] * jnp.exp2(g_end[i]) + _tn(_bf(v[i]), _bf(k32[i] * jnp.exp2(g_end[i] - cum[i]))) for i in n]
    return o, st_new


def _scan_iotas():
    row = lax.broadcasted_iota(jnp.int32, (CHUNK, LANES), 0)
    r64 = lax.broadcasted_iota(jnp.int32, (CHUNK, CHUNK), 0)
    c64 = lax.broadcasted_iota(jnp.int32, (CHUNK, CHUNK), 1)
    return row, r64, c64


SCAN_BATCH = 2


def _scan_chains(in_refs, per_chain, n_heads):
    kinds = [[] for _ in range(per_chain)]
    rev = []
    for d in range(2):
        for bi in range(SCAN_BATCH):
            group = in_refs[(d * SCAN_BATCH + bi) * per_chain:(d * SCAN_BATCH + bi + 1) * per_chain]
            for h in range(n_heads):
                for kind, ref in zip(kinds, group):
                    kind.append((ref, h))
                rev.append(d == 1)
    return kinds, rev


def _scan_store(o, st, of_ref, ob_ref, st_ref, n_heads):
    i = 0
    for d, o_ref in ((0, of_ref), (1, ob_ref)):
        for bi in range(SCAN_BATCH):
            for h in range(n_heads):
                o_ref[h, bi * CHUNK:(bi + 1) * CHUNK, :] = _bf(o[i])
                st_ref[d, bi, h] = st[i]
                i += 1


def _gla_scan_kernel(*refs, n_heads):
    n_in = 2 * SCAN_BATCH * 4
    of_ref, ob_ref, st_ref = refs[n_in:]

    @pl.when(pl.program_id(1) == 0)
    def _():
        st_ref[...] = jnp.zeros_like(st_ref)

    row, r64, c64 = _scan_iotas()
    (q, k, v, g), rev = _scan_chains(refs[:n_in], 4, n_heads)
    load = lambda items: [ref[h] for ref, h in items]
    st = [st_ref[d, bi, h] for d in range(2) for bi in range(SCAN_BATCH) for h in range(n_heads)]
    o, st = _gla_chunks(load(q), load(k), load(v), load(g), st, rev, row, r64, c64)
    _scan_store(o, st, of_ref, ob_ref, st_ref, n_heads)


CTX_CHUNKS = CTX_LEN // CHUNK
LAT_CHUNKS = SEQ // CHUNK
SCAN_STEPS = CTX_CHUNKS + LAT_CHUNKS


def _fwd_chunk(b, n):
    return jnp.where(n < CTX_CHUNKS, N_LAT // CHUNK + b * CTX_CHUNKS + n, b * LAT_CHUNKS + n - CTX_CHUNKS)


def _bwd_chunk(b, n):
    return jnp.where(n < CTX_CHUNKS, N_LAT // CHUNK + b * CTX_CHUNKS + CTX_CHUNKS - 1 - n,
                     b * LAT_CHUNKS + SCAN_STEPS - 1 - n)


def _scan_in_maps(rank):
    def make(chunk_of, bi):
        def index_map(p, n):
            c = chunk_of(p * SCAN_BATCH + bi, n)
            return {1: (c, 0, 0), 2: (0, c, 0), 3: (c, 0)}[rank]
        return index_map
    return [make(chunk_of, bi) for chunk_of in (_fwd_chunk, _bwd_chunk) for bi in range(SCAN_BATCH)]


def _scan_out_pos(step, reverse):
    if not reverse:
        return step
    return jnp.where(step < CTX_CHUNKS, CTX_CHUNKS - 1 - step, CTX_CHUNKS + SCAN_STEPS - 1 - step)


def _scan_out_specs(n_heads, dv):
    return [pl.BlockSpec((n_heads, SCAN_BATCH * CHUNK, dv),
                         lambda p, n, reverse=reverse: (0, p * SCAN_STEPS + _scan_out_pos(n, reverse), 0))
            for reverse in (False, True)]


def _scan_out_block(t, c):
    is_lat = t < LAT_TILES
    b = jnp.where(is_lat, t // TILES_PER_SEQ, t - LAT_TILES)
    pos = jnp.where(is_lat, CTX_CHUNKS + (t % TILES_PER_SEQ) * CHUNKS_PER_TILE + c, c)
    return ((b // SCAN_BATCH) * SCAN_STEPS + pos) * SCAN_BATCH + b % SCAN_BATCH


def _gla_scan(q, k_f, k_b, v, g_f, g_b):
    n_heads, _, dk = q.shape
    dv = v.shape[2]
    maps = _scan_in_maps(2)
    kspec = lambda m: pl.BlockSpec((n_heads, CHUNK, dk), m)
    vspec = lambda m: pl.BlockSpec((n_heads, CHUNK, dv), m)
    in_specs, args = [], []
    for d, k_arr, g_arr in ((0, k_f, g_f), (1, k_b, g_b)):
        for bi in range(SCAN_BATCH):
            m = maps[d * SCAN_BATCH + bi]
            in_specs += [kspec(m), kspec(m), vspec(m), kspec(m)]
            args += [q, k_arr, v, g_arr]
    oshape = _head_shape(n_heads, dv)
    return pl.pallas_call(
        functools.partial(_gla_scan_kernel, n_heads=n_heads),
        grid=(BATCH // SCAN_BATCH, SCAN_STEPS),
        in_specs=in_specs,
        out_specs=_scan_out_specs(n_heads, dv),
        out_shape=[oshape, oshape],
        scratch_shapes=[pltpu.VMEM((2, SCAN_BATCH, n_heads, dv, dk), F32)],
        compiler_params=_cparams("parallel", "arbitrary"),
        name="gla_scan",
    )(*args)


def _lane_pick(x, lane, idx):
    return jnp.sum(jnp.where(lane == idx, x, 0.0), axis=1, keepdims=True)


def _dn_chunks(q, k, v, beta, g_col, g_row, st, rev, r64, c64):
    n = range(len(q))
    dk = k[0].shape[1]
    g_end = [g_row[i][:, 0:1] if rev[i] else g_row[i][:, CHUNK - 1:CHUNK] for i in n]
    causal = {False: r64 >= c64, True: r64 <= c64}
    gam = [jnp.where(causal[rev[i]], jnp.exp(jnp.minimum(g_col[i] - g_row[i], 0.0)), 0.0) for i in n]
    q32 = [q[i].astype(F32) for i in n]
    k32 = [k[i].astype(F32) for i in n]
    kb = [k32[i] * beta[i] for i in n]
    k16 = [_bf(k[i]) for i in n]
    m = [jnp.where(r64 == c64, 0.0, _nt(_bf(kb[i]), k16[i]) * gam[i]) for i in n]
    att = [_nt(_bf(q[i]), k16[i]) * gam[i] for i in n]
    y = None
    s = 1
    while s < CHUNK:
        shift = (2 * s).bit_length() - 1
        rlate = (r64 & s) != 0
        clate = (c64 & s) != 0
        pair = {False: jnp.logical_and(rlate, jnp.logical_not(clate)), True: jnp.logical_and(clate, jnp.logical_not(rlate))}
        if 2 * s < CHUNK:
            same = (r64 >> shift) == (c64 >> shift)
            pair = {d: jnp.logical_and(p, same) for d, p in pair.items()}
        c = [jnp.where(pair[rev[i]], m[i], 0.0) for i in n]
        if y is None:
            y = [-c[i] for i in n]
        else:
            c16 = [_bf(c[i]) for i in n]
            y16 = [_bf(y[i]) for i in n]
            p = [c[i] + _mm(y16[i], c16[i]) for i in n]
            y = [y[i] - (p[i] + _mm(_bf(p[i]), y16[i])) for i in n]
        s *= 2
    e_col = [jnp.exp(g_col[i]) for i in n]
    rhs = [jnp.concatenate([kb[i] * e_col[i], v[i].astype(F32) * beta[i]], axis=1) for i in n]
    wu = [rhs[i] + _mm(_bf(y[i]), _bf(rhs[i])) for i in n]
    st16 = [_bf(st[i]) for i in n]
    v_new = [wu[i][:, dk:] - _nt(_bf(wu[i][:, :dk]), st16[i]) for i in n]
    o = [_nt(_bf(q32[i] * e_col[i]), st16[i]) + _mm(_bf(att[i]), _bf(v_new[i])) for i in n]
    st_new = [st[i] * jnp.exp(g_end[i]) + _tn(_bf(v_new[i]), _bf(k32[i] * jnp.exp(g_end[i] - g_col[i]))) for i in n]
    return o, st_new


def _dn_scan_kernel(*refs):
    n_in = 2 * SCAN_BATCH * 5
    of_ref, ob_ref, st_ref = refs[n_in:]

    @pl.when(pl.program_id(1) == 0)
    def _():
        st_ref[...] = jnp.zeros_like(st_ref)

    _, r64, c64 = _scan_iotas()
    (q, k, v, gc, gr), rev = _scan_chains(refs[:n_in], 5, DN_HEADS)
    load = lambda items: [ref[h] for ref, h in items]
    tables = {id(ref): ref[...] for ref, _ in gc}
    beta = [tables[id(ref)][:, d * DN_HEADS + h:d * DN_HEADS + h + 1] for (ref, h), d in zip(gc, rev)]
    g_col = [tables[id(ref)][:, (2 + d) * DN_HEADS + h:(2 + d) * DN_HEADS + h + 1] for (ref, h), d in zip(gc, rev)]
    g_row = [ref[0, (2 + d) * DN_HEADS + h:(2 + d) * DN_HEADS + h + 1, :] for (ref, h), d in zip(gr, rev)]
    st = [st_ref[d, bi, h] for d in range(2) for bi in range(SCAN_BATCH) for h in range(DN_HEADS)]
    o, st = _dn_chunks(load(q), load(k), load(v), beta, g_col, g_row, st, rev, r64, c64)
    _scan_store(o, st, of_ref, ob_ref, st_ref, DN_HEADS)


def _dn_scan(q, k, v, bgc, bgr):
    head_maps, col_maps, row_maps = _scan_in_maps(2), _scan_in_maps(3), _scan_in_maps(1)
    in_specs, args = [], []
    for i in range(2 * SCAN_BATCH):
        in_specs += [pl.BlockSpec((DN_HEADS, CHUNK, LANES), head_maps[i])] * 3
        in_specs += [pl.BlockSpec((CHUNK, LANES), col_maps[i]), pl.BlockSpec((1, 4 * DN_HEADS, CHUNK), row_maps[i])]
        args += [q, k, v, bgc, bgr]
    oshape = _head_shape(DN_HEADS, LANES)
    return pl.pallas_call(
        _dn_scan_kernel,
        grid=(BATCH // SCAN_BATCH, SCAN_STEPS),
        in_specs=in_specs,
        out_specs=_scan_out_specs(DN_HEADS, LANES),
        out_shape=[oshape, oshape],
        scratch_shapes=[pltpu.VMEM((2, SCAN_BATCH, DN_HEADS, LANES, LANES), F32)],
        compiler_params=_cparams("parallel", "arbitrary"),
        name="dn_scan",
    )(*args)


def _layer_norm(z, g, b):
    mu = jnp.mean(z, axis=1, keepdims=True)
    zc = z - mu
    var = jnp.mean(zc * zc, axis=1, keepdims=True)
    return zc * lax.rsqrt(var + LN_EPS) * g + b


def _post_kernel(*refs, n_heads):
    o_refs, refs = refs[:2 * CHUNKS_PER_TILE], refs[2 * CHUNKS_PER_TILE:]
    gate_ref, xl_ref, xc_ref, mod_ref, nw_ref, wo_ref, lg_ref, lb_ref, rw_ref, rb_ref = refs[:10]
    out_refs = refs[10:]
    x1_refs, (h2_ref, logit_ref) = out_refs[:-2], out_refs[-2:]
    x_rows = _stream_rows(xl_ref, xc_ref)
    groups = [slice(g * TILE // POST_GROUPS, (g + 1) * TILE // POST_GROUPS) for g in range(POST_GROUPS)]
    chunks_per_group = CHUNKS_PER_TILE // POST_GROUPS
    on = []
    for g, rows in enumerate(groups):
        parts = []
        for h in range(n_heads):
            o = jnp.concatenate([o_refs[c][h].astype(F32) + o_refs[CHUNKS_PER_TILE + c][h].astype(F32)
                                 for c in range(g * chunks_per_group, (g + 1) * chunks_per_group)], axis=0)
            parts.append(o * lax.rsqrt(jnp.mean(o * o, axis=1, keepdims=True) + RMS_EPS) * nw_ref[...])
        on.append(jnp.concatenate(parts, axis=1) * _silu(gate_ref[rows, :].astype(F32)))
    y = [_mm(_bf(a), wo_ref[...]) for a in on]
    z = [ALPHA * x_rows[rows] + mod_ref[0, 2:3, :] * yg for rows, yg in zip(groups, y)]
    x1 = [_layer_norm(zg, lg_ref[...], lb_ref[...]) for zg in z]
    h2 = [a * (1.0 + mod_ref[0, 4:5, :]) + mod_ref[0, 3:4, :] for a in x1]
    logits = [jnp.dot(a, rw_ref[...], precision=HIGHEST, preferred_element_type=F32) + rb_ref[...] for a in h2]
    _stream_store(x1_refs, jnp.concatenate(x1, axis=0))
    h2_ref[...] = _pack_halves(jnp.concatenate(h2, axis=0))
    logit_ref[...] = jnp.concatenate(logits, axis=0)


POST_GROUPS = 2


def _post(o_f, o_b, gate, xl, xc, mod, norm_w, w_out, ln_g, ln_b, rw, rb, with_ctx):
    n_heads, _, dv = o_f.shape
    n_tiles = ALL_TILES if with_ctx else LAT_TILES
    n_rows = n_tiles * TILE
    tok = lambda w: jax.ShapeDtypeStruct((n_rows, w), F32)
    return pl.pallas_call(
        functools.partial(_post_kernel, n_heads=n_heads),
        grid=(n_tiles,),
        in_specs=[pl.BlockSpec((n_heads, CHUNK, dv), lambda t, c=c: (0, _scan_out_block(t, c), 0))
                  for _ in range(2) for c in range(CHUNKS_PER_TILE)] + [_tok_spec(D_MODEL)] + _stream_specs() + [
            _mod_spec(), _full_spec(norm_w.shape), _full_spec(w_out.shape), _full_spec(ln_g.shape),
            _full_spec(ln_b.shape), _full_spec(rw.shape), _full_spec(rb.shape)],
        out_specs=_stream_specs()[:2 if with_ctx else 1] + [_tok_spec(PACKED_D), _tok_spec(LANES)],
        out_shape=_stream_shapes(with_ctx) + [tok(PACKED_D), tok(LANES)],
        compiler_params=_cparams("arbitrary"),
        name="post_mixer",
    )(*([o_f] * CHUNKS_PER_TILE + [o_b] * CHUNKS_PER_TILE), gate, xl, xc, mod, norm_w, w_out, ln_g, ln_b, rw, rb)


def _route_kernel(logit_ref, route_ref, slot_ref, tiles_ref, run_ref, start_ref, table_ref):
    sweep = pl.program_id(0)
    t = pl.program_id(1)
    lane = lax.broadcasted_iota(jnp.int32, (TILE, LANES), 1)
    lanef = lane.astype(F32)

    def add_counts(onehot):
        new_run = run_ref[0:1, :] + jnp.sum(onehot, axis=0, keepdims=True)
        run_ref[...] = jnp.broadcast_to(new_run, run_ref.shape)
        return new_run

    @pl.when(t == 0)
    def _():
        run_ref[...] = jnp.zeros_like(run_ref)

    @pl.when(sweep == 0)
    def _():
        neg = jnp.float32(-jnp.inf)
        l = jnp.where(lane < N_EXPERTS, logit_ref[...], neg)
        vals, picks = [], []
        onehot = jnp.zeros((TILE, LANES), F32)
        for _ in range(TOP_K):
            m = jnp.max(l, axis=1, keepdims=True)
            pick = jnp.min(jnp.where(l == m, lanef, float(LANES)), axis=1, keepdims=True)
            hot = lanef == pick
            l = jnp.where(hot, neg, l)
            onehot = onehot + jnp.where(hot, 1.0, 0.0)
            vals.append(m)
            picks.append(pick)
        es = [jnp.exp(v - vals[0]) for v in vals]
        denom = es[0] + es[1] + es[2] + es[3]
        out = jnp.zeros((TILE, LANES), F32)
        for kk in range(TOP_K):
            out = jnp.where(lane == kk, picks[kk], out)
            out = jnp.where(lane == ROUTE_WEIGHT + kk, es[kk] / denom, out)
        table_ref[t] = out
        new_run = add_counts(onehot)

        @pl.when(t == pl.num_programs(1) - 1)
        def _():
            lane8 = lax.broadcasted_iota(jnp.int32, (SUBLANES, LANES), 1)
            size = jnp.ceil(jnp.broadcast_to(new_run, (SUBLANES, LANES)) * (1.0 / EXPERT_TILE)) * EXPERT_TILE
            size = jnp.where(lane8 < N_EXPERTS, size, 0.0)
            ends = size
            sh = 1
            while sh < N_EXPERTS:
                ends = ends + jnp.where(lane8 >= sh, pltpu.roll(ends, sh, 1), 0.0)
                sh *= 2
            start_ref[...] = ends - size
            tile_start = lax.broadcasted_iota(jnp.int32, (TILE, 1), 0).astype(F32) * EXPERT_TILE
            ends_t = jnp.broadcast_to(ends[0:1, :], (TILE, LANES))
            group_end_t = jnp.broadcast_to((ends - size)[0:1, :] + new_run, (TILE, LANES))
            expert = jnp.sum(jnp.where(jnp.logical_and(lane < N_EXPERTS, ends_t <= tile_start), 1.0, 0.0),
                             axis=1, keepdims=True)
            expert = jnp.minimum(expert, N_EXPERTS - 1.0)
            group_end = jnp.sum(jnp.where(lanef == expert, group_end_t, 0.0), axis=1, keepdims=True)
            valid = jnp.clip(group_end - tile_start, 0.0, float(EXPERT_TILE))
            total = jnp.max(ends_t, axis=1, keepdims=True)
            own_end = jnp.sum(jnp.where(lanef == expert, ends_t, 0.0), axis=1, keepdims=True)
            nxt = jnp.sum(jnp.where(jnp.logical_and(lane < N_EXPERTS, ends_t <= own_end), 1.0, 0.0), axis=1, keepdims=True)
            nxt = jnp.where(own_end < total, nxt, -1.0)
            tiles_ref[...] = jnp.where(lane == 0, expert, jnp.where(lane == 1, valid, jnp.where(
                lane == 2, total * (1.0 / EXPERT_TILE), jnp.where(lane == 3, nxt, 0.0))))

    @pl.when(sweep == 1)
    def _():
        out = table_ref[t]
        hots = [lanef == out[:, kk:kk + 1] for kk in range(TOP_K)]
        onehot = jnp.zeros((TILE, LANES), F32)
        for hot in hots:
            onehot = onehot + jnp.where(hot, 1.0, 0.0)
        r = lax.broadcasted_iota(jnp.int32, (TILE, TILE), 0)
        c = lax.broadcasted_iota(jnp.int32, (TILE, TILE), 1)
        before = jnp.where(c < r, 1.0, 0.0).astype(BF16)
        slot_of = _mm(before, _bf(onehot)) + run_ref[0:1, :] + start_ref[0:1, :]
        for kk in range(TOP_K):
            slot = jnp.sum(jnp.where(hots[kk], slot_of, 0.0), axis=1, keepdims=True)
            out = jnp.where(lane == ROUTE_SLOT + kk, slot, out)
        route_ref[...] = out
        slot_ref[...] = out.T[ROUTE_WEIGHT:ROUTE_WEIGHT + SUBLANES, :].astype(jnp.int32)
        add_counts(onehot)


ROUTE_WEIGHT = SUBLANES
ROUTE_SLOT = SUBLANES + TOP_K


def _route(logits):
    n_tok = logits.shape[0]
    n_tiles = n_tok // TILE
    return pl.pallas_call(
        _route_kernel,
        grid=(2, n_tiles),
        in_specs=[pl.BlockSpec((TILE, LANES), lambda s, t: (t, 0))],
        out_specs=[pl.BlockSpec((TILE, LANES), lambda s, t: (s * t, 0)),
                   pl.BlockSpec((SUBLANES, TILE), lambda s, t: (0, s * t)),
                   pl.BlockSpec((TILE, LANES), lambda s, t: (0, 0))],
        out_shape=[jax.ShapeDtypeStruct(logits.shape, F32), jax.ShapeDtypeStruct((SUBLANES, n_tok), jnp.int32),
                   jax.ShapeDtypeStruct((TILE, LANES), F32)],
        scratch_shapes=[pltpu.VMEM((SUBLANES, LANES), F32), pltpu.VMEM((SUBLANES, LANES), F32),
                        pltpu.VMEM((n_tiles, TILE, LANES), F32)],
        compiler_params=_cparams("arbitrary", "arbitrary"),
        name="route",
    )(logits)


GATHER_ROWS = 16
GATHER_BUFS = 4
SCATTER_BUFS = 4


def _sc_gather(table, idx):
    n_rows = idx.shape[0]
    d = table.shape[1]
    info = plsc.get_sparse_core_info()
    n_workers = info.num_cores * info.num_subcores
    rows_per_worker = n_rows // n_workers
    n_chunks = rows_per_worker // GATHER_ROWS
    assert rows_per_worker * n_workers == n_rows and n_chunks * GATHER_ROWS == rows_per_worker
    assert n_chunks % GATHER_BUFS == 0
    mesh = plsc.VectorSubcoreMesh(core_axis_name="c", subcore_axis_name="s")
    scratch = ([pltpu.VMEM((GATHER_ROWS,), jnp.int32)] * GATHER_BUFS
               + [pltpu.VMEM((GATHER_ROWS, d), table.dtype)] * GATHER_BUFS
               + [pltpu.SemaphoreType.DMA] * (2 * GATHER_BUFS))

    def body(table_hbm, idx_hbm, out_hbm, *scr):
        idx_v, rows_v = scr[:GATHER_BUFS], scr[GATHER_BUFS:2 * GATHER_BUFS]
        gather_sem, write_sem = scr[2 * GATHER_BUFS:3 * GATHER_BUFS], scr[3 * GATHER_BUFS:]
        worker = lax.axis_index("s") * info.num_cores + lax.axis_index("c")
        base = worker * rows_per_worker

        @pl.loop(0, n_chunks, step=GATHER_BUFS)
        def _(c0):
            offs = [pl.multiple_of(base + (c0 + b) * GATHER_ROWS, SUBLANES) for b in range(GATHER_BUFS)]
            gathers = []
            for b in range(GATHER_BUFS):
                pltpu.sync_copy(idx_hbm.at[pl.ds(offs[b], GATHER_ROWS)], idx_v[b])
                gathers.append(pltpu.async_copy(table_hbm.at[idx_v[b]], rows_v[b], gather_sem[b]))
            writes = []
            for b in range(GATHER_BUFS):
                gathers[b].wait()
                writes.append(pltpu.async_copy(rows_v[b], out_hbm.at[pl.ds(offs[b], GATHER_ROWS)], write_sem[b]))
            for w in writes:
                w.wait()

    return pl.kernel(body, out_type=jax.ShapeDtypeStruct((n_rows, d), table.dtype), mesh=mesh,
                     scratch_types=scratch, name="sc_gather")(table, idx)


def _sc_scatter(src, idx, n_out):
    n, d = src.shape
    info = plsc.get_sparse_core_info()
    n_workers = info.num_cores * info.num_subcores
    rows_per_worker = n // n_workers
    n_chunks = rows_per_worker // GATHER_ROWS
    assert rows_per_worker * n_workers == n and n_chunks * GATHER_ROWS == rows_per_worker
    assert n_chunks % SCATTER_BUFS == 0
    mesh = plsc.VectorSubcoreMesh(core_axis_name="c", subcore_axis_name="s")
    scratch = ([pltpu.VMEM((GATHER_ROWS,), jnp.int32)] * (SCATTER_BUFS * TOP_K)
               + [pltpu.VMEM((GATHER_ROWS, d), src.dtype)] * SCATTER_BUFS + [pltpu.SemaphoreType.DMA] * SCATTER_BUFS)

    def body(src_hbm, idx_hbm, out_hbm, *scr):
        idx_v = scr[:SCATTER_BUFS * TOP_K]
        rows_v = scr[SCATTER_BUFS * TOP_K:SCATTER_BUFS * TOP_K + SCATTER_BUFS]
        sem = scr[SCATTER_BUFS * TOP_K + SCATTER_BUFS:]
        worker = lax.axis_index("s") * info.num_cores + lax.axis_index("c")
        base = worker * rows_per_worker

        @pl.loop(0, n_chunks, step=SCATTER_BUFS)
        def _(c0):
            copies = []
            for b in range(SCATTER_BUFS):
                off = pl.multiple_of(base + (c0 + b) * GATHER_ROWS, SUBLANES)
                pltpu.sync_copy(src_hbm.at[pl.ds(off, GATHER_ROWS)], rows_v[b])
                for kk in range(TOP_K):
                    pltpu.sync_copy(idx_hbm.at[pl.ds(pl.multiple_of(kk * n + off, SUBLANES), GATHER_ROWS)],
                                    idx_v[b * TOP_K + kk])
                for kk in range(TOP_K):
                    copies.append(pltpu.async_copy(rows_v[b], out_hbm.at[idx_v[b * TOP_K + kk]], sem[b]))
            for c in copies:
                c.wait()

    return pl.kernel(body, out_type=jax.ShapeDtypeStruct((n_out, d), src.dtype), mesh=mesh,
                     scratch_types=scratch, name="sc_scatter")(src, idx)


PAIR_BLOCK = 2 * LANES


def _expert_kernel(te_ref, tv_ref, nt_ref, nx_ref, xs_ref, w1_hbm, b1_ref, w2_hbm, b2_ref, ys_ref,
                   w1p_ref, w2b_ref, w1_buf, w2_buf, sem, slot_ref, *, layer):
    i = pl.program_id(0)
    valid = i < nt_ref[0]
    new_expert = jnp.logical_or(i == 0, te_ref[i] != te_ref[jnp.maximum(i - 1, 0)])

    def weight_copies(expert, slot):
        return (pltpu.make_async_copy(w1_hbm.at[layer, expert], w1_buf.at[slot], sem.at[slot, 0]),
                pltpu.make_async_copy(w2_hbm.at[layer, expert], w2_buf.at[slot], sem.at[slot, 1]))

    @pl.when(i == 0)
    def _():
        slot_ref[0] = 0
        for cp in weight_copies(te_ref[0], 0):
            cp.start()

    @pl.when(jnp.logical_and(valid, new_expert))
    def _():
        slot = slot_ref[0]
        for cp in weight_copies(te_ref[i], slot):
            cp.wait()

        @pl.when(nx_ref[i] >= 0)
        def _():
            for cp in weight_copies(nx_ref[i], 1 - slot):
                cp.start()

        slot_ref[0] = 1 - slot
        r = lax.broadcasted_iota(jnp.int32, (PAIR_BLOCK, PAIR_BLOCK), 0)
        c = lax.broadcasted_iota(jnp.int32, (PAIR_BLOCK, PAIR_BLOCK), 1)
        src = jnp.where(c < LANES, 2 * c, 2 * (c - LANES) + 1)
        perm = jnp.where(r == src, 1.0, 0.0).astype(BF16)
        for b in range(2 * D_FF // PAIR_BLOCK):
            cols = slice(b * PAIR_BLOCK, (b + 1) * PAIR_BLOCK)
            w1p_ref[:, cols] = _bf(_mm(_bf(w1_buf[slot, :, cols]), perm))
        w2b_ref[...] = _bf(w2_buf[slot])

    def ffn(n_rows):
        rowi = lax.broadcasted_iota(jnp.int32, (n_rows, D_MODEL), 0)
        x = _bf(jnp.where(rowi < tv_ref[i], _unpack_halves(xs_ref[0:n_rows, :]), 0.0))
        u = _mm(x, w1p_ref[...]) + b1_ref[0]
        parts = []
        for b in range(2 * D_FF // PAIR_BLOCK):
            glu = jnp.minimum(u[:, b * PAIR_BLOCK:b * PAIR_BLOCK + LANES], SWIGLU_LIMIT)
            lin = jnp.clip(u[:, b * PAIR_BLOCK + LANES:(b + 1) * PAIR_BLOCK], -SWIGLU_LIMIT, SWIGLU_LIMIT)
            parts.append(glu * _sigmoid(SWIGLU_ALPHA * glu) * (lin + 1.0))
        act = jnp.concatenate(parts, axis=1)
        ys_ref[0:n_rows, :] = _pack_halves(_mm(_bf(act), w2b_ref[...]) + b2_ref[0])

    half = EXPERT_TILE // 2

    @pl.when(jnp.logical_and(valid, tv_ref[i] > half))
    def _():
        ffn(EXPERT_TILE)

    @pl.when(jnp.logical_and(valid, tv_ref[i] <= half))
    def _():
        ffn(half)


def _experts(layer, tile_expert, tile_valid, n_tiles, next_expert, xs, w1, b1p, w2, b2):
    tile_map = lambda i, te, tv, nt, nx: (jnp.minimum(i, nt[0] - 1), 0)
    wmap = lambda i, te, tv, nt, nx: (te[i], 0, 0)
    grid_spec = pltpu.PrefetchScalarGridSpec(
        num_scalar_prefetch=4,
        grid=(N_SLOT_TILES,),
        in_specs=[pl.BlockSpec((EXPERT_TILE, PACKED_D), tile_map),
                  pl.BlockSpec(memory_space=pl.ANY), pl.BlockSpec((1, 1, 2 * D_FF), wmap),
                  pl.BlockSpec(memory_space=pl.ANY), pl.BlockSpec((1, 1, D_MODEL), wmap)],
        out_specs=pl.BlockSpec((EXPERT_TILE, PACKED_D), tile_map),
        scratch_shapes=[pltpu.VMEM((D_MODEL, 2 * D_FF), BF16), pltpu.VMEM((D_FF, D_MODEL), BF16),
                        pltpu.VMEM((2, D_MODEL, 2 * D_FF), F32), pltpu.VMEM((2, D_FF, D_MODEL), F32),
                        pltpu.SemaphoreType.DMA((2, 2)), pltpu.SMEM((1,), jnp.int32)],
    )
    return pl.pallas_call(
        functools.partial(_expert_kernel, layer=layer),
        grid_spec=grid_spec,
        out_shape=jax.ShapeDtypeStruct((N_SLOTS, PACKED_D), F32),
        compiler_params=pltpu.CompilerParams(dimension_semantics=("arbitrary",), vmem_limit_bytes=EXPERT_VMEM_LIMIT),
        name="experts",
    )(tile_expert, tile_valid, n_tiles, next_expert, xs, w1, b1p, w2, b2)


def _combine_kernel(y_ref, route_ref, x1l_ref, x1c_ref, mod_ref, lg_ref, lb_ref, *refs, first_tile, n_carried):
    x2_refs = refs[n_carried:]
    route = route_ref[...]
    lane = lax.broadcasted_iota(jnp.int32, route.shape, 1)
    f = None
    for kk in range(TOP_K):
        term = _lane_pick(route, lane, ROUTE_WEIGHT + kk) * _unpack_halves(y_ref[kk])
        f = term if f is None else f + term
    z = ALPHA * _stream_rows(x1l_ref, x1c_ref, first_tile) + mod_ref[0, 5:6, :] * f
    _stream_store(x2_refs, _layer_norm(z, lg_ref[...], lb_ref[...]), first_tile)


def _combine(y_tok, route, x1l, x1c, mod, ln_g, ln_b, first_tile, n_tiles, write_ctx, carried_lat=None):
    carried = [] if carried_lat is None else [carried_lat]
    args = [y_tok, route, x1l, x1c, mod, ln_g, ln_b] + carried
    return pl.pallas_call(
        functools.partial(_combine_kernel, first_tile=first_tile, n_carried=len(carried)),
        grid=(n_tiles,),
        in_specs=[pl.BlockSpec((TOP_K, TILE, PACKED_D), lambda t: (0, t, 0)), _tok_spec(LANES, first_tile)]
        + _stream_specs(first_tile) + [_mod_spec(first_tile), _full_spec(ln_g.shape), _full_spec(ln_b.shape)]
        + [pl.BlockSpec(memory_space=pl.ANY)] * len(carried),
        out_specs=_stream_specs(first_tile)[:2 if write_ctx else 1],
        out_shape=_stream_shapes(write_ctx),
        input_output_aliases={len(args) - 1: 0} if carried else {},
        compiler_params=_cparams("arbitrary"),
        name="combine",
    )(*args)


def _moe(layer, h2, logits, x1l, x1c, mod, ln_g, ln_b, w1, b1p, w2, b2, with_ctx):
    n_tok = h2.shape[0]
    assert N_SLOT_TILES <= TILE
    route, slot_rows, tiles = _route(logits)
    slots = slot_rows[TOP_K:2 * TOP_K]
    tiles = tiles[:N_SLOT_TILES, :SUBLANES].astype(jnp.int32)
    tile_expert, tile_valid, n_tiles, next_expert = tiles[:, 0], tiles[:, 1], tiles[:1, 2], tiles[:, 3]
    xs = _sc_scatter(h2, slots.reshape(-1), N_SLOTS)
    ys = _experts(layer, tile_expert, tile_valid, n_tiles, next_expert, xs, w1, b1p, w2, b2)
    tok_tiles = n_tok // TILE
    first_part = tok_tiles // 2
    parts = ((0, first_part), (first_part, tok_tiles - first_part))
    out = None
    for part, (lo, n) in enumerate(parts):
        idx = slots[:, lo * TILE:(lo + n) * TILE].reshape(-1)
        y_tok = _sc_gather(ys, idx).reshape(TOP_K, n * TILE, PACKED_D)
        last = part == len(parts) - 1
        out = _combine(y_tok, route, x1l, x1c, mod, ln_g, ln_b, lo, n, with_ctx and last,
                       carried_lat=None if out is None else out[0])
    return out


def _lat_to_col_major(lat):
    return lat.reshape(BATCH, SEQ // GRID_W, GRID_W, D_MODEL).transpose(0, 2, 1, 3).reshape(N_LAT, D_MODEL)


def _lat_to_row_major(lat):
    return lat.reshape(BATCH, GRID_W, SEQ // GRID_W, D_MODEL).transpose(0, 2, 1, 3).reshape(N_LAT, D_MODEL)


def kernel(x, c, ctx, c_ctx, ada_w, ada_b, ln_g, ln_b, dn_w_in, dn_conv_w, dn_a_log, dn_dt_bias, dn_norm_w, dn_w_out,
           hg_w_in, hg_lower_bound, hg_norm_w, hg_w_out, gla_w_in, gla_gk_w, gla_gk_b, gla_norm_w, gla_w_out,
           router_w, router_b, exp_w1, exp_b1, exp_w2, exp_b2):
    xl = x.reshape(N_LAT, D_MODEL)
    xc = ctx.reshape(N_CTX, D_MODEL)
    cc = jnp.zeros((MOD_ROWS, D_MODEL), F32).at[:BATCH].set(c).at[BATCH].set(c_ctx)
    mods = _ada_all(cc, ada_w, ada_b).reshape(DEPTH, MOD_ROWS, ADA_CHUNKS, D_MODEL)
    lb_soft = jax.nn.softmax(hg_lower_bound.astype(F32), axis=0)
    lower_bounds = jnp.cumsum(lb_soft, axis=0) - lb_soft[0]

    for i in range(DEPTH):
        last = i == DEPTH - 1
        kind, j = i % 3, i // 3
        mod = mods[i]
        col_major = i % 2 == 1
        if col_major:
            xl = _lat_to_col_major(xl)

        if kind == 0:
            nq = 2 * D_MODEL + 2 * D_MODEL
            w = dn_w_in[j]
            w_ba = jnp.zeros((D_MODEL, LANES), F32).at[:, :4 * DN_HEADS].set(w[:, nq:])
            a_neg = jnp.zeros((LANES,), F32).at[2 * DN_HEADS:4 * DN_HEADS].set(-jnp.exp(dn_a_log[j].astype(F32)).reshape(-1))
            dtb = jnp.zeros((LANES,), F32).at[2 * DN_HEADS:4 * DN_HEADS].set(dn_dt_bias[j].astype(F32).reshape(-1))
            nb = 4 * DN_HEADS
            p_qkv, gate, bgc, bgr = _dn_proj(xl, xc, mod, _bf(w), _bf(w_ba), _bf(w[:, nq:].T),
                                            a_neg.reshape(1, LANES), dtb.reshape(1, LANES),
                                            a_neg[:nb].reshape(nb, 1), dtb[:nb].reshape(nb, 1))
            cw = jnp.zeros((SUBLANES, DN_QKV), F32).at[:SHORT_CONV].set(dn_conv_w[j])
            q, k, v = _dn_conv(p_qkv, cw)
            o_f, o_b = _dn_scan(q, k, v, bgc, bgr)
            norm_w, w_out = dn_norm_w[j], dn_w_out[j]
        elif kind == 1:
            q, v, gate, k_f, g_f, k_b, g_b = _hg_proj(xl, xc, mod, _bf(hg_w_in[j]), lower_bounds[i].reshape(1, D_MODEL))
            o_f, o_b = _gla_scan(q, k_f, k_b, v, g_f, g_b)
            norm_w, w_out = hg_norm_w[j], hg_w_out[j]
        else:
            kd = GLA_HEADS * GLA_DK
            n_main = 2 * kd + 2 * D_MODEL
            w = gla_w_in[j]
            w_lr = jnp.zeros((D_MODEL, 2 * LANES), F32)
            w_lr = w_lr.at[:, :GLA_RANK].set(w[:, n_main:n_main + GLA_RANK])
            w_lr = w_lr.at[:, LANES:LANES + GLA_RANK].set(w[:, n_main + GLA_RANK:])
            gk_w = jnp.zeros((2, LANES, kd), F32).at[:, :GLA_RANK].set(gla_gk_w[j])
            q, k, v, gate, g_f, g_b = _gla_proj(xl, xc, mod, _bf(w), _bf(w_lr), _bf(gk_w),
                                                gla_gk_b[j].reshape(2, 1, kd))
            o_f, o_b = _gla_scan(q, k, k, v, g_f, g_b)
            norm_w, w_out = gla_norm_w[j], gla_w_out[j]

        with_ctx = not last
        rw = jnp.zeros((D_MODEL, LANES), F32).at[:, :N_EXPERTS].set(router_w[i])
        rb = jnp.zeros((1, LANES), F32).at[0, :N_EXPERTS].set(router_b[i])
        *x1, h2, logits = _post(o_f, o_b, gate, xl, xc, mod, norm_w.reshape(1, -1), _bf(w_out),
                                ln_g[i, 0].reshape(1, D_MODEL), ln_b[i, 0].reshape(1, D_MODEL), rw, rb, with_ctx)
        b1p = exp_b1[i].reshape(N_EXPERTS, 2 * D_FF // PAIR_BLOCK, LANES, 2).transpose(0, 1, 3, 2)
        x2 = _moe(i, h2, logits, x1[0], x1[1] if with_ctx else xc, mod,
                  ln_g[i, 1].reshape(1, D_MODEL), ln_b[i, 1].reshape(1, D_MODEL),
                  exp_w1, b1p.reshape(N_EXPERTS, 1, 2 * D_FF), exp_w2, exp_b2[i][:, None, :], with_ctx)
        xl = _lat_to_row_major(x2[0]) if col_major else x2[0]
        if with_ctx:
            xc = x2[1]
    return xl.reshape(BATCH, SEQ, D_MODEL)
```
